```python
import math
import jax, jax.numpy as jnp
from jax import lax
import numpy as np

D_MODEL = 1024
BATCH = 4
SEQ = 8192
DEPTH = 2
DEC_BATCH = 32
DEC_SEQ = 64
PAST_LEN = 4096

CHUNK = 64
Q_BLOCK = 128
EPS = 1e-6
ROPE_BASE = 10000.0
MLA_HEADS = 8
QK_NOPE = 64
QK_ROPE = 32
V_HEAD = 64
Q_LORA = 384
KV_LORA = 256
M_HEADS = 4
M_DK = 64
M_DV = 128
G_HEADS = 8
G_DK = 128
G_DV = 128
CONV_W = 4
MIX_A = MLA_HEADS * V_HEAD + M_HEADS * M_DV
MIX_C = G_HEADS * G_DV
G_CONV_CH = 2 * G_HEADS * G_DK + G_HEADS * G_DV
IN_A_SIZES = (Q_LORA, KV_LORA, QK_ROPE, M_HEADS * M_DK, M_HEADS * M_DK, M_HEADS * M_DV,
              M_HEADS, M_HEADS, M_HEADS * M_DV, MIX_A)
IN_C_SIZES = (G_CONV_CH, G_HEADS, G_HEADS, MIX_C)
IN_A_DIM = sum(IN_A_SIZES)
IN_C_DIM = sum(IN_C_SIZES)

kernel_name = 'hybrid_mla_mlstm_gdn_stream_step'


def rmsnorm(x, g):
    xf = x.astype(jnp.float32)
    y = xf * lax.rsqrt(jnp.mean(xf * xf, axis=-1, keepdims=True) + EPS)
    return (y * g.astype(jnp.float32)).astype(x.dtype)


def l2norm(x):
    return x * lax.rsqrt(jnp.sum(x * x, axis=-1, keepdims=True) + EPS)


def split_cols(y, sizes):
    offs = np.cumsum(np.array(sizes))[:-1].tolist()
    return jnp.split(y, offs, axis=-1)


def adaln(c, w, b):
    mod = jax.nn.silu(c) @ w + b
    shift, scale, gate = jnp.split(mod, 3, axis=-1)
    return shift[:, None, :], scale[:, None, :], gate[:, None, :]


def rope(x, pos):
    r = x.shape[-1]
    freqs = jnp.exp(jnp.arange(0, r, 2, dtype=jnp.float32) * (-math.log(ROPE_BASE) / r))
    ang = pos.astype(jnp.float32)[:, None] * freqs[None, :]
    ang = ang.reshape((ang.shape[0],) + (1,) * (x.ndim - 3) + (r // 2,))
    cos, sin = jnp.cos(ang), jnp.sin(ang)
    xf = x.astype(jnp.float32)
    x1, x2 = xf[..., : r // 2], xf[..., r // 2:]
    return jnp.concatenate([x1 * cos - x2 * sin, x1 * sin + x2 * cos], axis=-1).astype(x.dtype)


def split_heads(a, n, d):
    b, t = a.shape[:2]
    return a.reshape(b, t, n, d).transpose(0, 2, 1, 3).astype(jnp.float32)


def to_chunks(a, L):
    b, h, t = a.shape[:3]
    a = a.reshape((b, h, t // L, L) + a.shape[3:])
    return jnp.moveaxis(a, 2, 0)


def from_chunks(a):
    a = jnp.moveaxis(a, 0, 2)
    return a.reshape((a.shape[0], a.shape[1], a.shape[2] * a.shape[3]) + a.shape[4:])


def block_attend(q, k, v, qpos, kpos):
    s = jnp.einsum('bqhd,bkhd->bhqk', q, k).astype(jnp.float32) * (q.shape[-1] ** -0.5)
    mask = (kpos // CHUNK)[None, :] <= (qpos // CHUNK)[:, None]
    p = jax.nn.softmax(jnp.where(mask, s, -jnp.inf), axis=-1).astype(v.dtype)
    return jnp.einsum('bhqk,bkhd->bqhd', p, v)


def chunk_causal_attention(q, k, v, qpos, kpos):
    b, t, h, dq = q.shape
    if t > Q_BLOCK and t % Q_BLOCK == 0:
        nb = t // Q_BLOCK
        qb = q.reshape(b, nb, Q_BLOCK, h, dq).transpose(1, 0, 2, 3, 4)
        pb = qpos.reshape(nb, Q_BLOCK)
        ob = lax.map(lambda qp: block_attend(qp[0], k, v, qp[1], kpos), (qb, pb))
        return ob.transpose(1, 0, 2, 3, 4).reshape(b, t, h, v.shape[-1])
    return block_attend(q, k, v, qpos, kpos)


def mlstm_chunk(state, inp):
    C0, n0, m0 = state
    q, k, v, li, lf = inp
    L = q.shape[2]
    causal = jnp.tril(jnp.ones((L, L), dtype=bool))
    b = jnp.cumsum(lf, axis=-1)
    dmat = jnp.where(causal, b[..., :, None] - b[..., None, :] + li[..., None, :], -jnp.inf)
    inter = b + m0[..., None]
    m = jnp.maximum(inter, jnp.max(dmat, axis=-1))
    w_inter = jnp.exp(inter - m)
    sqk = jnp.einsum('bhlk,bhsk->bhls', q, k) * jnp.exp(dmat - m[..., None])
    num = w_inter[..., None] * jnp.einsum('bhlk,bhkv->bhlv', q, C0) + jnp.einsum('bhls,bhsv->bhlv', sqk, v)
    den = w_inter * jnp.einsum('bhlk,bhk->bhl', q, n0) + jnp.sum(sqk, axis=-1)
    h = num / jnp.maximum(jnp.abs(den), jnp.exp(-m))[..., None]
    m_end = m[..., -1]
    decay_end = jnp.exp(inter[..., -1] - m_end)
    wk = jnp.exp(dmat[..., -1, :] - m_end[..., None])
    C1 = decay_end[..., None, None] * C0 + jnp.einsum('bhl,bhlk,bhlv->bhkv', wk, k, v)
    n1 = decay_end[..., None] * n0 + jnp.einsum('bhl,bhlk->bhk', wk, k)
    return (C1, n1, m_end), h


def gdn_chunk(S0, inp):
    q, k, v, g, beta = inp
    L = q.shape[2]
    dv = v.shape[-1]
    incl = jnp.tril(jnp.ones((L, L), dtype=bool))
    strict = jnp.tril(jnp.ones((L, L), dtype=jnp.float32), k=-1)
    G = jnp.cumsum(g, axis=-1)
    decay = jnp.exp(jnp.where(incl, G[..., :, None] - G[..., None, :], -jnp.inf))
    A = beta[..., :, None] * jnp.einsum('bhlk,bhsk->bhls', k, k) * decay * strict
    rhs = jnp.concatenate([beta[..., None] * v, (beta * jnp.exp(G))[..., None] * k], axis=-1)
    sol = lax.linalg.triangular_solve(A + jnp.eye(L, dtype=jnp.float32), rhs,
                                      left_side=True, lower=True, unit_diagonal=True)
    u = sol[..., :dv] - jnp.einsum('bhlk,bhkv->bhlv', sol[..., dv:], S0)
    o = (jnp.exp(G)[..., None] * jnp.einsum('bhlk,bhkv->bhlv', q, S0)
         + jnp.einsum('bhls,bhsv->bhlv', jnp.einsum('bhlk,bhsk->bhls', q, k) * decay, u))
    S1 = (jnp.exp(G[..., -1])[..., None, None] * S0
          + jnp.einsum('bhlk,bhlv->bhkv', k * jnp.exp(G[..., -1:] - G)[..., None], u))
    return S1, o


def mla_mlstm_mixer(h, pos, c_past, kr_past, C0, n0, m0, w_in, g_q_a, w_q_b, g_kv_a, w_kv_b,
                    b_i, b_f, g_out, w_out):
    bsz, t, _ = h.shape
    dt = h.dtype
    q_a, c_new, kr_new, mq, mk, mv, mi, mf, mo, z = split_cols(h @ w_in, IN_A_SIZES)
    q = jnp.einsum('btc,chd->bthd', rmsnorm(q_a, g_q_a), w_q_b)
    q = jnp.concatenate([q[..., :QK_NOPE], rope(q[..., QK_NOPE:], pos)], axis=-1)
    c_new = rmsnorm(c_new, g_kv_a)
    kr_new = rope(kr_new, pos)
    c_all = jnp.concatenate([c_past.astype(dt), c_new], axis=1)
    kr_all = jnp.concatenate([kr_past.astype(dt), kr_new], axis=1)
    s_len = c_all.shape[1]
    kv = jnp.einsum('bsc,chd->bshd', c_all, w_kv_b)
    k = jnp.concatenate([kv[..., :QK_NOPE],
                         jnp.broadcast_to(kr_all[:, :, None, :], (bsz, s_len, MLA_HEADS, QK_ROPE))], axis=-1)
    v = kv[..., QK_NOPE:]
    kpos = jnp.arange(s_len, dtype=jnp.int32)
    o_mla = chunk_causal_attention(q, k, v, pos, kpos).reshape(bsz, t, MLA_HEADS * V_HEAD)
    qm = split_heads(mq, M_HEADS, M_DK)
    km = split_heads(mk, M_HEADS, M_DK) * (M_DK ** -0.5)
    vm = split_heads(mv, M_HEADS, M_DV)
    li = (mi + b_i).astype(jnp.float32).transpose(0, 2, 1)
    lf = jax.nn.log_sigmoid((mf + b_f).astype(jnp.float32)).transpose(0, 2, 1)
    L = min(CHUNK, t)
    xs = tuple(to_chunks(a, L) for a in (qm, km, vm, li, lf))
    init = (C0.astype(jnp.float32), n0.astype(jnp.float32), m0.astype(jnp.float32))
    (C1, n1, m1), hs = lax.scan(mlstm_chunk, init, xs)
    hm = from_chunks(hs).transpose(0, 2, 1, 3)
    hm = rmsnorm(hm, g_out).astype(dt).reshape(bsz, t, M_HEADS * M_DV) * jax.nn.sigmoid(mo)
    mixed = jnp.concatenate([o_mla, hm], axis=-1) * jax.nn.silu(z)
    return (mixed @ w_out, c_new, kr_new, C1.astype(dt), n1.astype(dt), m1.astype(dt))


def gdn_mixer(h, conv_past, S0, w_in, w_conv, a_log, dt_bias, g_out, w_out):
    bsz, t, _ = h.shape
    dt = h.dtype
    qkv, a, bb, z = split_cols(h @ w_in, IN_C_SIZES)
    xp = jnp.concatenate([conv_past.astype(dt), qkv], axis=1)
    conv = w_conv[0] * xp[:, 0:t]
    for j in range(1, CONV_W):
        conv = conv + w_conv[j] * xp[:, j:j + t]
    q, k, v = split_cols(jax.nn.silu(conv), (G_HEADS * G_DK, G_HEADS * G_DK, G_HEADS * G_DV))
    q = l2norm(split_heads(q, G_HEADS, G_DK)) * (G_DK ** -0.5)
    k = l2norm(split_heads(k, G_HEADS, G_DK))
    v = split_heads(v, G_HEADS, G_DV)
    g = (-jnp.exp(a_log.astype(jnp.float32))
         * jax.nn.softplus((a + dt_bias).astype(jnp.float32))).transpose(0, 2, 1)
    beta = jax.nn.sigmoid(bb.astype(jnp.float32)).transpose(0, 2, 1)
    L = min(CHUNK, t)
    xs = tuple(to_chunks(a_, L) for a_ in (q, k, v, g, beta))
    S1, os_ = lax.scan(gdn_chunk, S0.astype(jnp.float32), xs)
    o = from_chunks(os_).transpose(0, 2, 1, 3)
    o = rmsnorm(o, g_out).astype(dt).reshape(bsz, t, MIX_C) * jax.nn.silu(z)
    return (o @ w_out, xp[:, t:], S1.astype(dt))


def setup_inputs(seed: int = 0) -> dict:
    key = jax.random.key(seed)
    ks = jax.random.split(key, 40)
    d = D_MODEL

    def nrm(k, shape, s):
        return jax.random.normal(k, shape, jnp.float32) * s

    return {
        'x_prompt': nrm(ks[0], (BATCH, SEQ, d), 1.0),
        'x_sample': nrm(ks[1], (DEC_BATCH, DEC_SEQ, d), 1.0),
        'c_prompt': nrm(ks[2], (BATCH, d), 1.0),
        'c_sample': nrm(ks[3], (DEC_BATCH, d), 1.0),
        'cache_kv_latent': nrm(ks[4], (DEC_BATCH, PAST_LEN, KV_LORA), 1.0),
        'cache_k_rope': nrm(ks[5], (DEC_BATCH, PAST_LEN, QK_ROPE), 1.0),
        'state_mlstm_C': nrm(ks[6], (DEC_BATCH, M_HEADS, M_DK, M_DV), 0.5),
        'state_mlstm_n': nrm(ks[7], (DEC_BATCH, M_HEADS, M_DK), 0.5),
        'state_mlstm_m': nrm(ks[8], (DEC_BATCH, M_HEADS), 1.0),
        'state_gdn_S': nrm(ks[9], (DEC_BATCH, G_HEADS, G_DK, G_DV), 0.5),
        'state_gdn_conv': nrm(ks[10], (DEC_BATCH, CONV_W - 1, G_CONV_CH), 1.0),
        'a_w_ada': nrm(ks[11], (d, 3 * d), 0.5 * d ** -0.5),
        'a_b_ada': nrm(ks[12], (3 * d,), 0.01),
        'a_g_norm': 1.0 + nrm(ks[13], (d,), 0.1),
        'a_w_in': nrm(ks[14], (d, IN_A_DIM), d ** -0.5),
        'a_g_q_a': 1.0 + nrm(ks[15], (Q_LORA,), 0.1),
        'a_w_q_b': nrm(ks[16], (Q_LORA, MLA_HEADS, QK_NOPE + QK_ROPE), Q_LORA ** -0.5),
        'a_g_kv_a': 1.0 + nrm(ks[17], (KV_LORA,), 0.1),
        'a_w_kv_b': nrm(ks[18], (KV_LORA, MLA_HEADS, QK_NOPE + V_HEAD), KV_LORA ** -0.5),
        'a_b_i': nrm(ks[19], (M_HEADS,), 0.1),
        'a_b_f': 3.0 + nrm(ks[20], (M_HEADS,), 0.5),
        'a_g_out': 1.0 + nrm(ks[21], (M_HEADS, M_DV), 0.1),
        'a_w_out': nrm(ks[22], (MIX_A, d), MIX_A ** -0.5),
        'c_w_ada': nrm(ks[23], (d, 3 * d), 0.5 * d ** -0.5),
        'c_b_ada': nrm(ks[24], (3 * d,), 0.01),
        'c_g_norm': 1.0 + nrm(ks[25], (d,), 0.1),
        'c_w_in': nrm(ks[26], (d, IN_C_DIM), d ** -0.5),
        'c_w_conv': nrm(ks[27], (CONV_W, G_CONV_CH), CONV_W ** -0.5),
        'c_a_log': jnp.log(jax.random.uniform(ks[28], (G_HEADS,), jnp.float32, 0.02, 1.0)),
        'c_dt_bias': nrm(ks[29], (G_HEADS,), 0.1),
        'c_g_out': 1.0 + nrm(ks[30], (G_DV,), 0.1),
        'c_w_out': nrm(ks[31], (MIX_C, d), MIX_C ** -0.5),
        'g_final': 1.0 + nrm(ks[32], (d,), 0.1),
    }


def reference(x_prompt, x_sample, c_prompt, c_sample, cache_kv_latent, cache_k_rope,
              state_mlstm_C, state_mlstm_n, state_mlstm_m, state_gdn_S, state_gdn_conv,
              a_w_ada, a_b_ada, a_g_norm, a_w_in, a_g_q_a, a_w_q_b, a_g_kv_a, a_w_kv_b,
              a_b_i, a_b_f, a_g_out, a_w_out,
              c_w_ada, c_b_ada, c_g_norm, c_w_in, c_w_conv, c_a_log, c_dt_bias, c_g_out, c_w_out,
              g_final):
    def run(x, c, past_c, past_kr, C0, n0, m0, conv0, S0):
        t = x.shape[1]
        pos = past_c.shape[1] + jnp.arange(t, dtype=jnp.int32)
        for layer in range(DEPTH):
            if layer % 2 == 0:
                shift, scale, gate = adaln(c, a_w_ada, a_b_ada)
                hn = rmsnorm(x, a_g_norm) * (1 + scale) + shift
                y, kv_new, kr_new, C1, n1, m1 = mla_mlstm_mixer(
                    hn, pos, past_c, past_kr, C0, n0, m0, a_w_in, a_g_q_a, a_w_q_b, a_g_kv_a,
                    a_w_kv_b, a_b_i, a_b_f, a_g_out, a_w_out)
            else:
                shift, scale, gate = adaln(c, c_w_ada, c_b_ada)
                hn = rmsnorm(x, c_g_norm) * (1 + scale) + shift
                y, conv1, S1 = gdn_mixer(hn, conv0, S0, c_w_in, c_w_conv, c_a_log, c_dt_bias,
                                         c_g_out, c_w_out)
            x = x + gate * y
        return (rmsnorm(x, g_final), kv_new, kr_new, C1, n1, m1, conv1, S1)

    dt = x_prompt.dtype
    bp = x_prompt.shape[0]
    (y_prompt, p_kv, p_kr, p_C, p_n, p_m, p_conv, p_S) = run(
        x_prompt, c_prompt,
        jnp.zeros((bp, 0, KV_LORA), dt), jnp.zeros((bp, 0, QK_ROPE), dt),
        jnp.zeros((bp, M_HEADS, M_DK, M_DV), dt), jnp.zeros((bp, M_HEADS, M_DK), dt),
        jnp.zeros((bp, M_HEADS), dt), jnp.zeros((bp, CONV_W - 1, G_CONV_CH), dt),
        jnp.zeros((bp, G_HEADS, G_DK, G_DV), dt))
    (y_sample, s_kv, s_kr, s_C, s_n, s_m, s_conv, s_S) = run(
        x_sample, c_sample, cache_kv_latent, cache_k_rope, state_mlstm_C, state_mlstm_n,
        state_mlstm_m, state_gdn_conv, state_gdn_S)
    return (y_prompt, y_sample, p_kv, p_kr, p_C, p_n, p_m, p_S, p_conv,
            s_kv, s_kr, s_C, s_n, s_m, s_S, s_conv)
```

```python
import functools
import math

import jax
import jax.numpy as jnp
from jax import lax
from jax.experimental import pallas as pl
from jax.experimental.pallas import tpu as pltpu

F32 = jnp.float32
BF16 = jnp.bfloat16
HIGHEST = lax.Precision.HIGHEST

CHUNK = 64
EPS = 1e-6
ROPE_BASE = 10000.0
LANES = 128
SUBLANES = 8
VMEM_LIMIT = 56 * 1024 * 1024
NEG_INF = float("-inf")


def _params(*sem):
    return pltpu.CompilerParams(dimension_semantics=sem, vmem_limit_bytes=VMEM_LIMIT)


def _dot(a, b, precision=None):
    return jnp.dot(a, b, preferred_element_type=F32, precision=precision)


def _dot_nt(a, b):
    return lax.dot_general(a, b, (((1,), (1,)), ((), ())), preferred_element_type=F32)


def _dot_tn(a, b):
    return lax.dot_general(a, b, (((0,), (0,)), ((), ())), preferred_element_type=F32)


def _rms(x, g):
    return x * lax.rsqrt(jnp.mean(x * x, axis=-1, keepdims=True) + EPS) * g


def _sigmoid(x):
    return 1.0 / (1.0 + jnp.exp(-x))


def _silu(x):
    return x * _sigmoid(x)


def _softplus(x):
    return jnp.maximum(x, 0.0) + jnp.log1p(jnp.exp(-jnp.abs(x)))


def _log_sigmoid(x):
    return -_softplus(-x)


def _adaln_kernel(c_ref, w_ref, b_ref, o_ref):
    o_ref[...] = _dot(_silu(c_ref[...]), w_ref[...], HIGHEST) + b_ref[...]


def _adaln(c, w, b):
    n, d = c.shape
    d3 = w.shape[1]
    return pl.pallas_call(
        _adaln_kernel,
        out_shape=jax.ShapeDtypeStruct((n, d3), F32),
        grid=(d3 // d,),
        in_specs=[pl.BlockSpec((n, d), lambda j: (0, 0)),
                  pl.BlockSpec((d, d), lambda j: (0, j)),
                  pl.BlockSpec((1, d), lambda j: (0, j))],
        out_specs=pl.BlockSpec((n, d), lambda j: (0, j)),
        compiler_params=_params("arbitrary"),
        name="adaln",
    )(c, w, b.reshape(1, d3))


def _in_a_kernel(x_ref, shift_ref, scale_ref, g_ref, w1_ref, gq_ref, wq_ref, gkv_ref, wkv_ref, tab_ref,
                 q_ref, c_ref, kr_ref, m_ref, z_ref, gt_ref, *kv_refs,
                 heads, q_lora, kv_lora, rope, m_width, d_model, v_head, expand_kv):
    x = x_ref[0]
    hn = _rms(x, g_ref[...]) * (1.0 + scale_ref[0]) + shift_ref[0]
    y = _dot(hn.astype(BF16), w1_ref[...])
    o = 0
    qa = y[:, o:o + q_lora]; o += q_lora
    cl = y[:, o:o + kv_lora]; o += kv_lora
    kr1 = y[:, o:o + LANES]; o += LANES
    kr2 = y[:, o:o + LANES]; o += LANES
    m_ref[0] = y[:, o:o + m_width]; o += m_width
    z_ref[0] = y[:, o:o + d_model]; o += d_model
    gt_ref[0] = y[:, o:o + LANES]

    tab = tab_ref[...]
    cosq, sinq = tab[:, 0:LANES], tab[:, LANES:2 * LANES]
    cosk, sink = tab[:, 2 * LANES:3 * LANES], tab[:, 3 * LANES:4 * LANES]

    qq = _dot(_rms(qa, gq_ref[...]).astype(BF16), wq_ref[...])
    hw = heads * LANES
    for h in range(heads):
        sl = slice(h * LANES, (h + 1) * LANES)
        q_ref[0, :, sl] = (qq[:, sl] * cosq + qq[:, hw + h * LANES:hw + (h + 1) * LANES] * sinq).astype(BF16)

    cn = _rms(cl, gkv_ref[...])
    c_ref[0] = cn
    kr = kr1 * cosk + kr2 * sink
    kr_ref[0] = kr[:, LANES // 2:LANES // 2 + rope]
    if expand_kv:
        k_ref, v_ref = kv_refs
        kv = _dot(cn.astype(BF16), wkv_ref[...])
        for h in range(heads):
            sl = slice(h * LANES, (h + 1) * LANES)
            k_ref[0, :, sl] = (kv[:, sl] + kr).astype(BF16)
        v_ref[0] = kv[:, hw:hw + heads * v_head].astype(BF16)


def _in_a(x, shift, scale, g, w1, gq, wq, gkv, wkv, tab, *, tm, heads, q_lora, kv_lora, rope, m_width,
          v_head, expand_kv):
    b, t, d = x.shape
    n1 = w1.shape[1]
    grid = (b, t // tm)
    tok = lambda last: pl.BlockSpec((1, tm, last), lambda i, j: (i, j, 0))
    const = lambda a: pl.BlockSpec(a.shape, lambda i, j: (0,) * a.ndim)
    out_shape = [jax.ShapeDtypeStruct((b, t, heads * LANES), BF16),
                 jax.ShapeDtypeStruct((b, t, kv_lora), F32),
                 jax.ShapeDtypeStruct((b, t, rope), F32),
                 jax.ShapeDtypeStruct((b, t, m_width), F32),
                 jax.ShapeDtypeStruct((b, t, d), F32),
                 jax.ShapeDtypeStruct((b, t, LANES), F32)]
    out_specs = [tok(heads * LANES), tok(kv_lora), tok(rope), tok(m_width), tok(d), tok(LANES)]
    if expand_kv:
        out_shape += [jax.ShapeDtypeStruct((b, t, heads * LANES), BF16),
                      jax.ShapeDtypeStruct((b, t, heads * v_head), BF16)]
        out_specs += [tok(heads * LANES), tok(heads * v_head)]
    kern = functools.partial(_in_a_kernel, heads=heads, q_lora=q_lora, kv_lora=kv_lora, rope=rope,
                             m_width=m_width, d_model=d, v_head=v_head, expand_kv=expand_kv)
    return pl.pallas_call(
        kern, out_shape=out_shape, grid=grid,
        in_specs=[tok(d),
                  pl.BlockSpec((1, 1, d), lambda i, j: (i, 0, 0)),
                  pl.BlockSpec((1, 1, d), lambda i, j: (i, 0, 0)),
                  const(g), const(w1), const(gq), const(wq), const(gkv), const(wkv),
                  pl.BlockSpec((tm, 4 * LANES), lambda i, j: (j, 0))],
        out_specs=out_specs,
        compiler_params=_params("arbitrary", "arbitrary"),
        name="in_proj_a",
    )(x, shift, scale, g, w1, gq, wq, gkv, wkv, tab)


def _flash_kernel(q_ref, k_ref, v_ref, o_ref, m_sc, l_sc, acc_sc, *, heads, v_head, tq, tk, chunk):
    qi = pl.program_id(1)
    ki = pl.program_id(2)

    @pl.when(ki == 0)
    def _():
        m_sc[...] = jnp.full(m_sc.shape, NEG_INF, F32)
        l_sc[...] = jnp.zeros(l_sc.shape, F32)
        acc_sc[...] = jnp.zeros(acc_sc.shape, F32)

    def step(masked):
        if masked:
            qc = (qi * tq + lax.broadcasted_iota(jnp.int32, (tq, tk), 0)) // chunk
            kc = (ki * tk + lax.broadcasted_iota(jnp.int32, (tq, tk), 1)) // chunk
            mask = kc <= qc
        for h in range(heads):
            qh = q_ref[0, :, h * LANES:(h + 1) * LANES]
            kh = k_ref[0, :, h * LANES:(h + 1) * LANES]
            vh = v_ref[0, :, h * v_head:(h + 1) * v_head]
            s = _dot_nt(qh, kh)
            if masked:
                s = jnp.where(mask, s, NEG_INF)
            m_prev = m_sc[h]
            m_new = jnp.maximum(m_prev, jnp.max(s, axis=-1, keepdims=True))
            alpha = jnp.exp(m_prev - m_new)
            p = jnp.exp(s - m_new)
            l_sc[h] = alpha * l_sc[h] + jnp.sum(p, axis=-1, keepdims=True)
            acc_sc[h] = alpha * acc_sc[h] + _dot(p.astype(BF16), vh)
            m_sc[h] = m_new

    @pl.when(ki < qi)
    def _():
        step(False)

    @pl.when(ki == qi)
    def _():
        step(True)
        for h in range(heads):
            o_ref[0, :, h * v_head:(h + 1) * v_head] = acc_sc[h] / l_sc[h]


def _flash(q, k, v, *, heads, v_head, tq):
    b, t, _ = q.shape
    tk = tq
    nq = t // tq
    kern = functools.partial(_flash_kernel, heads=heads, v_head=v_head, tq=tq, tk=tk, chunk=CHUNK)
    return pl.pallas_call(
        kern,
        out_shape=jax.ShapeDtypeStruct((b, t, heads * v_head), F32),
        grid=(b, nq, nq),
        in_specs=[pl.BlockSpec((1, tq, heads * LANES), lambda i, j, kk: (i, j, 0)),
                  pl.BlockSpec((1, tk, heads * LANES), lambda i, j, kk: (i, jnp.minimum(kk, j), 0)),
                  pl.BlockSpec((1, tk, heads * v_head), lambda i, j, kk: (i, jnp.minimum(kk, j), 0))],
        out_specs=pl.BlockSpec((1, tq, heads * v_head), lambda i, j, kk: (i, j, 0)),
        scratch_shapes=[pltpu.VMEM((heads, tq, 1), F32), pltpu.VMEM((heads, tq, 1), F32),
                        pltpu.VMEM((heads, tq, v_head), F32)],
        compiler_params=_params("arbitrary", "arbitrary", "arbitrary"),
        name="flash_attn",
    )(q, k, v)


def _latent_attn_kernel(q_ref, cp_ref, krp_ref, cn_ref, krn_ref, wk_ref, wv_ref, o_ref, *,
                        heads, nope, rope, v_head):
    q = q_ref[0]
    qabs, qrope = [], []
    for h in range(heads):
        qabs.append(_dot_nt(q[:, h * LANES:h * LANES + nope], wk_ref[h]))
        qrope.append(q[:, h * LANES + nope:h * LANES + nope + rope])
    qabs = jnp.concatenate(qabs, axis=0).astype(BF16)
    qrope = jnp.concatenate(qrope, axis=0)
    cp = cp_ref[0].astype(BF16)
    cn = cn_ref[0].astype(BF16)
    s_p = _dot_nt(qabs, cp) + _dot_nt(qrope, krp_ref[0].astype(BF16))
    s_n = _dot_nt(qabs, cn) + _dot_nt(qrope, krn_ref[0].astype(BF16))
    m = jnp.maximum(jnp.max(s_p, axis=-1, keepdims=True), jnp.max(s_n, axis=-1, keepdims=True))
    p_p = jnp.exp(s_p - m)
    p_n = jnp.exp(s_n - m)
    l = jnp.sum(p_p, axis=-1, keepdims=True) + jnp.sum(p_n, axis=-1, keepdims=True)
    o_lat = (_dot(p_p.astype(BF16), cp) + _dot(p_n.astype(BF16), cn)) / l
    t = q.shape[0]
    for h in range(heads):
        o_ref[0, :, h * v_head:(h + 1) * v_head] = _dot(o_lat[h * t:(h + 1) * t].astype(BF16), wv_ref[h])


def _latent_attn(q, c_past, kr_past, c_new, kr_new, wk, wv, *, heads, nope, rope, v_head):
    b, t, _ = q.shape
    past, kv_lora = c_past.shape[1:]
    blk = lambda n, last: pl.BlockSpec((1, n, last), lambda i: (i, 0, 0))
    const = lambda a: pl.BlockSpec(a.shape, lambda i: (0,) * a.ndim)
    kern = functools.partial(_latent_attn_kernel, heads=heads, nope=nope, rope=rope, v_head=v_head)
    return pl.pallas_call(
        kern,
        out_shape=jax.ShapeDtypeStruct((b, t, heads * v_head), F32),
        grid=(b,),
        in_specs=[blk(t, heads * LANES), blk(past, kv_lora), blk(past, rope), blk(t, kv_lora), blk(t, rope),
                  const(wk), const(wv)],
        out_specs=blk(t, heads * v_head),
        compiler_params=_params("arbitrary"),
        name="latent_attn",
    )(q, c_past, kr_past, c_new, kr_new, wk, wv)


def _mlstm_kernel(m_ref, gc_ref, gr_ref, bc_ref, br_ref, gout_ref, c0_ref, n0_ref, m0_ref,
                  h_ref, c1_ref, n1_ref, m1_ref, c_sc, n_sc, m_sc, *, heads, dk, dv, chunk, nchunk):
    t = pl.program_id(1)

    @pl.when(t == 0)
    def _():
        c_sc[...] = c0_ref[0]
        n_sc[...] = n0_ref[0]
        m_sc[...] = m0_ref[0]

    row = lax.broadcasted_iota(jnp.int32, (chunk, chunk), 0)
    col = lax.broadcasted_iota(jnp.int32, (chunk, chunk), 1)
    causal = col <= row
    tril = causal.astype(F32)
    triu = (row <= col).astype(F32)
    o_k, o_v, o_o = heads * dk, 2 * heads * dk, 2 * heads * dk + heads * dv

    def body(c, carry):
        off = pl.multiple_of(c * chunk, chunk)
        gc = gc_ref[0, pl.ds(off, chunk), :] + bc_ref[...]
        gr = gr_ref[0, c] + br_ref[...]
        bcum_c = _dot(tril, _log_sigmoid(gc), HIGHEST)
        bcum_r = _dot(_log_sigmoid(gr), triu, HIGHEST)
        for h in range(heads):
            qf = m_ref[0, pl.ds(off, chunk), h * dk:(h + 1) * dk]
            q = qf.astype(BF16)
            k = m_ref[0, pl.ds(off, chunk), o_k + h * dk:o_k + (h + 1) * dk] * (dk ** -0.5)
            v = m_ref[0, pl.ds(off, chunk), o_v + h * dv:o_v + (h + 1) * dv].astype(BF16)
            mo = m_ref[0, pl.ds(off, chunk), o_o + h * dv:o_o + (h + 1) * dv]
            li_c, li_r = gc[:, h:h + 1], gr[h:h + 1, :]
            b_c, b_r = bcum_c[:, heads + h:heads + h + 1], bcum_r[heads + h:heads + h + 1, :]
            c0, n0, m0 = c_sc[h], n_sc[h], m_sc[h]
            dmat = jnp.where(causal, b_c - b_r + li_r, NEG_INF)
            inter = b_c + m0
            m = jnp.maximum(inter, jnp.max(dmat, axis=-1, keepdims=True))
            w_inter = jnp.exp(inter - m)
            kb = k.astype(BF16)
            sqk = _dot_nt(q, kb) * jnp.exp(dmat - m)
            num = w_inter * _dot(q, c0.astype(BF16)) + _dot(sqk.astype(BF16), v)
            den = (w_inter * jnp.sum(qf * n0, axis=-1, keepdims=True)
                   + jnp.sum(sqk, axis=-1, keepdims=True))
            hh = num / jnp.maximum(jnp.abs(den), jnp.exp(-m))
            m_end = m[chunk - 1:chunk, :]
            decay_end = jnp.exp(inter[chunk - 1:chunk, :] - m_end)
            wk = jnp.exp(b_c[chunk - 1:chunk, :] - b_c + li_c - m_end) * k
            c_sc[h] = decay_end * c0 + _dot_tn(wk.astype(BF16), v)
            n_sc[h] = decay_end * n0 + jnp.sum(wk, axis=0, keepdims=True)
            m_sc[h] = m_end
            hn = _rms(hh, gout_ref[:, h * dv:(h + 1) * dv])
            h_ref[0, pl.ds(off, chunk), h * dv:(h + 1) * dv] = hn * _sigmoid(mo)
        return carry

    lax.fori_loop(0, nchunk, body, 0)

    @pl.when(t == pl.num_programs(1) - 1)
    def _():
        c1_ref[0] = c_sc[...]
        n1_ref[0] = n_sc[...]
        m1_ref[0] = m_sc[...]


def _mlstm(m_slab, gates_c, gates_r, bias_c, bias_r, gout, c0, n0, m0, *, tc, heads, dk, dv):
    b, t, mw = m_slab.shape
    chunk = min(CHUNK, t)
    nchunk = tc // chunk
    kern = functools.partial(_mlstm_kernel, heads=heads, dk=dk, dv=dv, chunk=chunk, nchunk=nchunk)
    st = lambda *s: pl.BlockSpec((1,) + s, lambda i, j: (i,) + (0,) * len(s))
    const = lambda a: pl.BlockSpec(a.shape, lambda i, j: (0,) * a.ndim)
    return pl.pallas_call(
        kern,
        out_shape=[jax.ShapeDtypeStruct((b, t, heads * dv), F32),
                   jax.ShapeDtypeStruct((b, heads, dk, dv), F32),
                   jax.ShapeDtypeStruct((b, heads, 1, dk), F32),
                   jax.ShapeDtypeStruct((b, heads, 1, 1), F32)],
        grid=(b, t // tc),
        in_specs=[pl.BlockSpec((1, tc, mw), lambda i, j: (i, j, 0)),
                  pl.BlockSpec((1, tc, LANES), lambda i, j: (i, j, 0)),
                  pl.BlockSpec((1, nchunk, SUBLANES, chunk), lambda i, j: (i, j, 0, 0)),
                  const(bias_c), const(bias_r), const(gout),
                  st(heads, dk, dv), st(heads, 1, dk), st(heads, 1, 1)],
        out_specs=[pl.BlockSpec((1, tc, heads * dv), lambda i, j: (i, j, 0)),
                   st(heads, dk, dv), st(heads, 1, dk), st(heads, 1, 1)],
        scratch_shapes=[pltpu.VMEM((heads, dk, dv), F32), pltpu.VMEM((heads, 1, dk), F32),
                        pltpu.VMEM((heads, 1, 1), F32)],
        compiler_params=_params("arbitrary", "arbitrary"),
        name="mlstm_scan",
    )(m_slab, gates_c, gates_r, bias_c, bias_r, gout, c0, n0, m0)


def _mid_kernel(oa_ref, ob_ref, z_ref, x_ref, gate_ref, wo_ref, shift_ref, scale_ref, g_ref, w2_ref,
                x1_ref, qkv_ref, z2_ref, ab_ref, *, half, conv_ch, d_model):
    z = z_ref[0]
    ma = (oa_ref[0] * _silu(z[:, :half])).astype(BF16)
    mb = (ob_ref[0] * _silu(z[:, half:])).astype(BF16)
    y = _dot(ma, wo_ref[0:half, :]) + _dot(mb, wo_ref[half:, :])
    x1 = x_ref[0] + gate_ref[0] * y
    x1_ref[0] = x1
    hn = _rms(x1, g_ref[...]) * (1.0 + scale_ref[0]) + shift_ref[0]
    y2 = _dot(hn.astype(BF16), w2_ref[...])
    qkv_ref[0] = y2[:, :conv_ch]
    z2_ref[0] = y2[:, conv_ch:conv_ch + d_model]
    ab_ref[0] = y2[:, conv_ch + d_model:]


def _mid(oa, ob, z, x, gate, wo, shift, scale, g, w2, *, tm, conv_ch):
    b, t, d = x.shape
    half = oa.shape[-1]
    tok = lambda last: pl.BlockSpec((1, tm, last), lambda i, j: (i, j, 0))
    vec = pl.BlockSpec((1, 1, d), lambda i, j: (i, 0, 0))
    const = lambda a: pl.BlockSpec(a.shape, lambda i, j: (0,) * a.ndim)
    kern = functools.partial(_mid_kernel, half=half, conv_ch=conv_ch, d_model=d)
    return pl.pallas_call(
        kern,
        out_shape=[jax.ShapeDtypeStruct((b, t, d), F32), jax.ShapeDtypeStruct((b, t, conv_ch), F32),
                   jax.ShapeDtypeStruct((b, t, d), F32), jax.ShapeDtypeStruct((b, t, LANES), F32)],
        grid=(b, t // tm),
        in_specs=[tok(half), tok(ob.shape[-1]), tok(d), tok(d), vec, const(wo), vec, vec, const(g), const(w2)],
        out_specs=[tok(d), tok(conv_ch), tok(d), tok(LANES)],
        compiler_params=_params("arbitrary", "arbitrary"),
        name="out_a_in_c",
    )(oa, ob, z, x, gate, wo, shift, scale, g, w2)


def _conv_kernel(qkv_ref, past_ref, wc_ref, ab_ref, alog_ref, dtb_ref, act_ref, gb_ref, ext_sc, *,
                 tm, width, heads, dk):
    @pl.when(pl.program_id(1) == 0)
    def _():
        ext_sc[0:SUBLANES, :] = past_ref[0]

    ext_sc[SUBLANES:SUBLANES + tm, :] = qkv_ref[0]
    conv = wc_ref[width - 1:width, :] * ext_sc[SUBLANES:SUBLANES + tm, :]
    for j in range(width - 1):
        s = SUBLANES - (width - 1) + j
        conv = conv + wc_ref[j:j + 1, :] * ext_sc[s:s + tm, :]
    ext_sc[0:SUBLANES, :] = ext_sc[tm:tm + SUBLANES, :]
    act = _silu(conv)
    for h in range(2 * heads):
        xh = act[:, h * dk:(h + 1) * dk]
        xh = xh * lax.rsqrt(jnp.sum(xh * xh, axis=-1, keepdims=True) + EPS)
        if h < heads:
            xh = xh * (dk ** -0.5)
        act_ref[0, :, h * dk:(h + 1) * dk] = xh
    act_ref[0, :, 2 * heads * dk:] = act[:, 2 * heads * dk:]
    ab = ab_ref[0]
    g = -jnp.exp(alog_ref[...]) * _softplus(ab + dtb_ref[...])
    lane = lax.broadcasted_iota(jnp.int32, ab.shape, 1)
    gb_ref[0] = jnp.where(lane < heads, g, _sigmoid(ab))


def _conv(qkv, past8, wc8, ab, alog, dtb, *, tm, width, heads, dk):
    b, t, ch = qkv.shape
    tok = lambda last: pl.BlockSpec((1, tm, last), lambda i, j: (i, j, 0))
    const = lambda a: pl.BlockSpec(a.shape, lambda i, j: (0,) * a.ndim)
    kern = functools.partial(_conv_kernel, tm=tm, width=width, heads=heads, dk=dk)
    return pl.pallas_call(
        kern,
        out_shape=[jax.ShapeDtypeStruct((b, t, ch), F32), jax.ShapeDtypeStruct((b, t, LANES), F32)],
        grid=(b, t // tm),
        in_specs=[tok(ch), pl.BlockSpec((1, SUBLANES, ch), lambda i, j: (i, 0, 0)), const(wc8), tok(LANES),
                  const(alog), const(dtb)],
        out_specs=[tok(ch), tok(LANES)],
        scratch_shapes=[pltpu.VMEM((tm + SUBLANES, ch), F32)],
        compiler_params=_params("arbitrary", "arbitrary"),
        name="conv_gates",
    )(qkv, past8, wc8, ab, alog, dtb)


def _unit_lower_inverse(a4, group, chunk):
    w = group * chunk
    r = lax.broadcasted_iota(jnp.int32, (chunk, w), 0)
    cc = lax.broadcasted_iota(jnp.int32, (chunk, w), 1) % chunk
    br = lax.broadcasted_iota(jnp.int32, (w, w), 0) // chunk
    bc = lax.broadcasted_iota(jnp.int32, (w, w), 1) // chunk
    same_block = br == bc

    def blockdiag(x):
        return jnp.where(same_block, jnp.concatenate([x] * group, axis=0), 0.0)

    eye = (r == cc).astype(F32)
    d = eye - jnp.where((r // 2 == cc // 2) & (r % 2 == 1) & (cc % 2 == 0), a4, 0.0)
    s = 2
    while s < chunk:
        off = (r // (2 * s) == cc // (2 * s)) & (r % (2 * s) >= s) & (cc % (2 * s) < s)
        p = _dot(jnp.where(off, a4, 0.0), blockdiag(d), HIGHEST)
        d = d - _dot(d, blockdiag(p), HIGHEST)
        s *= 2
    return d


def _gdn_kernel(act_ref, gbc_ref, gbr_ref, gout_ref, s0_ref, o_ref, s1_ref, s_sc, *,
                heads, dk, dv, chunk, nchunk, group):
    t = pl.program_id(1)

    @pl.when(t == 0)
    def _():
        s_sc[...] = s0_ref[0]

    row = lax.broadcasted_iota(jnp.int32, (chunk, chunk), 0)
    col = lax.broadcasted_iota(jnp.int32, (chunk, chunk), 1)
    incl = col <= row
    strict = col < row
    tril = incl.astype(F32)
    triu = (row <= col).astype(F32)
    o_k, o_v = heads * dk, 2 * heads * dk

    def body(c, carry):
        off = pl.multiple_of(c * chunk, chunk)
        gbc = gbc_ref[0, pl.ds(off, chunk), :]
        gbr = gbr_ref[0, c]
        gcum_c = _dot(tril, gbc, HIGHEST)
        gcum_r = _dot(gbr, triu, HIGHEST)
        for g0 in range(0, heads, group):
            per_head, a_blocks = [], []
            for h in range(g0, g0 + group):
                k = act_ref[0, pl.ds(off, chunk), o_k + h * dk:o_k + (h + 1) * dk]
                kb = k.astype(BF16)
                g_c, g_r = gcum_c[:, h:h + 1], gcum_r[h:h + 1, :]
                beta = gbc[:, heads + h:heads + h + 1]
                decay = jnp.exp(jnp.where(incl, g_c - g_r, NEG_INF))
                a_blocks.append(jnp.where(strict, beta * _dot_nt(kb, kb) * decay, 0.0))
                per_head.append((k, kb, g_c, beta, decay))
            tinv = _unit_lower_inverse(jnp.concatenate(a_blocks, axis=1), group, chunk)
            for i, h in enumerate(range(g0, g0 + group)):
                k, kb, g_c, beta, decay = per_head[i]
                q = act_ref[0, pl.ds(off, chunk), h * dk:(h + 1) * dk].astype(BF16)
                v = act_ref[0, pl.ds(off, chunk), o_v + h * dv:o_v + (h + 1) * dv]
                th = tinv[:, i * chunk:(i + 1) * chunk]
                s0 = s_sc[h]
                s0b = s0.astype(BF16)
                eg = jnp.exp(g_c)
                g_last = g_c[chunk - 1:chunk, :]
                sol_v = _dot(th, beta * v, HIGHEST)
                sol_k = _dot(th, (beta * eg) * k, HIGHEST)
                u = sol_v - _dot(sol_k.astype(BF16), s0b)
                ub = u.astype(BF16)
                o = eg * _dot(q, s0b) + _dot((_dot_nt(q, kb) * decay).astype(BF16), ub)
                s_sc[h] = jnp.exp(g_last) * s0 + _dot_tn((k * jnp.exp(g_last - g_c)).astype(BF16), ub)
                o_ref[0, pl.ds(off, chunk), h * dv:(h + 1) * dv] = _rms(o, gout_ref[...])
        return carry

    lax.fori_loop(0, nchunk, body, 0)

    @pl.when(t == pl.num_programs(1) - 1)
    def _():
        s1_ref[0] = s_sc[...]


def _gdn(act, gb_c, gb_r, gout, s0, *, tc, heads, dk, dv):
    b, t, ch = act.shape
    chunk = min(CHUNK, t)
    nchunk = tc // chunk
    group = (2 * LANES) // chunk
    kern = functools.partial(_gdn_kernel, heads=heads, dk=dk, dv=dv, chunk=chunk, nchunk=nchunk, group=group)
    return pl.pallas_call(
        kern,
        out_shape=[jax.ShapeDtypeStruct((b, t, heads * dv), F32), jax.ShapeDtypeStruct(s0.shape, F32)],
        grid=(b, t // tc),
        in_specs=[pl.BlockSpec((1, tc, ch), lambda i, j: (i, j, 0)),
                  pl.BlockSpec((1, tc, LANES), lambda i, j: (i, j, 0)),
                  pl.BlockSpec((1, nchunk, 2 * SUBLANES, chunk), lambda i, j: (i, j, 0, 0)),
                  pl.BlockSpec(gout.shape, lambda i, j: (0, 0)),
                  pl.BlockSpec((1, heads, dk, dv), lambda i, j: (i, 0, 0, 0))],
        out_specs=[pl.BlockSpec((1, tc, heads * dv), lambda i, j: (i, j, 0)),
                   pl.BlockSpec((1, heads, dk, dv), lambda i, j: (i, 0, 0, 0))],
        scratch_shapes=[pltpu.VMEM((heads, dk, dv), F32)],
        compiler_params=_params("arbitrary", "arbitrary"),
        name="gdn_scan",
    )(act, gb_c, gb_r, gout, s0)


def _final_kernel(o_ref, z_ref, x_ref, gate_ref, wo_ref, g_ref, y_ref):
    mixed = (o_ref[0] * _silu(z_ref[0])).astype(BF16)
    x2 = x_ref[0] + gate_ref[0] * _dot(mixed, wo_ref[...])
    y_ref[0] = _rms(x2, g_ref[...])


def _final(o, z, x, gate, wo, g, *, tm):
    b, t, d = x.shape
    tok = lambda last: pl.BlockSpec((1, tm, last), lambda i, j: (i, j, 0))
    const = lambda a: pl.BlockSpec(a.shape, lambda i, j: (0,) * a.ndim)
    return pl.pallas_call(
        _final_kernel,
        out_shape=jax.ShapeDtypeStruct((b, t, d), F32),
        grid=(b, t // tm),
        in_specs=[tok(o.shape[-1]), tok(d), tok(d), pl.BlockSpec((1, 1, d), lambda i, j: (i, 0, 0)),
                  const(wo), const(g)],
        out_specs=tok(d),
        compiler_params=_params("arbitrary", "arbitrary"),
        name="out_c_final",
    )(o, z, x, gate, wo, g)


def _pad_lanes(w, width=LANES, at=0):
    out = jnp.zeros(w.shape[:-1] + (width,), w.dtype)
    return out.at[..., at:at + w.shape[-1]].set(w)


def _rot_half_cols(w):
    r = w.shape[-1] // 2
    return jnp.concatenate([-w[..., r:], w[..., :r]], axis=-1)


def _rope_tables(pos, rope, scale):
    freqs = jnp.exp(jnp.arange(0, rope, 2, dtype=F32) * (-math.log(ROPE_BASE) / rope))
    ang = pos.astype(F32)[:, None] * freqs[None, :]
    cos = jnp.concatenate([jnp.cos(ang), jnp.cos(ang)], axis=-1)
    sin = jnp.concatenate([jnp.sin(ang), jnp.sin(ang)], axis=-1)
    half = LANES // 2
    cosk = _pad_lanes(cos, at=half)
    sink = _pad_lanes(sin, at=half)
    ones = _pad_lanes(jnp.ones((pos.shape[0], half), F32))
    return jnp.concatenate([(cosk + ones) * scale, sink * scale, cosk, sink], axis=-1)


def _tokens_on_lanes(a, chunk, rows):
    b, t = a.shape[:2]
    return a[..., :rows].reshape(b, t // chunk, chunk, rows).transpose(0, 1, 3, 2)


def kernel(x_prompt, x_sample, c_prompt, c_sample, cache_kv_latent, cache_k_rope, state_mlstm_C, state_mlstm_n, state_mlstm_m, state_gdn_S, state_gdn_conv, a_w_ada, a_b_ada, a_g_norm, a_w_in, a_g_q_a, a_w_q_b, a_g_kv_a, a_w_kv_b, a_b_i, a_b_f, a_g_out, a_w_out, c_w_ada, c_b_ada, c_g_norm, c_w_in, c_w_conv, c_a_log, c_dt_bias, c_g_out, c_w_out, g_final):
    d = x_prompt.shape[-1]
    q_lora, heads, qk = a_w_q_b.shape
    kv_lora = a_w_kv_b.shape[0]
    rope = cache_k_rope.shape[-1]
    nope = qk - rope
    v_head = a_w_kv_b.shape[2] - nope
    m_heads, m_dv = a_g_out.shape
    m_dk = state_mlstm_C.shape[2]
    g_heads = c_a_log.shape[0]
    g_dk, g_dv = state_gdn_S.shape[2:]
    width = c_w_conv.shape[0]
    conv_ch = c_w_conv.shape[1]
    assert nope + rope <= LANES and nope == LANES // 2 and 2 * m_heads <= SUBLANES and 2 * g_heads <= 2 * SUBLANES

    sizes = (q_lora, kv_lora, rope, m_heads * m_dk, m_heads * m_dk, m_heads * m_dv, m_heads, m_heads,
             m_heads * m_dv, heads * v_head + m_heads * m_dv)
    offs = [0]
    for s in sizes:
        offs.append(offs[-1] + s)
    w_qa, w_c, w_kr, w_mq, w_mk, w_mv, w_mi, w_mf, w_mo, w_z = [a_w_in[:, offs[i]:offs[i + 1]] for i in range(10)]
    half = LANES // 2
    w1 = jnp.concatenate([w_qa, w_c, _pad_lanes(w_kr, at=half), _pad_lanes(_rot_half_cols(w_kr), at=half),
                          w_mq, w_mk, w_mv, w_mo, w_z, _pad_lanes(jnp.concatenate([w_mi, w_mf], axis=1))],
                         axis=1).astype(BF16)
    m_width = 2 * m_heads * m_dk + 2 * m_heads * m_dv
    wq_rope = a_w_q_b[..., nope:]
    wq_main = _pad_lanes(a_w_q_b).reshape(q_lora, heads * LANES)
    wq_rot = _pad_lanes(_rot_half_cols(wq_rope), at=nope).reshape(q_lora, heads * LANES)
    wq = jnp.concatenate([wq_main, wq_rot], axis=1).astype(BF16)
    wkv = jnp.concatenate([_pad_lanes(a_w_kv_b[..., :nope]).reshape(kv_lora, heads * LANES),
                           a_w_kv_b[..., nope:].reshape(kv_lora, heads * v_head)], axis=1).astype(BF16)
    wk_abs = a_w_kv_b[..., :nope].transpose(1, 0, 2).astype(BF16)
    wv_abs = a_w_kv_b[..., nope:].transpose(1, 0, 2).astype(BF16)
    wo_a = a_w_out.astype(BF16)
    csz = (conv_ch, g_heads, g_heads, g_heads * g_dv)
    w_qkv, w_a, w_b, w_zc = [c_w_in[:, sum(csz[:i]):sum(csz[:i + 1])] for i in range(4)]
    w2 = jnp.concatenate([w_qkv, w_zc, _pad_lanes(jnp.concatenate([w_a, w_b], axis=1))], axis=1).astype(BF16)
    wo_c = c_w_out.astype(BF16)
    wc8 = jnp.zeros((SUBLANES, conv_ch), F32).at[:width].set(c_w_conv)
    row = lambda a: a.reshape(1, -1).astype(F32)
    bias_c = _pad_lanes(jnp.concatenate([a_b_i, a_b_f]).reshape(1, -1))
    bias_r = jnp.zeros((SUBLANES, 1), F32).at[:2 * m_heads, 0].set(jnp.concatenate([a_b_i, a_b_f]))
    alog = _pad_lanes(c_a_log.reshape(1, -1))
    dtb = _pad_lanes(c_dt_bias.reshape(1, -1))

    bp, bs = c_prompt.shape[0], c_sample.shape[0]
    c_all = jnp.concatenate([c_prompt, c_sample], axis=0)
    pad = (-c_all.shape[0]) % SUBLANES
    c_all = jnp.pad(c_all, ((0, pad), (0, 0)))
    mod_a = _adaln(c_all, a_w_ada, a_b_ada)
    mod_c = _adaln(c_all, c_w_ada, c_b_ada)

    def mods(mod, lo, hi):
        return [mod[lo:hi, i * d:(i + 1) * d][:, None, :] for i in range(3)]

    def run(x, mod_lo, mod_hi, c_past, kr_past, c0, n0, m0, conv0, s0):
        b, t, _ = x.shape
        past = 0 if c_past is None else c_past.shape[1]
        chunk = min(CHUNK, t)
        tm = min(t, 256)
        tc = min(t, 512)
        shift_a, scale_a, gate_a = mods(mod_a, mod_lo, mod_hi)
        shift_c, scale_c, gate_c = mods(mod_c, mod_lo, mod_hi)
        tab = _rope_tables(past + jnp.arange(t, dtype=jnp.int32), rope, qk ** -0.5)
        expand = c_past is None
        outs = _in_a(x, shift_a, scale_a, row(a_g_norm), w1, row(a_g_q_a), wq, row(a_g_kv_a), wkv, tab,
                     tm=tm, heads=heads, q_lora=q_lora, kv_lora=kv_lora, rope=rope, m_width=m_width,
                     v_head=v_head, expand_kv=expand)
        q, c_new, kr_new, m_slab, z, gates = outs[:6]
        if expand:
            o_mla = _flash(q, outs[6], outs[7], heads=heads, v_head=v_head, tq=min(t, 512))
        else:
            o_mla = _latent_attn(q, c_past, kr_past, c_new, kr_new, wk_abs, wv_abs,
                                 heads=heads, nope=nope, rope=rope, v_head=v_head)
        hm, c1, n1, m1 = _mlstm(m_slab, gates, _tokens_on_lanes(gates, chunk, SUBLANES), bias_c, bias_r,
                                row(a_g_out), c0, n0.reshape(b, m_heads, 1, m_dk), m0.reshape(b, m_heads, 1, 1),
                                tc=tc, heads=m_heads, dk=m_dk, dv=m_dv)
        x1, qkv, zc, ab = _mid(o_mla, hm, z, x, gate_a, wo_a, shift_c, scale_c, row(c_g_norm), w2,
                               tm=tm, conv_ch=conv_ch)
        past8 = jnp.pad(conv0, ((0, 0), (SUBLANES - (width - 1), 0), (0, 0)))
        act, gb = _conv(qkv, past8, wc8, ab, alog, dtb, tm=tm, width=width, heads=g_heads, dk=g_dk)
        o_gdn, s1 = _gdn(act, gb, _tokens_on_lanes(gb, chunk, 2 * SUBLANES), row(c_g_out), s0,
                         tc=tc, heads=g_heads, dk=g_dk, dv=g_dv)
        y = _final(o_gdn, zc, x1, gate_c, wo_c, row(g_final), tm=tm)
        conv1 = jnp.concatenate([conv0, qkv], axis=1)[:, t:] if t < width - 1 else qkv[:, t - (width - 1):]
        return (y, c_new, kr_new, c1, n1.reshape(b, m_heads, m_dk), m1.reshape(b, m_heads), conv1, s1)

    dt = x_prompt.dtype
    (y_p, p_kv, p_kr, p_c, p_n, p_m, p_conv, p_s) = run(
        x_prompt, 0, bp, None, None,
        jnp.zeros((bp, m_heads, m_dk, m_dv), dt), jnp.zeros((bp, m_heads, m_dk), dt), jnp.zeros((bp, m_heads), dt),
        jnp.zeros((bp, width - 1, conv_ch), dt), jnp.zeros((bp, g_heads, g_dk, g_dv), dt))
    (y_s, s_kv, s_kr, s_c, s_n, s_m, s_conv, s_s) = run(
        x_sample, bp, bp + bs, cache_kv_latent, cache_k_rope, state_mlstm_C, state_mlstm_n, state_mlstm_m,
        state_gdn_conv, state_gdn_S)
    return (y_p, y_s, p_kv, p_kr, p_c, p_n, p_m, p_s, p_conv,
            s_kv, s_kr, s_c, s_n, s_m, s_s, s_conv)
```

```python
import functools
import math

import jax
import jax.numpy as jnp
from jax import lax
from jax.experimental import pallas as pl
from jax.experimental.pallas import tpu as pltpu

F32 = jnp.float32
BF16 = jnp.bfloat16
HIGHEST = lax.Precision.HIGHEST

CHUNK = 64
EPS = 1e-6
ROPE_BASE = 10000.0
LANES = 128
SUBLANES = 8
VMEM_LIMIT = 56 * 1024 * 1024
NEG_INF = float("-inf")
GDN_PREP_TC = 128


def _params(*sem):
    return pltpu.CompilerParams(dimension_semantics=sem, vmem_limit_bytes=VMEM_LIMIT)


def _dot(a, b, precision=None):
    return jnp.dot(a, b, preferred_element_type=F32, precision=precision)


def _dot_nt(a, b):
    return lax.dot_general(a, b, (((1,), (1,)), ((), ())), preferred_element_type=F32)


def _dot_tn(a, b):
    return lax.dot_general(a, b, (((0,), (0,)), ((), ())), preferred_element_type=F32)


def _rms(x, g):
    return x * lax.rsqrt(jnp.mean(x * x, axis=-1, keepdims=True) + EPS) * g


def _sigmoid(x):
    return 1.0 / (1.0 + jnp.exp(-x))


def _silu(x):
    return x * _sigmoid(x)


def _softplus(x):
    return jnp.maximum(x, 0.0) + jnp.log1p(jnp.exp(-jnp.abs(x)))


def _log_sigmoid(x):
    return -_softplus(-x)


def _adaln_kernel(c_ref, w_ref, b_ref, o_ref):
    o_ref[...] = _dot(_silu(c_ref[...]), w_ref[...], HIGHEST) + b_ref[...]


def _adaln(c, w, b):
    n, d = c.shape
    d3 = w.shape[1]
    return pl.pallas_call(
        _adaln_kernel,
        out_shape=jax.ShapeDtypeStruct((n, d3), F32),
        grid=(d3 // d,),
        in_specs=[pl.BlockSpec((n, d), lambda j: (0, 0)),
                  pl.BlockSpec((d, d), lambda j: (0, j)),
                  pl.BlockSpec((1, d), lambda j: (0, j))],
        out_specs=pl.BlockSpec((n, d), lambda j: (0, j)),
        compiler_params=_params("arbitrary"),
        name="adaln",
    )(c, w, b.reshape(1, d3))


def _in_a_kernel(x_ref, shift_ref, scale_ref, g_ref, w1_ref, gq_ref, wq_ref, gkv_ref, wkv_ref, tab_ref,
                 q_ref, c_ref, kr_ref, m_ref, z_ref, gt_ref, *kv_refs,
                 heads, q_lora, kv_lora, rope, m_width, d_model, v_head, expand_kv):
    x = x_ref[0]
    hn = _rms(x, g_ref[...]) * (1.0 + scale_ref[0]) + shift_ref[0]
    y = _dot(hn.astype(BF16), w1_ref[...])
    o = 0
    qa = y[:, o:o + q_lora]; o += q_lora
    cl = y[:, o:o + kv_lora]; o += kv_lora
    kr1 = y[:, o:o + LANES]; o += LANES
    kr2 = y[:, o:o + LANES]; o += LANES
    m_ref[0] = y[:, o:o + m_width]; o += m_width
    z_ref[0] = y[:, o:o + d_model]; o += d_model
    gt_ref[0] = y[:, o:o + LANES]

    tab = tab_ref[...]
    cosq, sinq = tab[:, 0:LANES], tab[:, LANES:2 * LANES]
    cosk, sink = tab[:, 2 * LANES:3 * LANES], tab[:, 3 * LANES:4 * LANES]

    qq = _dot(_rms(qa, gq_ref[...]).astype(BF16), wq_ref[...])
    hw = heads * LANES
    for h in range(heads):
        sl = slice(h * LANES, (h + 1) * LANES)
        q_ref[0, :, sl] = (qq[:, sl] * cosq + qq[:, hw + h * LANES:hw + (h + 1) * LANES] * sinq).astype(BF16)

    cn = _rms(cl, gkv_ref[...])
    c_ref[0] = cn
    kr = kr1 * cosk + kr2 * sink
    kr_ref[0] = kr[:, LANES // 2:LANES // 2 + rope]
    if expand_kv:
        k_ref, v_ref = kv_refs
        kv = _dot(cn.astype(BF16), wkv_ref[...])
        for h in range(heads):
            sl = slice(h * LANES, (h + 1) * LANES)
            k_ref[0, :, sl] = (kv[:, sl] + kr).astype(BF16)
        v_ref[0] = kv[:, hw:hw + heads * v_head].astype(BF16)


def _in_a(x, shift, scale, g, w1, gq, wq, gkv, wkv, tab, *, tm, heads, q_lora, kv_lora, rope, m_width,
          v_head, expand_kv):
    b, t, d = x.shape
    n1 = w1.shape[1]
    grid = (b, t // tm)
    tok = lambda last: pl.BlockSpec((1, tm, last), lambda i, j: (i, j, 0))
    const = lambda a: pl.BlockSpec(a.shape, lambda i, j: (0,) * a.ndim)
    out_shape = [jax.ShapeDtypeStruct((b, t, heads * LANES), BF16),
                 jax.ShapeDtypeStruct((b, t, kv_lora), F32),
                 jax.ShapeDtypeStruct((b, t, rope), F32),
                 jax.ShapeDtypeStruct((b, t, m_width), F32),
                 jax.ShapeDtypeStruct((b, t, d), F32),
                 jax.ShapeDtypeStruct((b, t, LANES), F32)]
    out_specs = [tok(heads * LANES), tok(kv_lora), tok(rope), tok(m_width), tok(d), tok(LANES)]
    if expand_kv:
        out_shape += [jax.ShapeDtypeStruct((b, t, heads * LANES), BF16),
                      jax.ShapeDtypeStruct((b, t, heads * v_head), BF16)]
        out_specs += [tok(heads * LANES), tok(heads * v_head)]
    kern = functools.partial(_in_a_kernel, heads=heads, q_lora=q_lora, kv_lora=kv_lora, rope=rope,
                             m_width=m_width, d_model=d, v_head=v_head, expand_kv=expand_kv)
    return pl.pallas_call(
        kern, out_shape=out_shape, grid=grid,
        in_specs=[tok(d),
                  pl.BlockSpec((1, 1, d), lambda i, j: (i, 0, 0)),
                  pl.BlockSpec((1, 1, d), lambda i, j: (i, 0, 0)),
                  const(g), const(w1), const(gq), const(wq), const(gkv), const(wkv),
                  pl.BlockSpec((tm, 4 * LANES), lambda i, j: (j, 0))],
        out_specs=out_specs,
        compiler_params=_params("arbitrary", "arbitrary"),
        name="in_proj_a",
    )(x, shift, scale, g, w1, gq, wq, gkv, wkv, tab)


def _flash_kernel(q_ref, k_ref, vt_ref, o_ref, m_sc, l_sc, acc_sc, *, heads, v_head, tq, tk, chunk):
    qi = pl.program_id(1)
    ki = pl.program_id(2)

    @pl.when(ki == 0)
    def _():
        m_sc[...] = jnp.full(m_sc.shape, NEG_INF, F32)
        l_sc[...] = jnp.zeros(l_sc.shape, F32)
        acc_sc[...] = jnp.zeros(acc_sc.shape, F32)

    def step(masked):
        if masked:
            kc = (ki * tk + lax.broadcasted_iota(jnp.int32, (tk, tq), 0)) // chunk
            qc = (qi * tq + lax.broadcasted_iota(jnp.int32, (tk, tq), 1)) // chunk
            mask = kc <= qc
        for h in range(heads):
            qh = q_ref[0, :, h * LANES:(h + 1) * LANES]
            kh = k_ref[0, :, h * LANES:(h + 1) * LANES]
            vth = vt_ref[0, h * v_head:(h + 1) * v_head, :]
            rows = slice(h * v_head, (h + 1) * v_head)
            st = _dot_nt(kh, qh)
            if masked:
                st = jnp.where(mask, st, NEG_INF)
            m_prev = m_sc[h]
            m_new = jnp.maximum(m_prev, jnp.max(st, axis=0, keepdims=True))
            alpha = jnp.exp(m_prev - m_new)
            p = jnp.exp(st - m_new)
            l_sc[h] = alpha * l_sc[h] + jnp.sum(p, axis=0, keepdims=True)
            acc_sc[rows, :] = alpha * acc_sc[rows, :] + _dot(vth, p.astype(BF16))
            m_sc[h] = m_new

    @pl.when(ki < qi)
    def _():
        step(False)

    @pl.when(ki == qi)
    def _():
        step(True)
        for h in range(heads):
            rows = slice(h * v_head, (h + 1) * v_head)
            acc_sc[rows, :] = acc_sc[rows, :] / l_sc[h]
        o_ref[0] = acc_sc[...].T


def _flash(q, k, vt, *, heads, v_head, tq):
    b, t, _ = q.shape
    tk = tq
    nq = t // tq
    kern = functools.partial(_flash_kernel, heads=heads, v_head=v_head, tq=tq, tk=tk, chunk=CHUNK)
    return pl.pallas_call(
        kern,
        out_shape=jax.ShapeDtypeStruct((b, t, heads * v_head), F32),
        grid=(b, nq, nq),
        in_specs=[pl.BlockSpec((1, tq, heads * LANES), lambda i, j, kk: (i, j, 0)),
                  pl.BlockSpec((1, tk, heads * LANES), lambda i, j, kk: (i, jnp.minimum(kk, j), 0)),
                  pl.BlockSpec((1, heads * v_head, tk), lambda i, j, kk: (i, 0, jnp.minimum(kk, j)))],
        out_specs=pl.BlockSpec((1, tq, heads * v_head), lambda i, j, kk: (i, j, 0)),
        scratch_shapes=[pltpu.VMEM((heads, 1, tq), F32), pltpu.VMEM((heads, 1, tq), F32),
                        pltpu.VMEM((heads * v_head, tq), F32)],
        compiler_params=_params("arbitrary", "arbitrary", "arbitrary"),
        name="flash_attn",
    )(q, k, vt)


def _latent_attn_kernel(q_ref, cp_ref, krp_ref, cn_ref, krn_ref, wk_ref, wv_ref, o_ref, *,
                        heads, nope, rope, v_head):
    q = q_ref[0]
    qabs, qrope = [], []
    for h in range(heads):
        qabs.append(_dot_nt(q[:, h * LANES:h * LANES + nope], wk_ref[h]))
        qrope.append(q[:, h * LANES + nope:h * LANES + nope + rope])
    qabs = jnp.concatenate(qabs, axis=0).astype(BF16)
    qrope = jnp.concatenate(qrope, axis=0)
    cp = cp_ref[0].astype(BF16)
    cn = cn_ref[0].astype(BF16)
    s_p = _dot_nt(qabs, cp) + _dot_nt(qrope, krp_ref[0].astype(BF16))
    s_n = _dot_nt(qabs, cn) + _dot_nt(qrope, krn_ref[0].astype(BF16))
    m = jnp.maximum(jnp.max(s_p, axis=-1, keepdims=True), jnp.max(s_n, axis=-1, keepdims=True))
    p_p = jnp.exp(s_p - m)
    p_n = jnp.exp(s_n - m)
    l = jnp.sum(p_p, axis=-1, keepdims=True) + jnp.sum(p_n, axis=-1, keepdims=True)
    o_lat = (_dot(p_p.astype(BF16), cp) + _dot(p_n.astype(BF16), cn)) / l
    t = q.shape[0]
    for h in range(heads):
        o_ref[0, :, h * v_head:(h + 1) * v_head] = _dot(o_lat[h * t:(h + 1) * t].astype(BF16), wv_ref[h])


def _latent_attn(q, c_past, kr_past, c_new, kr_new, wk, wv, *, heads, nope, rope, v_head):
    b, t, _ = q.shape
    past, kv_lora = c_past.shape[1:]
    blk = lambda n, last: pl.BlockSpec((1, n, last), lambda i: (i, 0, 0))
    const = lambda a: pl.BlockSpec(a.shape, lambda i: (0,) * a.ndim)
    kern = functools.partial(_latent_attn_kernel, heads=heads, nope=nope, rope=rope, v_head=v_head)
    return pl.pallas_call(
        kern,
        out_shape=jax.ShapeDtypeStruct((b, t, heads * v_head), F32),
        grid=(b,),
        in_specs=[blk(t, heads * LANES), blk(past, kv_lora), blk(past, rope), blk(t, kv_lora), blk(t, rope),
                  const(wk), const(wv)],
        out_specs=blk(t, heads * v_head),
        compiler_params=_params("arbitrary"),
        name="latent_attn",
    )(q, c_past, kr_past, c_new, kr_new, wk, wv)


def _mlstm_kernel(m_ref, gc_ref, gr_ref, bc_ref, br_ref, gout_ref, c0_ref, n0_ref, m0_ref,
                  h_ref, c1_ref, n1_ref, m1_ref, c_sc, n_sc, m_sc, *, heads, dk, dv, chunk, nchunk):
    t = pl.program_id(1)

    @pl.when(t == 0)
    def _():
        c_sc[...] = c0_ref[0]
        n_sc[...] = n0_ref[0]
        m_sc[...] = m0_ref[0]

    row = lax.broadcasted_iota(jnp.int32, (chunk, chunk), 0)
    col = lax.broadcasted_iota(jnp.int32, (chunk, chunk), 1)
    causal = col <= row
    tril = causal.astype(F32)
    triu = (row <= col).astype(F32)
    o_k, o_v, o_o = heads * dk, 2 * heads * dk, 2 * heads * dk + heads * dv

    def body(c, carry):
        off = pl.multiple_of(c * chunk, chunk)
        gc = gc_ref[0, pl.ds(off, chunk), :] + bc_ref[...]
        gr = gr_ref[0, c] + br_ref[...]
        bcum_c = _dot(tril, _log_sigmoid(gc), HIGHEST)
        bcum_r = _dot(_log_sigmoid(gr), triu, HIGHEST)
        for h in range(heads):
            qf = m_ref[0, pl.ds(off, chunk), h * dk:(h + 1) * dk]
            q = qf.astype(BF16)
            k = m_ref[0, pl.ds(off, chunk), o_k + h * dk:o_k + (h + 1) * dk] * (dk ** -0.5)
            v = m_ref[0, pl.ds(off, chunk), o_v + h * dv:o_v + (h + 1) * dv].astype(BF16)
            mo = m_ref[0, pl.ds(off, chunk), o_o + h * dv:o_o + (h + 1) * dv]
            li_c, li_r = gc[:, h:h + 1], gr[h:h + 1, :]
            b_c, b_r = bcum_c[:, heads + h:heads + h + 1], bcum_r[heads + h:heads + h + 1, :]
            c0, n0, m0 = c_sc[h], n_sc[h], m_sc[h]
            dmat = jnp.where(causal, b_c - b_r + li_r, NEG_INF)
            inter = b_c + m0
            m = jnp.maximum(inter, jnp.max(dmat, axis=-1, keepdims=True))
            w_inter = jnp.exp(inter - m)
            kb = k.astype(BF16)
            sqk = _dot_nt(q, kb) * jnp.exp(dmat - m)
            num = w_inter * _dot(q, c0.astype(BF16)) + _dot(sqk.astype(BF16), v)
            den = (w_inter * jnp.sum(qf * n0, axis=-1, keepdims=True)
                   + jnp.sum(sqk, axis=-1, keepdims=True))
            hh = num / jnp.maximum(jnp.abs(den), jnp.exp(-m))
            m_end = m[chunk - 1:chunk, :]
            decay_end = jnp.exp(inter[chunk - 1:chunk, :] - m_end)
            wk = jnp.exp(b_c[chunk - 1:chunk, :] - b_c + li_c - m_end) * k
            c_sc[h] = decay_end * c0 + _dot_tn(wk.astype(BF16), v)
            n_sc[h] = decay_end * n0 + jnp.sum(wk, axis=0, keepdims=True)
            m_sc[h] = m_end
            hn = _rms(hh, gout_ref[:, h * dv:(h + 1) * dv])
            h_ref[0, pl.ds(off, chunk), h * dv:(h + 1) * dv] = hn * _sigmoid(mo)
        return carry

    lax.fori_loop(0, nchunk, body, 0)

    @pl.when(t == pl.num_programs(1) - 1)
    def _():
        c1_ref[0] = c_sc[...]
        n1_ref[0] = n_sc[...]
        m1_ref[0] = m_sc[...]


def _mlstm(m_slab, gates_c, gates_r, bias_c, bias_r, gout, c0, n0, m0, *, tc, heads, dk, dv):
    b, t, mw = m_slab.shape
    chunk = min(CHUNK, t)
    nchunk = tc // chunk
    kern = functools.partial(_mlstm_kernel, heads=heads, dk=dk, dv=dv, chunk=chunk, nchunk=nchunk)
    st = lambda *s: pl.BlockSpec((1,) + s, lambda i, j: (i,) + (0,) * len(s))
    const = lambda a: pl.BlockSpec(a.shape, lambda i, j: (0,) * a.ndim)
    return pl.pallas_call(
        kern,
        out_shape=[jax.ShapeDtypeStruct((b, t, heads * dv), F32),
                   jax.ShapeDtypeStruct((b, heads, dk, dv), F32),
                   jax.ShapeDtypeStruct((b, heads, 1, dk), F32),
                   jax.ShapeDtypeStruct((b, heads, 1, 1), F32)],
        grid=(b, t // tc),
        in_specs=[pl.BlockSpec((1, tc, mw), lambda i, j: (i, j, 0)),
                  pl.BlockSpec((1, tc, LANES), lambda i, j: (i, j, 0)),
                  pl.BlockSpec((1, nchunk, SUBLANES, chunk), lambda i, j: (i, j, 0, 0)),
                  const(bias_c), const(bias_r), const(gout),
                  st(heads, dk, dv), st(heads, 1, dk), st(heads, 1, 1)],
        out_specs=[pl.BlockSpec((1, tc, heads * dv), lambda i, j: (i, j, 0)),
                   st(heads, dk, dv), st(heads, 1, dk), st(heads, 1, 1)],
        scratch_shapes=[pltpu.VMEM((heads, dk, dv), F32), pltpu.VMEM((heads, 1, dk), F32),
                        pltpu.VMEM((heads, 1, 1), F32)],
        compiler_params=_params("arbitrary", "arbitrary"),
        name="mlstm_scan",
    )(m_slab, gates_c, gates_r, bias_c, bias_r, gout, c0, n0, m0)


def _mid_kernel(oa_ref, ob_ref, z_ref, x_ref, gate_ref, wo_ref, shift_ref, scale_ref, g_ref, w2_ref,
                x1_ref, qkv_ref, z2_ref, ab_ref, *, half, conv_ch, d_model):
    z = z_ref[0]
    ma = (oa_ref[0] * _silu(z[:, :half])).astype(BF16)
    mb = (ob_ref[0] * _silu(z[:, half:])).astype(BF16)
    y = _dot(ma, wo_ref[0:half, :]) + _dot(mb, wo_ref[half:, :])
    x1 = x_ref[0] + gate_ref[0] * y
    x1_ref[0] = x1
    hn = _rms(x1, g_ref[...]) * (1.0 + scale_ref[0]) + shift_ref[0]
    y2 = _dot(hn.astype(BF16), w2_ref[...])
    qkv_ref[0] = y2[:, :conv_ch]
    z2_ref[0] = y2[:, conv_ch:conv_ch + d_model]
    ab_ref[0] = y2[:, conv_ch + d_model:]


def _mid(oa, ob, z, x, gate, wo, shift, scale, g, w2, *, tm, conv_ch):
    b, t, d = x.shape
    half = oa.shape[-1]
    tok = lambda last: pl.BlockSpec((1, tm, last), lambda i, j: (i, j, 0))
    vec = pl.BlockSpec((1, 1, d), lambda i, j: (i, 0, 0))
    const = lambda a: pl.BlockSpec(a.shape, lambda i, j: (0,) * a.ndim)
    kern = functools.partial(_mid_kernel, half=half, conv_ch=conv_ch, d_model=d)
    return pl.pallas_call(
        kern,
        out_shape=[jax.ShapeDtypeStruct((b, t, d), F32), jax.ShapeDtypeStruct((b, t, conv_ch), F32),
                   jax.ShapeDtypeStruct((b, t, d), F32), jax.ShapeDtypeStruct((b, t, LANES), F32)],
        grid=(b, t // tm),
        in_specs=[tok(half), tok(ob.shape[-1]), tok(d), tok(d), vec, const(wo), vec, vec, const(g), const(w2)],
        out_specs=[tok(d), tok(conv_ch), tok(d), tok(LANES)],
        compiler_params=_params("arbitrary", "arbitrary"),
        name="out_a_in_c",
    )(oa, ob, z, x, gate, wo, shift, scale, g, w2)


def _conv_kernel(qkv_ref, past_ref, wc_ref, ab_ref, alog_ref, dtb_ref, act_ref, gb_ref, ext_sc, *,
                 tm, width, heads, dk):
    @pl.when(pl.program_id(1) == 0)
    def _():
        ext_sc[0:SUBLANES, :] = past_ref[0]

    ext_sc[SUBLANES:SUBLANES + tm, :] = qkv_ref[0]
    conv = wc_ref[width - 1:width, :] * ext_sc[SUBLANES:SUBLANES + tm, :]
    for j in range(width - 1):
        s = SUBLANES - (width - 1) + j
        conv = conv + wc_ref[j:j + 1, :] * ext_sc[s:s + tm, :]
    ext_sc[0:SUBLANES, :] = ext_sc[tm:tm + SUBLANES, :]
    act = _silu(conv)
    for h in range(2 * heads):
        xh = act[:, h * dk:(h + 1) * dk]
        xh = xh * lax.rsqrt(jnp.sum(xh * xh, axis=-1, keepdims=True) + EPS)
        if h < heads:
            xh = xh * (dk ** -0.5)
        act_ref[0, :, h * dk:(h + 1) * dk] = xh
    act_ref[0, :, 2 * heads * dk:] = act[:, 2 * heads * dk:]
    ab = ab_ref[0]
    g = -jnp.exp(alog_ref[...]) * _softplus(ab + dtb_ref[...])
    lane = lax.broadcasted_iota(jnp.int32, ab.shape, 1)
    gb_ref[0] = jnp.where(lane < heads, g, _sigmoid(ab))


def _conv(qkv, past8, wc8, ab, alog, dtb, *, tm, width, heads, dk):
    b, t, ch = qkv.shape
    tok = lambda last: pl.BlockSpec((1, tm, last), lambda i, j: (i, j, 0))
    const = lambda a: pl.BlockSpec(a.shape, lambda i, j: (0,) * a.ndim)
    kern = functools.partial(_conv_kernel, tm=tm, width=width, heads=heads, dk=dk)
    return pl.pallas_call(
        kern,
        out_shape=[jax.ShapeDtypeStruct((b, t, ch), F32), jax.ShapeDtypeStruct((b, t, LANES), F32)],
        grid=(b, t // tm),
        in_specs=[tok(ch), pl.BlockSpec((1, SUBLANES, ch), lambda i, j: (i, 0, 0)), const(wc8), tok(LANES),
                  const(alog), const(dtb)],
        out_specs=[tok(ch), tok(LANES)],
        scratch_shapes=[pltpu.VMEM((tm + SUBLANES, ch), F32)],
        compiler_params=_params("arbitrary", "arbitrary"),
        name="conv_gates",
    )(qkv, past8, wc8, ab, alog, dtb)


def _blockdiag(x, group, chunk):
    w = group * chunk
    br = lax.broadcasted_iota(jnp.int32, (w, w), 0) // chunk
    bc = lax.broadcasted_iota(jnp.int32, (w, w), 1) // chunk
    xb = x.astype(BF16)
    return jnp.where(br == bc, jnp.concatenate([xb] * group, axis=0), jnp.zeros((), BF16))


def _unit_lower_inverses_minus_eye(a_list, group, chunk):
    w = group * chunk
    r = lax.broadcasted_iota(jnp.int32, (chunk, w), 0)
    cc = lax.broadcasted_iota(jnp.int32, (chunk, w), 1) % chunk
    es = [-jnp.where((r // 2 == cc // 2) & (r % 2 == 1) & (cc % 2 == 0), a4, 0.0) for a4 in a_list]
    s = 2
    while s < chunk:
        off = (r // (2 * s) == cc // (2 * s)) & (r % (2 * s) >= s) & (cc % (2 * s) < s)
        a_offs = [jnp.where(off, a4, 0.0) for a4 in a_list]
        ps = [a + _dot(a.astype(BF16), _blockdiag(e, group, chunk)) for a, e in zip(a_offs, es)]
        es = [e - (p + _dot(e.astype(BF16), _blockdiag(p, group, chunk))) for e, p in zip(es, ps)]
        s *= 2
    return es


def _gdn_prep_kernel(act_ref, gbc_ref, gbr_ref, w_ref, uv_ref, kd_ref, attn_ref, eg_ref, *,
                     heads, dk, dv, chunk, nchunk, group):
    row = lax.broadcasted_iota(jnp.int32, (chunk, chunk), 0)
    col = lax.broadcasted_iota(jnp.int32, (chunk, chunk), 1)
    incl = col <= row
    strict = col < row
    tril = incl.astype(F32)
    triu = (row <= col).astype(F32)
    lane = lax.broadcasted_iota(jnp.int32, (chunk, LANES), 1)
    o_k, o_v = heads * dk, 2 * heads * dk

    problems, a_list, rhs_list = [], [], []
    for c in range(nchunk):
        rows = slice(c * chunk, (c + 1) * chunk)
        gbc = gbc_ref[0, rows, :]
        gbr = gbr_ref[0, c]
        gcum_c = _dot(tril, gbc, HIGHEST)
        gcum_r = _dot(gbr, triu, HIGHEST)
        eg_ref[0, rows, :] = jnp.where(lane < heads, jnp.exp(gcum_c), 0.0)
        for g0 in range(0, heads, group):
            a_blocks, rhs_blocks = [], []
            for h in range(g0, g0 + group):
                q = act_ref[0, rows, h * dk:(h + 1) * dk].astype(BF16)
                k = act_ref[0, rows, o_k + h * dk:o_k + (h + 1) * dk]
                v = act_ref[0, rows, o_v + h * dv:o_v + (h + 1) * dv]
                kb = k.astype(BF16)
                g_c, g_r = gcum_c[:, h:h + 1], gcum_r[h:h + 1, :]
                beta = gbc[:, heads + h:heads + h + 1]
                decay = jnp.exp(jnp.where(incl, g_c - g_r, NEG_INF))
                a_blocks.append(jnp.where(strict, beta * _dot_nt(kb, kb) * decay, 0.0))
                rhs_blocks.append(jnp.concatenate([beta * v, (beta * jnp.exp(g_c)) * k], axis=1))
                attn_ref[0, rows, h * chunk:(h + 1) * chunk] = (_dot_nt(q, kb) * decay).astype(BF16)
                kd_ref[0, rows, h * dk:(h + 1) * dk] = (k * jnp.exp(g_c[chunk - 1:chunk, :] - g_c)).astype(BF16)
            problems.append((rows, g0))
            a_list.append(jnp.concatenate(a_blocks, axis=1))
            rhs_list.append(jnp.concatenate(rhs_blocks, axis=0))
    e_list = _unit_lower_inverses_minus_eye(a_list, group, chunk)
    for (rows, g0), e, rhs in zip(problems, e_list, rhs_list):
        sol = rhs + _dot(_blockdiag(e, group, chunk), rhs.astype(BF16))
        for i, h in enumerate(range(g0, g0 + group)):
            uv_ref[0, rows, h * dv:(h + 1) * dv] = sol[i * chunk:(i + 1) * chunk, :dv]
            w_ref[0, rows, h * dk:(h + 1) * dk] = sol[i * chunk:(i + 1) * chunk, dv:].astype(BF16)


def _gdn_prep(act, gb_c, gb_r, *, tc, heads, dk, dv):
    b, t, _ = act.shape
    chunk = min(CHUNK, t)
    tc = min(tc, t)
    nchunk = tc // chunk
    group = (2 * LANES) // chunk
    kern = functools.partial(_gdn_prep_kernel, heads=heads, dk=dk, dv=dv, chunk=chunk, nchunk=nchunk, group=group)
    tok = lambda last: pl.BlockSpec((1, tc, last), lambda i, j: (i, j, 0))
    return pl.pallas_call(
        kern,
        out_shape=[jax.ShapeDtypeStruct((b, t, heads * dk), BF16), jax.ShapeDtypeStruct((b, t, heads * dv), F32),
                   jax.ShapeDtypeStruct((b, t, heads * dk), BF16), jax.ShapeDtypeStruct((b, t, heads * chunk), BF16),
                   jax.ShapeDtypeStruct((b, t, LANES), F32)],
        grid=(b, t // tc),
        in_specs=[tok(act.shape[-1]), tok(LANES),
                  pl.BlockSpec((1, nchunk, 2 * SUBLANES, chunk), lambda i, j: (i, j, 0, 0))],
        out_specs=[tok(heads * dk), tok(heads * dv), tok(heads * dk), tok(heads * chunk), tok(LANES)],
        compiler_params=_params("arbitrary", "arbitrary"),
        name="gdn_prep",
    )(act, gb_c, gb_r)


def _gdn_scan_kernel(q_ref, w_ref, uv_ref, kd_ref, attn_ref, eg_ref, gout_ref, s0_ref, o_ref, s1_ref, s_sc, *,
                     heads, dk, dv, chunk, nchunk):
    t = pl.program_id(1)

    @pl.when(t == 0)
    def _():
        s_sc[...] = s0_ref[0]

    def body(c, carry):
        rows = pl.ds(pl.multiple_of(c * chunk, chunk), chunk)
        eg = eg_ref[0, rows, :]
        for h in range(heads):
            s0 = s_sc[h]
            s0b = s0.astype(BF16)
            q = q_ref[0, rows, h * dk:(h + 1) * dk].astype(BF16)
            u = uv_ref[0, rows, h * dv:(h + 1) * dv] - _dot(w_ref[0, rows, h * dk:(h + 1) * dk], s0b)
            ub = u.astype(BF16)
            eg_h = eg[:, h:h + 1]
            o = eg_h * _dot(q, s0b) + _dot(attn_ref[0, rows, h * chunk:(h + 1) * chunk], ub)
            s_sc[h] = eg_h[chunk - 1:chunk, :] * s0 + _dot_tn(kd_ref[0, rows, h * dk:(h + 1) * dk], ub)
            o_ref[0, rows, h * dv:(h + 1) * dv] = _rms(o, gout_ref[...])
        return carry

    lax.fori_loop(0, nchunk, body, 0)

    @pl.when(t == pl.num_programs(1) - 1)
    def _():
        s1_ref[0] = s_sc[...]


def _gdn_scan(act, w, uv, kd, attn, eg, gout, s0, *, tc, heads, dk, dv):
    b, t, _ = act.shape
    chunk = min(CHUNK, t)
    nchunk = tc // chunk
    kern = functools.partial(_gdn_scan_kernel, heads=heads, dk=dk, dv=dv, chunk=chunk, nchunk=nchunk)
    tok = lambda last: pl.BlockSpec((1, tc, last), lambda i, j: (i, j, 0))
    state = pl.BlockSpec((1, heads, dk, dv), lambda i, j: (i, 0, 0, 0))
    return pl.pallas_call(
        kern,
        out_shape=[jax.ShapeDtypeStruct((b, t, heads * dv), F32), jax.ShapeDtypeStruct(s0.shape, F32)],
        grid=(b, t // tc),
        in_specs=[tok(heads * dk),
                  tok(heads * dk), tok(heads * dv), tok(heads * dk), tok(heads * chunk), tok(LANES),
                  pl.BlockSpec(gout.shape, lambda i, j: (0, 0)), state],
        out_specs=[tok(heads * dv), state],
        scratch_shapes=[pltpu.VMEM((heads, dk, dv), F32)],
        compiler_params=_params("arbitrary", "arbitrary"),
        name="gdn_scan",
    )(act, w, uv, kd, attn, eg, gout, s0)


def _final_kernel(o_ref, z_ref, x_ref, gate_ref, wo_ref, g_ref, y_ref):
    mixed = (o_ref[0] * _silu(z_ref[0])).astype(BF16)
    x2 = x_ref[0] + gate_ref[0] * _dot(mixed, wo_ref[...])
    y_ref[0] = _rms(x2, g_ref[...])


def _final(o, z, x, gate, wo, g, *, tm):
    b, t, d = x.shape
    tok = lambda last: pl.BlockSpec((1, tm, last), lambda i, j: (i, j, 0))
    const = lambda a: pl.BlockSpec(a.shape, lambda i, j: (0,) * a.ndim)
    return pl.pallas_call(
        _final_kernel,
        out_shape=jax.ShapeDtypeStruct((b, t, d), F32),
        grid=(b, t // tm),
        in_specs=[tok(o.shape[-1]), tok(d), tok(d), pl.BlockSpec((1, 1, d), lambda i, j: (i, 0, 0)),
                  const(wo), const(g)],
        out_specs=tok(d),
        compiler_params=_params("arbitrary", "arbitrary"),
        name="out_c_final",
    )(o, z, x, gate, wo, g)


def _pad_lanes(w, width=LANES, at=0):
    out = jnp.zeros(w.shape[:-1] + (width,), w.dtype)
    return out.at[..., at:at + w.shape[-1]].set(w)


def _rot_half_cols(w):
    r = w.shape[-1] // 2
    return jnp.concatenate([-w[..., r:], w[..., :r]], axis=-1)


def _rope_tables(pos, rope, scale):
    freqs = jnp.exp(jnp.arange(0, rope, 2, dtype=F32) * (-math.log(ROPE_BASE) / rope))
    ang = pos.astype(F32)[:, None] * freqs[None, :]
    cos = jnp.concatenate([jnp.cos(ang), jnp.cos(ang)], axis=-1)
    sin = jnp.concatenate([jnp.sin(ang), jnp.sin(ang)], axis=-1)
    half = LANES // 2
    cosk = _pad_lanes(cos, at=half)
    sink = _pad_lanes(sin, at=half)
    ones = _pad_lanes(jnp.ones((pos.shape[0], half), F32))
    return jnp.concatenate([(cosk + ones) * scale, sink * scale, cosk, sink], axis=-1)


def _tokens_on_lanes(a, chunk, rows):
    b, t = a.shape[:2]
    return a[..., :rows].reshape(b, t // chunk, chunk, rows).transpose(0, 1, 3, 2)


def kernel(x_prompt, x_sample, c_prompt, c_sample, cache_kv_latent, cache_k_rope, state_mlstm_C, state_mlstm_n, state_mlstm_m, state_gdn_S, state_gdn_conv, a_w_ada, a_b_ada, a_g_norm, a_w_in, a_g_q_a, a_w_q_b, a_g_kv_a, a_w_kv_b, a_b_i, a_b_f, a_g_out, a_w_out, c_w_ada, c_b_ada, c_g_norm, c_w_in, c_w_conv, c_a_log, c_dt_bias, c_g_out, c_w_out, g_final):
    d = x_prompt.shape[-1]
    q_lora, heads, qk = a_w_q_b.shape
    kv_lora = a_w_kv_b.shape[0]
    rope = cache_k_rope.shape[-1]
    nope = qk - rope
    v_head = a_w_kv_b.shape[2] - nope
    m_heads, m_dv = a_g_out.shape
    m_dk = state_mlstm_C.shape[2]
    g_heads = c_a_log.shape[0]
    g_dk, g_dv = state_gdn_S.shape[2:]
    width = c_w_conv.shape[0]
    conv_ch = c_w_conv.shape[1]
    assert nope + rope <= LANES and nope == LANES // 2 and 2 * m_heads <= SUBLANES and 2 * g_heads <= 2 * SUBLANES

    sizes = (q_lora, kv_lora, rope, m_heads * m_dk, m_heads * m_dk, m_heads * m_dv, m_heads, m_heads,
             m_heads * m_dv, heads * v_head + m_heads * m_dv)
    offs = [0]
    for s in sizes:
        offs.append(offs[-1] + s)
    w_qa, w_c, w_kr, w_mq, w_mk, w_mv, w_mi, w_mf, w_mo, w_z = [a_w_in[:, offs[i]:offs[i + 1]] for i in range(10)]
    half = LANES // 2
    w1 = jnp.concatenate([w_qa, w_c, _pad_lanes(w_kr, at=half), _pad_lanes(_rot_half_cols(w_kr), at=half),
                          w_mq, w_mk, w_mv, w_mo, w_z, _pad_lanes(jnp.concatenate([w_mi, w_mf], axis=1))],
                         axis=1).astype(BF16)
    m_width = 2 * m_heads * m_dk + 2 * m_heads * m_dv
    wq_rope = a_w_q_b[..., nope:]
    wq_main = _pad_lanes(a_w_q_b).reshape(q_lora, heads * LANES)
    wq_rot = _pad_lanes(_rot_half_cols(wq_rope), at=nope).reshape(q_lora, heads * LANES)
    wq = jnp.concatenate([wq_main, wq_rot], axis=1).astype(BF16)
    wkv = jnp.concatenate([_pad_lanes(a_w_kv_b[..., :nope]).reshape(kv_lora, heads * LANES),
                           a_w_kv_b[..., nope:].reshape(kv_lora, heads * v_head)], axis=1).astype(BF16)
    wk_abs = a_w_kv_b[..., :nope].transpose(1, 0, 2).astype(BF16)
    wv_abs = a_w_kv_b[..., nope:].transpose(1, 0, 2).astype(BF16)
    wo_a = a_w_out.astype(BF16)
    csz = (conv_ch, g_heads, g_heads, g_heads * g_dv)
    w_qkv, w_a, w_b, w_zc = [c_w_in[:, sum(csz[:i]):sum(csz[:i + 1])] for i in range(4)]
    w2 = jnp.concatenate([w_qkv, w_zc, _pad_lanes(jnp.concatenate([w_a, w_b], axis=1))], axis=1).astype(BF16)
    wo_c = c_w_out.astype(BF16)
    wc8 = jnp.zeros((SUBLANES, conv_ch), F32).at[:width].set(c_w_conv)
    row = lambda a: a.reshape(1, -1).astype(F32)
    bias_c = _pad_lanes(jnp.concatenate([a_b_i, a_b_f]).reshape(1, -1))
    bias_r = jnp.zeros((SUBLANES, 1), F32).at[:2 * m_heads, 0].set(jnp.concatenate([a_b_i, a_b_f]))
    alog = _pad_lanes(c_a_log.reshape(1, -1))
    dtb = _pad_lanes(c_dt_bias.reshape(1, -1))

    bp, bs = c_prompt.shape[0], c_sample.shape[0]
    c_all = jnp.concatenate([c_prompt, c_sample], axis=0)
    pad = (-c_all.shape[0]) % SUBLANES
    c_all = jnp.pad(c_all, ((0, pad), (0, 0)))
    mod_a = _adaln(c_all, a_w_ada, a_b_ada)
    mod_c = _adaln(c_all, c_w_ada, c_b_ada)

    def mods(mod, lo, hi):
        return [mod[lo:hi, i * d:(i + 1) * d][:, None, :] for i in range(3)]

    def run(x, mod_lo, mod_hi, c_past, kr_past, c0, n0, m0, conv0, s0):
        b, t, _ = x.shape
        past = 0 if c_past is None else c_past.shape[1]
        chunk = min(CHUNK, t)
        tm = min(t, 256)
        tc = min(t, 512)
        shift_a, scale_a, gate_a = mods(mod_a, mod_lo, mod_hi)
        shift_c, scale_c, gate_c = mods(mod_c, mod_lo, mod_hi)
        tab = _rope_tables(past + jnp.arange(t, dtype=jnp.int32), rope, qk ** -0.5)
        expand = c_past is None
        outs = _in_a(x, shift_a, scale_a, row(a_g_norm), w1, row(a_g_q_a), wq, row(a_g_kv_a), wkv, tab,
                     tm=tm, heads=heads, q_lora=q_lora, kv_lora=kv_lora, rope=rope, m_width=m_width,
                     v_head=v_head, expand_kv=expand)
        q, c_new, kr_new, m_slab, z, gates = outs[:6]
        if expand:
            o_mla = _flash(q, outs[6], outs[7].transpose(0, 2, 1), heads=heads, v_head=v_head, tq=min(t, 512))
        else:
            o_mla = _latent_attn(q, c_past, kr_past, c_new, kr_new, wk_abs, wv_abs,
                                 heads=heads, nope=nope, rope=rope, v_head=v_head)
        hm, c1, n1, m1 = _mlstm(m_slab, gates, _tokens_on_lanes(gates, chunk, SUBLANES), bias_c, bias_r,
                                row(a_g_out), c0, n0.reshape(b, m_heads, 1, m_dk), m0.reshape(b, m_heads, 1, 1),
                                tc=tc, heads=m_heads, dk=m_dk, dv=m_dv)
        x1, qkv, zc, ab = _mid(o_mla, hm, z, x, gate_a, wo_a, shift_c, scale_c, row(c_g_norm), w2,
                               tm=tm, conv_ch=conv_ch)
        past8 = jnp.pad(conv0, ((0, 0), (SUBLANES - (width - 1), 0), (0, 0)))
        act, gb = _conv(qkv, past8, wc8, ab, alog, dtb, tm=tm, width=width, heads=g_heads, dk=g_dk)
        w, uv, kd, attn, eg = _gdn_prep(act, gb, _tokens_on_lanes(gb, chunk, 2 * SUBLANES),
                                        tc=GDN_PREP_TC, heads=g_heads, dk=g_dk, dv=g_dv)
        o_gdn, s1 = _gdn_scan(act, w, uv, kd, attn, eg, row(c_g_out), s0, tc=tc, heads=g_heads, dk=g_dk, dv=g_dv)
        y = _final(o_gdn, zc, x1, gate_c, wo_c, row(g_final), tm=tm)
        conv1 = jnp.concatenate([conv0, qkv], axis=1)[:, t:] if t < width - 1 else qkv[:, t - (width - 1):]
        return (y, c_new, kr_new, c1, n1.reshape(b, m_heads, m_dk), m1.reshape(b, m_heads), conv1, s1)

    dt = x_prompt.dtype
    (y_p, p_kv, p_kr, p_c, p_n, p_m, p_conv, p_s) = run(
        x_prompt, 0, bp, None, None,
        jnp.zeros((bp, m_heads, m_dk, m_dv), dt), jnp.zeros((bp, m_heads, m_dk), dt), jnp.zeros((bp, m_heads), dt),
        jnp.zeros((bp, width - 1, conv_ch), dt), jnp.zeros((bp, g_heads, g_dk, g_dv), dt))
    (y_s, s_kv, s_kr, s_c, s_n, s_m, s_conv, s_s) = run(
        x_sample, bp, bp + bs, cache_kv_latent, cache_k_rope, state_mlstm_C, state_mlstm_n, state_mlstm_m,
        state_gdn_conv, state_gdn_S)
    return (y_p, y_s, p_kv, p_kr, p_c, p_n, p_m, p_s, p_conv,
            s_kv, s_kr, s_c, s_n, s_m, s_s, s_conv)
```

```python
import functools
import math

import jax
import jax.numpy as jnp
from jax import lax
from jax.experimental import pallas as pl
from jax.experimental.pallas import tpu as pltpu

F32 = jnp.float32
BF16 = jnp.bfloat16
HIGHEST = lax.Precision.HIGHEST

CHUNK = 64
EPS = 1e-6
ROPE_BASE = 10000.0
LANES = 128
SUBLANES = 8
VMEM_LIMIT = 56 * 1024 * 1024
NEG_INF = float("-inf")
FLASH_TQ = 512
FLASH_TK = 512
MLSTM_PREP_TC = 256
GDN_PREP_TC = 128


def _params(*sem):
    return pltpu.CompilerParams(dimension_semantics=sem, vmem_limit_bytes=VMEM_LIMIT)


def _dot(a, b, precision=None):
    return jnp.dot(a, b, preferred_element_type=F32, precision=precision)


def _dot_nt(a, b):
    return lax.dot_general(a, b, (((1,), (1,)), ((), ())), preferred_element_type=F32)


def _dot_tn(a, b):
    return lax.dot_general(a, b, (((0,), (0,)), ((), ())), preferred_element_type=F32)


def _rms(x, g):
    return x * lax.rsqrt(jnp.mean(x * x, axis=-1, keepdims=True) + EPS) * g


def _sigmoid(x):
    return 1.0 / (1.0 + jnp.exp(-x))


def _silu(x):
    return x * _sigmoid(x)


def _softplus(x):
    return jnp.maximum(x, 0.0) + jnp.log1p(jnp.exp(-jnp.abs(x)))


def _log_sigmoid(x):
    return -_softplus(-x)


def _adaln_kernel(c_ref, w_ref, b_ref, o_ref):
    o_ref[...] = _dot(_silu(c_ref[...]), w_ref[...], HIGHEST) + b_ref[...]


def _adaln(c, w, b):
    n, d = c.shape
    d3 = w.shape[1]
    return pl.pallas_call(
        _adaln_kernel,
        out_shape=jax.ShapeDtypeStruct((n, d3), F32),
        grid=(d3 // d,),
        in_specs=[pl.BlockSpec((n, d), lambda j: (0, 0)),
                  pl.BlockSpec((d, d), lambda j: (0, j)),
                  pl.BlockSpec((1, d), lambda j: (0, j))],
        out_specs=pl.BlockSpec((n, d), lambda j: (0, j)),
        compiler_params=_params("arbitrary"),
        name="adaln",
    )(c, w, b.reshape(1, d3))


def _in_a_kernel(x_ref, shift_ref, scale_ref, g_ref, w1_ref, gq_ref, wq_ref, gkv_ref, wkv_ref, tab_ref,
                 q_ref, c_ref, kr_ref, m_ref, z_ref, gt_ref, *kv_refs,
                 heads, q_lora, kv_lora, rope, m_width, d_model, v_head, expand_kv):
    x = x_ref[0]
    hn = _rms(x, g_ref[...]) * (1.0 + scale_ref[0]) + shift_ref[0]
    y = _dot(hn.astype(BF16), w1_ref[...])
    o = 0
    qa = y[:, o:o + q_lora]; o += q_lora
    cl = y[:, o:o + kv_lora]; o += kv_lora
    kr1 = y[:, o:o + LANES]; o += LANES
    kr2 = y[:, o:o + LANES]; o += LANES
    m_ref[0] = y[:, o:o + m_width]; o += m_width
    z_ref[0] = y[:, o:o + d_model]; o += d_model
    gt_ref[0] = y[:, o:o + LANES]

    tab = tab_ref[...]
    cosq, sinq = tab[:, 0:LANES], tab[:, LANES:2 * LANES]
    cosk, sink = tab[:, 2 * LANES:3 * LANES], tab[:, 3 * LANES:4 * LANES]

    qq = _dot(_rms(qa, gq_ref[...]).astype(BF16), wq_ref[...])
    hw = heads * LANES
    for h in range(heads):
        sl = slice(h * LANES, (h + 1) * LANES)
        q_ref[0, :, sl] = (qq[:, sl] * cosq + qq[:, hw + h * LANES:hw + (h + 1) * LANES] * sinq).astype(BF16)

    cn = _rms(cl, gkv_ref[...])
    c_ref[0] = cn
    kr = kr1 * cosk + kr2 * sink
    kr_ref[0] = kr[:, LANES // 2:LANES // 2 + rope]
    if expand_kv:
        k_ref, v_ref = kv_refs
        kv = _dot(cn.astype(BF16), wkv_ref[...])
        for h in range(heads):
            sl = slice(h * LANES, (h + 1) * LANES)
            k_ref[0, :, sl] = (kv[:, sl] + kr).astype(BF16)
        v_ref[0] = kv[:, hw:hw + heads * v_head].astype(BF16)


def _in_a(x, shift, scale, g, w1, gq, wq, gkv, wkv, tab, *, tm, heads, q_lora, kv_lora, rope, m_width,
          v_head, expand_kv):
    b, t, d = x.shape
    n1 = w1.shape[1]
    grid = (b, t // tm)
    tok = lambda last: pl.BlockSpec((1, tm, last), lambda i, j: (i, j, 0))
    const = lambda a: pl.BlockSpec(a.shape, lambda i, j: (0,) * a.ndim)
    out_shape = [jax.ShapeDtypeStruct((b, t, heads * LANES), BF16),
                 jax.ShapeDtypeStruct((b, t, kv_lora), F32),
                 jax.ShapeDtypeStruct((b, t, rope), F32),
                 jax.ShapeDtypeStruct((b, t, m_width), F32),
                 jax.ShapeDtypeStruct((b, t, d), F32),
                 jax.ShapeDtypeStruct((b, t, LANES), F32)]
    out_specs = [tok(heads * LANES), tok(kv_lora), tok(rope), tok(m_width), tok(d), tok(LANES)]
    if expand_kv:
        out_shape += [jax.ShapeDtypeStruct((b, t, heads * LANES), BF16),
                      jax.ShapeDtypeStruct((b, t, heads * v_head), BF16)]
        out_specs += [tok(heads * LANES), tok(heads * v_head)]
    kern = functools.partial(_in_a_kernel, heads=heads, q_lora=q_lora, kv_lora=kv_lora, rope=rope,
                             m_width=m_width, d_model=d, v_head=v_head, expand_kv=expand_kv)
    return pl.pallas_call(
        kern, out_shape=out_shape, grid=grid,
        in_specs=[tok(d),
                  pl.BlockSpec((1, 1, d), lambda i, j: (i, 0, 0)),
                  pl.BlockSpec((1, 1, d), lambda i, j: (i, 0, 0)),
                  const(g), const(w1), const(gq), const(wq), const(gkv), const(wkv),
                  pl.BlockSpec((tm, 4 * LANES), lambda i, j: (j, 0))],
        out_specs=out_specs,
        compiler_params=_params("arbitrary", "arbitrary"),
        name="in_proj_a",
    )(x, shift, scale, g, w1, gq, wq, gkv, wkv, tab)


def _flash_kernel(qi_ref, ki_ref, q_ref, k_ref, vt_ref, o_ref, m_sc, l_sc, acc_sc, *,
                  heads, v_head, tq, tk, chunk):
    step_id = pl.program_id(1)
    qi = qi_ref[step_id]
    ki = ki_ref[step_id]

    @pl.when(ki == 0)
    def _():
        m_sc[...] = jnp.full(m_sc.shape, NEG_INF, F32)
        l_sc[...] = jnp.zeros(l_sc.shape, F32)
        acc_sc[...] = jnp.zeros(acc_sc.shape, F32)

    def step(masked):
        if masked:
            kc = (ki * tk + lax.broadcasted_iota(jnp.int32, (tk, tq), 0)) // chunk
            qc = (qi * tq + lax.broadcasted_iota(jnp.int32, (tk, tq), 1)) // chunk
            mask = kc <= qc
        for h in range(heads):
            qh = q_ref[0, :, h * LANES:(h + 1) * LANES]
            kh = k_ref[0, :, h * LANES:(h + 1) * LANES]
            vth = vt_ref[0, h * v_head:(h + 1) * v_head, :]
            rows = slice(h * v_head, (h + 1) * v_head)
            st = _dot_nt(kh, qh)
            if masked:
                st = jnp.where(mask, st, NEG_INF)
            m_prev = m_sc[h]
            m_new = jnp.maximum(m_prev, jnp.max(st, axis=0, keepdims=True))
            alpha = jnp.exp2(m_prev - m_new)
            p = jnp.exp2(st - m_new)
            l_sc[h] = alpha * l_sc[h] + jnp.sum(p, axis=0, keepdims=True)
            acc_sc[rows, :] = alpha * acc_sc[rows, :] + _dot(vth, p.astype(BF16))
            m_sc[h] = m_new

    full = (ki + 1) * tk <= qi * tq + chunk

    @pl.when(full)
    def _():
        step(False)

    @pl.when(jnp.logical_not(full))
    def _():
        step(True)

    @pl.when(ki == ((qi + 1) * tq - 1) // tk)
    def _():
        for h in range(heads):
            rows = slice(h * v_head, (h + 1) * v_head)
            acc_sc[rows, :] = acc_sc[rows, :] / l_sc[h]
        o_ref[0] = acc_sc[...].T


def _flash(q, k, vt, *, heads, v_head, tq, tk):
    b, t, _ = q.shape
    pairs = [(i, j) for i in range(t // tq) for j in range(((i + 1) * tq - 1) // tk + 1)]
    qi_tab = jnp.asarray([p[0] for p in pairs], jnp.int32)
    ki_tab = jnp.asarray([p[1] for p in pairs], jnp.int32)
    kern = functools.partial(_flash_kernel, heads=heads, v_head=v_head, tq=tq, tk=tk, chunk=CHUNK)
    grid_spec = pltpu.PrefetchScalarGridSpec(
        num_scalar_prefetch=2,
        grid=(b, len(pairs)),
        in_specs=[pl.BlockSpec((1, tq, heads * LANES), lambda i, s, qt, kt: (i, qt[s], 0)),
                  pl.BlockSpec((1, tk, heads * LANES), lambda i, s, qt, kt: (i, kt[s], 0)),
                  pl.BlockSpec((1, heads * v_head, tk), lambda i, s, qt, kt: (i, 0, kt[s]))],
        out_specs=pl.BlockSpec((1, tq, heads * v_head), lambda i, s, qt, kt: (i, qt[s], 0)),
        scratch_shapes=[pltpu.VMEM((heads, 1, tq), F32), pltpu.VMEM((heads, 1, tq), F32),
                        pltpu.VMEM((heads * v_head, tq), F32)])
    return pl.pallas_call(
        kern,
        out_shape=jax.ShapeDtypeStruct((b, t, heads * v_head), F32),
        grid_spec=grid_spec,
        compiler_params=_params("arbitrary", "arbitrary"),
        name="flash_attn",
    )(qi_tab, ki_tab, q, k, vt)


def _latent_attn_kernel(q_ref, cp_ref, krp_ref, cn_ref, krn_ref, wk_ref, wv_ref, o_ref, *,
                        heads, nope, rope, v_head):
    q = q_ref[0]
    qabs, qrope = [], []
    for h in range(heads):
        qabs.append(_dot_nt(q[:, h * LANES:h * LANES + nope], wk_ref[h]))
        qrope.append(q[:, h * LANES + nope:h * LANES + nope + rope])
    qabs = jnp.concatenate(qabs, axis=0).astype(BF16)
    qrope = jnp.concatenate(qrope, axis=0)
    cp = cp_ref[0].astype(BF16)
    cn = cn_ref[0].astype(BF16)
    s_p = _dot_nt(qabs, cp) + _dot_nt(qrope, krp_ref[0].astype(BF16))
    s_n = _dot_nt(qabs, cn) + _dot_nt(qrope, krn_ref[0].astype(BF16))
    m = jnp.maximum(jnp.max(s_p, axis=-1, keepdims=True), jnp.max(s_n, axis=-1, keepdims=True))
    p_p = jnp.exp2(s_p - m)
    p_n = jnp.exp2(s_n - m)
    l = jnp.sum(p_p, axis=-1, keepdims=True) + jnp.sum(p_n, axis=-1, keepdims=True)
    o_lat = (_dot(p_p.astype(BF16), cp) + _dot(p_n.astype(BF16), cn)) / l
    t = q.shape[0]
    for h in range(heads):
        o_ref[0, :, h * v_head:(h + 1) * v_head] = _dot(o_lat[h * t:(h + 1) * t].astype(BF16), wv_ref[h])


def _latent_attn(q, c_past, kr_past, c_new, kr_new, wk, wv, *, heads, nope, rope, v_head):
    b, t, _ = q.shape
    past, kv_lora = c_past.shape[1:]
    blk = lambda n, last: pl.BlockSpec((1, n, last), lambda i: (i, 0, 0))
    const = lambda a: pl.BlockSpec(a.shape, lambda i: (0,) * a.ndim)
    kern = functools.partial(_latent_attn_kernel, heads=heads, nope=nope, rope=rope, v_head=v_head)
    return pl.pallas_call(
        kern,
        out_shape=jax.ShapeDtypeStruct((b, t, heads * v_head), F32),
        grid=(b,),
        in_specs=[blk(t, heads * LANES), blk(past, kv_lora), blk(past, rope), blk(t, kv_lora), blk(t, rope),
                  const(wk), const(wv)],
        out_specs=blk(t, heads * v_head),
        compiler_params=_params("arbitrary"),
        name="latent_attn",
    )(q, c_past, kr_past, c_new, kr_new, wk, wv)


def _mlstm_prep_kernel(m_ref, gc_ref, gr_ref, bc_ref, br_ref, pv_ref, kv_ref, ks_ref, b_ref, st_ref, *,
                       heads, dk, dv, chunk, nchunk):
    row = lax.broadcasted_iota(jnp.int32, (chunk, chunk), 0)
    col = lax.broadcasted_iota(jnp.int32, (chunk, chunk), 1)
    causal = col <= row
    tril = causal.astype(F32)
    triu = (row <= col).astype(F32)
    lane = lax.broadcasted_iota(jnp.int32, (chunk, LANES), 1)
    o_k, o_v = heads * dk, 2 * heads * dk

    for c in range(nchunk):
        rows = slice(c * chunk, (c + 1) * chunk)
        gc = gc_ref[0, rows, :] + bc_ref[...]
        gr = gr_ref[0, c] + br_ref[...]
        bcum_c = _dot(tril, _log_sigmoid(gc), HIGHEST)
        bcum_r = _dot(_log_sigmoid(gr), triu, HIGHEST)
        b_ref[0, rows, :] = bcum_c
        stats = jnp.zeros((chunk, LANES), F32)
        for h in range(heads):
            q = m_ref[0, rows, h * dk:(h + 1) * dk].astype(BF16)
            k = m_ref[0, rows, o_k + h * dk:o_k + (h + 1) * dk] * (dk ** -0.5)
            v = m_ref[0, rows, o_v + h * dv:o_v + (h + 1) * dv].astype(BF16)
            li_c, li_r = gc[:, h:h + 1], gr[h:h + 1, :]
            b_c, b_r = bcum_c[:, heads + h:heads + h + 1], bcum_r[heads + h:heads + h + 1, :]
            dmat = jnp.where(causal, b_c - b_r + li_r, NEG_INF)
            mx = jnp.max(dmat, axis=-1, keepdims=True)
            p0 = _dot_nt(q, k.astype(BF16)) * jnp.exp(dmat - mx)
            pv_ref[0, rows, h * dv:(h + 1) * dv] = _dot(p0.astype(BF16), v)
            wk0 = jnp.exp(b_c[chunk - 1:chunk, :] - b_c + li_c - mx[chunk - 1:chunk, :]) * k
            kv_ref[0, c, h] = _dot_tn(wk0.astype(BF16), v)
            ks_ref[0, c, h] = jnp.sum(wk0, axis=0, keepdims=True)
            stats = jnp.where(lane == h, mx, stats)
            stats = jnp.where(lane == heads + h, jnp.sum(p0, axis=-1, keepdims=True), stats)
        st_ref[0, rows, :] = stats


def _mlstm_scan_kernel(q_ref, mo_ref, pv_ref, kv_ref, ks_ref, b_ref, st_ref, gout_ref, c0_ref, n0_ref, m0_ref,
                       h_ref, c1_ref, n1_ref, m1_ref, c_sc, n_sc, m_sc, *, heads, dk, dv, chunk, nchunk):
    t = pl.program_id(1)

    @pl.when(t == 0)
    def _():
        c_sc[...] = c0_ref[0]
        n_sc[...] = n0_ref[0]
        m_sc[...] = m0_ref[0]

    def body(c, carry):
        rows = pl.ds(pl.multiple_of(c * chunk, chunk), chunk)
        bcum = b_ref[0, rows, :]
        stats = st_ref[0, rows, :]
        for h in range(heads):
            c0, n0, m0 = c_sc[h], n_sc[h], m_sc[h]
            qf = q_ref[0, rows, h * dk:(h + 1) * dk]
            b_c = bcum[:, heads + h:heads + h + 1]
            mx = stats[:, h:h + 1]
            inter = b_c + m0
            m = jnp.maximum(inter, mx)
            w_inter = jnp.exp(inter - m)
            r = jnp.exp(mx - m)
            num = w_inter * _dot(qf.astype(BF16), c0.astype(BF16)) + r * pv_ref[0, rows, h * dv:(h + 1) * dv]
            den = w_inter * jnp.sum(qf * n0, axis=-1, keepdims=True) + r * stats[:, heads + h:heads + h + 1]
            hh = num / jnp.maximum(jnp.abs(den), jnp.exp(-m))
            m_end = m[chunk - 1:chunk, :]
            decay_end = w_inter[chunk - 1:chunk, :]
            f_new = r[chunk - 1:chunk, :]
            c_sc[h] = decay_end * c0 + f_new * kv_ref[0, c, h]
            n_sc[h] = decay_end * n0 + f_new * ks_ref[0, c, h]
            m_sc[h] = m_end
            hn = _rms(hh, gout_ref[:, h * dv:(h + 1) * dv])
            h_ref[0, rows, h * dv:(h + 1) * dv] = hn * _sigmoid(mo_ref[0, rows, h * dv:(h + 1) * dv])
        return carry

    lax.fori_loop(0, nchunk, body, 0)

    @pl.when(t == pl.num_programs(1) - 1)
    def _():
        c1_ref[0] = c_sc[...]
        n1_ref[0] = n_sc[...]
        m1_ref[0] = m_sc[...]


def _mlstm(m_slab, gates_c, gates_r, bias_c, bias_r, gout, c0, n0, m0, *, tc, heads, dk, dv):
    b, t, mw = m_slab.shape
    chunk = min(CHUNK, t)
    const = lambda a: pl.BlockSpec(a.shape, lambda i, j: (0,) * a.ndim)
    qkv_w = 2 * heads * dk + heads * dv
    mo_blk, rem = divmod(qkv_w, heads * dv)
    assert rem == 0 and mw == qkv_w + heads * dv

    tp = min(t, MLSTM_PREP_TC)
    npc = tp // chunk
    tokp = lambda last: pl.BlockSpec((1, tp, last), lambda i, j: (i, j, 0))
    per_chunk = lambda *s: pl.BlockSpec((1, npc, heads) + s, lambda i, j: (i, j, 0) + (0,) * len(s))
    pv, kv, ks, bcum, stats = pl.pallas_call(
        functools.partial(_mlstm_prep_kernel, heads=heads, dk=dk, dv=dv, chunk=chunk, nchunk=npc),
        out_shape=[jax.ShapeDtypeStruct((b, t, heads * dv), F32),
                   jax.ShapeDtypeStruct((b, t // chunk, heads, dk, dv), F32),
                   jax.ShapeDtypeStruct((b, t // chunk, heads, 1, dk), F32),
                   jax.ShapeDtypeStruct((b, t, LANES), F32),
                   jax.ShapeDtypeStruct((b, t, LANES), F32)],
        grid=(b, t // tp),
        in_specs=[tokp(qkv_w), tokp(LANES),
                  pl.BlockSpec((1, npc, SUBLANES, chunk), lambda i, j: (i, j, 0, 0)),
                  const(bias_c), const(bias_r)],
        out_specs=[tokp(heads * dv), per_chunk(dk, dv), per_chunk(1, dk), tokp(LANES), tokp(LANES)],
        compiler_params=_params("arbitrary", "arbitrary"),
        name="mlstm_prep",
    )(m_slab, gates_c, gates_r, bias_c, bias_r)

    nchunk = tc // chunk
    tok = lambda last, blk=0: pl.BlockSpec((1, tc, last), lambda i, j: (i, j, blk))
    per_chunk = lambda *s: pl.BlockSpec((1, nchunk, heads) + s, lambda i, j: (i, j, 0) + (0,) * len(s))
    st = lambda *s: pl.BlockSpec((1,) + s, lambda i, j: (i,) + (0,) * len(s))
    return pl.pallas_call(
        functools.partial(_mlstm_scan_kernel, heads=heads, dk=dk, dv=dv, chunk=chunk, nchunk=nchunk),
        out_shape=[jax.ShapeDtypeStruct((b, t, heads * dv), F32),
                   jax.ShapeDtypeStruct((b, heads, dk, dv), F32),
                   jax.ShapeDtypeStruct((b, heads, 1, dk), F32),
                   jax.ShapeDtypeStruct((b, heads, 1, 1), F32)],
        grid=(b, t // tc),
        in_specs=[tok(heads * dk),
                  tok(heads * dv, mo_blk),
                  tok(heads * dv), per_chunk(dk, dv), per_chunk(1, dk), tok(LANES), tok(LANES),
                  const(gout), st(heads, dk, dv), st(heads, 1, dk), st(heads, 1, 1)],
        out_specs=[tok(heads * dv), st(heads, dk, dv), st(heads, 1, dk), st(heads, 1, 1)],
        scratch_shapes=[pltpu.VMEM((heads, dk, dv), F32), pltpu.VMEM((heads, 1, dk), F32),
                        pltpu.VMEM((heads, 1, 1), F32)],
        compiler_params=_params("arbitrary", "arbitrary"),
        name="mlstm_scan",
    )(m_slab, m_slab, pv, kv, ks, bcum, stats, gout, c0, n0, m0)


def _mid_kernel(oa_ref, ob_ref, z_ref, x_ref, gate_ref, wo_ref, shift_ref, scale_ref, g_ref, w2_ref,
                x1_ref, qkv_ref, z2_ref, ab_ref, *, half, conv_ch, d_model):
    z = z_ref[0]
    ma = (oa_ref[0] * _silu(z[:, :half])).astype(BF16)
    mb = (ob_ref[0] * _silu(z[:, half:])).astype(BF16)
    y = _dot(ma, wo_ref[0:half, :]) + _dot(mb, wo_ref[half:, :])
    x1 = x_ref[0] + gate_ref[0] * y
    x1_ref[0] = x1
    hn = _rms(x1, g_ref[...]) * (1.0 + scale_ref[0]) + shift_ref[0]
    y2 = _dot(hn.astype(BF16), w2_ref[...])
    qkv_ref[0] = y2[:, :conv_ch]
    z2_ref[0] = y2[:, conv_ch:conv_ch + d_model]
    ab_ref[0] = y2[:, conv_ch + d_model:]


def _mid(oa, ob, z, x, gate, wo, shift, scale, g, w2, *, tm, conv_ch):
    b, t, d = x.shape
    half = oa.shape[-1]
    tok = lambda last: pl.BlockSpec((1, tm, last), lambda i, j: (i, j, 0))
    vec = pl.BlockSpec((1, 1, d), lambda i, j: (i, 0, 0))
    const = lambda a: pl.BlockSpec(a.shape, lambda i, j: (0,) * a.ndim)
    kern = functools.partial(_mid_kernel, half=half, conv_ch=conv_ch, d_model=d)
    return pl.pallas_call(
        kern,
        out_shape=[jax.ShapeDtypeStruct((b, t, d), F32), jax.ShapeDtypeStruct((b, t, conv_ch), F32),
                   jax.ShapeDtypeStruct((b, t, d), F32), jax.ShapeDtypeStruct((b, t, LANES), F32)],
        grid=(b, t // tm),
        in_specs=[tok(half), tok(ob.shape[-1]), tok(d), tok(d), vec, const(wo), vec, vec, const(g), const(w2)],
        out_specs=[tok(d), tok(conv_ch), tok(d), tok(LANES)],
        compiler_params=_params("arbitrary", "arbitrary"),
        name="out_a_in_c",
    )(oa, ob, z, x, gate, wo, shift, scale, g, w2)


def _conv_kernel(qkv_ref, past_ref, wc_ref, ab_ref, alog_ref, dtb_ref, act_ref, gb_ref, ext_sc, *,
                 tm, width, heads, dk):
    @pl.when(pl.program_id(1) == 0)
    def _():
        ext_sc[0:SUBLANES, :] = past_ref[0]

    ext_sc[SUBLANES:SUBLANES + tm, :] = qkv_ref[0]
    conv = wc_ref[width - 1:width, :] * ext_sc[SUBLANES:SUBLANES + tm, :]
    for j in range(width - 1):
        s = SUBLANES - (width - 1) + j
        conv = conv + wc_ref[j:j + 1, :] * ext_sc[s:s + tm, :]
    ext_sc[0:SUBLANES, :] = ext_sc[tm:tm + SUBLANES, :]
    act = _silu(conv)
    for h in range(2 * heads):
        xh = act[:, h * dk:(h + 1) * dk]
        xh = xh * lax.rsqrt(jnp.sum(xh * xh, axis=-1, keepdims=True) + EPS)
        if h < heads:
            xh = xh * (dk ** -0.5)
        act_ref[0, :, h * dk:(h + 1) * dk] = xh
    act_ref[0, :, 2 * heads * dk:] = act[:, 2 * heads * dk:]
    ab = ab_ref[0]
    g = -jnp.exp(alog_ref[...]) * _softplus(ab + dtb_ref[...])
    lane = lax.broadcasted_iota(jnp.int32, ab.shape, 1)
    gb_ref[0] = jnp.where(lane < heads, g, _sigmoid(ab))


def _conv(qkv, past8, wc8, ab, alog, dtb, *, tm, width, heads, dk):
    b, t, ch = qkv.shape
    tok = lambda last: pl.BlockSpec((1, tm, last), lambda i, j: (i, j, 0))
    const = lambda a: pl.BlockSpec(a.shape, lambda i, j: (0,) * a.ndim)
    kern = functools.partial(_conv_kernel, tm=tm, width=width, heads=heads, dk=dk)
    return pl.pallas_call(
        kern,
        out_shape=[jax.ShapeDtypeStruct((b, t, ch), F32), jax.ShapeDtypeStruct((b, t, LANES), F32)],
        grid=(b, t // tm),
        in_specs=[tok(ch), pl.BlockSpec((1, SUBLANES, ch), lambda i, j: (i, 0, 0)), const(wc8), tok(LANES),
                  const(alog), const(dtb)],
        out_specs=[tok(ch), tok(LANES)],
        scratch_shapes=[pltpu.VMEM((tm + SUBLANES, ch), F32)],
        compiler_params=_params("arbitrary", "arbitrary"),
        name="conv_gates",
    )(qkv, past8, wc8, ab, alog, dtb)


def _blockdiag(x, group, chunk):
    w = group * chunk
    br = lax.broadcasted_iota(jnp.int32, (w, w), 0) // chunk
    bc = lax.broadcasted_iota(jnp.int32, (w, w), 1) // chunk
    xb = x.astype(BF16)
    return jnp.where(br == bc, jnp.concatenate([xb] * group, axis=0), jnp.zeros((), BF16))


def _unit_lower_inverses_minus_eye(a_list, group, chunk):
    w = group * chunk
    r = lax.broadcasted_iota(jnp.int32, (chunk, w), 0)
    cc = lax.broadcasted_iota(jnp.int32, (chunk, w), 1) % chunk
    es = [-jnp.where((r // 2 == cc // 2) & (r % 2 == 1) & (cc % 2 == 0), a4, 0.0) for a4 in a_list]
    s = 2
    while s < chunk:
        off = (r // (2 * s) == cc // (2 * s)) & (r % (2 * s) >= s) & (cc % (2 * s) < s)
        a_offs = [jnp.where(off, a4, 0.0) for a4 in a_list]
        ps = [a + _dot(a.astype(BF16), _blockdiag(e, group, chunk)) for a, e in zip(a_offs, es)]
        es = [e - (p + _dot(e.astype(BF16), _blockdiag(p, group, chunk))) for e, p in zip(es, ps)]
        s *= 2
    return es


def _gdn_prep_kernel(act_ref, gbc_ref, gbr_ref, w_ref, uv_ref, kd_ref, attn_ref, eg_ref, *,
                     heads, dk, dv, chunk, nchunk, group):
    row = lax.broadcasted_iota(jnp.int32, (chunk, chunk), 0)
    col = lax.broadcasted_iota(jnp.int32, (chunk, chunk), 1)
    incl = col <= row
    strict = col < row
    tril = incl.astype(F32)
    triu = (row <= col).astype(F32)
    lane = lax.broadcasted_iota(jnp.int32, (chunk, LANES), 1)
    o_k, o_v = heads * dk, 2 * heads * dk

    problems, a_list, rhs_list = [], [], []
    for c in range(nchunk):
        rows = slice(c * chunk, (c + 1) * chunk)
        gbc = gbc_ref[0, rows, :]
        gbr = gbr_ref[0, c]
        gcum_c = _dot(tril, gbc, HIGHEST)
        gcum_r = _dot(gbr, triu, HIGHEST)
        eg_ref[0, rows, :] = jnp.where(lane < heads, jnp.exp(gcum_c), 0.0)
        for g0 in range(0, heads, group):
            a_blocks, rhs_blocks = [], []
            for h in range(g0, g0 + group):
                q = act_ref[0, rows, h * dk:(h + 1) * dk].astype(BF16)
                k = act_ref[0, rows, o_k + h * dk:o_k + (h + 1) * dk]
                v = act_ref[0, rows, o_v + h * dv:o_v + (h + 1) * dv]
                kb = k.astype(BF16)
                g_c, g_r = gcum_c[:, h:h + 1], gcum_r[h:h + 1, :]
                beta = gbc[:, heads + h:heads + h + 1]
                decay = jnp.exp(jnp.where(incl, g_c - g_r, NEG_INF))
                a_blocks.append(jnp.where(strict, beta * _dot_nt(kb, kb) * decay, 0.0))
                rhs_blocks.append(jnp.concatenate([beta * v, (beta * jnp.exp(g_c)) * k], axis=1))
                attn_ref[0, rows, h * chunk:(h + 1) * chunk] = (_dot_nt(q, kb) * decay).astype(BF16)
                kd_ref[0, rows, h * dk:(h + 1) * dk] = (k * jnp.exp(g_c[chunk - 1:chunk, :] - g_c)).astype(BF16)
            problems.append((rows, g0))
            a_list.append(jnp.concatenate(a_blocks, axis=1))
            rhs_list.append(jnp.concatenate(rhs_blocks, axis=0))
    e_list = _unit_lower_inverses_minus_eye(a_list, group, chunk)
    for (rows, g0), e, rhs in zip(problems, e_list, rhs_list):
        sol = rhs + _dot(_blockdiag(e, group, chunk), rhs.astype(BF16))
        for i, h in enumerate(range(g0, g0 + group)):
            uv_ref[0, rows, h * dv:(h + 1) * dv] = sol[i * chunk:(i + 1) * chunk, :dv]
            w_ref[0, rows, h * dk:(h + 1) * dk] = sol[i * chunk:(i + 1) * chunk, dv:].astype(BF16)


def _gdn_prep(act, gb_c, gb_r, *, tc, heads, dk, dv):
    b, t, _ = act.shape
    chunk = min(CHUNK, t)
    tc = min(tc, t)
    nchunk = tc // chunk
    group = (2 * LANES) // chunk
    kern = functools.partial(_gdn_prep_kernel, heads=heads, dk=dk, dv=dv, chunk=chunk, nchunk=nchunk, group=group)
    tok = lambda last: pl.BlockSpec((1, tc, last), lambda i, j: (i, j, 0))
    return pl.pallas_call(
        kern,
        out_shape=[jax.ShapeDtypeStruct((b, t, heads * dk), BF16), jax.ShapeDtypeStruct((b, t, heads * dv), F32),
                   jax.ShapeDtypeStruct((b, t, heads * dk), BF16), jax.ShapeDtypeStruct((b, t, heads * chunk), BF16),
                   jax.ShapeDtypeStruct((b, t, LANES), F32)],
        grid=(b, t // tc),
        in_specs=[tok(act.shape[-1]), tok(LANES),
                  pl.BlockSpec((1, nchunk, 2 * SUBLANES, chunk), lambda i, j: (i, j, 0, 0))],
        out_specs=[tok(heads * dk), tok(heads * dv), tok(heads * dk), tok(heads * chunk), tok(LANES)],
        compiler_params=_params("arbitrary", "arbitrary"),
        name="gdn_prep",
    )(act, gb_c, gb_r)


def _gdn_scan_kernel(q_ref, w_ref, uv_ref, kd_ref, attn_ref, eg_ref, gout_ref, s0_ref, o_ref, s1_ref, s_sc, *,
                     heads, dk, dv, chunk, nchunk):
    t = pl.program_id(1)

    @pl.when(t == 0)
    def _():
        s_sc[...] = s0_ref[0]

    def body(c, carry):
        rows = pl.ds(pl.multiple_of(c * chunk, chunk), chunk)
        eg = eg_ref[0, rows, :]
        for h in range(heads):
            s0 = s_sc[h]
            s0b = s0.astype(BF16)
            q = q_ref[0, rows, h * dk:(h + 1) * dk].astype(BF16)
            u = uv_ref[0, rows, h * dv:(h + 1) * dv] - _dot(w_ref[0, rows, h * dk:(h + 1) * dk], s0b)
            ub = u.astype(BF16)
            eg_h = eg[:, h:h + 1]
            o = eg_h * _dot(q, s0b) + _dot(attn_ref[0, rows, h * chunk:(h + 1) * chunk], ub)
            s_sc[h] = eg_h[chunk - 1:chunk, :] * s0 + _dot_tn(kd_ref[0, rows, h * dk:(h + 1) * dk], ub)
            o_ref[0, rows, h * dv:(h + 1) * dv] = _rms(o, gout_ref[...])
        return carry

    lax.fori_loop(0, nchunk, body, 0)

    @pl.when(t == pl.num_programs(1) - 1)
    def _():
        s1_ref[0] = s_sc[...]


def _gdn_scan(act, w, uv, kd, attn, eg, gout, s0, *, tc, heads, dk, dv):
    b, t, _ = act.shape
    chunk = min(CHUNK, t)
    nchunk = tc // chunk
    kern = functools.partial(_gdn_scan_kernel, heads=heads, dk=dk, dv=dv, chunk=chunk, nchunk=nchunk)
    tok = lambda last: pl.BlockSpec((1, tc, last), lambda i, j: (i, j, 0))
    state = pl.BlockSpec((1, heads, dk, dv), lambda i, j: (i, 0, 0, 0))
    return pl.pallas_call(
        kern,
        out_shape=[jax.ShapeDtypeStruct((b, t, heads * dv), F32), jax.ShapeDtypeStruct(s0.shape, F32)],
        grid=(b, t // tc),
        in_specs=[tok(heads * dk),
                  tok(heads * dk), tok(heads * dv), tok(heads * dk), tok(heads * chunk), tok(LANES),
                  pl.BlockSpec(gout.shape, lambda i, j: (0, 0)), state],
        out_specs=[tok(heads * dv), state],
        scratch_shapes=[pltpu.VMEM((heads, dk, dv), F32)],
        compiler_params=_params("arbitrary", "arbitrary"),
        name="gdn_scan",
    )(act, w, uv, kd, attn, eg, gout, s0)


def _final_kernel(o_ref, z_ref, x_ref, gate_ref, wo_ref, g_ref, y_ref):
    mixed = (o_ref[0] * _silu(z_ref[0])).astype(BF16)
    x2 = x_ref[0] + gate_ref[0] * _dot(mixed, wo_ref[...])
    y_ref[0] = _rms(x2, g_ref[...])


def _final(o, z, x, gate, wo, g, *, tm):
    b, t, d = x.shape
    tok = lambda last: pl.BlockSpec((1, tm, last), lambda i, j: (i, j, 0))
    const = lambda a: pl.BlockSpec(a.shape, lambda i, j: (0,) * a.ndim)
    return pl.pallas_call(
        _final_kernel,
        out_shape=jax.ShapeDtypeStruct((b, t, d), F32),
        grid=(b, t // tm),
        in_specs=[tok(o.shape[-1]), tok(d), tok(d), pl.BlockSpec((1, 1, d), lambda i, j: (i, 0, 0)),
                  const(wo), const(g)],
        out_specs=tok(d),
        compiler_params=_params("arbitrary", "arbitrary"),
        name="out_c_final",
    )(o, z, x, gate, wo, g)


def _pad_lanes(w, width=LANES, at=0):
    out = jnp.zeros(w.shape[:-1] + (width,), w.dtype)
    return out.at[..., at:at + w.shape[-1]].set(w)


def _rot_half_cols(w):
    r = w.shape[-1] // 2
    return jnp.concatenate([-w[..., r:], w[..., :r]], axis=-1)


def _rope_tables(pos, rope, scale):
    freqs = jnp.exp(jnp.arange(0, rope, 2, dtype=F32) * (-math.log(ROPE_BASE) / rope))
    ang = pos.astype(F32)[:, None] * freqs[None, :]
    cos = jnp.concatenate([jnp.cos(ang), jnp.cos(ang)], axis=-1)
    sin = jnp.concatenate([jnp.sin(ang), jnp.sin(ang)], axis=-1)
    half = LANES // 2
    cosk = _pad_lanes(cos, at=half)
    sink = _pad_lanes(sin, at=half)
    ones = _pad_lanes(jnp.ones((pos.shape[0], half), F32))
    return jnp.concatenate([(cosk + ones) * scale, sink * scale, cosk, sink], axis=-1)


def _tokens_on_lanes(a, chunk, rows):
    b, t = a.shape[:2]
    return a[..., :rows].reshape(b, t // chunk, chunk, rows).transpose(0, 1, 3, 2)


def kernel(x_prompt, x_sample, c_prompt, c_sample, cache_kv_latent, cache_k_rope, state_mlstm_C, state_mlstm_n, state_mlstm_m, state_gdn_S, state_gdn_conv, a_w_ada, a_b_ada, a_g_norm, a_w_in, a_g_q_a, a_w_q_b, a_g_kv_a, a_w_kv_b, a_b_i, a_b_f, a_g_out, a_w_out, c_w_ada, c_b_ada, c_g_norm, c_w_in, c_w_conv, c_a_log, c_dt_bias, c_g_out, c_w_out, g_final):
    d = x_prompt.shape[-1]
    q_lora, heads, qk = a_w_q_b.shape
    kv_lora = a_w_kv_b.shape[0]
    rope = cache_k_rope.shape[-1]
    nope = qk - rope
    v_head = a_w_kv_b.shape[2] - nope
    m_heads, m_dv = a_g_out.shape
    m_dk = state_mlstm_C.shape[2]
    g_heads = c_a_log.shape[0]
    g_dk, g_dv = state_gdn_S.shape[2:]
    width = c_w_conv.shape[0]
    conv_ch = c_w_conv.shape[1]
    assert nope + rope <= LANES and nope == LANES // 2 and 2 * m_heads <= SUBLANES and 2 * g_heads <= 2 * SUBLANES

    sizes = (q_lora, kv_lora, rope, m_heads * m_dk, m_heads * m_dk, m_heads * m_dv, m_heads, m_heads,
             m_heads * m_dv, heads * v_head + m_heads * m_dv)
    offs = [0]
    for s in sizes:
        offs.append(offs[-1] + s)
    w_qa, w_c, w_kr, w_mq, w_mk, w_mv, w_mi, w_mf, w_mo, w_z = [a_w_in[:, offs[i]:offs[i + 1]] for i in range(10)]
    half = LANES // 2
    w1 = jnp.concatenate([w_qa, w_c, _pad_lanes(w_kr, at=half), _pad_lanes(_rot_half_cols(w_kr), at=half),
                          w_mq, w_mk, w_mv, w_mo, w_z, _pad_lanes(jnp.concatenate([w_mi, w_mf], axis=1))],
                         axis=1).astype(BF16)
    m_width = 2 * m_heads * m_dk + 2 * m_heads * m_dv
    wq_rope = a_w_q_b[..., nope:]
    wq_main = _pad_lanes(a_w_q_b).reshape(q_lora, heads * LANES)
    wq_rot = _pad_lanes(_rot_half_cols(wq_rope), at=nope).reshape(q_lora, heads * LANES)
    wq = jnp.concatenate([wq_main, wq_rot], axis=1).astype(BF16)
    wkv = jnp.concatenate([_pad_lanes(a_w_kv_b[..., :nope]).reshape(kv_lora, heads * LANES),
                           a_w_kv_b[..., nope:].reshape(kv_lora, heads * v_head)], axis=1).astype(BF16)
    wk_abs = a_w_kv_b[..., :nope].transpose(1, 0, 2).astype(BF16)
    wv_abs = a_w_kv_b[..., nope:].transpose(1, 0, 2).astype(BF16)
    wo_a = a_w_out.astype(BF16)
    csz = (conv_ch, g_heads, g_heads, g_heads * g_dv)
    w_qkv, w_a, w_b, w_zc = [c_w_in[:, sum(csz[:i]):sum(csz[:i + 1])] for i in range(4)]
    w2 = jnp.concatenate([w_qkv, w_zc, _pad_lanes(jnp.concatenate([w_a, w_b], axis=1))], axis=1).astype(BF16)
    wo_c = c_w_out.astype(BF16)
    wc8 = jnp.zeros((SUBLANES, conv_ch), F32).at[:width].set(c_w_conv)
    row = lambda a: a.reshape(1, -1).astype(F32)
    bias_c = _pad_lanes(jnp.concatenate([a_b_i, a_b_f]).reshape(1, -1))
    bias_r = jnp.zeros((SUBLANES, 1), F32).at[:2 * m_heads, 0].set(jnp.concatenate([a_b_i, a_b_f]))
    alog = _pad_lanes(c_a_log.reshape(1, -1))
    dtb = _pad_lanes(c_dt_bias.reshape(1, -1))

    bp, bs = c_prompt.shape[0], c_sample.shape[0]
    c_all = jnp.concatenate([c_prompt, c_sample], axis=0)
    pad = (-c_all.shape[0]) % SUBLANES
    c_all = jnp.pad(c_all, ((0, pad), (0, 0)))
    mod_a = _adaln(c_all, a_w_ada, a_b_ada)
    mod_c = _adaln(c_all, c_w_ada, c_b_ada)

    def mods(mod, lo, hi):
        return [mod[lo:hi, i * d:(i + 1) * d][:, None, :] for i in range(3)]

    def run(x, mod_lo, mod_hi, c_past, kr_past, c0, n0, m0, conv0, s0):
        b, t, _ = x.shape
        past = 0 if c_past is None else c_past.shape[1]
        chunk = min(CHUNK, t)
        tm = min(t, 256)
        tc = min(t, 512)
        shift_a, scale_a, gate_a = mods(mod_a, mod_lo, mod_hi)
        shift_c, scale_c, gate_c = mods(mod_c, mod_lo, mod_hi)
        tab = _rope_tables(past + jnp.arange(t, dtype=jnp.int32), rope, qk ** -0.5 * math.log2(math.e))
        expand = c_past is None
        outs = _in_a(x, shift_a, scale_a, row(a_g_norm), w1, row(a_g_q_a), wq, row(a_g_kv_a), wkv, tab,
                     tm=tm, heads=heads, q_lora=q_lora, kv_lora=kv_lora, rope=rope, m_width=m_width,
                     v_head=v_head, expand_kv=expand)
        q, c_new, kr_new, m_slab, z, gates = outs[:6]
        if expand:
            o_mla = _flash(q, outs[6], outs[7].transpose(0, 2, 1), heads=heads, v_head=v_head,
                           tq=min(t, FLASH_TQ), tk=min(t, FLASH_TK))
        else:
            o_mla = _latent_attn(q, c_past, kr_past, c_new, kr_new, wk_abs, wv_abs,
                                 heads=heads, nope=nope, rope=rope, v_head=v_head)
        hm, c1, n1, m1 = _mlstm(m_slab, gates, _tokens_on_lanes(gates, chunk, SUBLANES), bias_c, bias_r,
                                row(a_g_out), c0, n0.reshape(b, m_heads, 1, m_dk), m0.reshape(b, m_heads, 1, 1),
                                tc=tc, heads=m_heads, dk=m_dk, dv=m_dv)
        x1, qkv, zc, ab = _mid(o_mla, hm, z, x, gate_a, wo_a, shift_c, scale_c, row(c_g_norm), w2,
                               tm=tm, conv_ch=conv_ch)
        past8 = jnp.pad(conv0, ((0, 0), (SUBLANES - (width - 1), 0), (0, 0)))
        act, gb = _conv(qkv, past8, wc8, ab, alog, dtb, tm=tm, width=width, heads=g_heads, dk=g_dk)
        w, uv, kd, attn, eg = _gdn_prep(act, gb, _tokens_on_lanes(gb, chunk, 2 * SUBLANES),
                                        tc=GDN_PREP_TC, heads=g_heads, dk=g_dk, dv=g_dv)
        o_gdn, s1 = _gdn_scan(act, w, uv, kd, attn, eg, row(c_g_out), s0, tc=tc, heads=g_heads, dk=g_dk, dv=g_dv)
        y = _final(o_gdn, zc, x1, gate_c, wo_c, row(g_final), tm=tm)
        conv1 = jnp.concatenate([conv0, qkv], axis=1)[:, t:] if t < width - 1 else qkv[:, t - (width - 1):]
        return (y, c_new, kr_new, c1, n1.reshape(b, m_heads, m_dk), m1.reshape(b, m_heads), conv1, s1)

    dt = x_prompt.dtype
    (y_p, p_kv, p_kr, p_c, p_n, p_m, p_conv, p_s) = run(
        x_prompt, 0, bp, None, None,
        jnp.zeros((bp, m_heads, m_dk, m_dv), dt), jnp.zeros((bp, m_heads, m_dk), dt), jnp.zeros((bp, m_heads), dt),
        jnp.zeros((bp, width - 1, conv_ch), dt), jnp.zeros((bp, g_heads, g_dk, g_dv), dt))
    (y_s, s_kv, s_kr, s_c, s_n, s_m, s_conv, s_s) = run(
        x_sample, bp, bp + bs, cache_kv_latent, cache_k_rope, state_mlstm_C, state_mlstm_n, state_mlstm_m,
        state_gdn_conv, state_gdn_S)
    return (y_p, y_s, p_kv, p_kr, p_c, p_n, p_m, p_s, p_conv,
            s_kv, s_kr, s_c, s_n, s_m, s_s, s_conv)
```

```python
import functools
import math

import jax
import jax.numpy as jnp
from jax import lax
from jax.experimental import pallas as pl
from jax.experimental.pallas import tpu as pltpu

F32 = jnp.float32
BF16 = jnp.bfloat16
HIGHEST = lax.Precision.HIGHEST

CHUNK = 64
EPS = 1e-6
ROPE_BASE = 10000.0
LANES = 128
SUBLANES = 8
VMEM_LIMIT = 56 * 1024 * 1024
NEG_INF = float("-inf")
FLASH_TQ = 512
FLASH_TK = 512
FLASH_HEAD_GROUP = 4
FLASH_BOUND_SLACK = 1.0 + 2.0 ** -6
FLASH_ROW_SUM_MIN = 2.0 ** -100
MLSTM_PREP_TC = 256
GDN_PREP_TC = 256


def _params(*sem):
    return pltpu.CompilerParams(dimension_semantics=sem, vmem_limit_bytes=VMEM_LIMIT)


def _dot(a, b, precision=None):
    return jnp.dot(a, b, preferred_element_type=F32, precision=precision)


def _dot_nt(a, b):
    return lax.dot_general(a, b, (((1,), (1,)), ((), ())), preferred_element_type=F32)


def _dot_tn(a, b):
    return lax.dot_general(a, b, (((0,), (0,)), ((), ())), preferred_element_type=F32)


def _rms(x, g):
    return x * lax.rsqrt(jnp.mean(x * x, axis=-1, keepdims=True) + EPS) * g


def _sigmoid(x):
    return 1.0 / (1.0 + jnp.exp(-x))


def _silu(x):
    return x * _sigmoid(x)


def _softplus(x):
    return jnp.maximum(x, 0.0) + jnp.log1p(jnp.exp(-jnp.abs(x)))


def _log_sigmoid(x):
    return -_softplus(-x)


def _adaln_kernel(c_ref, w_ref, b_ref, o_ref):
    o_ref[...] = _dot(_silu(c_ref[...]), w_ref[...], HIGHEST) + b_ref[...]


def _adaln(c, w, b):
    n, d = c.shape
    d3 = w.shape[1]
    return pl.pallas_call(
        _adaln_kernel,
        out_shape=jax.ShapeDtypeStruct((n, d3), F32),
        grid=(d3 // d,),
        in_specs=[pl.BlockSpec((n, d), lambda j: (0, 0)),
                  pl.BlockSpec((d, d), lambda j: (0, j)),
                  pl.BlockSpec((1, d), lambda j: (0, j))],
        out_specs=pl.BlockSpec((n, d), lambda j: (0, j)),
        compiler_params=_params("arbitrary"),
        name="adaln",
    )(c, w, b.reshape(1, d3))


def _in_a_kernel(x_ref, shift_ref, scale_ref, g_ref, w1_ref, gq_ref, wq_ref, gkv_ref, wkv_ref, tab_ref,
                 q_ref, c_ref, kr_ref, m_ref, z_ref, gt_ref, *kv_refs,
                 heads, q_lora, kv_lora, rope, m_width, d_model, v_head, expand_kv):
    x = x_ref[0]
    hn = _rms(x, g_ref[...]) * (1.0 + scale_ref[0]) + shift_ref[0]
    y = _dot(hn.astype(BF16), w1_ref[...])
    o = 0
    qa = y[:, o:o + q_lora]; o += q_lora
    cl = y[:, o:o + kv_lora]; o += kv_lora
    kr1 = y[:, o:o + LANES]; o += LANES
    kr2 = y[:, o:o + LANES]; o += LANES
    m_ref[0] = y[:, o:o + m_width]; o += m_width
    z_ref[0] = y[:, o:o + d_model]; o += d_model
    gt_ref[0] = y[:, o:o + LANES]

    tab = tab_ref[...]
    cosq, sinq = tab[:, 0:LANES], tab[:, LANES:2 * LANES]
    cosk, sink = tab[:, 2 * LANES:3 * LANES], tab[:, 3 * LANES:4 * LANES]

    def sq_norm(xb):
        xf = xb.astype(F32)
        return jnp.sum(xf * xf, axis=-1, keepdims=True)

    lane = lax.broadcasted_iota(jnp.int32, (x.shape[0], LANES), 1)
    qq = _dot(_rms(qa, gq_ref[...]).astype(BF16), wq_ref[...])
    hw = heads * LANES
    qn2 = jnp.zeros((x.shape[0], LANES), F32)
    for h in range(heads):
        sl = slice(h * LANES, (h + 1) * LANES)
        qb = (qq[:, sl] * cosq + qq[:, hw + h * LANES:hw + (h + 1) * LANES] * sinq).astype(BF16)
        q_ref[0, :, sl] = qb
        if expand_kv:
            qn2 = jnp.where(lane == h, sq_norm(qb), qn2)

    cn = _rms(cl, gkv_ref[...])
    c_ref[0] = cn
    kr = kr1 * cosk + kr2 * sink
    kr_ref[0] = kr[:, LANES // 2:LANES // 2 + rope]
    if expand_kv:
        k_ref, v_ref, qn_ref, kmax_ref = kv_refs
        kv = _dot(cn.astype(BF16), wkv_ref[...])
        kn2 = jnp.zeros((x.shape[0], LANES), F32)
        for h in range(heads):
            sl = slice(h * LANES, (h + 1) * LANES)
            kb = (kv[:, sl] + kr).astype(BF16)
            k_ref[0, :, sl] = kb
            kn2 = jnp.where(lane == h, sq_norm(kb), kn2)
        v_ref[0] = kv[:, hw:hw + heads * v_head].astype(BF16)
        qn_ref[0] = jnp.sqrt(qn2)
        kmax_ref[0, 0] = jnp.sqrt(jnp.max(kn2, axis=0, keepdims=True))


def _in_a(x, shift, scale, g, w1, gq, wq, gkv, wkv, tab, *, tm, heads, q_lora, kv_lora, rope, m_width,
          v_head, expand_kv):
    b, t, d = x.shape
    n1 = w1.shape[1]
    grid = (b, t // tm)
    tok = lambda last: pl.BlockSpec((1, tm, last), lambda i, j: (i, j, 0))
    const = lambda a: pl.BlockSpec(a.shape, lambda i, j: (0,) * a.ndim)
    out_shape = [jax.ShapeDtypeStruct((b, t, heads * LANES), BF16),
                 jax.ShapeDtypeStruct((b, t, kv_lora), F32),
                 jax.ShapeDtypeStruct((b, t, rope), F32),
                 jax.ShapeDtypeStruct((b, t, m_width), F32),
                 jax.ShapeDtypeStruct((b, t, d), F32),
                 jax.ShapeDtypeStruct((b, t, LANES), F32)]
    out_specs = [tok(heads * LANES), tok(kv_lora), tok(rope), tok(m_width), tok(d), tok(LANES)]
    if expand_kv:
        out_shape += [jax.ShapeDtypeStruct((b, t, heads * LANES), BF16),
                      jax.ShapeDtypeStruct((b, t, heads * v_head), BF16),
                      jax.ShapeDtypeStruct((b, t, LANES), F32),
                      jax.ShapeDtypeStruct((b, t // tm, 1, LANES), F32)]
        out_specs += [tok(heads * LANES), tok(heads * v_head), tok(LANES),
                      pl.BlockSpec((1, 1, 1, LANES), lambda i, j: (i, j, 0, 0))]
    kern = functools.partial(_in_a_kernel, heads=heads, q_lora=q_lora, kv_lora=kv_lora, rope=rope,
                             m_width=m_width, d_model=d, v_head=v_head, expand_kv=expand_kv)
    return pl.pallas_call(
        kern, out_shape=out_shape, grid=grid,
        in_specs=[tok(d),
                  pl.BlockSpec((1, 1, d), lambda i, j: (i, 0, 0)),
                  pl.BlockSpec((1, 1, d), lambda i, j: (i, 0, 0)),
                  const(g), const(w1), const(gq), const(wq), const(gkv), const(wkv),
                  pl.BlockSpec((tm, 4 * LANES), lambda i, j: (j, 0))],
        out_specs=out_specs,
        compiler_params=_params("arbitrary", "arbitrary"),
        name="in_proj_a",
    )(x, shift, scale, g, w1, gq, wq, gkv, wkv, tab)


def _flash_kernel(qi_ref, ki_ref, q_ref, k_ref, vt_ref, o_ref, m_sc, l_sc, acc_sc, *,
                  heads, v_head, tq, tk, chunk):
    step_id = pl.program_id(1)
    qi = qi_ref[step_id]
    ki = ki_ref[step_id]

    @pl.when(ki == 0)
    def _():
        m_sc[...] = jnp.full(m_sc.shape, NEG_INF, F32)
        l_sc[...] = jnp.zeros(l_sc.shape, F32)
        acc_sc[...] = jnp.zeros(acc_sc.shape, F32)

    def step(masked):
        if masked:
            kc = (ki * tk + lax.broadcasted_iota(jnp.int32, (tk, tq), 0)) // chunk
            qc = (qi * tq + lax.broadcasted_iota(jnp.int32, (tk, tq), 1)) // chunk
            mask = kc <= qc
        for h in range(heads):
            qh = q_ref[0, :, h * LANES:(h + 1) * LANES]
            kh = k_ref[0, :, h * LANES:(h + 1) * LANES]
            vth = vt_ref[0, h * v_head:(h + 1) * v_head, :]
            rows = slice(h * v_head, (h + 1) * v_head)
            st = _dot_nt(kh, qh)
            if masked:
                st = jnp.where(mask, st, NEG_INF)
            m_prev = m_sc[h]
            m_new = jnp.maximum(m_prev, jnp.max(st, axis=0, keepdims=True))
            alpha = jnp.exp2(m_prev - m_new)
            p = jnp.exp2(st - m_new)
            l_sc[h] = alpha * l_sc[h] + jnp.sum(p, axis=0, keepdims=True)
            acc_sc[rows, :] = alpha * acc_sc[rows, :] + _dot(vth, p.astype(BF16))
            m_sc[h] = m_new

    full = (ki + 1) * tk <= qi * tq + chunk

    @pl.when(full)
    def _():
        step(False)

    @pl.when(jnp.logical_not(full))
    def _():
        step(True)

    @pl.when(ki == ((qi + 1) * tq - 1) // tk)
    def _():
        for h in range(heads):
            rows = slice(h * v_head, (h + 1) * v_head)
            acc_sc[rows, :] = acc_sc[rows, :] / l_sc[h]
        o_ref[0] = acc_sc[...].T


def _flash_bound_kernel(qi_ref, ki_ref, q_ref, k_ref, vt_ref, qn_ref, kmax_ref, o_ref, l_ref,
                        mb_sc, l_sc, acc_sc, *, heads, v_head, tq, tk, chunk):
    step_id = pl.program_id(1)
    qi = qi_ref[step_id]
    ki = ki_ref[step_id]

    @pl.when(ki == 0)
    def _():
        kmax = jnp.max(kmax_ref[0], axis=0) * FLASH_BOUND_SLACK
        for h in range(heads):
            mb_sc[h:h + 1, :] = qn_ref[0, h:h + 1, :] * kmax[:, h:h + 1]
        l_sc[...] = jnp.zeros(l_sc.shape, F32)
        acc_sc[...] = jnp.zeros(acc_sc.shape, F32)

    def step(masked):
        if masked:
            kc = (ki * tk + lax.broadcasted_iota(jnp.int32, (tk, tq), 0)) // chunk
            qc = (qi * tq + lax.broadcasted_iota(jnp.int32, (tk, tq), 1)) // chunk
            mask = kc <= qc
        for h0 in range(0, heads, FLASH_HEAD_GROUP):
            hs = range(h0, h0 + FLASH_HEAD_GROUP)
            st = [_dot_nt(k_ref[0, :, h * LANES:(h + 1) * LANES], q_ref[0, :, h * LANES:(h + 1) * LANES])
                  for h in hs]
            if masked:
                st = [jnp.where(mask, s, NEG_INF) for s in st]
            p = [jnp.exp2(s - mb_sc[h:h + 1, :]) for s, h in zip(st, hs)]
            for x, h in zip(p, hs):
                l_sc[h:h + 1, :] += jnp.sum(x, axis=0, keepdims=True)
            pv = [_dot(vt_ref[0, h * v_head:(h + 1) * v_head, :], x.astype(BF16)) for x, h in zip(p, hs)]
            for x, h in zip(pv, hs):
                acc_sc[h * v_head:(h + 1) * v_head, :] += x

    full = (ki + 1) * tk <= qi * tq + chunk

    @pl.when(full)
    def _():
        step(False)

    @pl.when(jnp.logical_not(full))
    def _():
        step(True)

    @pl.when(ki == ((qi + 1) * tq - 1) // tk)
    def _():
        for h in range(heads):
            rows = slice(h * v_head, (h + 1) * v_head)
            acc_sc[rows, :] = acc_sc[rows, :] / l_sc[h:h + 1, :]
        o_ref[0] = acc_sc[...].T
        l_ref[0] = l_sc[...]


def _flash(q, k, vt, *, heads, v_head, tq, tk, bound=None):
    b, t, _ = q.shape
    pairs = [(i, j) for i in range(t // tq) for j in range(((i + 1) * tq - 1) // tk + 1)]
    qi_tab = jnp.asarray([p[0] for p in pairs], jnp.int32)
    ki_tab = jnp.asarray([p[1] for p in pairs], jnp.int32)
    in_specs = [pl.BlockSpec((1, tq, heads * LANES), lambda i, s, qt, kt: (i, qt[s], 0)),
                pl.BlockSpec((1, tk, heads * LANES), lambda i, s, qt, kt: (i, kt[s], 0)),
                pl.BlockSpec((1, heads * v_head, tk), lambda i, s, qt, kt: (i, 0, kt[s]))]
    o_shape = jax.ShapeDtypeStruct((b, t, heads * v_head), F32)
    o_spec = pl.BlockSpec((1, tq, heads * v_head), lambda i, s, qt, kt: (i, qt[s], 0))
    acc = pltpu.VMEM((heads * v_head, tq), F32)
    if bound is None:
        kern, name, args = _flash_kernel, "flash_attn", (q, k, vt)
        out_shape, out_specs = o_shape, o_spec
        scratch = [pltpu.VMEM((heads, 1, tq), F32), pltpu.VMEM((heads, 1, tq), F32), acc]
    else:
        qn, kmax = bound
        assert qn.shape == (b, heads, t)
        kern, name, args = _flash_bound_kernel, "flash_attn_bound", (q, k, vt, qn, kmax)
        in_specs += [pl.BlockSpec((1, heads, tq), lambda i, s, qt, kt: (i, 0, qt[s])),
                     pl.BlockSpec((1,) + kmax.shape[1:], lambda i, s, qt, kt: (i, 0, 0, 0))]
        out_shape = [o_shape, jax.ShapeDtypeStruct((b, heads, t), F32)]
        out_specs = [o_spec, pl.BlockSpec((1, heads, tq), lambda i, s, qt, kt: (i, 0, qt[s]))]
        scratch = [pltpu.VMEM((heads, tq), F32), pltpu.VMEM((heads, tq), F32), acc]
    grid_spec = pltpu.PrefetchScalarGridSpec(
        num_scalar_prefetch=2, grid=(b, len(pairs)), in_specs=in_specs, out_specs=out_specs,
        scratch_shapes=scratch)
    return pl.pallas_call(
        functools.partial(kern, heads=heads, v_head=v_head, tq=tq, tk=tk, chunk=CHUNK),
        out_shape=out_shape,
        grid_spec=grid_spec,
        compiler_params=_params("arbitrary", "arbitrary"),
        name=name,
    )(qi_tab, ki_tab, *args)


def _latent_attn_kernel(q_ref, cp_ref, krp_ref, cn_ref, krn_ref, wk_ref, wv_ref, o_ref, *,
                        heads, nope, rope, v_head):
    q = q_ref[0]
    qabs, qrope = [], []
    for h in range(heads):
        qabs.append(_dot_nt(q[:, h * LANES:h * LANES + nope], wk_ref[h]))
        qrope.append(q[:, h * LANES + nope:h * LANES + nope + rope])
    qabs = jnp.concatenate(qabs, axis=0).astype(BF16)
    qrope = jnp.concatenate(qrope, axis=0)
    cp = cp_ref[0].astype(BF16)
    cn = cn_ref[0].astype(BF16)
    s_p = _dot_nt(qabs, cp) + _dot_nt(qrope, krp_ref[0].astype(BF16))
    s_n = _dot_nt(qabs, cn) + _dot_nt(qrope, krn_ref[0].astype(BF16))
    m = jnp.maximum(jnp.max(s_p, axis=-1, keepdims=True), jnp.max(s_n, axis=-1, keepdims=True))
    p_p = jnp.exp2(s_p - m)
    p_n = jnp.exp2(s_n - m)
    l = jnp.sum(p_p, axis=-1, keepdims=True) + jnp.sum(p_n, axis=-1, keepdims=True)
    o_lat = (_dot(p_p.astype(BF16), cp) + _dot(p_n.astype(BF16), cn)) / l
    t = q.shape[0]
    for h in range(heads):
        o_ref[0, :, h * v_head:(h + 1) * v_head] = _dot(o_lat[h * t:(h + 1) * t].astype(BF16), wv_ref[h])


def _latent_attn(q, c_past, kr_past, c_new, kr_new, wk, wv, *, heads, nope, rope, v_head):
    b, t, _ = q.shape
    past, kv_lora = c_past.shape[1:]
    blk = lambda n, last: pl.BlockSpec((1, n, last), lambda i: (i, 0, 0))
    const = lambda a: pl.BlockSpec(a.shape, lambda i: (0,) * a.ndim)
    kern = functools.partial(_latent_attn_kernel, heads=heads, nope=nope, rope=rope, v_head=v_head)
    return pl.pallas_call(
        kern,
        out_shape=jax.ShapeDtypeStruct((b, t, heads * v_head), F32),
        grid=(b,),
        in_specs=[blk(t, heads * LANES), blk(past, kv_lora), blk(past, rope), blk(t, kv_lora), blk(t, rope),
                  const(wk), const(wv)],
        out_specs=blk(t, heads * v_head),
        compiler_params=_params("arbitrary"),
        name="latent_attn",
    )(q, c_past, kr_past, c_new, kr_new, wk, wv)


def _mlstm_prep_kernel(m_ref, kt_ref, gc_ref, gr_ref, bc_ref, br_ref, pv_ref, kv_ref, b_ref, st_ref, *,
                       heads, dk, dv, chunk, nchunk):
    row = lax.broadcasted_iota(jnp.int32, (chunk, chunk), 0)
    col = lax.broadcasted_iota(jnp.int32, (chunk, chunk), 1)
    causal = col <= row
    tril = causal.astype(F32)
    triu = (row <= col).astype(F32)
    lane = lax.broadcasted_iota(jnp.int32, (chunk, LANES), 1)
    o_k, o_v = heads * dk, 2 * heads * dk

    rows = [slice(c * chunk, (c + 1) * chunk) for c in range(nchunk)]
    gc = [gc_ref[0, r, :] + bc_ref[...] for r in rows]
    gr = [gr_ref[0, c] + br_ref[...] for c in range(nchunk)]
    bcum_c = [_dot(tril, _log_sigmoid(g), HIGHEST) for g in gc]
    bcum_r = [_dot(_log_sigmoid(g), triu, HIGHEST) for g in gr]
    for c in range(nchunk):
        b_ref[0, rows[c], :] = bcum_c[c]
    ch = [(c, h) for c in range(nchunk) for h in range(heads)]
    v = [m_ref[0, rows[c], o_v + h * dv:o_v + (h + 1) * dv].astype(BF16) for c, h in ch]
    qk = [_dot_nt(m_ref[0, rows[c], h * dk:(h + 1) * dk].astype(BF16),
                  (m_ref[0, rows[c], o_k + h * dk:o_k + (h + 1) * dk] * (dk ** -0.5)).astype(BF16)) for c, h in ch]
    li_r = [gr[c][h:h + 1, :] for c, h in ch]
    b_r = [bcum_r[c][heads + h:heads + h + 1, :] for c, h in ch]
    dmat = [jnp.where(causal, bcum_c[c][:, heads + h:heads + h + 1] - b_r[i] + li_r[i], NEG_INF)
            for i, (c, h) in enumerate(ch)]
    mx = [jnp.max(d, axis=-1, keepdims=True) for d in dmat]
    p0 = [s * jnp.exp(d - m) for s, d, m in zip(qk, dmat, mx)]
    for i, (c, h) in enumerate(ch):
        pv_ref[0, rows[c], h * dv:(h + 1) * dv] = _dot(p0[i].astype(BF16), v[i])
    w_r = [jnp.exp(b_r[i][:, chunk - 1:chunk] - b_r[i] + li_r[i] - mx[i][chunk - 1:chunk, :]) * (dk ** -0.5)
           for i in range(len(ch))]
    for i, (c, h) in enumerate(ch):
        wkt = (kt_ref[0, h * dk:(h + 1) * dk, c * chunk:(c + 1) * chunk] * w_r[i]).astype(BF16)
        kv_ref[0, c, h] = _dot(wkt, jnp.concatenate([v[i], jnp.ones_like(v[i])], axis=1))
    psum = [jnp.sum(p, axis=-1, keepdims=True) for p in p0]
    for c in range(nchunk):
        stats = jnp.zeros((chunk, LANES), F32)
        for h in range(heads):
            stats = jnp.where(lane == h, mx[c * heads + h], stats)
            stats = jnp.where(lane == heads + h, psum[c * heads + h], stats)
        st_ref[0, rows[c], :] = stats


def _mlstm_scan_kernel(q_ref, mo_ref, pv_ref, kv_ref, b_ref, st_ref, gout_ref, c0_ref, m0_ref,
                       h_ref, c1_ref, m1_ref, c_sc, m_sc, *, heads, dk, dv, chunk, nchunk):
    t = pl.program_id(1)

    @pl.when(t == 0)
    def _():
        c_sc[...] = c0_ref[0]
        m_sc[...] = m0_ref[0]

    def body(c, carry):
        rows = pl.ds(pl.multiple_of(c * chunk, chunk), chunk)
        bcum = b_ref[0, rows, :]
        stats = st_ref[0, rows, :]
        rep = lambda col: jnp.broadcast_to(col, (chunk, dv))
        hs = range(heads)
        c2 = [c_sc[h] for h in hs]
        qc = [_dot(q_ref[0, rows, h * dk:(h + 1) * dk].astype(BF16), c2[h].astype(BF16)) for h in hs]
        mx = [rep(stats[:, h:h + 1]) for h in hs]
        inter = [rep(bcum[:, heads + h:heads + h + 1]) + m_sc[h] for h in hs]
        m = [jnp.maximum(inter[h], mx[h]) for h in hs]
        w_inter = [jnp.exp(inter[h] - m[h]) for h in hs]
        r = [jnp.exp(mx[h] - m[h]) for h in hs]
        for h in hs:
            decay_end = w_inter[h][chunk - 1:chunk, :]
            f_new = r[h][chunk - 1:chunk, :]
            c_sc[h] = (jnp.concatenate([decay_end, decay_end], axis=1) * c2[h]
                       + jnp.concatenate([f_new, f_new], axis=1) * kv_ref[0, c, h])
            m_sc[h] = m[h][chunk - 1:chunk, 0:1]
        num = [w_inter[h] * qc[h][:, :dv] + r[h] * pv_ref[0, rows, h * dv:(h + 1) * dv] for h in hs]
        den = [w_inter[h] * qc[h][:, dv:] + r[h] * rep(stats[:, heads + h:heads + h + 1]) for h in hs]
        hh = [num[h] / jnp.maximum(jnp.abs(den[h]), jnp.exp(-m[h])) for h in hs]
        hn = [_rms(hh[h], gout_ref[:, h * dv:(h + 1) * dv]) for h in hs]
        for h in hs:
            h_ref[0, rows, h * dv:(h + 1) * dv] = hn[h] * _sigmoid(mo_ref[0, rows, h * dv:(h + 1) * dv])
        return carry

    lax.fori_loop(0, nchunk, body, 0)

    @pl.when(t == pl.num_programs(1) - 1)
    def _():
        c1_ref[0] = c_sc[...]
        m1_ref[0] = m_sc[...]


def _mlstm(m_slab, gates_c, gates_r, bias_c, bias_r, gout, c0, n0, m0, *, tc, heads, dk, dv):
    b, t, mw = m_slab.shape
    chunk = min(CHUNK, t)
    const = lambda a: pl.BlockSpec(a.shape, lambda i, j: (0,) * a.ndim)
    qkv_w = 2 * heads * dk + heads * dv
    mo_blk, rem = divmod(qkv_w, heads * dv)
    assert rem == 0 and mw == qkv_w + heads * dv

    tp = min(t, MLSTM_PREP_TC)
    npc = tp // chunk
    tokp = lambda last: pl.BlockSpec((1, tp, last), lambda i, j: (i, j, 0))
    per_chunk = lambda *s: pl.BlockSpec((1, npc, heads) + s, lambda i, j: (i, j, 0) + (0,) * len(s))
    pv, kv, bcum, stats = pl.pallas_call(
        functools.partial(_mlstm_prep_kernel, heads=heads, dk=dk, dv=dv, chunk=chunk, nchunk=npc),
        out_shape=[jax.ShapeDtypeStruct((b, t, heads * dv), F32),
                   jax.ShapeDtypeStruct((b, t // chunk, heads, dk, 2 * dv), F32),
                   jax.ShapeDtypeStruct((b, t, LANES), F32),
                   jax.ShapeDtypeStruct((b, t, LANES), F32)],
        grid=(b, t // tp),
        in_specs=[tokp(qkv_w), pl.BlockSpec((1, heads * dk, tp), lambda i, j: (i, 0, j)), tokp(LANES),
                  pl.BlockSpec((1, npc, SUBLANES, chunk), lambda i, j: (i, j, 0, 0)),
                  const(bias_c), const(bias_r)],
        out_specs=[tokp(heads * dv), per_chunk(dk, 2 * dv), tokp(LANES), tokp(LANES)],
        compiler_params=_params("arbitrary", "arbitrary"),
        name="mlstm_prep",
    )(m_slab, m_slab[..., heads * dk:2 * heads * dk].transpose(0, 2, 1), gates_c, gates_r, bias_c, bias_r)

    nchunk = tc // chunk
    tok = lambda last, blk=0: pl.BlockSpec((1, tc, last), lambda i, j: (i, j, blk))
    per_chunk = lambda *s: pl.BlockSpec((1, nchunk, heads) + s, lambda i, j: (i, j, 0) + (0,) * len(s))
    st = lambda *s: pl.BlockSpec((1,) + s, lambda i, j: (i,) + (0,) * len(s))
    cn0 = jnp.concatenate([c0, jnp.broadcast_to(n0[..., None], c0.shape)], axis=-1)
    hm, cn1, m1 = pl.pallas_call(
        functools.partial(_mlstm_scan_kernel, heads=heads, dk=dk, dv=dv, chunk=chunk, nchunk=nchunk),
        out_shape=[jax.ShapeDtypeStruct((b, t, heads * dv), F32),
                   jax.ShapeDtypeStruct((b, heads, dk, 2 * dv), F32),
                   jax.ShapeDtypeStruct((b, heads, 1, 1), F32)],
        grid=(b, t // tc),
        in_specs=[tok(heads * dk),
                  tok(heads * dv, mo_blk),
                  tok(heads * dv), per_chunk(dk, 2 * dv), tok(LANES), tok(LANES),
                  const(gout), st(heads, dk, 2 * dv), st(heads, 1, 1)],
        out_specs=[tok(heads * dv), st(heads, dk, 2 * dv), st(heads, 1, 1)],
        scratch_shapes=[pltpu.VMEM((heads, dk, 2 * dv), F32), pltpu.VMEM((heads, 1, 1), F32)],
        compiler_params=_params("arbitrary", "arbitrary"),
        name="mlstm_scan",
    )(m_slab, m_slab, pv, kv, bcum, stats, gout, cn0, m0)
    return hm, cn1[..., :dv], cn1[..., dv], m1


def _mid_kernel(oa_ref, ob_ref, z_ref, x_ref, gate_ref, wo_ref, shift_ref, scale_ref, g_ref, w2_ref,
                x1_ref, qkv_ref, z2_ref, ab_ref, *, half, conv_ch, d_model):
    z = z_ref[0]
    ma = (oa_ref[0] * _silu(z[:, :half])).astype(BF16)
    mb = (ob_ref[0] * _silu(z[:, half:])).astype(BF16)
    y = _dot(ma, wo_ref[0:half, :]) + _dot(mb, wo_ref[half:, :])
    x1 = x_ref[0] + gate_ref[0] * y
    x1_ref[0] = x1
    hn = _rms(x1, g_ref[...]) * (1.0 + scale_ref[0]) + shift_ref[0]
    y2 = _dot(hn.astype(BF16), w2_ref[...])
    qkv_ref[0] = y2[:, :conv_ch]
    z2_ref[0] = y2[:, conv_ch:conv_ch + d_model]
    ab_ref[0] = y2[:, conv_ch + d_model:]


def _mid(oa, ob, z, x, gate, wo, shift, scale, g, w2, *, tm, conv_ch):
    b, t, d = x.shape
    half = oa.shape[-1]
    tok = lambda last: pl.BlockSpec((1, tm, last), lambda i, j: (i, j, 0))
    vec = pl.BlockSpec((1, 1, d), lambda i, j: (i, 0, 0))
    const = lambda a: pl.BlockSpec(a.shape, lambda i, j: (0,) * a.ndim)
    kern = functools.partial(_mid_kernel, half=half, conv_ch=conv_ch, d_model=d)
    return pl.pallas_call(
        kern,
        out_shape=[jax.ShapeDtypeStruct((b, t, d), F32), jax.ShapeDtypeStruct((b, t, conv_ch), F32),
                   jax.ShapeDtypeStruct((b, t, d), F32), jax.ShapeDtypeStruct((b, t, LANES), F32)],
        grid=(b, t // tm),
        in_specs=[tok(half), tok(ob.shape[-1]), tok(d), tok(d), vec, const(wo), vec, vec, const(g), const(w2)],
        out_specs=[tok(d), tok(conv_ch), tok(d), tok(LANES)],
        compiler_params=_params("arbitrary", "arbitrary"),
        name="out_a_in_c",
    )(oa, ob, z, x, gate, wo, shift, scale, g, w2)


def _conv_kernel(qkv_ref, past_ref, wc_ref, ab_ref, alog_ref, dtb_ref, act_ref, gb_ref, ext_sc, *,
                 tm, width, heads, dk):
    @pl.when(pl.program_id(1) == 0)
    def _():
        ext_sc[0:SUBLANES, :] = past_ref[0]

    ext_sc[SUBLANES:SUBLANES + tm, :] = qkv_ref[0]
    conv = wc_ref[width - 1:width, :] * ext_sc[SUBLANES:SUBLANES + tm, :]
    for j in range(width - 1):
        s = SUBLANES - (width - 1) + j
        conv = conv + wc_ref[j:j + 1, :] * ext_sc[s:s + tm, :]
    ext_sc[0:SUBLANES, :] = ext_sc[tm:tm + SUBLANES, :]
    act = _silu(conv)
    for h in range(2 * heads):
        xh = act[:, h * dk:(h + 1) * dk]
        xh = xh * lax.rsqrt(jnp.sum(xh * xh, axis=-1, keepdims=True) + EPS)
        if h < heads:
            xh = xh * (dk ** -0.5)
        act_ref[0, :, h * dk:(h + 1) * dk] = xh
    act_ref[0, :, 2 * heads * dk:] = act[:, 2 * heads * dk:]
    ab = ab_ref[0]
    g = -jnp.exp(alog_ref[...]) * _softplus(ab + dtb_ref[...])
    lane = lax.broadcasted_iota(jnp.int32, ab.shape, 1)
    gb_ref[0] = jnp.where(lane < heads, g, _sigmoid(ab))


def _conv(qkv, past8, wc8, ab, alog, dtb, *, tm, width, heads, dk):
    b, t, ch = qkv.shape
    tok = lambda last: pl.BlockSpec((1, tm, last), lambda i, j: (i, j, 0))
    const = lambda a: pl.BlockSpec(a.shape, lambda i, j: (0,) * a.ndim)
    kern = functools.partial(_conv_kernel, tm=tm, width=width, heads=heads, dk=dk)
    return pl.pallas_call(
        kern,
        out_shape=[jax.ShapeDtypeStruct((b, t, ch), F32), jax.ShapeDtypeStruct((b, t, LANES), F32)],
        grid=(b, t // tm),
        in_specs=[tok(ch), pl.BlockSpec((1, SUBLANES, ch), lambda i, j: (i, 0, 0)), const(wc8), tok(LANES),
                  const(alog), const(dtb)],
        out_specs=[tok(ch), tok(LANES)],
        scratch_shapes=[pltpu.VMEM((tm + SUBLANES, ch), F32)],
        compiler_params=_params("arbitrary", "arbitrary"),
        name="conv_gates",
    )(qkv, past8, wc8, ab, alog, dtb)


def _blockdiag(x, group, chunk):
    w = group * chunk
    br = lax.broadcasted_iota(jnp.int32, (w, w), 0) // chunk
    bc = lax.broadcasted_iota(jnp.int32, (w, w), 1) // chunk
    xb = x.astype(BF16)
    return jnp.where(br == bc, jnp.concatenate([xb] * group, axis=0), jnp.zeros((), BF16))


def _unit_lower_inverses_minus_eye(a_list, group, chunk):
    w = group * chunk
    r = lax.broadcasted_iota(jnp.int32, (chunk, w), 0)
    cc = lax.broadcasted_iota(jnp.int32, (chunk, w), 1) % chunk
    es = [-jnp.where((r // 2 == cc // 2) & (r % 2 == 1) & (cc % 2 == 0), a4, 0.0) for a4 in a_list]
    s = 2
    while s < chunk:
        off = (r // (2 * s) == cc // (2 * s)) & (r % (2 * s) >= s) & (cc % (2 * s) < s)
        a_offs = [jnp.where(off, a4, 0.0) for a4 in a_list]
        ps = [a + _dot(a.astype(BF16), _blockdiag(e, group, chunk)) for a, e in zip(a_offs, es)]
        es = [e - (p + _dot(e.astype(BF16), _blockdiag(p, group, chunk))) for e, p in zip(es, ps)]
        s *= 2
    return es


def _gdn_prep_kernel(act_ref, gbc_ref, gbr_ref, w_ref, uv_ref, kd_ref, attn_ref, eg_ref, *,
                     heads, dk, dv, chunk, nchunk, group):
    row = lax.broadcasted_iota(jnp.int32, (chunk, chunk), 0)
    col = lax.broadcasted_iota(jnp.int32, (chunk, chunk), 1)
    incl = col <= row
    strict = col < row
    tril = incl.astype(F32)
    triu = (row <= col).astype(F32)
    lane = lax.broadcasted_iota(jnp.int32, (chunk, LANES), 1)
    o_k, o_v = heads * dk, 2 * heads * dk

    rows = [slice(c * chunk, (c + 1) * chunk) for c in range(nchunk)]
    gbc = [gbc_ref[0, r, :] for r in rows]
    gcum_c = [_dot(tril, g, HIGHEST) for g in gbc]
    gcum_r = [_dot(gbr_ref[0, c], triu, HIGHEST) for c in range(nchunk)]
    for c in range(nchunk):
        eg_ref[0, rows[c], :] = jnp.where(lane < heads, jnp.exp(gcum_c[c]), 0.0)
    ch = [(c, h) for c in range(nchunk) for h in range(heads)]
    k = [act_ref[0, rows[c], o_k + h * dk:o_k + (h + 1) * dk] for c, h in ch]
    kb = [x.astype(BF16) for x in k]
    kk = [_dot_nt(x, x) for x in kb]
    qk = [_dot_nt(act_ref[0, rows[c], h * dk:(h + 1) * dk].astype(BF16), kb[i]) for i, (c, h) in enumerate(ch)]
    g_c = [gcum_c[c][:, h:h + 1] for c, h in ch]
    beta = [gbc[c][:, heads + h:heads + h + 1] for c, h in ch]
    decay = [jnp.exp(jnp.where(incl, g_c[i] - gcum_r[c][h:h + 1, :], NEG_INF)) for i, (c, h) in enumerate(ch)]
    a_blk = [jnp.where(strict, beta[i] * kk[i] * decay[i], 0.0) for i in range(len(ch))]
    for i, (c, h) in enumerate(ch):
        attn_ref[0, rows[c], h * chunk:(h + 1) * chunk] = (qk[i] * decay[i]).astype(BF16)
        kd_ref[0, rows[c], h * dk:(h + 1) * dk] = (k[i] * jnp.exp(g_c[i][chunk - 1:chunk, :] - g_c[i])).astype(BF16)
    rhs_blk = [jnp.concatenate([beta[i] * act_ref[0, rows[c], o_v + h * dv:o_v + (h + 1) * dv],
                                (beta[i] * jnp.exp(g_c[i])) * k[i]], axis=1) for i, (c, h) in enumerate(ch)]
    problems = [(c, g0) for c in range(nchunk) for g0 in range(0, heads, group)]
    a_list = [jnp.concatenate(a_blk[c * heads + g0:c * heads + g0 + group], axis=1) for c, g0 in problems]
    rhs_list = [jnp.concatenate(rhs_blk[c * heads + g0:c * heads + g0 + group], axis=0) for c, g0 in problems]
    e_list = _unit_lower_inverses_minus_eye(a_list, group, chunk)
    sols = [rhs + _dot(_blockdiag(e, group, chunk), rhs.astype(BF16)) for e, rhs in zip(e_list, rhs_list)]
    for (c, g0), sol in zip(problems, sols):
        for i, h in enumerate(range(g0, g0 + group)):
            uv_ref[0, rows[c], h * dv:(h + 1) * dv] = sol[i * chunk:(i + 1) * chunk, :dv]
            w_ref[0, rows[c], h * dk:(h + 1) * dk] = sol[i * chunk:(i + 1) * chunk, dv:].astype(BF16)


def _gdn_prep(act, gb_c, gb_r, *, tc, heads, dk, dv):
    b, t, _ = act.shape
    chunk = min(CHUNK, t)
    tc = min(tc, t)
    nchunk = tc // chunk
    group = (2 * LANES) // chunk
    kern = functools.partial(_gdn_prep_kernel, heads=heads, dk=dk, dv=dv, chunk=chunk, nchunk=nchunk, group=group)
    tok = lambda last: pl.BlockSpec((1, tc, last), lambda i, j: (i, j, 0))
    return pl.pallas_call(
        kern,
        out_shape=[jax.ShapeDtypeStruct((b, t, heads * dk), BF16), jax.ShapeDtypeStruct((b, t, heads * dv), F32),
                   jax.ShapeDtypeStruct((b, t, heads * dk), BF16), jax.ShapeDtypeStruct((b, t, heads * chunk), BF16),
                   jax.ShapeDtypeStruct((b, t, LANES), F32)],
        grid=(b, t // tc),
        in_specs=[tok(act.shape[-1]), tok(LANES),
                  pl.BlockSpec((1, nchunk, 2 * SUBLANES, chunk), lambda i, j: (i, j, 0, 0))],
        out_specs=[tok(heads * dk), tok(heads * dv), tok(heads * dk), tok(heads * chunk), tok(LANES)],
        compiler_params=_params("arbitrary", "arbitrary"),
        name="gdn_prep",
    )(act, gb_c, gb_r)


def _gdn_scan_kernel(q_ref, w_ref, uv_ref, kd_ref, attn_ref, eg_ref, gout_ref, s0_ref, o_ref, s1_ref, s_sc, *,
                     heads, dk, dv, chunk, nchunk):
    t = pl.program_id(1)

    @pl.when(t == 0)
    def _():
        s_sc[...] = s0_ref[0]

    def body(c, carry):
        rows = pl.ds(pl.multiple_of(c * chunk, chunk), chunk)
        eg = eg_ref[0, rows, :]
        hs = range(heads)
        s0 = [s_sc[h] for h in hs]
        s0b = [s.astype(BF16) for s in s0]
        ws = [_dot(w_ref[0, rows, h * dk:(h + 1) * dk], s0b[h]) for h in hs]
        ub = [(uv_ref[0, rows, h * dv:(h + 1) * dv] - ws[h]).astype(BF16) for h in hs]
        eg_h = [jnp.broadcast_to(eg[:, h:h + 1], (chunk, dv)) for h in hs]
        ku = [_dot_tn(kd_ref[0, rows, h * dk:(h + 1) * dk], ub[h]) for h in hs]
        for h in hs:
            s_sc[h] = eg_h[h][chunk - 1:chunk, :] * s0[h] + ku[h]
        qs = [_dot(q_ref[0, rows, h * dk:(h + 1) * dk].astype(BF16), s0b[h]) for h in hs]
        au = [_dot(attn_ref[0, rows, h * chunk:(h + 1) * chunk], ub[h]) for h in hs]
        on = [_rms(eg_h[h] * qs[h] + au[h], gout_ref[...]) for h in hs]
        for h in hs:
            o_ref[0, rows, h * dv:(h + 1) * dv] = on[h]
        return carry

    lax.fori_loop(0, nchunk, body, 0)

    @pl.when(t == pl.num_programs(1) - 1)
    def _():
        s1_ref[0] = s_sc[...]


def _gdn_scan(act, w, uv, kd, attn, eg, gout, s0, *, tc, heads, dk, dv):
    b, t, _ = act.shape
    chunk = min(CHUNK, t)
    nchunk = tc // chunk
    kern = functools.partial(_gdn_scan_kernel, heads=heads, dk=dk, dv=dv, chunk=chunk, nchunk=nchunk)
    tok = lambda last: pl.BlockSpec((1, tc, last), lambda i, j: (i, j, 0))
    state = pl.BlockSpec((1, heads, dk, dv), lambda i, j: (i, 0, 0, 0))
    return pl.pallas_call(
        kern,
        out_shape=[jax.ShapeDtypeStruct((b, t, heads * dv), F32), jax.ShapeDtypeStruct(s0.shape, F32)],
        grid=(b, t // tc),
        in_specs=[tok(heads * dk),
                  tok(heads * dk), tok(heads * dv), tok(heads * dk), tok(heads * chunk), tok(LANES),
                  pl.BlockSpec(gout.shape, lambda i, j: (0, 0)), state],
        out_specs=[tok(heads * dv), state],
        scratch_shapes=[pltpu.VMEM((heads, dk, dv), F32)],
        compiler_params=_params("arbitrary", "arbitrary"),
        name="gdn_scan",
    )(act, w, uv, kd, attn, eg, gout, s0)


def _final_kernel(o_ref, z_ref, x_ref, gate_ref, wo_ref, g_ref, y_ref):
    mixed = (o_ref[0] * _silu(z_ref[0])).astype(BF16)
    x2 = x_ref[0] + gate_ref[0] * _dot(mixed, wo_ref[...])
    y_ref[0] = _rms(x2, g_ref[...])


def _final(o, z, x, gate, wo, g, *, tm):
    b, t, d = x.shape
    tok = lambda last: pl.BlockSpec((1, tm, last), lambda i, j: (i, j, 0))
    const = lambda a: pl.BlockSpec(a.shape, lambda i, j: (0,) * a.ndim)
    return pl.pallas_call(
        _final_kernel,
        out_shape=jax.ShapeDtypeStruct((b, t, d), F32),
        grid=(b, t // tm),
        in_specs=[tok(o.shape[-1]), tok(d), tok(d), pl.BlockSpec((1, 1, d), lambda i, j: (i, 0, 0)),
                  const(wo), const(g)],
        out_specs=tok(d),
        compiler_params=_params("arbitrary", "arbitrary"),
        name="out_c_final",
    )(o, z, x, gate, wo, g)


def _pad_lanes(w, width=LANES, at=0):
    out = jnp.zeros(w.shape[:-1] + (width,), w.dtype)
    return out.at[..., at:at + w.shape[-1]].set(w)


def _rot_half_cols(w):
    r = w.shape[-1] // 2
    return jnp.concatenate([-w[..., r:], w[..., :r]], axis=-1)


def _rope_tables(pos, rope, scale):
    freqs = jnp.exp(jnp.arange(0, rope, 2, dtype=F32) * (-math.log(ROPE_BASE) / rope))
    ang = pos.astype(F32)[:, None] * freqs[None, :]
    cos = jnp.concatenate([jnp.cos(ang), jnp.cos(ang)], axis=-1)
    sin = jnp.concatenate([jnp.sin(ang), jnp.sin(ang)], axis=-1)
    half = LANES // 2
    cosk = _pad_lanes(cos, at=half)
    sink = _pad_lanes(sin, at=half)
    ones = _pad_lanes(jnp.ones((pos.shape[0], half), F32))
    return jnp.concatenate([(cosk + ones) * scale, sink * scale, cosk, sink], axis=-1)


def _tokens_on_lanes(a, chunk, rows):
    b, t = a.shape[:2]
    return a[..., :rows].reshape(b, t // chunk, chunk, rows).transpose(0, 1, 3, 2)


def kernel(x_prompt, x_sample, c_prompt, c_sample, cache_kv_latent, cache_k_rope, state_mlstm_C, state_mlstm_n, state_mlstm_m, state_gdn_S, state_gdn_conv, a_w_ada, a_b_ada, a_g_norm, a_w_in, a_g_q_a, a_w_q_b, a_g_kv_a, a_w_kv_b, a_b_i, a_b_f, a_g_out, a_w_out, c_w_ada, c_b_ada, c_g_norm, c_w_in, c_w_conv, c_a_log, c_dt_bias, c_g_out, c_w_out, g_final):
    d = x_prompt.shape[-1]
    q_lora, heads, qk = a_w_q_b.shape
    kv_lora = a_w_kv_b.shape[0]
    rope = cache_k_rope.shape[-1]
    nope = qk - rope
    v_head = a_w_kv_b.shape[2] - nope
    m_heads, m_dv = a_g_out.shape
    m_dk = state_mlstm_C.shape[2]
    g_heads = c_a_log.shape[0]
    g_dk, g_dv = state_gdn_S.shape[2:]
    width = c_w_conv.shape[0]
    conv_ch = c_w_conv.shape[1]
    assert nope + rope <= LANES and nope == LANES // 2 and 2 * m_heads <= SUBLANES and 2 * g_heads <= 2 * SUBLANES

    sizes = (q_lora, kv_lora, rope, m_heads * m_dk, m_heads * m_dk, m_heads * m_dv, m_heads, m_heads,
             m_heads * m_dv, heads * v_head + m_heads * m_dv)
    offs = [0]
    for s in sizes:
        offs.append(offs[-1] + s)
    w_qa, w_c, w_kr, w_mq, w_mk, w_mv, w_mi, w_mf, w_mo, w_z = [a_w_in[:, offs[i]:offs[i + 1]] for i in range(10)]
    half = LANES // 2
    w1 = jnp.concatenate([w_qa, w_c, _pad_lanes(w_kr, at=half), _pad_lanes(_rot_half_cols(w_kr), at=half),
                          w_mq, w_mk, w_mv, w_mo, w_z, _pad_lanes(jnp.concatenate([w_mi, w_mf], axis=1))],
                         axis=1).astype(BF16)
    m_width = 2 * m_heads * m_dk + 2 * m_heads * m_dv
    wq_rope = a_w_q_b[..., nope:]
    wq_main = _pad_lanes(a_w_q_b).reshape(q_lora, heads * LANES)
    wq_rot = _pad_lanes(_rot_half_cols(wq_rope), at=nope).reshape(q_lora, heads * LANES)
    wq = jnp.concatenate([wq_main, wq_rot], axis=1).astype(BF16)
    wkv = jnp.concatenate([_pad_lanes(a_w_kv_b[..., :nope]).reshape(kv_lora, heads * LANES),
                           a_w_kv_b[..., nope:].reshape(kv_lora, heads * v_head)], axis=1).astype(BF16)
    wk_abs = a_w_kv_b[..., :nope].transpose(1, 0, 2).astype(BF16)
    wv_abs = a_w_kv_b[..., nope:].transpose(1, 0, 2).astype(BF16)
    wo_a = a_w_out.astype(BF16)
    csz = (conv_ch, g_heads, g_heads, g_heads * g_dv)
    w_qkv, w_a, w_b, w_zc = [c_w_in[:, sum(csz[:i]):sum(csz[:i + 1])] for i in range(4)]
    w2 = jnp.concatenate([w_qkv, w_zc, _pad_lanes(jnp.concatenate([w_a, w_b], axis=1))], axis=1).astype(BF16)
    wo_c = c_w_out.astype(BF16)
    wc8 = jnp.zeros((SUBLANES, conv_ch), F32).at[:width].set(c_w_conv)
    row = lambda a: a.reshape(1, -1).astype(F32)
    bias_c = _pad_lanes(jnp.concatenate([a_b_i, a_b_f]).reshape(1, -1))
    bias_r = jnp.zeros((SUBLANES, 1), F32).at[:2 * m_heads, 0].set(jnp.concatenate([a_b_i, a_b_f]))
    alog = _pad_lanes(c_a_log.reshape(1, -1))
    dtb = _pad_lanes(c_dt_bias.reshape(1, -1))

    bp, bs = c_prompt.shape[0], c_sample.shape[0]
    c_all = jnp.concatenate([c_prompt, c_sample], axis=0)
    pad = (-c_all.shape[0]) % SUBLANES
    c_all = jnp.pad(c_all, ((0, pad), (0, 0)))
    mod_a = _adaln(c_all, a_w_ada, a_b_ada)
    mod_c = _adaln(c_all, c_w_ada, c_b_ada)

    def mods(mod, lo, hi):
        return [mod[lo:hi, i * d:(i + 1) * d][:, None, :] for i in range(3)]

    def run(x, mod_lo, mod_hi, c_past, kr_past, c0, n0, m0, conv0, s0):
        b, t, _ = x.shape
        past = 0 if c_past is None else c_past.shape[1]
        chunk = min(CHUNK, t)
        tm = min(t, 256)
        tc = min(t, 512)
        shift_a, scale_a, gate_a = mods(mod_a, mod_lo, mod_hi)
        shift_c, scale_c, gate_c = mods(mod_c, mod_lo, mod_hi)
        tab = _rope_tables(past + jnp.arange(t, dtype=jnp.int32), rope, qk ** -0.5 * math.log2(math.e))
        expand = c_past is None
        outs = _in_a(x, shift_a, scale_a, row(a_g_norm), w1, row(a_g_q_a), wq, row(a_g_kv_a), wkv, tab,
                     tm=tm, heads=heads, q_lora=q_lora, kv_lora=kv_lora, rope=rope, m_width=m_width,
                     v_head=v_head, expand_kv=expand)
        q, c_new, kr_new, m_slab, z, gates = outs[:6]
        if expand:
            k_all, vt = outs[6], outs[7].transpose(0, 2, 1)
            qn = outs[8][..., :heads].transpose(0, 2, 1)
            tiles = dict(heads=heads, v_head=v_head, tq=min(t, FLASH_TQ), tk=min(t, FLASH_TK))
            o_fast, row_sums = _flash(q, k_all, vt, bound=(qn, outs[9]), **tiles)
            o_mla = lax.cond(jnp.min(row_sums) >= FLASH_ROW_SUM_MIN,
                             lambda: o_fast, lambda: _flash(q, k_all, vt, **tiles))
        else:
            o_mla = _latent_attn(q, c_past, kr_past, c_new, kr_new, wk_abs, wv_abs,
                                 heads=heads, nope=nope, rope=rope, v_head=v_head)
        hm, c1, n1, m1 = _mlstm(m_slab, gates, _tokens_on_lanes(gates, chunk, SUBLANES), bias_c, bias_r,
                                row(a_g_out), c0, n0, m0.reshape(b, m_heads, 1, 1),
                                tc=tc, heads=m_heads, dk=m_dk, dv=m_dv)
        x1, qkv, zc, ab = _mid(o_mla, hm, z, x, gate_a, wo_a, shift_c, scale_c, row(c_g_norm), w2,
                               tm=tm, conv_ch=conv_ch)
        past8 = jnp.pad(conv0, ((0, 0), (SUBLANES - (width - 1), 0), (0, 0)))
        act, gb = _conv(qkv, past8, wc8, ab, alog, dtb, tm=tm, width=width, heads=g_heads, dk=g_dk)
        w, uv, kd, attn, eg = _gdn_prep(act, gb, _tokens_on_lanes(gb, chunk, 2 * SUBLANES),
                                        tc=GDN_PREP_TC, heads=g_heads, dk=g_dk, dv=g_dv)
        o_gdn, s1 = _gdn_scan(act, w, uv, kd, attn, eg, row(c_g_out), s0, tc=tc, heads=g_heads, dk=g_dk, dv=g_dv)
        y = _final(o_gdn, zc, x1, gate_c, wo_c, row(g_final), tm=tm)
        conv1 = jnp.concatenate([conv0, qkv], axis=1)[:, t:] if t < width - 1 else qkv[:, t - (width - 1):]
        return (y, c_new, kr_new, c1, n1, m1.reshape(b, m_heads), conv1, s1)

    dt = x_prompt.dtype
    (y_p, p_kv, p_kr, p_c, p_n, p_m, p_conv, p_s) = run(
        x_prompt, 0, bp, None, None,
        jnp.zeros((bp, m_heads, m_dk, m_dv), dt), jnp.zeros((bp, m_heads, m_dk), dt), jnp.zeros((bp, m_heads), dt),
        jnp.zeros((bp, width - 1, conv_ch), dt), jnp.zeros((bp, g_heads, g_dk, g_dv), dt))
    (y_s, s_kv, s_kr, s_c, s_n, s_m, s_conv, s_s) = run(
        x_sample, bp, bp + bs, cache_kv_latent, cache_k_rope, state_mlstm_C, state_mlstm_n, state_mlstm_m,
        state_gdn_conv, state_gdn_S)
    return (y_p, y_s, p_kv, p_kr, p_c, p_n, p_m, p_s, p_conv,
            s_kv, s_kr, s_c, s_n, s_m, s_s, s_conv)
```

```python
import functools
import math

import jax
import jax.numpy as jnp
from jax import lax
from jax.experimental import pallas as pl
from jax.experimental.pallas import tpu as pltpu

F32 = jnp.float32
BF16 = jnp.bfloat16
HIGHEST = lax.Precision.HIGHEST

CHUNK = 64
EPS = 1e-6
ROPE_BASE = 10000.0
LANES = 128
SUBLANES = 8
VMEM_LIMIT = 56 * 1024 * 1024
NEG_INF = float("-inf")
PROJ_TM = 256
FLASH_TQ = 512
FLASH_TK = 512
FLASH_HEAD_GROUP = 4
FLASH_BOUND_SLACK = 1.0 + 2.0 ** -6
FLASH_ROW_SUM_MIN = 2.0 ** -100
MLSTM_PREP_TC = 256
GDN_PREP_TC = 256


def _params(*sem):
    return pltpu.CompilerParams(dimension_semantics=sem, vmem_limit_bytes=VMEM_LIMIT)


def _dot(a, b, precision=None):
    return jnp.dot(a, b, preferred_element_type=F32, precision=precision)


def _dot_nt(a, b):
    return lax.dot_general(a, b, (((1,), (1,)), ((), ())), preferred_element_type=F32)


def _dot_tn(a, b):
    return lax.dot_general(a, b, (((0,), (0,)), ((), ())), preferred_element_type=F32)


def _rms(x, g):
    return x * lax.rsqrt(jnp.mean(x * x, axis=-1, keepdims=True) + EPS) * g


def _per_row(m, rows):
    n, d = m.shape
    if n == 1:
        return m
    return jnp.concatenate([jnp.broadcast_to(m[i:i + 1], (rows // n, d)) for i in range(n)], axis=0)


def _sigmoid(x):
    return 0.5 * jnp.tanh(0.5 * x) + 0.5


def _silu(x):
    return x * _sigmoid(x)


def _softplus(x):
    return jnp.maximum(x, 0.0) + jnp.log1p(jnp.exp(-jnp.abs(x)))


def _log_sigmoid(x):
    return -_softplus(-x)


def _adaln_kernel(c_ref, w_ref, b_ref, o_ref):
    o_ref[...] = _dot(_silu(c_ref[...]), w_ref[...], HIGHEST) + b_ref[...]


def _adaln(c, w, b):
    n, d = c.shape
    d3 = w.shape[1]
    return pl.pallas_call(
        _adaln_kernel,
        out_shape=jax.ShapeDtypeStruct((n, d3), F32),
        grid=(d3 // d,),
        in_specs=[pl.BlockSpec((n, d), lambda j: (0, 0)),
                  pl.BlockSpec((d, d), lambda j: (0, j)),
                  pl.BlockSpec((1, d), lambda j: (0, j))],
        out_specs=pl.BlockSpec((n, d), lambda j: (0, j)),
        compiler_params=_params("arbitrary"),
        name="adaln",
    )(c, w, b.reshape(1, d3))


def _in_a_kernel(x_ref, shift_ref, scale_ref, g_ref, w1_ref, gq_ref, wq_ref, gkv_ref, wkv_ref, tab_ref,
                 q_ref, c_ref, kr_ref, m_ref, z_ref, gt_ref, *kv_refs,
                 heads, q_lora, kv_lora, rope, m_width, mk_cols, d_model, v_head, expand_kv):
    x = x_ref[0]
    hn = _rms(x, g_ref[...]) * (1.0 + _per_row(scale_ref[0], x_ref.shape[1])) + _per_row(shift_ref[0], x_ref.shape[1])
    y = _dot(hn.astype(BF16), w1_ref[...])
    o = 0
    qa = y[:, o:o + q_lora]; o += q_lora
    cl = y[:, o:o + kv_lora]; o += kv_lora
    kr1 = y[:, o:o + LANES]; o += LANES
    kr2 = y[:, o:o + LANES]; o += LANES
    m_ref[0] = y[:, o:o + m_width]
    mk = y[:, o + mk_cols[0]:o + mk_cols[1]]
    o += m_width
    z_ref[0] = y[:, o:o + d_model]; o += d_model
    gt_ref[0] = y[:, o:o + LANES]

    tab = tab_ref[...]
    cosq, sinq = tab[:, 0:LANES], tab[:, LANES:2 * LANES]
    cosk, sink = tab[:, 2 * LANES:3 * LANES], tab[:, 3 * LANES:4 * LANES]

    def sq_norm(xb):
        xf = xb.astype(F32)
        return jnp.sum(xf * xf, axis=-1, keepdims=True)

    lane = lax.broadcasted_iota(jnp.int32, (x.shape[0], LANES), 1)
    qq = _dot(_rms(qa, gq_ref[...]).astype(BF16), wq_ref[...])
    hw = heads * LANES
    qn2 = jnp.zeros((x.shape[0], LANES), F32)
    for h in range(heads):
        sl = slice(h * LANES, (h + 1) * LANES)
        qb = (qq[:, sl] * cosq + qq[:, hw + h * LANES:hw + (h + 1) * LANES] * sinq).astype(BF16)
        q_ref[0, :, sl] = qb
        if expand_kv:
            qn2 = jnp.where(lane == h, sq_norm(qb), qn2)

    cn = _rms(cl, gkv_ref[...])
    c_ref[0] = cn
    kr = kr1 * cosk + kr2 * sink
    kr_ref[0] = kr[:, LANES // 2:LANES // 2 + rope]
    if expand_kv:
        k_ref, vt_ref, qn_ref, kmax_ref, kt_ref = kv_refs
        kv = _dot(cn.astype(BF16), wkv_ref[...])
        kn2 = jnp.zeros((x.shape[0], LANES), F32)
        for h in range(heads):
            sl = slice(h * LANES, (h + 1) * LANES)
            kb = (kv[:, sl] + kr).astype(BF16)
            k_ref[0, :, sl] = kb
            kn2 = jnp.where(lane == h, sq_norm(kb), kn2)
        vt_ref[0] = kv[:, hw:hw + heads * v_head].T.astype(BF16)
        kt_ref[0] = mk.T
        qn_ref[0] = jnp.sqrt(qn2)
        kmax_ref[0, 0] = jnp.sqrt(jnp.max(kn2, axis=0, keepdims=True))


def _in_a(x, shift, scale, g, w1, gq, wq, gkv, wkv, tab, *, tm, heads, q_lora, kv_lora, rope, m_width,
          mk_cols, v_head, expand_kv):
    b, t, d = x.shape
    grid = (b, t // tm)
    tok = lambda last: pl.BlockSpec((1, tm, last), lambda i, j: (i, j, 0))
    tok_t = lambda rows: pl.BlockSpec((1, rows, tm), lambda i, j: (i, 0, j))
    const = lambda a: pl.BlockSpec(a.shape, lambda i, j: (0,) * a.ndim)
    out_shape = [jax.ShapeDtypeStruct((b, t, heads * LANES), BF16),
                 jax.ShapeDtypeStruct((b, t, kv_lora), F32),
                 jax.ShapeDtypeStruct((b, t, rope), F32),
                 jax.ShapeDtypeStruct((b, t, m_width), F32),
                 jax.ShapeDtypeStruct((b, t, d), F32),
                 jax.ShapeDtypeStruct((b, t, LANES), F32)]
    out_specs = [tok(heads * LANES), tok(kv_lora), tok(rope), tok(m_width), tok(d), tok(LANES)]
    if expand_kv:
        out_shape += [jax.ShapeDtypeStruct((b, t, heads * LANES), BF16),
                      jax.ShapeDtypeStruct((b, heads * v_head, t), BF16),
                      jax.ShapeDtypeStruct((b, t, LANES), F32),
                      jax.ShapeDtypeStruct((b, t // tm, 1, LANES), F32),
                      jax.ShapeDtypeStruct((b, mk_cols[1] - mk_cols[0], t), F32)]
        out_specs += [tok(heads * LANES), tok_t(heads * v_head), tok(LANES),
                      pl.BlockSpec((1, 1, 1, LANES), lambda i, j: (i, j, 0, 0)), tok_t(mk_cols[1] - mk_cols[0])]
    kern = functools.partial(_in_a_kernel, heads=heads, q_lora=q_lora, kv_lora=kv_lora, rope=rope,
                             m_width=m_width, mk_cols=mk_cols, d_model=d, v_head=v_head, expand_kv=expand_kv)
    return pl.pallas_call(
        kern, out_shape=out_shape, grid=grid,
        in_specs=[tok(d),
                  pl.BlockSpec((1,) + shift.shape[1:], lambda i, j: (i, 0, 0)),
                  pl.BlockSpec((1,) + scale.shape[1:], lambda i, j: (i, 0, 0)),
                  const(g), const(w1), const(gq), const(wq), const(gkv), const(wkv),
                  pl.BlockSpec((tm, 4 * LANES), lambda i, j: (j, 0))],
        out_specs=out_specs,
        compiler_params=_params("arbitrary", "arbitrary"),
        name="in_proj_a",
    )(x, shift, scale, g, w1, gq, wq, gkv, wkv, tab)


def _flash_kernel(qi_ref, ki_ref, q_ref, k_ref, vt_ref, o_ref, m_sc, l_sc, acc_sc, *,
                  heads, v_head, tq, tk, chunk):
    step_id = pl.program_id(1)
    qi = qi_ref[step_id]
    ki = ki_ref[step_id]

    @pl.when(ki == 0)
    def _():
        m_sc[...] = jnp.full(m_sc.shape, NEG_INF, F32)
        l_sc[...] = jnp.zeros(l_sc.shape, F32)
        acc_sc[...] = jnp.zeros(acc_sc.shape, F32)

    def step(masked):
        if masked:
            kc = (ki * tk + lax.broadcasted_iota(jnp.int32, (tk, tq), 0)) // chunk
            qc = (qi * tq + lax.broadcasted_iota(jnp.int32, (tk, tq), 1)) // chunk
            mask = kc <= qc
        for h in range(heads):
            qh = q_ref[0, :, h * LANES:(h + 1) * LANES]
            kh = k_ref[0, :, h * LANES:(h + 1) * LANES]
            vth = vt_ref[0, h * v_head:(h + 1) * v_head, :]
            rows = slice(h * v_head, (h + 1) * v_head)
            st = _dot_nt(kh, qh)
            if masked:
                st = jnp.where(mask, st, NEG_INF)
            m_prev = m_sc[h]
            m_new = jnp.maximum(m_prev, jnp.max(st, axis=0, keepdims=True))
            alpha = jnp.exp2(m_prev - m_new)
            p = jnp.exp2(st - m_new)
            l_sc[h] = alpha * l_sc[h] + jnp.sum(p, axis=0, keepdims=True)
            acc_sc[rows, :] = alpha * acc_sc[rows, :] + _dot(vth, p.astype(BF16))
            m_sc[h] = m_new

    full = (ki + 1) * tk <= qi * tq + chunk

    @pl.when(full)
    def _():
        step(False)

    @pl.when(jnp.logical_not(full))
    def _():
        step(True)

    @pl.when(ki == ((qi + 1) * tq - 1) // tk)
    def _():
        for h in range(heads):
            rows = slice(h * v_head, (h + 1) * v_head)
            acc_sc[rows, :] = acc_sc[rows, :] / l_sc[h]
        o_ref[0] = acc_sc[...].T


def _flash_bound_kernel(qi_ref, ki_ref, q_ref, k_ref, vt_ref, qn_ref, kmax_ref, o_ref, l_ref,
                        mb_sc, l_sc, acc_sc, *, heads, v_head, tq, tk, chunk):
    step_id = pl.program_id(1)
    qi = qi_ref[step_id]
    ki = ki_ref[step_id]

    @pl.when(ki == 0)
    def _():
        kmax = jnp.max(kmax_ref[0], axis=0) * FLASH_BOUND_SLACK
        for h in range(heads):
            mb_sc[h:h + 1, :] = qn_ref[0, h:h + 1, :] * kmax[:, h:h + 1]
        l_sc[...] = jnp.zeros(l_sc.shape, F32)
        acc_sc[...] = jnp.zeros(acc_sc.shape, F32)

    def step(masked):
        if masked:
            kc = (ki * tk + lax.broadcasted_iota(jnp.int32, (tk, tq), 0)) // chunk
            qc = (qi * tq + lax.broadcasted_iota(jnp.int32, (tk, tq), 1)) // chunk
            mask = kc <= qc
        for h0 in range(0, heads, FLASH_HEAD_GROUP):
            hs = range(h0, h0 + FLASH_HEAD_GROUP)
            st = [_dot_nt(k_ref[0, :, h * LANES:(h + 1) * LANES], q_ref[0, :, h * LANES:(h + 1) * LANES])
                  for h in hs]
            if masked:
                st = [jnp.where(mask, s, NEG_INF) for s in st]
            p = [jnp.exp2(s - mb_sc[h:h + 1, :]) for s, h in zip(st, hs)]
            for x, h in zip(p, hs):
                l_sc[h:h + 1, :] += jnp.sum(x, axis=0, keepdims=True)
            pv = [_dot(vt_ref[0, h * v_head:(h + 1) * v_head, :], x.astype(BF16)) for x, h in zip(p, hs)]
            for x, h in zip(pv, hs):
                acc_sc[h * v_head:(h + 1) * v_head, :] += x

    full = (ki + 1) * tk <= qi * tq + chunk

    @pl.when(full)
    def _():
        step(False)

    @pl.when(jnp.logical_not(full))
    def _():
        step(True)

    @pl.when(ki == ((qi + 1) * tq - 1) // tk)
    def _():
        for h in range(heads):
            rows = slice(h * v_head, (h + 1) * v_head)
            acc_sc[rows, :] = acc_sc[rows, :] / l_sc[h:h + 1, :]
        o_ref[0] = acc_sc[...].T
        l_ref[0] = l_sc[...]


def _flash(q, k, vt, *, heads, v_head, tq, tk, bound=None):
    b, t, _ = q.shape
    pairs = [(i, j) for i in range(t // tq) for j in range(((i + 1) * tq - 1) // tk + 1)]
    qi_tab = jnp.asarray([p[0] for p in pairs], jnp.int32)
    ki_tab = jnp.asarray([p[1] for p in pairs], jnp.int32)
    in_specs = [pl.BlockSpec((1, tq, heads * LANES), lambda i, s, qt, kt: (i, qt[s], 0)),
                pl.BlockSpec((1, tk, heads * LANES), lambda i, s, qt, kt: (i, kt[s], 0)),
                pl.BlockSpec((1, heads * v_head, tk), lambda i, s, qt, kt: (i, 0, kt[s]))]
    o_shape = jax.ShapeDtypeStruct((b, t, heads * v_head), F32)
    o_spec = pl.BlockSpec((1, tq, heads * v_head), lambda i, s, qt, kt: (i, qt[s], 0))
    acc = pltpu.VMEM((heads * v_head, tq), F32)
    if bound is None:
        kern, name, args = _flash_kernel, "flash_attn", (q, k, vt)
        out_shape, out_specs = o_shape, o_spec
        scratch = [pltpu.VMEM((heads, 1, tq), F32), pltpu.VMEM((heads, 1, tq), F32), acc]
    else:
        qn, kmax = bound
        assert qn.shape == (b, heads, t)
        kern, name, args = _flash_bound_kernel, "flash_attn_bound", (q, k, vt, qn, kmax)
        in_specs += [pl.BlockSpec((1, heads, tq), lambda i, s, qt, kt: (i, 0, qt[s])),
                     pl.BlockSpec((1,) + kmax.shape[1:], lambda i, s, qt, kt: (i, 0, 0, 0))]
        out_shape = [o_shape, jax.ShapeDtypeStruct((b, heads, t), F32)]
        out_specs = [o_spec, pl.BlockSpec((1, heads, tq), lambda i, s, qt, kt: (i, 0, qt[s]))]
        scratch = [pltpu.VMEM((heads, tq), F32), pltpu.VMEM((heads, tq), F32), acc]
    grid_spec = pltpu.PrefetchScalarGridSpec(
        num_scalar_prefetch=2, grid=(b, len(pairs)), in_specs=in_specs, out_specs=out_specs,
        scratch_shapes=scratch)
    return pl.pallas_call(
        functools.partial(kern, heads=heads, v_head=v_head, tq=tq, tk=tk, chunk=CHUNK),
        out_shape=out_shape,
        grid_spec=grid_spec,
        compiler_params=_params("arbitrary", "arbitrary"),
        name=name,
    )(qi_tab, ki_tab, *args)


def _latent_attn_kernel(q_ref, cp_ref, krp_ref, cn_ref, krn_ref, wk_ref, wv_ref, o_ref, *,
                        heads, nope, rope, v_head):
    q = q_ref[0]
    qabs, qrope = [], []
    for h in range(heads):
        qabs.append(_dot_nt(q[:, h * LANES:h * LANES + nope], wk_ref[h]))
        qrope.append(q[:, h * LANES + nope:h * LANES + nope + rope])
    qabs = jnp.concatenate(qabs, axis=0).astype(BF16)
    qrope = jnp.concatenate(qrope, axis=0)
    cp = cp_ref[0].astype(BF16)
    cn = cn_ref[0].astype(BF16)
    s_p = _dot_nt(qabs, cp) + _dot_nt(qrope, krp_ref[0].astype(BF16))
    s_n = _dot_nt(qabs, cn) + _dot_nt(qrope, krn_ref[0].astype(BF16))
    m = jnp.maximum(jnp.max(s_p, axis=-1, keepdims=True), jnp.max(s_n, axis=-1, keepdims=True))
    p_p = jnp.exp2(s_p - m)
    p_n = jnp.exp2(s_n - m)
    l = jnp.sum(p_p, axis=-1, keepdims=True) + jnp.sum(p_n, axis=-1, keepdims=True)
    o_lat = (_dot(p_p.astype(BF16), cp) + _dot(p_n.astype(BF16), cn)) / l
    t = q.shape[0]
    for h in range(heads):
        o_ref[0, :, h * v_head:(h + 1) * v_head] = _dot(o_lat[h * t:(h + 1) * t].astype(BF16), wv_ref[h])


def _latent_attn(q, c_past, kr_past, c_new, kr_new, wk, wv, *, heads, nope, rope, v_head):
    b, t, _ = q.shape
    past, kv_lora = c_past.shape[1:]
    blk = lambda n, last: pl.BlockSpec((1, n, last), lambda i: (i, 0, 0))
    const = lambda a: pl.BlockSpec(a.shape, lambda i: (0,) * a.ndim)
    kern = functools.partial(_latent_attn_kernel, heads=heads, nope=nope, rope=rope, v_head=v_head)
    return pl.pallas_call(
        kern,
        out_shape=jax.ShapeDtypeStruct((b, t, heads * v_head), F32),
        grid=(b,),
        in_specs=[blk(t, heads * LANES), blk(past, kv_lora), blk(past, rope), blk(t, kv_lora), blk(t, rope),
                  const(wk), const(wv)],
        out_specs=blk(t, heads * v_head),
        compiler_params=_params("arbitrary"),
        name="latent_attn",
    )(q, c_past, kr_past, c_new, kr_new, wk, wv)


def _mlstm_prep_kernel(m_ref, kt_ref, gc_ref, gr_ref, bc_ref, br_ref, pv_ref, kv_ref, b_ref, st_ref, *,
                       heads, dk, dv, chunk, nchunk):
    row = lax.broadcasted_iota(jnp.int32, (chunk, chunk), 0)
    col = lax.broadcasted_iota(jnp.int32, (chunk, chunk), 1)
    causal = col <= row
    tril = causal.astype(F32)
    triu = (row <= col).astype(F32)
    lane = lax.broadcasted_iota(jnp.int32, (chunk, LANES), 1)
    o_k, o_v = heads * dk, 2 * heads * dk

    rows = [slice(c * chunk, (c + 1) * chunk) for c in range(nchunk)]
    gc = [gc_ref[0, r, :] + bc_ref[...] for r in rows]
    gr = [gr_ref[0, c] + br_ref[...] for c in range(nchunk)]
    bcum_c = [_dot(tril, _log_sigmoid(g), HIGHEST) for g in gc]
    bcum_r = [_dot(_log_sigmoid(g), triu, HIGHEST) for g in gr]
    for c in range(nchunk):
        b_ref[0, rows[c], :] = bcum_c[c]
    ch = [(c, h) for c in range(nchunk) for h in range(heads)]
    v = [m_ref[0, rows[c], o_v + h * dv:o_v + (h + 1) * dv].astype(BF16) for c, h in ch]
    qk = [_dot_nt(m_ref[0, rows[c], h * dk:(h + 1) * dk].astype(BF16),
                  (m_ref[0, rows[c], o_k + h * dk:o_k + (h + 1) * dk] * (dk ** -0.5)).astype(BF16)) for c, h in ch]
    li_r = [gr[c][h:h + 1, :] for c, h in ch]
    b_r = [bcum_r[c][heads + h:heads + h + 1, :] for c, h in ch]
    dmat = [jnp.where(causal, bcum_c[c][:, heads + h:heads + h + 1] - b_r[i] + li_r[i], NEG_INF)
            for i, (c, h) in enumerate(ch)]
    mx = [jnp.max(d, axis=-1, keepdims=True) for d in dmat]
    p0 = [s * jnp.exp(d - m) for s, d, m in zip(qk, dmat, mx)]
    for i, (c, h) in enumerate(ch):
        pv_ref[0, rows[c], h * dv:(h + 1) * dv] = _dot(p0[i].astype(BF16), v[i])
    w_r = [jnp.exp(b_r[i][:, chunk - 1:chunk] - b_r[i] + li_r[i] - mx[i][chunk - 1:chunk, :]) * (dk ** -0.5)
           for i in range(len(ch))]
    for i, (c, h) in enumerate(ch):
        wkt = (kt_ref[0, h * dk:(h + 1) * dk, c * chunk:(c + 1) * chunk] * w_r[i]).astype(BF16)
        kv_ref[0, c, h] = _dot(wkt, jnp.concatenate([v[i], jnp.ones_like(v[i])], axis=1))
    psum = [jnp.sum(p, axis=-1, keepdims=True) for p in p0]
    for c in range(nchunk):
        stats = jnp.zeros((chunk, LANES), F32)
        for h in range(heads):
            stats = jnp.where(lane == h, mx[c * heads + h], stats)
            stats = jnp.where(lane == heads + h, psum[c * heads + h], stats)
        st_ref[0, rows[c], :] = stats


def _mlstm_scan_kernel(q_ref, mo_ref, pv_ref, kv_ref, b_ref, st_ref, gout_ref, c0_ref, m0_ref,
                       h_ref, c1_ref, m1_ref, c_sc, m_sc, *, heads, dk, dv, chunk, nchunk):
    t = pl.program_id(1)

    @pl.when(t == 0)
    def _():
        c_sc[...] = c0_ref[0]
        m_sc[...] = m0_ref[0]

    def body(c, carry):
        rows = pl.ds(pl.multiple_of(c * chunk, chunk), chunk)
        bcum = b_ref[0, rows, :]
        stats = st_ref[0, rows, :]
        rep = lambda col: jnp.broadcast_to(col, (chunk, dv))
        hs = range(heads)
        c2 = [c_sc[h] for h in hs]
        qc = [_dot(q_ref[0, rows, h * dk:(h + 1) * dk].astype(BF16), c2[h].astype(BF16)) for h in hs]
        mx = [rep(stats[:, h:h + 1]) for h in hs]
        inter = [rep(bcum[:, heads + h:heads + h + 1]) + m_sc[h] for h in hs]
        m = [jnp.maximum(inter[h], mx[h]) for h in hs]
        w_inter = [jnp.exp(inter[h] - m[h]) for h in hs]
        r = [jnp.exp(mx[h] - m[h]) for h in hs]
        for h in hs:
            decay_end = w_inter[h][chunk - 1:chunk, :]
            f_new = r[h][chunk - 1:chunk, :]
            c_sc[h] = (jnp.concatenate([decay_end, decay_end], axis=1) * c2[h]
                       + jnp.concatenate([f_new, f_new], axis=1) * kv_ref[0, c, h])
            m_sc[h] = m[h][chunk - 1:chunk, 0:1]
        num = [w_inter[h] * qc[h][:, :dv] + r[h] * pv_ref[0, rows, h * dv:(h + 1) * dv] for h in hs]
        den = [w_inter[h] * qc[h][:, dv:] + r[h] * rep(stats[:, heads + h:heads + h + 1]) for h in hs]
        hh = [num[h] / jnp.maximum(jnp.abs(den[h]), jnp.exp(-m[h])) for h in hs]
        hn = [_rms(hh[h], gout_ref[:, h * dv:(h + 1) * dv]) for h in hs]
        for h in hs:
            h_ref[0, rows, h * dv:(h + 1) * dv] = hn[h] * _sigmoid(mo_ref[0, rows, h * dv:(h + 1) * dv])
        return carry

    lax.fori_loop(0, nchunk, body, 0)

    @pl.when(t == pl.num_programs(1) - 1)
    def _():
        c1_ref[0] = c_sc[...]
        m1_ref[0] = m_sc[...]


def _mlstm_prep(m_slab, kt, gates_c, gates_r, bias_c, bias_r, *, chunk, heads, dk, dv):
    b, t, mw = m_slab.shape
    const = lambda a: pl.BlockSpec(a.shape, lambda i, j: (0,) * a.ndim)
    qkv_w = 2 * heads * dk + heads * dv
    tp = min(t, MLSTM_PREP_TC)
    npc = tp // chunk
    tokp = lambda last: pl.BlockSpec((1, tp, last), lambda i, j: (i, j, 0))
    per_chunk = lambda *s: pl.BlockSpec((1, npc, heads) + s, lambda i, j: (i, j, 0) + (0,) * len(s))
    return pl.pallas_call(
        functools.partial(_mlstm_prep_kernel, heads=heads, dk=dk, dv=dv, chunk=chunk, nchunk=npc),
        out_shape=[jax.ShapeDtypeStruct((b, t, heads * dv), F32),
                   jax.ShapeDtypeStruct((b, t // chunk, heads, dk, 2 * dv), F32),
                   jax.ShapeDtypeStruct((b, t, LANES), F32),
                   jax.ShapeDtypeStruct((b, t, LANES), F32)],
        grid=(b, t // tp),
        in_specs=[tokp(qkv_w), pl.BlockSpec((1, heads * dk, tp), lambda i, j: (i, 0, j)), tokp(LANES),
                  pl.BlockSpec((1, npc, SUBLANES, chunk), lambda i, j: (i, j, 0, 0)),
                  const(bias_c), const(bias_r)],
        out_specs=[tokp(heads * dv), per_chunk(dk, 2 * dv), tokp(LANES), tokp(LANES)],
        compiler_params=_params("arbitrary", "arbitrary"),
        name="mlstm_prep",
    )(m_slab, kt, gates_c, gates_r, bias_c, bias_r)


def _mlstm_scan(m_slab, pv, kv, bcum, stats, gout, c0, n0, m0, *, tc, heads, dk, dv):
    b, t, mw = m_slab.shape
    chunk = min(CHUNK, t)
    const = lambda a: pl.BlockSpec(a.shape, lambda i, j: (0,) * a.ndim)
    qkv_w = 2 * heads * dk + heads * dv
    mo_blk, rem = divmod(qkv_w, heads * dv)
    assert rem == 0 and mw == qkv_w + heads * dv
    nchunk = tc // chunk
    tok = lambda last, blk=0: pl.BlockSpec((1, tc, last), lambda i, j: (i, j, blk))
    per_chunk = lambda *s: pl.BlockSpec((1, nchunk, heads) + s, lambda i, j: (i, j, 0) + (0,) * len(s))
    st = lambda *s: pl.BlockSpec((1,) + s, lambda i, j: (i,) + (0,) * len(s))
    cn0 = jnp.concatenate([c0, jnp.broadcast_to(n0[..., None], c0.shape)], axis=-1)
    hm, cn1, m1 = pl.pallas_call(
        functools.partial(_mlstm_scan_kernel, heads=heads, dk=dk, dv=dv, chunk=chunk, nchunk=nchunk),
        out_shape=[jax.ShapeDtypeStruct((b, t, heads * dv), F32),
                   jax.ShapeDtypeStruct((b, heads, dk, 2 * dv), F32),
                   jax.ShapeDtypeStruct((b, heads, 1, 1), F32)],
        grid=(b, t // tc),
        in_specs=[tok(heads * dk),
                  tok(heads * dv, mo_blk),
                  tok(heads * dv), per_chunk(dk, 2 * dv), tok(LANES), tok(LANES),
                  const(gout), st(heads, dk, 2 * dv), st(heads, 1, 1)],
        out_specs=[tok(heads * dv), st(heads, dk, 2 * dv), st(heads, 1, 1)],
        scratch_shapes=[pltpu.VMEM((heads, dk, 2 * dv), F32), pltpu.VMEM((heads, 1, 1), F32)],
        compiler_params=_params("arbitrary", "arbitrary"),
        name="mlstm_scan",
    )(m_slab, m_slab, pv, kv, bcum, stats, gout, cn0, m0)
    return hm, cn1[..., :dv], cn1[..., dv], m1


def _mid_kernel(oa_ref, ob_ref, z_ref, x_ref, gate_ref, wo_ref, shift_ref, scale_ref, g_ref, w2_ref,
                x1_ref, qkv_ref, z2_ref, ab_ref, *, half, conv_ch, d_model):
    z = z_ref[0]
    ma = (oa_ref[0] * _silu(z[:, :half])).astype(BF16)
    mb = (ob_ref[0] * _silu(z[:, half:])).astype(BF16)
    y = _dot(ma, wo_ref[0:half, :]) + _dot(mb, wo_ref[half:, :])
    x1 = x_ref[0] + _per_row(gate_ref[0], x_ref.shape[1]) * y
    x1_ref[0] = x1
    hn = _rms(x1, g_ref[...]) * (1.0 + _per_row(scale_ref[0], x_ref.shape[1])) + _per_row(shift_ref[0], x_ref.shape[1])
    y2 = _dot(hn.astype(BF16), w2_ref[...])
    qkv_ref[0] = y2[:, :conv_ch]
    z2_ref[0] = y2[:, conv_ch:conv_ch + d_model]
    ab_ref[0] = y2[:, conv_ch + d_model:]


def _mid(oa, ob, z, x, gate, wo, shift, scale, g, w2, *, tm, conv_ch):
    b, t, d = x.shape
    half = oa.shape[-1]
    tok = lambda last: pl.BlockSpec((1, tm, last), lambda i, j: (i, j, 0))
    vec = pl.BlockSpec((1,) + gate.shape[1:], lambda i, j: (i, 0, 0))
    const = lambda a: pl.BlockSpec(a.shape, lambda i, j: (0,) * a.ndim)
    kern = functools.partial(_mid_kernel, half=half, conv_ch=conv_ch, d_model=d)
    return pl.pallas_call(
        kern,
        out_shape=[jax.ShapeDtypeStruct((b, t, d), F32), jax.ShapeDtypeStruct((b, t, conv_ch), F32),
                   jax.ShapeDtypeStruct((b, t, d), F32), jax.ShapeDtypeStruct((b, t, LANES), F32)],
        grid=(b, t // tm),
        in_specs=[tok(half), tok(ob.shape[-1]), tok(d), tok(d), vec, const(wo), vec, vec, const(g), const(w2)],
        out_specs=[tok(d), tok(conv_ch), tok(d), tok(LANES)],
        compiler_params=_params("arbitrary", "arbitrary"),
        name="out_a_in_c",
    )(oa, ob, z, x, gate, wo, shift, scale, g, w2)


def _conv_kernel(qkv_ref, past_ref, wc_ref, ab_ref, alog_ref, dtb_ref, act_ref, gb_ref, ext_sc, *,
                 tm, width, heads, dk):
    @pl.when(pl.program_id(1) == 0)
    def _():
        ext_sc[0:SUBLANES, :] = past_ref[0]

    ext_sc[SUBLANES:SUBLANES + tm, :] = qkv_ref[0]
    conv = wc_ref[width - 1:width, :] * ext_sc[SUBLANES:SUBLANES + tm, :]
    for j in range(width - 1):
        s = SUBLANES - (width - 1) + j
        conv = conv + wc_ref[j:j + 1, :] * ext_sc[s:s + tm, :]
    ext_sc[0:SUBLANES, :] = ext_sc[tm:tm + SUBLANES, :]
    act = _silu(conv)
    for h in range(2 * heads):
        xh = act[:, h * dk:(h + 1) * dk]
        xh = xh * lax.rsqrt(jnp.sum(xh * xh, axis=-1, keepdims=True) + EPS)
        if h < heads:
            xh = xh * (dk ** -0.5)
        act_ref[0, :, h * dk:(h + 1) * dk] = xh
    act_ref[0, :, 2 * heads * dk:] = act[:, 2 * heads * dk:]
    ab = ab_ref[0]
    g = -jnp.exp(alog_ref[...]) * _softplus(ab + dtb_ref[...])
    lane = lax.broadcasted_iota(jnp.int32, ab.shape, 1)
    gb_ref[0] = jnp.where(lane < heads, g, _sigmoid(ab))


def _conv(qkv, past8, wc8, ab, alog, dtb, *, tm, width, heads, dk):
    b, t, ch = qkv.shape
    tok = lambda last: pl.BlockSpec((1, tm, last), lambda i, j: (i, j, 0))
    const = lambda a: pl.BlockSpec(a.shape, lambda i, j: (0,) * a.ndim)
    kern = functools.partial(_conv_kernel, tm=tm, width=width, heads=heads, dk=dk)
    return pl.pallas_call(
        kern,
        out_shape=[jax.ShapeDtypeStruct((b, t, ch), F32), jax.ShapeDtypeStruct((b, t, LANES), F32)],
        grid=(b, t // tm),
        in_specs=[tok(ch), pl.BlockSpec((1, SUBLANES, ch), lambda i, j: (i, 0, 0)), const(wc8), tok(LANES),
                  const(alog), const(dtb)],
        out_specs=[tok(ch), tok(LANES)],
        scratch_shapes=[pltpu.VMEM((tm + SUBLANES, ch), F32)],
        compiler_params=_params("arbitrary", "arbitrary"),
        name="conv_gates",
    )(qkv, past8, wc8, ab, alog, dtb)


def _blockdiag(x, group, chunk):
    w = group * chunk
    br = lax.broadcasted_iota(jnp.int32, (w, w), 0) // chunk
    bc = lax.broadcasted_iota(jnp.int32, (w, w), 1) // chunk
    xb = x.astype(BF16)
    return jnp.where(br == bc, jnp.concatenate([xb] * group, axis=0), jnp.zeros((), BF16))


def _unit_lower_inverses_minus_eye(a_list, group, chunk):
    w = group * chunk
    r = lax.broadcasted_iota(jnp.int32, (chunk, w), 0)
    cc = lax.broadcasted_iota(jnp.int32, (chunk, w), 1) % chunk
    es = [-jnp.where((r // 2 == cc // 2) & (r % 2 == 1) & (cc % 2 == 0), a4, 0.0) for a4 in a_list]
    s = 2
    while s < chunk:
        off = (r // (2 * s) == cc // (2 * s)) & (r % (2 * s) >= s) & (cc % (2 * s) < s)
        a_offs = [jnp.where(off, a4, 0.0) for a4 in a_list]
        ps = [a + _dot(a.astype(BF16), _blockdiag(e, group, chunk)) for a, e in zip(a_offs, es)]
        es = [e - (p + _dot(e.astype(BF16), _blockdiag(p, group, chunk))) for e, p in zip(es, ps)]
        s *= 2
    return es


def _gdn_prep_kernel(act_ref, gbc_ref, gbr_ref, w_ref, uv_ref, kd_ref, attn_ref, eg_ref, *,
                     heads, dk, dv, chunk, nchunk, group):
    row = lax.broadcasted_iota(jnp.int32, (chunk, chunk), 0)
    col = lax.broadcasted_iota(jnp.int32, (chunk, chunk), 1)
    incl = col <= row
    strict = col < row
    tril = incl.astype(F32)
    triu = (row <= col).astype(F32)
    lane = lax.broadcasted_iota(jnp.int32, (chunk, LANES), 1)
    o_k, o_v = heads * dk, 2 * heads * dk

    rows = [slice(c * chunk, (c + 1) * chunk) for c in range(nchunk)]
    gbc = [gbc_ref[0, r, :] for r in rows]
    gcum_c = [_dot(tril, g, HIGHEST) for g in gbc]
    gcum_r = [_dot(gbr_ref[0, c], triu, HIGHEST) for c in range(nchunk)]
    for c in range(nchunk):
        eg_ref[0, rows[c], :] = jnp.where(lane < heads, jnp.exp(gcum_c[c]), 0.0)
    ch = [(c, h) for c in range(nchunk) for h in range(heads)]
    k = [act_ref[0, rows[c], o_k + h * dk:o_k + (h + 1) * dk] for c, h in ch]
    kb = [x.astype(BF16) for x in k]
    kk = [_dot_nt(x, x) for x in kb]
    qk = [_dot_nt(act_ref[0, rows[c], h * dk:(h + 1) * dk].astype(BF16), kb[i]) for i, (c, h) in enumerate(ch)]
    g_c = [gcum_c[c][:, h:h + 1] for c, h in ch]
    beta = [gbc[c][:, heads + h:heads + h + 1] for c, h in ch]
    decay = [jnp.exp(jnp.where(incl, g_c[i] - gcum_r[c][h:h + 1, :], NEG_INF)) for i, (c, h) in enumerate(ch)]
    a_blk = [jnp.where(strict, beta[i] * kk[i] * decay[i], 0.0) for i in range(len(ch))]
    for i, (c, h) in enumerate(ch):
        attn_ref[0, rows[c], h * chunk:(h + 1) * chunk] = (qk[i] * decay[i]).astype(BF16)
        kd_ref[0, rows[c], h * dk:(h + 1) * dk] = (k[i] * jnp.exp(g_c[i][chunk - 1:chunk, :] - g_c[i])).astype(BF16)
    rhs_blk = [jnp.concatenate([beta[i] * act_ref[0, rows[c], o_v + h * dv:o_v + (h + 1) * dv],
                                (beta[i] * jnp.exp(g_c[i])) * k[i]], axis=1) for i, (c, h) in enumerate(ch)]
    problems = [(c, g0) for c in range(nchunk) for g0 in range(0, heads, group)]
    a_list = [jnp.concatenate(a_blk[c * heads + g0:c * heads + g0 + group], axis=1) for c, g0 in problems]
    rhs_list = [jnp.concatenate(rhs_blk[c * heads + g0:c * heads + g0 + group], axis=0) for c, g0 in problems]
    e_list = _unit_lower_inverses_minus_eye(a_list, group, chunk)
    sols = [rhs + _dot(_blockdiag(e, group, chunk), rhs.astype(BF16)) for e, rhs in zip(e_list, rhs_list)]
    for (c, g0), sol in zip(problems, sols):
        for i, h in enumerate(range(g0, g0 + group)):
            uv_ref[0, rows[c], h * dv:(h + 1) * dv] = sol[i * chunk:(i + 1) * chunk, :dv]
            w_ref[0, rows[c], h * dk:(h + 1) * dk] = sol[i * chunk:(i + 1) * chunk, dv:].astype(BF16)


def _gdn_prep(act, gb_c, gb_r, *, chunk, tc, heads, dk, dv):
    b, t, _ = act.shape
    tc = min(tc, t)
    nchunk = tc // chunk
    group = (2 * LANES) // chunk
    kern = functools.partial(_gdn_prep_kernel, heads=heads, dk=dk, dv=dv, chunk=chunk, nchunk=nchunk, group=group)
    tok = lambda last: pl.BlockSpec((1, tc, last), lambda i, j: (i, j, 0))
    return pl.pallas_call(
        kern,
        out_shape=[jax.ShapeDtypeStruct((b, t, heads * dk), BF16), jax.ShapeDtypeStruct((b, t, heads * dv), F32),
                   jax.ShapeDtypeStruct((b, t, heads * dk), BF16), jax.ShapeDtypeStruct((b, t, heads * chunk), BF16),
                   jax.ShapeDtypeStruct((b, t, LANES), F32)],
        grid=(b, t // tc),
        in_specs=[tok(act.shape[-1]), tok(LANES),
                  pl.BlockSpec((1, nchunk, 2 * SUBLANES, chunk), lambda i, j: (i, j, 0, 0))],
        out_specs=[tok(heads * dk), tok(heads * dv), tok(heads * dk), tok(heads * chunk), tok(LANES)],
        compiler_params=_params("arbitrary", "arbitrary"),
        name="gdn_prep",
    )(act, gb_c, gb_r)


def _gdn_scan_kernel(q_ref, w_ref, uv_ref, kd_ref, attn_ref, eg_ref, gout_ref, s0_ref, o_ref, s1_ref, s_sc, *,
                     heads, dk, dv, chunk, nchunk):
    t = pl.program_id(1)

    @pl.when(t == 0)
    def _():
        s_sc[...] = s0_ref[0]

    def body(c, carry):
        rows = pl.ds(pl.multiple_of(c * chunk, chunk), chunk)
        eg = eg_ref[0, rows, :]
        hs = range(heads)
        s0 = [s_sc[h] for h in hs]
        s0b = [s.astype(BF16) for s in s0]
        ws = [_dot(w_ref[0, rows, h * dk:(h + 1) * dk], s0b[h]) for h in hs]
        ub = [(uv_ref[0, rows, h * dv:(h + 1) * dv] - ws[h]).astype(BF16) for h in hs]
        eg_h = [jnp.broadcast_to(eg[:, h:h + 1], (chunk, dv)) for h in hs]
        ku = [_dot_tn(kd_ref[0, rows, h * dk:(h + 1) * dk], ub[h]) for h in hs]
        for h in hs:
            s_sc[h] = eg_h[h][chunk - 1:chunk, :] * s0[h] + ku[h]
        qs = [_dot(q_ref[0, rows, h * dk:(h + 1) * dk].astype(BF16), s0b[h]) for h in hs]
        au = [_dot(attn_ref[0, rows, h * chunk:(h + 1) * chunk], ub[h]) for h in hs]
        on = [_rms(eg_h[h] * qs[h] + au[h], gout_ref[...]) for h in hs]
        for h in hs:
            o_ref[0, rows, h * dv:(h + 1) * dv] = on[h]
        return carry

    lax.fori_loop(0, nchunk, body, 0)

    @pl.when(t == pl.num_programs(1) - 1)
    def _():
        s1_ref[0] = s_sc[...]


def _gdn_scan(act, w, uv, kd, attn, eg, gout, s0, *, tc, heads, dk, dv):
    b, t, _ = act.shape
    chunk = min(CHUNK, t)
    nchunk = tc // chunk
    kern = functools.partial(_gdn_scan_kernel, heads=heads, dk=dk, dv=dv, chunk=chunk, nchunk=nchunk)
    tok = lambda last: pl.BlockSpec((1, tc, last), lambda i, j: (i, j, 0))
    state = pl.BlockSpec((1, heads, dk, dv), lambda i, j: (i, 0, 0, 0))
    return pl.pallas_call(
        kern,
        out_shape=[jax.ShapeDtypeStruct((b, t, heads * dv), F32), jax.ShapeDtypeStruct(s0.shape, F32)],
        grid=(b, t // tc),
        in_specs=[tok(heads * dk),
                  tok(heads * dk), tok(heads * dv), tok(heads * dk), tok(heads * chunk), tok(LANES),
                  pl.BlockSpec(gout.shape, lambda i, j: (0, 0)), state],
        out_specs=[tok(heads * dv), state],
        scratch_shapes=[pltpu.VMEM((heads, dk, dv), F32)],
        compiler_params=_params("arbitrary", "arbitrary"),
        name="gdn_scan",
    )(act, w, uv, kd, attn, eg, gout, s0)


def _final_kernel(o_ref, z_ref, x_ref, gate_ref, wo_ref, g_ref, y_ref):
    mixed = (o_ref[0] * _silu(z_ref[0])).astype(BF16)
    x2 = x_ref[0] + _per_row(gate_ref[0], x_ref.shape[1]) * _dot(mixed, wo_ref[...])
    y_ref[0] = _rms(x2, g_ref[...])


def _final(o, z, x, gate, wo, g, *, tm):
    b, t, d = x.shape
    tok = lambda last: pl.BlockSpec((1, tm, last), lambda i, j: (i, j, 0))
    const = lambda a: pl.BlockSpec(a.shape, lambda i, j: (0,) * a.ndim)
    return pl.pallas_call(
        _final_kernel,
        out_shape=jax.ShapeDtypeStruct((b, t, d), F32),
        grid=(b, t // tm),
        in_specs=[tok(o.shape[-1]), tok(d), tok(d), pl.BlockSpec((1,) + gate.shape[1:], lambda i, j: (i, 0, 0)),
                  const(wo), const(g)],
        out_specs=tok(d),
        compiler_params=_params("arbitrary", "arbitrary"),
        name="out_c_final",
    )(o, z, x, gate, wo, g)


def _pad_lanes(w, width=LANES, at=0):
    out = jnp.zeros(w.shape[:-1] + (width,), w.dtype)
    return out.at[..., at:at + w.shape[-1]].set(w)


def _rot_half_cols(w):
    r = w.shape[-1] // 2
    return jnp.concatenate([-w[..., r:], w[..., :r]], axis=-1)


def _rope_tables(pos, rope, scale):
    freqs = jnp.exp(jnp.arange(0, rope, 2, dtype=F32) * (-math.log(ROPE_BASE) / rope))
    ang = pos.astype(F32)[:, None] * freqs[None, :]
    cos = jnp.concatenate([jnp.cos(ang), jnp.cos(ang)], axis=-1)
    sin = jnp.concatenate([jnp.sin(ang), jnp.sin(ang)], axis=-1)
    half = LANES // 2
    cosk = _pad_lanes(cos, at=half)
    sink = _pad_lanes(sin, at=half)
    ones = _pad_lanes(jnp.ones((pos.shape[0], half), F32))
    return jnp.concatenate([(cosk + ones) * scale, sink * scale, cosk, sink], axis=-1)


def _tokens_on_lanes(a, chunk, rows):
    b, t = a.shape[:2]
    return a[..., :rows].reshape(b, t // chunk, chunk, rows).transpose(0, 1, 3, 2)


def kernel(x_prompt, x_sample, c_prompt, c_sample, cache_kv_latent, cache_k_rope, state_mlstm_C, state_mlstm_n, state_mlstm_m, state_gdn_S, state_gdn_conv, a_w_ada, a_b_ada, a_g_norm, a_w_in, a_g_q_a, a_w_q_b, a_g_kv_a, a_w_kv_b, a_b_i, a_b_f, a_g_out, a_w_out, c_w_ada, c_b_ada, c_g_norm, c_w_in, c_w_conv, c_a_log, c_dt_bias, c_g_out, c_w_out, g_final):
    d = x_prompt.shape[-1]
    q_lora, heads, qk = a_w_q_b.shape
    kv_lora = a_w_kv_b.shape[0]
    rope = cache_k_rope.shape[-1]
    nope = qk - rope
    v_head = a_w_kv_b.shape[2] - nope
    m_heads, m_dv = a_g_out.shape
    m_dk = state_mlstm_C.shape[2]
    g_heads = c_a_log.shape[0]
    g_dk, g_dv = state_gdn_S.shape[2:]
    width = c_w_conv.shape[0]
    conv_ch = c_w_conv.shape[1]
    assert nope + rope <= LANES and nope == LANES // 2 and 2 * m_heads <= SUBLANES and 2 * g_heads <= 2 * SUBLANES

    sizes = (q_lora, kv_lora, rope, m_heads * m_dk, m_heads * m_dk, m_heads * m_dv, m_heads, m_heads,
             m_heads * m_dv, heads * v_head + m_heads * m_dv)
    offs = [0]
    for s in sizes:
        offs.append(offs[-1] + s)
    w_qa, w_c, w_kr, w_mq, w_mk, w_mv, w_mi, w_mf, w_mo, w_z = [a_w_in[:, offs[i]:offs[i + 1]] for i in range(10)]
    half = LANES // 2
    w1 = jnp.concatenate([w_qa, w_c, _pad_lanes(w_kr, at=half), _pad_lanes(_rot_half_cols(w_kr), at=half),
                          w_mq, w_mk, w_mv, w_mo, w_z, _pad_lanes(jnp.concatenate([w_mi, w_mf], axis=1))],
                         axis=1).astype(BF16)
    m_width = 2 * m_heads * m_dk + 2 * m_heads * m_dv
    wq_rope = a_w_q_b[..., nope:]
    wq_main = _pad_lanes(a_w_q_b).reshape(q_lora, heads * LANES)
    wq_rot = _pad_lanes(_rot_half_cols(wq_rope), at=nope).reshape(q_lora, heads * LANES)
    wq = jnp.concatenate([wq_main, wq_rot], axis=1).astype(BF16)
    wkv = jnp.concatenate([_pad_lanes(a_w_kv_b[..., :nope]).reshape(kv_lora, heads * LANES),
                           a_w_kv_b[..., nope:].reshape(kv_lora, heads * v_head)], axis=1).astype(BF16)
    wk_abs = a_w_kv_b[..., :nope].transpose(1, 0, 2).astype(BF16)
    wv_abs = a_w_kv_b[..., nope:].transpose(1, 0, 2).astype(BF16)
    wo_a = a_w_out.astype(BF16)
    csz = (conv_ch, g_heads, g_heads, g_heads * g_dv)
    w_qkv, w_a, w_b, w_zc = [c_w_in[:, sum(csz[:i]):sum(csz[:i + 1])] for i in range(4)]
    w2 = jnp.concatenate([w_qkv, w_zc, _pad_lanes(jnp.concatenate([w_a, w_b], axis=1))], axis=1).astype(BF16)
    wo_c = c_w_out.astype(BF16)
    wc8 = jnp.zeros((SUBLANES, conv_ch), F32).at[:width].set(c_w_conv)
    row = lambda a: a.reshape(1, -1).astype(F32)
    bias_c = _pad_lanes(jnp.concatenate([a_b_i, a_b_f]).reshape(1, -1))
    bias_r = jnp.zeros((SUBLANES, 1), F32).at[:2 * m_heads, 0].set(jnp.concatenate([a_b_i, a_b_f]))
    alog = _pad_lanes(c_a_log.reshape(1, -1))
    dtb = _pad_lanes(c_dt_bias.reshape(1, -1))

    bp, bs = c_prompt.shape[0], c_sample.shape[0]
    c_all = jnp.concatenate([c_prompt, c_sample], axis=0)
    pad = (-c_all.shape[0]) % SUBLANES
    c_all = jnp.pad(c_all, ((0, pad), (0, 0)))
    mod_a = _adaln(c_all, a_w_ada, a_b_ada)
    mod_c = _adaln(c_all, c_w_ada, c_b_ada)

    def mods(mod, lo, hi):
        return [mod[lo:hi, i * d:(i + 1) * d][:, None, :] for i in range(3)]

    def run(x, mod_lo, mod_hi, c_past, kr_past, c0, n0, m0, conv0, s0):
        b, t, _ = x.shape
        past = 0 if c_past is None else c_past.shape[1]
        chunk = min(CHUNK, t)
        tc = min(t, 512)
        expand = c_past is None
        nb = 1 if expand else math.gcd(b, max(1, PROJ_TM // t))
        tm = min(nb * t, PROJ_TM)
        grp = lambda a: a.reshape((b // nb, nb * a.shape[1]) + a.shape[2:])
        ungrp = lambda a: a.reshape((b, a.shape[1] // nb) + a.shape[2:])
        shift_a, scale_a, gate_a = [grp(m) for m in mods(mod_a, mod_lo, mod_hi)]
        shift_c, scale_c, gate_c = [grp(m) for m in mods(mod_c, mod_lo, mod_hi)]
        tab = _rope_tables(past + jnp.arange(t, dtype=jnp.int32), rope, qk ** -0.5 * math.log2(math.e))
        outs = _in_a(grp(x), shift_a, scale_a, row(a_g_norm), w1, row(a_g_q_a), wq, row(a_g_kv_a), wkv,
                     jnp.tile(tab, (nb, 1)),
                     tm=tm, heads=heads, q_lora=q_lora, kv_lora=kv_lora, rope=rope, m_width=m_width,
                     mk_cols=(m_heads * m_dk, 2 * m_heads * m_dk), v_head=v_head, expand_kv=expand)
        q, c_new, kr_new, m_slab, z, gates = [ungrp(a) for a in outs[:6]]
        kt = None
        if expand:
            k_all, vt, kt = outs[6], outs[7], outs[10]
            qn = outs[8][..., :heads].transpose(0, 2, 1)
            tiles = dict(heads=heads, v_head=v_head, tq=min(t, FLASH_TQ), tk=min(t, FLASH_TK))
            o_fast, row_sums = _flash(q, k_all, vt, bound=(qn, outs[9]), **tiles)
            o_mla = lax.cond(jnp.min(row_sums) >= FLASH_ROW_SUM_MIN,
                             lambda: o_fast, lambda: _flash(q, k_all, vt, **tiles))
        else:
            o_mla = _latent_attn(q, c_past, kr_past, c_new, kr_new, wk_abs, wv_abs,
                                 heads=heads, nope=nope, rope=rope, v_head=v_head)
        fold = t == chunk and b > 1
        flat = (lambda a: a.reshape((1, b * a.shape[1]) + a.shape[2:])) if fold else (lambda a: a)
        unflat = (lambda a: a.reshape((b, a.shape[1] // b) + a.shape[2:])) if fold else (lambda a: a)
        m_flat = flat(m_slab)
        if kt is None:
            kt = m_flat[..., m_heads * m_dk:2 * m_heads * m_dk].transpose(0, 2, 1)
        pv, kvs, bcum, stats = [unflat(a) for a in _mlstm_prep(
            m_flat, kt, flat(gates), flat(_tokens_on_lanes(gates, chunk, SUBLANES)), bias_c, bias_r,
            chunk=chunk, heads=m_heads, dk=m_dk, dv=m_dv)]
        hm, c1, n1, m1 = _mlstm_scan(m_slab, pv, kvs, bcum, stats, row(a_g_out), c0, n0, m0.reshape(b, m_heads, 1, 1),
                                     tc=tc, heads=m_heads, dk=m_dk, dv=m_dv)
        x1, qkv, zc, ab = [ungrp(a) for a in _mid(grp(o_mla), grp(hm), grp(z), grp(x), gate_a, wo_a, shift_c, scale_c,
                                                  row(c_g_norm), w2, tm=tm, conv_ch=conv_ch)]
        past8 = jnp.pad(conv0, ((0, 0), (SUBLANES - (width - 1), 0), (0, 0)))
        act, gb = _conv(qkv, past8, wc8, ab, alog, dtb, tm=min(t, PROJ_TM), width=width, heads=g_heads, dk=g_dk)
        w, uv, kd, attn, eg = [unflat(a) for a in _gdn_prep(
            flat(act), flat(gb), flat(_tokens_on_lanes(gb, chunk, 2 * SUBLANES)),
            chunk=chunk, tc=GDN_PREP_TC, heads=g_heads, dk=g_dk, dv=g_dv)]
        o_gdn, s1 = _gdn_scan(act, w, uv, kd, attn, eg, row(c_g_out), s0, tc=tc, heads=g_heads, dk=g_dk, dv=g_dv)
        y = ungrp(_final(grp(o_gdn), grp(zc), grp(x1), gate_c, wo_c, row(g_final), tm=tm))
        conv1 = jnp.concatenate([conv0, qkv], axis=1)[:, t:] if t < width - 1 else qkv[:, t - (width - 1):]
        return (y, c_new, kr_new, c1, n1, m1.reshape(b, m_heads), conv1, s1)

    dt = x_prompt.dtype
    (y_p, p_kv, p_kr, p_c, p_n, p_m, p_conv, p_s) = run(
        x_prompt, 0, bp, None, None,
        jnp.zeros((bp, m_heads, m_dk, m_dv), dt), jnp.zeros((bp, m_heads, m_dk), dt), jnp.zeros((bp, m_heads), dt),
        jnp.zeros((bp, width - 1, conv_ch), dt), jnp.zeros((bp, g_heads, g_dk, g_dv), dt))
    (y_s, s_kv, s_kr, s_c, s_n, s_m, s_conv, s_s) = run(
        x_sample, bp, bp + bs, cache_kv_latent, cache_k_rope, state_mlstm_C, state_mlstm_n, state_mlstm_m,
        state_gdn_conv, state_gdn_S)
    return (y_p, y_s, p_kv, p_kr, p_c, p_n, p_m, p_s, p_conv,
            s_kv, s_kr, s_c, s_n, s_m, s_s, s_conv)
```

```python
import functools
import math

import jax
import jax.numpy as jnp
from jax import lax
from jax.experimental import pallas as pl
from jax.experimental.pallas import tpu as pltpu

F32 = jnp.float32
BF16 = jnp.bfloat16
HIGHEST = lax.Precision.HIGHEST

CHUNK = 64
EPS = 1e-6
ROPE_BASE = 10000.0
LANES = 128
SUBLANES = 8
VMEM_LIMIT = 56 * 1024 * 1024
NEG_INF = float("-inf")
SCAN_SEQS = 4
GDN_SCAN_TC = 256
PROJ_TM = 256
FLASH_TQ = 512
FLASH_TK = 512
FLASH_HEAD_GROUP = 4
FLASH_BOUND_SLACK = 1.0 + 2.0 ** -6
FLASH_ROW_SUM_MIN = 2.0 ** -100
MLSTM_PREP_TC = 256
GDN_PREP_TC = 256


def _params(*sem):
    return pltpu.CompilerParams(dimension_semantics=sem, vmem_limit_bytes=VMEM_LIMIT)


def _dot(a, b, precision=None):
    return jnp.dot(a, b, preferred_element_type=F32, precision=precision)


def _dot_nt(a, b):
    return lax.dot_general(a, b, (((1,), (1,)), ((), ())), preferred_element_type=F32)


def _dot_tn(a, b):
    return lax.dot_general(a, b, (((0,), (0,)), ((), ())), preferred_element_type=F32)


def _rms(x, g):
    return x * lax.rsqrt(jnp.mean(x * x, axis=-1, keepdims=True) + EPS) * g


def _per_row(m, rows):
    n, d = m.shape
    if n == 1:
        return m
    return jnp.concatenate([jnp.broadcast_to(m[i:i + 1], (rows // n, d)) for i in range(n)], axis=0)


def _sigmoid(x):
    return 0.5 * jnp.tanh(0.5 * x) + 0.5


def _silu(x):
    return x * _sigmoid(x)


def _softplus(x):
    return jnp.maximum(x, 0.0) + jnp.log1p(jnp.exp(-jnp.abs(x)))


def _log_sigmoid(x):
    return -_softplus(-x)


def _adaln_kernel(c_ref, w_ref, b_ref, o_ref):
    o_ref[...] = _dot(_silu(c_ref[...]), w_ref[...], HIGHEST) + b_ref[...]


def _adaln(c, w, b):
    n, d = c.shape
    d3 = w.shape[1]
    return pl.pallas_call(
        _adaln_kernel,
        out_shape=jax.ShapeDtypeStruct((n, d3), F32),
        grid=(d3 // d,),
        in_specs=[pl.BlockSpec((n, d), lambda j: (0, 0)),
                  pl.BlockSpec((d, d), lambda j: (0, j)),
                  pl.BlockSpec((1, d), lambda j: (0, j))],
        out_specs=pl.BlockSpec((n, d), lambda j: (0, j)),
        compiler_params=_params("arbitrary"),
        name="adaln",
    )(c, w, b.reshape(1, d3))


def _in_a_kernel(x_ref, shift_ref, scale_ref, g_ref, w1_ref, gq_ref, wq_ref, gkv_ref, wkv_ref, tab_ref,
                 q_ref, c_ref, kr_ref, m_ref, z_ref, gt_ref, *kv_refs,
                 heads, q_lora, kv_lora, rope, m_width, mk_cols, d_model, v_head, expand_kv):
    x = x_ref[0]
    hn = _rms(x, g_ref[...]) * (1.0 + _per_row(scale_ref[0], x_ref.shape[1])) + _per_row(shift_ref[0], x_ref.shape[1])
    y = _dot(hn.astype(BF16), w1_ref[...])
    o = 0
    qa = y[:, o:o + q_lora]; o += q_lora
    cl = y[:, o:o + kv_lora]; o += kv_lora
    kr1 = y[:, o:o + LANES]; o += LANES
    kr2 = y[:, o:o + LANES]; o += LANES
    m_ref[0] = y[:, o:o + m_width]
    mk = y[:, o + mk_cols[0]:o + mk_cols[1]]
    o += m_width
    z_ref[0] = y[:, o:o + d_model]; o += d_model
    gt_ref[0] = y[:, o:o + LANES]

    tab = tab_ref[...]
    cosq, sinq = tab[:, 0:LANES], tab[:, LANES:2 * LANES]
    cosk, sink = tab[:, 2 * LANES:3 * LANES], tab[:, 3 * LANES:4 * LANES]

    def sq_norm(xb):
        xf = xb.astype(F32)
        return jnp.sum(xf * xf, axis=-1, keepdims=True)

    lane = lax.broadcasted_iota(jnp.int32, (x.shape[0], LANES), 1)
    qq = _dot(_rms(qa, gq_ref[...]).astype(BF16), wq_ref[...])
    hw = heads * LANES
    qn2 = jnp.zeros((x.shape[0], LANES), F32)
    for h in range(heads):
        sl = slice(h * LANES, (h + 1) * LANES)
        qb = (qq[:, sl] * cosq + qq[:, hw + h * LANES:hw + (h + 1) * LANES] * sinq).astype(BF16)
        q_ref[0, :, sl] = qb
        if expand_kv:
            qn2 = jnp.where(lane == h, sq_norm(qb), qn2)

    cn = _rms(cl, gkv_ref[...])
    c_ref[0] = cn
    kr = kr1 * cosk + kr2 * sink
    kr_ref[0] = kr[:, LANES // 2:LANES // 2 + rope]
    if expand_kv:
        k_ref, vt_ref, qn_ref, kmax_ref, kt_ref = kv_refs
        kv = _dot(cn.astype(BF16), wkv_ref[...])
        kn2 = jnp.zeros((x.shape[0], LANES), F32)
        for h in range(heads):
            sl = slice(h * LANES, (h + 1) * LANES)
            kb = (kv[:, sl] + kr).astype(BF16)
            k_ref[0, :, sl] = kb
            kn2 = jnp.where(lane == h, sq_norm(kb), kn2)
        vt_ref[0] = kv[:, hw:hw + heads * v_head].T.astype(BF16)
        kt_ref[0] = mk.T
        qn_ref[0] = jnp.sqrt(qn2)
        kmax_ref[0, 0] = jnp.sqrt(jnp.max(kn2, axis=0, keepdims=True))


def _in_a(x, shift, scale, g, w1, gq, wq, gkv, wkv, tab, *, tm, heads, q_lora, kv_lora, rope, m_width,
          mk_cols, v_head, expand_kv):
    b, t, d = x.shape
    grid = (b, t // tm)
    tok = lambda last: pl.BlockSpec((1, tm, last), lambda i, j: (i, j, 0))
    tok_t = lambda rows: pl.BlockSpec((1, rows, tm), lambda i, j: (i, 0, j))
    const = lambda a: pl.BlockSpec(a.shape, lambda i, j: (0,) * a.ndim)
    out_shape = [jax.ShapeDtypeStruct((b, t, heads * LANES), BF16),
                 jax.ShapeDtypeStruct((b, t, kv_lora), F32),
                 jax.ShapeDtypeStruct((b, t, rope), F32),
                 jax.ShapeDtypeStruct((b, t, m_width), F32),
                 jax.ShapeDtypeStruct((b, t, d), F32),
                 jax.ShapeDtypeStruct((b, t, LANES), F32)]
    out_specs = [tok(heads * LANES), tok(kv_lora), tok(rope), tok(m_width), tok(d), tok(LANES)]
    if expand_kv:
        out_shape += [jax.ShapeDtypeStruct((b, t, heads * LANES), BF16),
                      jax.ShapeDtypeStruct((b, heads * v_head, t), BF16),
                      jax.ShapeDtypeStruct((b, t, LANES), F32),
                      jax.ShapeDtypeStruct((b, t // tm, 1, LANES), F32),
                      jax.ShapeDtypeStruct((b, mk_cols[1] - mk_cols[0], t), F32)]
        out_specs += [tok(heads * LANES), tok_t(heads * v_head), tok(LANES),
                      pl.BlockSpec((1, 1, 1, LANES), lambda i, j: (i, j, 0, 0)), tok_t(mk_cols[1] - mk_cols[0])]
    kern = functools.partial(_in_a_kernel, heads=heads, q_lora=q_lora, kv_lora=kv_lora, rope=rope,
                             m_width=m_width, mk_cols=mk_cols, d_model=d, v_head=v_head, expand_kv=expand_kv)
    return pl.pallas_call(
        kern, out_shape=out_shape, grid=grid,
        in_specs=[tok(d),
                  pl.BlockSpec((1,) + shift.shape[1:], lambda i, j: (i, 0, 0)),
                  pl.BlockSpec((1,) + scale.shape[1:], lambda i, j: (i, 0, 0)),
                  const(g), const(w1), const(gq), const(wq), const(gkv), const(wkv),
                  pl.BlockSpec((tm, 4 * LANES), lambda i, j: (j, 0))],
        out_specs=out_specs,
        compiler_params=_params("arbitrary", "arbitrary"),
        name="in_proj_a",
    )(x, shift, scale, g, w1, gq, wq, gkv, wkv, tab)


def _flash_kernel(qi_ref, ki_ref, q_ref, k_ref, vt_ref, o_ref, m_sc, l_sc, acc_sc, *,
                  heads, v_head, tq, tk, chunk):
    step_id = pl.program_id(1)
    qi = qi_ref[step_id]
    ki = ki_ref[step_id]

    @pl.when(ki == 0)
    def _():
        m_sc[...] = jnp.full(m_sc.shape, NEG_INF, F32)
        l_sc[...] = jnp.zeros(l_sc.shape, F32)
        acc_sc[...] = jnp.zeros(acc_sc.shape, F32)

    def step(masked):
        if masked:
            kc = (ki * tk + lax.broadcasted_iota(jnp.int32, (tk, tq), 0)) // chunk
            qc = (qi * tq + lax.broadcasted_iota(jnp.int32, (tk, tq), 1)) // chunk
            mask = kc <= qc
        for h in range(heads):
            qh = q_ref[0, :, h * LANES:(h + 1) * LANES]
            kh = k_ref[0, :, h * LANES:(h + 1) * LANES]
            vth = vt_ref[0, h * v_head:(h + 1) * v_head, :]
            rows = slice(h * v_head, (h + 1) * v_head)
            st = _dot_nt(kh, qh)
            if masked:
                st = jnp.where(mask, st, NEG_INF)
            m_prev = m_sc[h]
            m_new = jnp.maximum(m_prev, jnp.max(st, axis=0, keepdims=True))
            alpha = jnp.exp2(m_prev - m_new)
            p = jnp.exp2(st - m_new)
            l_sc[h] = alpha * l_sc[h] + jnp.sum(p, axis=0, keepdims=True)
            acc_sc[rows, :] = alpha * acc_sc[rows, :] + _dot(vth, p.astype(BF16))
            m_sc[h] = m_new

    full = (ki + 1) * tk <= qi * tq + chunk

    @pl.when(full)
    def _():
        step(False)

    @pl.when(jnp.logical_not(full))
    def _():
        step(True)

    @pl.when(ki == ((qi + 1) * tq - 1) // tk)
    def _():
        for h in range(heads):
            rows = slice(h * v_head, (h + 1) * v_head)
            acc_sc[rows, :] = acc_sc[rows, :] / l_sc[h]
        o_ref[0] = acc_sc[...].T


def _flash_bound_kernel(qi_ref, ki_ref, q_ref, k_ref, vt_ref, qn_ref, kmax_ref, o_ref, l_ref,
                        mb_sc, l_sc, acc_sc, *, heads, v_head, tq, tk, chunk):
    step_id = pl.program_id(1)
    qi = qi_ref[step_id]
    ki = ki_ref[step_id]

    @pl.when(ki == 0)
    def _():
        kmax = jnp.max(kmax_ref[0], axis=0) * FLASH_BOUND_SLACK
        for h in range(heads):
            mb_sc[h:h + 1, :] = qn_ref[0, h:h + 1, :] * kmax[:, h:h + 1]
        l_sc[...] = jnp.zeros(l_sc.shape, F32)
        acc_sc[...] = jnp.zeros(acc_sc.shape, F32)

    def step(masked):
        if masked:
            kc = (ki * tk + lax.broadcasted_iota(jnp.int32, (tk, tq), 0)) // chunk
            qc = (qi * tq + lax.broadcasted_iota(jnp.int32, (tk, tq), 1)) // chunk
            mask = kc <= qc
        for h0 in range(0, heads, FLASH_HEAD_GROUP):
            hs = range(h0, h0 + FLASH_HEAD_GROUP)
            st = [_dot_nt(k_ref[0, :, h * LANES:(h + 1) * LANES], q_ref[0, :, h * LANES:(h + 1) * LANES])
                  for h in hs]
            if masked:
                st = [jnp.where(mask, s, NEG_INF) for s in st]
            p = [jnp.exp2(s - mb_sc[h:h + 1, :]) for s, h in zip(st, hs)]
            for x, h in zip(p, hs):
                l_sc[h:h + 1, :] += jnp.sum(x, axis=0, keepdims=True)
            pv = [_dot(vt_ref[0, h * v_head:(h + 1) * v_head, :], x.astype(BF16)) for x, h in zip(p, hs)]
            for x, h in zip(pv, hs):
                acc_sc[h * v_head:(h + 1) * v_head, :] += x

    full = (ki + 1) * tk <= qi * tq + chunk

    @pl.when(full)
    def _():
        step(False)

    @pl.when(jnp.logical_not(full))
    def _():
        step(True)

    @pl.when(ki == ((qi + 1) * tq - 1) // tk)
    def _():
        for h in range(heads):
            rows = slice(h * v_head, (h + 1) * v_head)
            acc_sc[rows, :] = acc_sc[rows, :] / l_sc[h:h + 1, :]
        o_ref[0] = acc_sc[...].T
        l_ref[0] = l_sc[...]


def _flash(q, k, vt, *, heads, v_head, tq, tk, bound=None):
    b, t, _ = q.shape
    pairs = [(i, j) for i in range(t // tq) for j in range(((i + 1) * tq - 1) // tk + 1)]
    qi_tab = jnp.asarray([p[0] for p in pairs], jnp.int32)
    ki_tab = jnp.asarray([p[1] for p in pairs], jnp.int32)
    in_specs = [pl.BlockSpec((1, tq, heads * LANES), lambda i, s, qt, kt: (i, qt[s], 0)),
                pl.BlockSpec((1, tk, heads * LANES), lambda i, s, qt, kt: (i, kt[s], 0)),
                pl.BlockSpec((1, heads * v_head, tk), lambda i, s, qt, kt: (i, 0, kt[s]))]
    o_shape = jax.ShapeDtypeStruct((b, t, heads * v_head), F32)
    o_spec = pl.BlockSpec((1, tq, heads * v_head), lambda i, s, qt, kt: (i, qt[s], 0))
    acc = pltpu.VMEM((heads * v_head, tq), F32)
    if bound is None:
        kern, name, args = _flash_kernel, "flash_attn", (q, k, vt)
        out_shape, out_specs = o_shape, o_spec
        scratch = [pltpu.VMEM((heads, 1, tq), F32), pltpu.VMEM((heads, 1, tq), F32), acc]
    else:
        qn, kmax = bound
        assert qn.shape == (b, heads, t)
        kern, name, args = _flash_bound_kernel, "flash_attn_bound", (q, k, vt, qn, kmax)
        in_specs += [pl.BlockSpec((1, heads, tq), lambda i, s, qt, kt: (i, 0, qt[s])),
                     pl.BlockSpec((1,) + kmax.shape[1:], lambda i, s, qt, kt: (i, 0, 0, 0))]
        out_shape = [o_shape, jax.ShapeDtypeStruct((b, heads, t), F32)]
        out_specs = [o_spec, pl.BlockSpec((1, heads, tq), lambda i, s, qt, kt: (i, 0, qt[s]))]
        scratch = [pltpu.VMEM((heads, tq), F32), pltpu.VMEM((heads, tq), F32), acc]
    grid_spec = pltpu.PrefetchScalarGridSpec(
        num_scalar_prefetch=2, grid=(b, len(pairs)), in_specs=in_specs, out_specs=out_specs,
        scratch_shapes=scratch)
    return pl.pallas_call(
        functools.partial(kern, heads=heads, v_head=v_head, tq=tq, tk=tk, chunk=CHUNK),
        out_shape=out_shape,
        grid_spec=grid_spec,
        compiler_params=_params("arbitrary", "arbitrary"),
        name=name,
    )(qi_tab, ki_tab, *args)


def _latent_attn_kernel(q_ref, cp_ref, krp_ref, cn_ref, krn_ref, wk_ref, wv_ref, o_ref, *,
                        heads, nope, rope, v_head):
    q = q_ref[0]
    qabs, qrope = [], []
    for h in range(heads):
        qabs.append(_dot_nt(q[:, h * LANES:h * LANES + nope], wk_ref[h]))
        qrope.append(q[:, h * LANES + nope:h * LANES + nope + rope])
    qabs = jnp.concatenate(qabs, axis=0).astype(BF16)
    qrope = jnp.concatenate(qrope, axis=0)
    cp = cp_ref[0].astype(BF16)
    cn = cn_ref[0].astype(BF16)
    s_p = _dot_nt(qabs, cp) + _dot_nt(qrope, krp_ref[0].astype(BF16))
    s_n = _dot_nt(qabs, cn) + _dot_nt(qrope, krn_ref[0].astype(BF16))
    m = jnp.maximum(jnp.max(s_p, axis=-1, keepdims=True), jnp.max(s_n, axis=-1, keepdims=True))
    p_p = jnp.exp2(s_p - m)
    p_n = jnp.exp2(s_n - m)
    l = jnp.sum(p_p, axis=-1, keepdims=True) + jnp.sum(p_n, axis=-1, keepdims=True)
    o_lat = (_dot(p_p.astype(BF16), cp) + _dot(p_n.astype(BF16), cn)) / l
    t = q.shape[0]
    for h in range(heads):
        o_ref[0, :, h * v_head:(h + 1) * v_head] = _dot(o_lat[h * t:(h + 1) * t].astype(BF16), wv_ref[h])


def _latent_attn(q, c_past, kr_past, c_new, kr_new, wk, wv, *, heads, nope, rope, v_head):
    b, t, _ = q.shape
    past, kv_lora = c_past.shape[1:]
    blk = lambda n, last: pl.BlockSpec((1, n, last), lambda i: (i, 0, 0))
    const = lambda a: pl.BlockSpec(a.shape, lambda i: (0,) * a.ndim)
    kern = functools.partial(_latent_attn_kernel, heads=heads, nope=nope, rope=rope, v_head=v_head)
    return pl.pallas_call(
        kern,
        out_shape=jax.ShapeDtypeStruct((b, t, heads * v_head), F32),
        grid=(b,),
        in_specs=[blk(t, heads * LANES), blk(past, kv_lora), blk(past, rope), blk(t, kv_lora), blk(t, rope),
                  const(wk), const(wv)],
        out_specs=blk(t, heads * v_head),
        compiler_params=_params("arbitrary"),
        name="latent_attn",
    )(q, c_past, kr_past, c_new, kr_new, wk, wv)


def _mlstm_prep_kernel(m_ref, kt_ref, gc_ref, gr_ref, bc_ref, br_ref, pv_ref, kv_ref, b_ref, st_ref, *,
                       heads, dk, dv, chunk, nchunk):
    row = lax.broadcasted_iota(jnp.int32, (chunk, chunk), 0)
    col = lax.broadcasted_iota(jnp.int32, (chunk, chunk), 1)
    causal = col <= row
    tril = causal.astype(F32)
    triu = (row <= col).astype(F32)
    lane = lax.broadcasted_iota(jnp.int32, (chunk, LANES), 1)
    o_k, o_v = heads * dk, 2 * heads * dk

    rows = [slice(c * chunk, (c + 1) * chunk) for c in range(nchunk)]
    gc = [gc_ref[0, r, :] + bc_ref[...] for r in rows]
    gr = [gr_ref[0, c] + br_ref[...] for c in range(nchunk)]
    bcum_c = [_dot(tril, _log_sigmoid(g), HIGHEST) for g in gc]
    bcum_r = [_dot(_log_sigmoid(g), triu, HIGHEST) for g in gr]
    for c in range(nchunk):
        b_ref[0, rows[c], :] = bcum_c[c]
    ch = [(c, h) for c in range(nchunk) for h in range(heads)]
    v = [m_ref[0, rows[c], o_v + h * dv:o_v + (h + 1) * dv].astype(BF16) for c, h in ch]
    qk = [_dot_nt(m_ref[0, rows[c], h * dk:(h + 1) * dk].astype(BF16),
                  (m_ref[0, rows[c], o_k + h * dk:o_k + (h + 1) * dk] * (dk ** -0.5)).astype(BF16)) for c, h in ch]
    li_r = [gr[c][h:h + 1, :] for c, h in ch]
    b_r = [bcum_r[c][heads + h:heads + h + 1, :] for c, h in ch]
    dmat = [jnp.where(causal, bcum_c[c][:, heads + h:heads + h + 1] - b_r[i] + li_r[i], NEG_INF)
            for i, (c, h) in enumerate(ch)]
    mx = [jnp.max(d, axis=-1, keepdims=True) for d in dmat]
    p0 = [s * jnp.exp(d - m) for s, d, m in zip(qk, dmat, mx)]
    for i, (c, h) in enumerate(ch):
        pv_ref[0, rows[c], h * dv:(h + 1) * dv] = _dot(p0[i].astype(BF16), v[i])
    w_r = [jnp.exp(b_r[i][:, chunk - 1:chunk] - b_r[i] + li_r[i] - mx[i][chunk - 1:chunk, :]) * (dk ** -0.5)
           for i in range(len(ch))]
    for i, (c, h) in enumerate(ch):
        wkt = (kt_ref[0, h * dk:(h + 1) * dk, c * chunk:(c + 1) * chunk] * w_r[i]).astype(BF16)
        kv_ref[0, c, h] = _dot(wkt, jnp.concatenate([v[i], jnp.ones_like(v[i])], axis=1))
    psum = [jnp.sum(p, axis=-1, keepdims=True) for p in p0]
    for c in range(nchunk):
        stats = jnp.zeros((chunk, LANES), F32)
        for h in range(heads):
            stats = jnp.where(lane == h, mx[c * heads + h], stats)
            stats = jnp.where(lane == heads + h, psum[c * heads + h], stats)
        st_ref[0, rows[c], :] = stats


def _mlstm_scan_kernel(q_ref, mo_ref, pv_ref, kv_ref, b_ref, st_ref, gout_ref, c0_ref, m0_ref,
                       h_ref, c1_ref, m1_ref, c_sc, m_sc, *, heads, dk, dv, chunk, nchunk):
    t = pl.program_id(1)
    nseq = q_ref.shape[0]

    @pl.when(t == 0)
    def _():
        c_sc[...] = c0_ref[...]
        m_sc[...] = m0_ref[...]

    def body(c, carry):
        rows = pl.ds(pl.multiple_of(c * chunk, chunk), chunk)
        rep = lambda col: jnp.broadcast_to(col, (chunk, dv))
        ps = [(s, h) for s in range(nseq) for h in range(heads)]
        bcum = [b_ref[s, rows, :] for s in range(nseq)]
        stats = [st_ref[s, rows, :] for s in range(nseq)]
        c2 = [c_sc[s, h] for s, h in ps]
        qc = [_dot(q_ref[s, rows, h * dk:(h + 1) * dk].astype(BF16), c2[i].astype(BF16))
              for i, (s, h) in enumerate(ps)]
        mx = [rep(stats[s][:, h:h + 1]) for s, h in ps]
        inter = [rep(bcum[s][:, heads + h:heads + h + 1]) + m_sc[s, h] for s, h in ps]
        m = [jnp.maximum(a, b) for a, b in zip(inter, mx)]
        w_inter = [jnp.exp(a - b) for a, b in zip(inter, m)]
        r = [jnp.exp(a - b) for a, b in zip(mx, m)]
        for i, (s, h) in enumerate(ps):
            decay_end = w_inter[i][chunk - 1:chunk, :]
            f_new = r[i][chunk - 1:chunk, :]
            c_sc[s, h] = (jnp.concatenate([decay_end, decay_end], axis=1) * c2[i]
                          + jnp.concatenate([f_new, f_new], axis=1) * kv_ref[s, c, h])
            m_sc[s, h] = m[i][chunk - 1:chunk, 0:1]
        num = [w_inter[i] * qc[i][:, :dv] + r[i] * pv_ref[s, rows, h * dv:(h + 1) * dv] for i, (s, h) in enumerate(ps)]
        den = [w_inter[i] * qc[i][:, dv:] + r[i] * rep(stats[s][:, heads + h:heads + h + 1])
               for i, (s, h) in enumerate(ps)]
        hh = [a / jnp.maximum(jnp.abs(b), jnp.exp(-c_)) for a, b, c_ in zip(num, den, m)]
        hn = [_rms(hh[i], gout_ref[:, h * dv:(h + 1) * dv]) for i, (s, h) in enumerate(ps)]
        for i, (s, h) in enumerate(ps):
            h_ref[s, rows, h * dv:(h + 1) * dv] = hn[i] * _sigmoid(mo_ref[s, rows, h * dv:(h + 1) * dv])
        return carry

    lax.fori_loop(0, nchunk, body, 0)

    @pl.when(t == pl.num_programs(1) - 1)
    def _():
        c1_ref[...] = c_sc[...]
        m1_ref[...] = m_sc[...]


def _mlstm_prep(m_slab, kt, gates_c, gates_r, bias_c, bias_r, *, chunk, heads, dk, dv):
    b, t, mw = m_slab.shape
    const = lambda a: pl.BlockSpec(a.shape, lambda i, j: (0,) * a.ndim)
    qkv_w = 2 * heads * dk + heads * dv
    tp = min(t, MLSTM_PREP_TC)
    npc = tp // chunk
    tokp = lambda last: pl.BlockSpec((1, tp, last), lambda i, j: (i, j, 0))
    per_chunk = lambda *s: pl.BlockSpec((1, npc, heads) + s, lambda i, j: (i, j, 0) + (0,) * len(s))
    return pl.pallas_call(
        functools.partial(_mlstm_prep_kernel, heads=heads, dk=dk, dv=dv, chunk=chunk, nchunk=npc),
        out_shape=[jax.ShapeDtypeStruct((b, t, heads * dv), F32),
                   jax.ShapeDtypeStruct((b, t // chunk, heads, dk, 2 * dv), F32),
                   jax.ShapeDtypeStruct((b, t, LANES), F32),
                   jax.ShapeDtypeStruct((b, t, LANES), F32)],
        grid=(b, t // tp),
        in_specs=[tokp(qkv_w), pl.BlockSpec((1, heads * dk, tp), lambda i, j: (i, 0, j)), tokp(LANES),
                  pl.BlockSpec((1, npc, SUBLANES, chunk), lambda i, j: (i, j, 0, 0)),
                  const(bias_c), const(bias_r)],
        out_specs=[tokp(heads * dv), per_chunk(dk, 2 * dv), tokp(LANES), tokp(LANES)],
        compiler_params=_params("arbitrary", "arbitrary"),
        name="mlstm_prep",
    )(m_slab, kt, gates_c, gates_r, bias_c, bias_r)


def _mlstm_scan(m_slab, pv, kv, bcum, stats, gout, c0, n0, m0, *, tc, heads, dk, dv):
    b, t, mw = m_slab.shape
    chunk = min(CHUNK, t)
    const = lambda a: pl.BlockSpec(a.shape, lambda i, j: (0,) * a.ndim)
    qkv_w = 2 * heads * dk + heads * dv
    mo_blk, rem = divmod(qkv_w, heads * dv)
    assert rem == 0 and mw == qkv_w + heads * dv
    nchunk = tc // chunk
    ns = math.gcd(b, SCAN_SEQS)
    tok = lambda last, blk=0: pl.BlockSpec((ns, tc, last), lambda i, j: (i, j, blk))
    per_chunk = lambda *s: pl.BlockSpec((ns, nchunk, heads) + s, lambda i, j: (i, j, 0) + (0,) * len(s))
    st = lambda *s: pl.BlockSpec((ns,) + s, lambda i, j: (i,) + (0,) * len(s))
    cn0 = jnp.concatenate([c0, jnp.broadcast_to(n0[..., None], c0.shape)], axis=-1)
    hm, cn1, m1 = pl.pallas_call(
        functools.partial(_mlstm_scan_kernel, heads=heads, dk=dk, dv=dv, chunk=chunk, nchunk=nchunk),
        out_shape=[jax.ShapeDtypeStruct((b, t, heads * dv), F32),
                   jax.ShapeDtypeStruct((b, heads, dk, 2 * dv), F32),
                   jax.ShapeDtypeStruct((b, heads, 1, 1), F32)],
        grid=(b // ns, t // tc),
        in_specs=[tok(heads * dk),
                  tok(heads * dv, mo_blk),
                  tok(heads * dv), per_chunk(dk, 2 * dv), tok(LANES), tok(LANES),
                  const(gout), st(heads, dk, 2 * dv), st(heads, 1, 1)],
        out_specs=[tok(heads * dv), st(heads, dk, 2 * dv), st(heads, 1, 1)],
        scratch_shapes=[pltpu.VMEM((ns, heads, dk, 2 * dv), F32), pltpu.VMEM((ns, heads, 1, 1), F32)],
        compiler_params=_params("arbitrary", "arbitrary"),
        name="mlstm_scan",
    )(m_slab, m_slab, pv, kv, bcum, stats, gout, cn0, m0)
    return hm, cn1[..., :dv], cn1[..., dv], m1


def _mid_kernel(oa_ref, ob_ref, z_ref, x_ref, gate_ref, wo_ref, shift_ref, scale_ref, g_ref, w2_ref,
                x1_ref, qkv_ref, z2_ref, ab_ref, *, half, conv_ch, d_model):
    z = z_ref[0]
    ma = (oa_ref[0] * _silu(z[:, :half])).astype(BF16)
    mb = (ob_ref[0] * _silu(z[:, half:])).astype(BF16)
    y = _dot(ma, wo_ref[0:half, :]) + _dot(mb, wo_ref[half:, :])
    x1 = x_ref[0] + _per_row(gate_ref[0], x_ref.shape[1]) * y
    x1_ref[0] = x1
    hn = _rms(x1, g_ref[...]) * (1.0 + _per_row(scale_ref[0], x_ref.shape[1])) + _per_row(shift_ref[0], x_ref.shape[1])
    y2 = _dot(hn.astype(BF16), w2_ref[...])
    qkv_ref[0] = y2[:, :conv_ch]
    z2_ref[0] = y2[:, conv_ch:conv_ch + d_model]
    ab_ref[0] = y2[:, conv_ch + d_model:]


def _mid(oa, ob, z, x, gate, wo, shift, scale, g, w2, *, tm, conv_ch):
    b, t, d = x.shape
    half = oa.shape[-1]
    tok = lambda last: pl.BlockSpec((1, tm, last), lambda i, j: (i, j, 0))
    vec = pl.BlockSpec((1,) + gate.shape[1:], lambda i, j: (i, 0, 0))
    const = lambda a: pl.BlockSpec(a.shape, lambda i, j: (0,) * a.ndim)
    kern = functools.partial(_mid_kernel, half=half, conv_ch=conv_ch, d_model=d)
    return pl.pallas_call(
        kern,
        out_shape=[jax.ShapeDtypeStruct((b, t, d), F32), jax.ShapeDtypeStruct((b, t, conv_ch), F32),
                   jax.ShapeDtypeStruct((b, t, d), F32), jax.ShapeDtypeStruct((b, t, LANES), F32)],
        grid=(b, t // tm),
        in_specs=[tok(half), tok(ob.shape[-1]), tok(d), tok(d), vec, const(wo), vec, vec, const(g), const(w2)],
        out_specs=[tok(d), tok(conv_ch), tok(d), tok(LANES)],
        compiler_params=_params("arbitrary", "arbitrary"),
        name="out_a_in_c",
    )(oa, ob, z, x, gate, wo, shift, scale, g, w2)


def _conv_kernel(qkv_ref, past_ref, wc_ref, ab_ref, alog_ref, dtb_ref, act_ref, gb_ref, ext_sc, *,
                 tm, width, heads, dk):
    @pl.when(pl.program_id(1) == 0)
    def _():
        ext_sc[0:SUBLANES, :] = past_ref[0]

    ext_sc[SUBLANES:SUBLANES + tm, :] = qkv_ref[0]
    conv = wc_ref[width - 1:width, :] * ext_sc[SUBLANES:SUBLANES + tm, :]
    for j in range(width - 1):
        s = SUBLANES - (width - 1) + j
        conv = conv + wc_ref[j:j + 1, :] * ext_sc[s:s + tm, :]
    ext_sc[0:SUBLANES, :] = ext_sc[tm:tm + SUBLANES, :]
    act = _silu(conv)
    for h in range(2 * heads):
        xh = act[:, h * dk:(h + 1) * dk]
        xh = xh * lax.rsqrt(jnp.sum(xh * xh, axis=-1, keepdims=True) + EPS)
        if h < heads:
            xh = xh * (dk ** -0.5)
        act_ref[0, :, h * dk:(h + 1) * dk] = xh
    act_ref[0, :, 2 * heads * dk:] = act[:, 2 * heads * dk:]
    ab = ab_ref[0]
    g = -jnp.exp(alog_ref[...]) * _softplus(ab + dtb_ref[...])
    lane = lax.broadcasted_iota(jnp.int32, ab.shape, 1)
    gb_ref[0] = jnp.where(lane < heads, g, _sigmoid(ab))


def _conv(qkv, past8, wc8, ab, alog, dtb, *, tm, width, heads, dk):
    b, t, ch = qkv.shape
    tok = lambda last: pl.BlockSpec((1, tm, last), lambda i, j: (i, j, 0))
    const = lambda a: pl.BlockSpec(a.shape, lambda i, j: (0,) * a.ndim)
    kern = functools.partial(_conv_kernel, tm=tm, width=width, heads=heads, dk=dk)
    return pl.pallas_call(
        kern,
        out_shape=[jax.ShapeDtypeStruct((b, t, ch), F32), jax.ShapeDtypeStruct((b, t, LANES), F32)],
        grid=(b, t // tm),
        in_specs=[tok(ch), pl.BlockSpec((1, SUBLANES, ch), lambda i, j: (i, 0, 0)), const(wc8), tok(LANES),
                  const(alog), const(dtb)],
        out_specs=[tok(ch), tok(LANES)],
        scratch_shapes=[pltpu.VMEM((tm + SUBLANES, ch), F32)],
        compiler_params=_params("arbitrary", "arbitrary"),
        name="conv_gates",
    )(qkv, past8, wc8, ab, alog, dtb)


def _blockdiag(x, group, chunk):
    w = group * chunk
    br = lax.broadcasted_iota(jnp.int32, (w, w), 0) // chunk
    bc = lax.broadcasted_iota(jnp.int32, (w, w), 1) // chunk
    xb = x.astype(BF16)
    return jnp.where(br == bc, jnp.concatenate([xb] * group, axis=0), jnp.zeros((), BF16))


def _unit_lower_inverses_minus_eye(a_list, group, chunk):
    w = group * chunk
    r = lax.broadcasted_iota(jnp.int32, (chunk, w), 0)
    cc = lax.broadcasted_iota(jnp.int32, (chunk, w), 1) % chunk
    es = [-jnp.where((r // 2 == cc // 2) & (r % 2 == 1) & (cc % 2 == 0), a4, 0.0) for a4 in a_list]
    s = 2
    while s < chunk:
        off = (r // (2 * s) == cc // (2 * s)) & (r % (2 * s) >= s) & (cc % (2 * s) < s)
        a_offs = [jnp.where(off, a4, 0.0) for a4 in a_list]
        ps = [a + _dot(a.astype(BF16), _blockdiag(e, group, chunk)) for a, e in zip(a_offs, es)]
        es = [e - (p + _dot(e.astype(BF16), _blockdiag(p, group, chunk))) for e, p in zip(es, ps)]
        s *= 2
    return es


def _gdn_prep_kernel(act_ref, gbc_ref, gbr_ref, w_ref, uv_ref, kd_ref, attn_ref, eg_ref, *,
                     heads, dk, dv, chunk, nchunk, group):
    row = lax.broadcasted_iota(jnp.int32, (chunk, chunk), 0)
    col = lax.broadcasted_iota(jnp.int32, (chunk, chunk), 1)
    incl = col <= row
    strict = col < row
    tril = incl.astype(F32)
    triu = (row <= col).astype(F32)
    lane = lax.broadcasted_iota(jnp.int32, (chunk, LANES), 1)
    o_k, o_v = heads * dk, 2 * heads * dk

    rows = [slice(c * chunk, (c + 1) * chunk) for c in range(nchunk)]
    gbc = [gbc_ref[0, r, :] for r in rows]
    gcum_c = [_dot(tril, g, HIGHEST) for g in gbc]
    gcum_r = [_dot(gbr_ref[0, c], triu, HIGHEST) for c in range(nchunk)]
    for c in range(nchunk):
        eg_ref[0, rows[c], :] = jnp.where(lane < heads, jnp.exp(gcum_c[c]), 0.0)
    ch = [(c, h) for c in range(nchunk) for h in range(heads)]
    k = [act_ref[0, rows[c], o_k + h * dk:o_k + (h + 1) * dk] for c, h in ch]
    kb = [x.astype(BF16) for x in k]
    kk = [_dot_nt(x, x) for x in kb]
    qk = [_dot_nt(act_ref[0, rows[c], h * dk:(h + 1) * dk].astype(BF16), kb[i]) for i, (c, h) in enumerate(ch)]
    g_c = [gcum_c[c][:, h:h + 1] for c, h in ch]
    beta = [gbc[c][:, heads + h:heads + h + 1] for c, h in ch]
    decay = [jnp.exp(jnp.where(incl, g_c[i] - gcum_r[c][h:h + 1, :], NEG_INF)) for i, (c, h) in enumerate(ch)]
    a_blk = [jnp.where(strict, beta[i] * kk[i] * decay[i], 0.0) for i in range(len(ch))]
    for i, (c, h) in enumerate(ch):
        attn_ref[0, rows[c], h * chunk:(h + 1) * chunk] = (qk[i] * decay[i]).astype(BF16)
        kd_ref[0, rows[c], h * dk:(h + 1) * dk] = (k[i] * jnp.exp(g_c[i][chunk - 1:chunk, :] - g_c[i])).astype(BF16)
    rhs_blk = [jnp.concatenate([beta[i] * act_ref[0, rows[c], o_v + h * dv:o_v + (h + 1) * dv],
                                (beta[i] * jnp.exp(g_c[i])) * k[i]], axis=1) for i, (c, h) in enumerate(ch)]
    problems = [(c, g0) for c in range(nchunk) for g0 in range(0, heads, group)]
    a_list = [jnp.concatenate(a_blk[c * heads + g0:c * heads + g0 + group], axis=1) for c, g0 in problems]
    rhs_list = [jnp.concatenate(rhs_blk[c * heads + g0:c * heads + g0 + group], axis=0) for c, g0 in problems]
    e_list = _unit_lower_inverses_minus_eye(a_list, group, chunk)
    sols = [rhs + _dot(_blockdiag(e, group, chunk), rhs.astype(BF16)) for e, rhs in zip(e_list, rhs_list)]
    for (c, g0), sol in zip(problems, sols):
        for i, h in enumerate(range(g0, g0 + group)):
            uv_ref[0, rows[c], h * dv:(h + 1) * dv] = sol[i * chunk:(i + 1) * chunk, :dv]
            w_ref[0, rows[c], h * dk:(h + 1) * dk] = sol[i * chunk:(i + 1) * chunk, dv:].astype(BF16)


def _gdn_prep(act, gb_c, gb_r, *, chunk, tc, heads, dk, dv):
    b, t, _ = act.shape
    tc = min(tc, t)
    nchunk = tc // chunk
    group = (2 * LANES) // chunk
    kern = functools.partial(_gdn_prep_kernel, heads=heads, dk=dk, dv=dv, chunk=chunk, nchunk=nchunk, group=group)
    tok = lambda last: pl.BlockSpec((1, tc, last), lambda i, j: (i, j, 0))
    return pl.pallas_call(
        kern,
        out_shape=[jax.ShapeDtypeStruct((b, t, heads * dk), BF16), jax.ShapeDtypeStruct((b, t, heads * dv), F32),
                   jax.ShapeDtypeStruct((b, t, heads * dk), BF16), jax.ShapeDtypeStruct((b, t, heads * chunk), BF16),
                   jax.ShapeDtypeStruct((b, t, LANES), F32)],
        grid=(b, t // tc),
        in_specs=[tok(act.shape[-1]), tok(LANES),
                  pl.BlockSpec((1, nchunk, 2 * SUBLANES, chunk), lambda i, j: (i, j, 0, 0))],
        out_specs=[tok(heads * dk), tok(heads * dv), tok(heads * dk), tok(heads * chunk), tok(LANES)],
        compiler_params=_params("arbitrary", "arbitrary"),
        name="gdn_prep",
    )(act, gb_c, gb_r)


def _gdn_scan_kernel(q_ref, w_ref, uv_ref, kd_ref, attn_ref, eg_ref, gout_ref, s0_ref, o_ref, s1_ref, s_sc, *,
                     heads, dk, dv, chunk, nchunk):
    t = pl.program_id(1)
    nseq = q_ref.shape[0]

    @pl.when(t == 0)
    def _():
        s_sc[...] = s0_ref[...]

    def body(c, carry):
        rows = pl.ds(pl.multiple_of(c * chunk, chunk), chunk)
        ps = [(s, h) for s in range(nseq) for h in range(heads)]
        eg = [eg_ref[s, rows, :] for s in range(nseq)]
        s0 = [s_sc[s, h] for s, h in ps]
        s0b = [x.astype(BF16) for x in s0]
        ws = [_dot(w_ref[s, rows, h * dk:(h + 1) * dk], s0b[i]) for i, (s, h) in enumerate(ps)]
        ub = [(uv_ref[s, rows, h * dv:(h + 1) * dv] - ws[i]).astype(BF16) for i, (s, h) in enumerate(ps)]
        eg_h = [jnp.broadcast_to(eg[s][:, h:h + 1], (chunk, dv)) for s, h in ps]
        ku = [_dot_tn(kd_ref[s, rows, h * dk:(h + 1) * dk], ub[i]) for i, (s, h) in enumerate(ps)]
        for i, (s, h) in enumerate(ps):
            s_sc[s, h] = eg_h[i][chunk - 1:chunk, :] * s0[i] + ku[i]
        qs = [_dot(q_ref[s, rows, h * dk:(h + 1) * dk].astype(BF16), s0b[i]) for i, (s, h) in enumerate(ps)]
        au = [_dot(attn_ref[s, rows, h * chunk:(h + 1) * chunk], ub[i]) for i, (s, h) in enumerate(ps)]
        on = [_rms(eg_h[i] * qs[i] + au[i], gout_ref[...]) for i in range(len(ps))]
        for i, (s, h) in enumerate(ps):
            o_ref[s, rows, h * dv:(h + 1) * dv] = on[i]
        return carry

    lax.fori_loop(0, nchunk, body, 0)

    @pl.when(t == pl.num_programs(1) - 1)
    def _():
        s1_ref[...] = s_sc[...]


def _gdn_scan(act, w, uv, kd, attn, eg, gout, s0, *, tc, heads, dk, dv):
    b, t, _ = act.shape
    chunk = min(CHUNK, t)
    nchunk = tc // chunk
    kern = functools.partial(_gdn_scan_kernel, heads=heads, dk=dk, dv=dv, chunk=chunk, nchunk=nchunk)
    ns = math.gcd(b, SCAN_SEQS)
    tok = lambda last: pl.BlockSpec((ns, tc, last), lambda i, j: (i, j, 0))
    state = pl.BlockSpec((ns, heads, dk, dv), lambda i, j: (i, 0, 0, 0))
    return pl.pallas_call(
        kern,
        out_shape=[jax.ShapeDtypeStruct((b, t, heads * dv), F32), jax.ShapeDtypeStruct(s0.shape, F32)],
        grid=(b // ns, t // tc),
        in_specs=[tok(heads * dk),
                  tok(heads * dk), tok(heads * dv), tok(heads * dk), tok(heads * chunk), tok(LANES),
                  pl.BlockSpec(gout.shape, lambda i, j: (0, 0)), state],
        out_specs=[tok(heads * dv), state],
        scratch_shapes=[pltpu.VMEM((ns, heads, dk, dv), F32)],
        compiler_params=_params("arbitrary", "arbitrary"),
        name="gdn_scan",
    )(act, w, uv, kd, attn, eg, gout, s0)


def _final_kernel(o_ref, z_ref, x_ref, gate_ref, wo_ref, g_ref, y_ref):
    mixed = (o_ref[0] * _silu(z_ref[0])).astype(BF16)
    x2 = x_ref[0] + _per_row(gate_ref[0], x_ref.shape[1]) * _dot(mixed, wo_ref[...])
    y_ref[0] = _rms(x2, g_ref[...])


def _final(o, z, x, gate, wo, g, *, tm):
    b, t, d = x.shape
    tok = lambda last: pl.BlockSpec((1, tm, last), lambda i, j: (i, j, 0))
    const = lambda a: pl.BlockSpec(a.shape, lambda i, j: (0,) * a.ndim)
    return pl.pallas_call(
        _final_kernel,
        out_shape=jax.ShapeDtypeStruct((b, t, d), F32),
        grid=(b, t // tm),
        in_specs=[tok(o.shape[-1]), tok(d), tok(d), pl.BlockSpec((1,) + gate.shape[1:], lambda i, j: (i, 0, 0)),
                  const(wo), const(g)],
        out_specs=tok(d),
        compiler_params=_params("arbitrary", "arbitrary"),
        name="out_c_final",
    )(o, z, x, gate, wo, g)


def _pad_lanes(w, width=LANES, at=0):
    out = jnp.zeros(w.shape[:-1] + (width,), w.dtype)
    return out.at[..., at:at + w.shape[-1]].set(w)


def _rot_half_cols(w):
    r = w.shape[-1] // 2
    return jnp.concatenate([-w[..., r:], w[..., :r]], axis=-1)


def _rope_tables(pos, rope, scale):
    freqs = jnp.exp(jnp.arange(0, rope, 2, dtype=F32) * (-math.log(ROPE_BASE) / rope))
    ang = pos.astype(F32)[:, None] * freqs[None, :]
    cos = jnp.concatenate([jnp.cos(ang), jnp.cos(ang)], axis=-1)
    sin = jnp.concatenate([jnp.sin(ang), jnp.sin(ang)], axis=-1)
    half = LANES // 2
    cosk = _pad_lanes(cos, at=half)
    sink = _pad_lanes(sin, at=half)
    ones = _pad_lanes(jnp.ones((pos.shape[0], half), F32))
    return jnp.concatenate([(cosk + ones) * scale, sink * scale, cosk, sink], axis=-1)


def _tokens_on_lanes(a, chunk, rows):
    b, t = a.shape[:2]
    return a[..., :rows].reshape(b, t // chunk, chunk, rows).transpose(0, 1, 3, 2)


def kernel(x_prompt, x_sample, c_prompt, c_sample, cache_kv_latent, cache_k_rope, state_mlstm_C, state_mlstm_n, state_mlstm_m, state_gdn_S, state_gdn_conv, a_w_ada, a_b_ada, a_g_norm, a_w_in, a_g_q_a, a_w_q_b, a_g_kv_a, a_w_kv_b, a_b_i, a_b_f, a_g_out, a_w_out, c_w_ada, c_b_ada, c_g_norm, c_w_in, c_w_conv, c_a_log, c_dt_bias, c_g_out, c_w_out, g_final):
    d = x_prompt.shape[-1]
    q_lora, heads, qk = a_w_q_b.shape
    kv_lora = a_w_kv_b.shape[0]
    rope = cache_k_rope.shape[-1]
    nope = qk - rope
    v_head = a_w_kv_b.shape[2] - nope
    m_heads, m_dv = a_g_out.shape
    m_dk = state_mlstm_C.shape[2]
    g_heads = c_a_log.shape[0]
    g_dk, g_dv = state_gdn_S.shape[2:]
    width = c_w_conv.shape[0]
    conv_ch = c_w_conv.shape[1]
    assert nope + rope <= LANES and nope == LANES // 2 and 2 * m_heads <= SUBLANES and 2 * g_heads <= 2 * SUBLANES

    sizes = (q_lora, kv_lora, rope, m_heads * m_dk, m_heads * m_dk, m_heads * m_dv, m_heads, m_heads,
             m_heads * m_dv, heads * v_head + m_heads * m_dv)
    offs = [0]
    for s in sizes:
        offs.append(offs[-1] + s)
    w_qa, w_c, w_kr, w_mq, w_mk, w_mv, w_mi, w_mf, w_mo, w_z = [a_w_in[:, offs[i]:offs[i + 1]] for i in range(10)]
    half = LANES // 2
    w1 = jnp.concatenate([w_qa, w_c, _pad_lanes(w_kr, at=half), _pad_lanes(_rot_half_cols(w_kr), at=half),
                          w_mq, w_mk, w_mv, w_mo, w_z, _pad_lanes(jnp.concatenate([w_mi, w_mf], axis=1))],
                         axis=1).astype(BF16)
    m_width = 2 * m_heads * m_dk + 2 * m_heads * m_dv
    wq_rope = a_w_q_b[..., nope:]
    wq_main = _pad_lanes(a_w_q_b).reshape(q_lora, heads * LANES)
    wq_rot = _pad_lanes(_rot_half_cols(wq_rope), at=nope).reshape(q_lora, heads * LANES)
    wq = jnp.concatenate([wq_main, wq_rot], axis=1).astype(BF16)
    wkv = jnp.concatenate([_pad_lanes(a_w_kv_b[..., :nope]).reshape(kv_lora, heads * LANES),
                           a_w_kv_b[..., nope:].reshape(kv_lora, heads * v_head)], axis=1).astype(BF16)
    wk_abs = a_w_kv_b[..., :nope].transpose(1, 0, 2).astype(BF16)
    wv_abs = a_w_kv_b[..., nope:].transpose(1, 0, 2).astype(BF16)
    wo_a = a_w_out.astype(BF16)
    csz = (conv_ch, g_heads, g_heads, g_heads * g_dv)
    w_qkv, w_a, w_b, w_zc = [c_w_in[:, sum(csz[:i]):sum(csz[:i + 1])] for i in range(4)]
    w2 = jnp.concatenate([w_qkv, w_zc, _pad_lanes(jnp.concatenate([w_a, w_b], axis=1))], axis=1).astype(BF16)
    wo_c = c_w_out.astype(BF16)
    wc8 = jnp.zeros((SUBLANES, conv_ch), F32).at[:width].set(c_w_conv)
    row = lambda a: a.reshape(1, -1).astype(F32)
    bias_c = _pad_lanes(jnp.concatenate([a_b_i, a_b_f]).reshape(1, -1))
    bias_r = jnp.zeros((SUBLANES, 1), F32).at[:2 * m_heads, 0].set(jnp.concatenate([a_b_i, a_b_f]))
    alog = _pad_lanes(c_a_log.reshape(1, -1))
    dtb = _pad_lanes(c_dt_bias.reshape(1, -1))

    bp, bs = c_prompt.shape[0], c_sample.shape[0]
    c_all = jnp.concatenate([c_prompt, c_sample], axis=0)
    pad = (-c_all.shape[0]) % SUBLANES
    c_all = jnp.pad(c_all, ((0, pad), (0, 0)))
    mod_a = _adaln(c_all, a_w_ada, a_b_ada)
    mod_c = _adaln(c_all, c_w_ada, c_b_ada)

    def mods(mod, lo, hi):
        return [mod[lo:hi, i * d:(i + 1) * d][:, None, :] for i in range(3)]

    def run(x, mod_lo, mod_hi, c_past, kr_past, c0, n0, m0, conv0, s0):
        b, t, _ = x.shape
        past = 0 if c_past is None else c_past.shape[1]
        chunk = min(CHUNK, t)
        tc = min(t, 512)
        expand = c_past is None
        nb = 1 if expand else math.gcd(b, max(1, PROJ_TM // t))
        tm = min(nb * t, PROJ_TM)
        grp = lambda a: a.reshape((b // nb, nb * a.shape[1]) + a.shape[2:])
        ungrp = lambda a: a.reshape((b, a.shape[1] // nb) + a.shape[2:])
        shift_a, scale_a, gate_a = [grp(m) for m in mods(mod_a, mod_lo, mod_hi)]
        shift_c, scale_c, gate_c = [grp(m) for m in mods(mod_c, mod_lo, mod_hi)]
        tab = _rope_tables(past + jnp.arange(t, dtype=jnp.int32), rope, qk ** -0.5 * math.log2(math.e))
        outs = _in_a(grp(x), shift_a, scale_a, row(a_g_norm), w1, row(a_g_q_a), wq, row(a_g_kv_a), wkv,
                     jnp.tile(tab, (nb, 1)),
                     tm=tm, heads=heads, q_lora=q_lora, kv_lora=kv_lora, rope=rope, m_width=m_width,
                     mk_cols=(m_heads * m_dk, 2 * m_heads * m_dk), v_head=v_head, expand_kv=expand)
        q, c_new, kr_new, m_slab, z, gates = [ungrp(a) for a in outs[:6]]
        kt = None
        if expand:
            k_all, vt, kt = outs[6], outs[7], outs[10]
            qn = outs[8][..., :heads].transpose(0, 2, 1)
            tiles = dict(heads=heads, v_head=v_head, tq=min(t, FLASH_TQ), tk=min(t, FLASH_TK))
            o_fast, row_sums = _flash(q, k_all, vt, bound=(qn, outs[9]), **tiles)
            o_mla = lax.cond(jnp.min(row_sums) >= FLASH_ROW_SUM_MIN,
                             lambda: o_fast, lambda: _flash(q, k_all, vt, **tiles))
        else:
            o_mla = _latent_attn(q, c_past, kr_past, c_new, kr_new, wk_abs, wv_abs,
                                 heads=heads, nope=nope, rope=rope, v_head=v_head)
        fold = t == chunk and b > 1
        flat = (lambda a: a.reshape((1, b * a.shape[1]) + a.shape[2:])) if fold else (lambda a: a)
        unflat = (lambda a: a.reshape((b, a.shape[1] // b) + a.shape[2:])) if fold else (lambda a: a)
        m_flat = flat(m_slab)
        if kt is None:
            kt = m_flat[..., m_heads * m_dk:2 * m_heads * m_dk].transpose(0, 2, 1)
        pv, kvs, bcum, stats = [unflat(a) for a in _mlstm_prep(
            m_flat, kt, flat(gates), flat(_tokens_on_lanes(gates, chunk, SUBLANES)), bias_c, bias_r,
            chunk=chunk, heads=m_heads, dk=m_dk, dv=m_dv)]
        hm, c1, n1, m1 = _mlstm_scan(m_slab, pv, kvs, bcum, stats, row(a_g_out), c0, n0, m0.reshape(b, m_heads, 1, 1),
                                     tc=tc, heads=m_heads, dk=m_dk, dv=m_dv)
        x1, qkv, zc, ab = [ungrp(a) for a in _mid(grp(o_mla), grp(hm), grp(z), grp(x), gate_a, wo_a, shift_c, scale_c,
                                                  row(c_g_norm), w2, tm=tm, conv_ch=conv_ch)]
        past8 = jnp.pad(conv0, ((0, 0), (SUBLANES - (width - 1), 0), (0, 0)))
        act, gb = _conv(qkv, past8, wc8, ab, alog, dtb, tm=min(t, PROJ_TM), width=width, heads=g_heads, dk=g_dk)
        w, uv, kd, attn, eg = [unflat(a) for a in _gdn_prep(
            flat(act), flat(gb), flat(_tokens_on_lanes(gb, chunk, 2 * SUBLANES)),
            chunk=chunk, tc=GDN_PREP_TC, heads=g_heads, dk=g_dk, dv=g_dv)]
        o_gdn, s1 = _gdn_scan(act, w, uv, kd, attn, eg, row(c_g_out), s0, tc=min(t, GDN_SCAN_TC),
                              heads=g_heads, dk=g_dk, dv=g_dv)
        y = ungrp(_final(grp(o_gdn), grp(zc), grp(x1), gate_c, wo_c, row(g_final), tm=tm))
        conv1 = jnp.concatenate([conv0, qkv], axis=1)[:, t:] if t < width - 1 else qkv[:, t - (width - 1):]
        return (y, c_new, kr_new, c1, n1, m1.reshape(b, m_heads), conv1, s1)

    dt = x_prompt.dtype
    (y_p, p_kv, p_kr, p_c, p_n, p_m, p_conv, p_s) = run(
        x_prompt, 0, bp, None, None,
        jnp.zeros((bp, m_heads, m_dk, m_dv), dt), jnp.zeros((bp, m_heads, m_dk), dt), jnp.zeros((bp, m_heads), dt),
        jnp.zeros((bp, width - 1, conv_ch), dt), jnp.zeros((bp, g_heads, g_dk, g_dv), dt))
    (y_s, s_kv, s_kr, s_c, s_n, s_m, s_conv, s_s) = run(
        x_sample, bp, bp + bs, cache_kv_latent, cache_k_rope, state_mlstm_C, state_mlstm_n, state_mlstm_m,
        state_gdn_conv, state_gdn_S)
    return (y_p, y_s, p_kv, p_kr, p_c, p_n, p_m, p_s, p_conv,
            s_kv, s_kr, s_c, s_n, s_m, s_s, s_conv)
```

```python
import functools
import math

import jax
import jax.numpy as jnp
from jax import lax
from jax.experimental import pallas as pl
from jax.experimental.pallas import tpu as pltpu

F32 = jnp.float32
BF16 = jnp.bfloat16
HIGHEST = lax.Precision.HIGHEST

CHUNK = 64
EPS = 1e-6
ROPE_BASE = 10000.0
LANES = 128
SUBLANES = 8
VMEM_LIMIT = 56 * 1024 * 1024
NEG_INF = float("-inf")
SCAN_SEQS = 4
GDN_SCAN_TC = 256
PROJ_TM = 512
FLASH_TQ = 512
FLASH_TK = 512
FLASH_HEAD_GROUP = 4
FLASH_BOUND_SLACK = 1.0 + 2.0 ** -6
FLASH_ROW_SUM_MIN = 2.0 ** -100
MLSTM_PREP_TC = 256
GDN_PREP_TC = 512


def _params(*sem):
    return pltpu.CompilerParams(dimension_semantics=sem, vmem_limit_bytes=VMEM_LIMIT)


def _dot(a, b, precision=None):
    return jnp.dot(a, b, preferred_element_type=F32, precision=precision)


def _dot_nt(a, b):
    return lax.dot_general(a, b, (((1,), (1,)), ((), ())), preferred_element_type=F32)


def _dot_tn(a, b):
    return lax.dot_general(a, b, (((0,), (0,)), ((), ())), preferred_element_type=F32)


def _rms(x, g):
    return x * lax.rsqrt(jnp.mean(x * x, axis=-1, keepdims=True) + EPS) * g


def _per_row(m, rows):
    n, d = m.shape
    if n == 1:
        return m
    return jnp.concatenate([jnp.broadcast_to(m[i:i + 1], (rows // n, d)) for i in range(n)], axis=0)


def _sigmoid(x):
    return 0.5 * jnp.tanh(0.5 * x) + 0.5


def _silu(x):
    return x * _sigmoid(x)


def _softplus(x):
    return jnp.maximum(x, 0.0) + jnp.log1p(jnp.exp(-jnp.abs(x)))


def _log_sigmoid(x):
    return -_softplus(-x)


def _adaln_kernel(c_ref, w_ref, b_ref, o_ref):
    o_ref[...] = _dot(_silu(c_ref[...]), w_ref[...], HIGHEST) + b_ref[...]


def _adaln(c, w, b):
    n, d = c.shape
    d3 = w.shape[1]
    return pl.pallas_call(
        _adaln_kernel,
        out_shape=jax.ShapeDtypeStruct((n, d3), F32),
        grid=(d3 // d,),
        in_specs=[pl.BlockSpec((n, d), lambda j: (0, 0)),
                  pl.BlockSpec((d, d), lambda j: (0, j)),
                  pl.BlockSpec((1, d), lambda j: (0, j))],
        out_specs=pl.BlockSpec((n, d), lambda j: (0, j)),
        compiler_params=_params("arbitrary"),
        name="adaln",
    )(c, w, b.reshape(1, d3))


def _in_a_kernel(x_ref, shift_ref, scale_ref, g_ref, w1_ref, gq_ref, wq_ref, gkv_ref, wkv_ref, tab_ref,
                 q_ref, c_ref, kr_ref, m_ref, z_ref, gt_ref, *kv_refs,
                 heads, q_lora, kv_lora, rope, m_width, mk_cols, d_model, v_head, expand_kv):
    x = x_ref[0]
    hn = _rms(x, g_ref[...]) * (1.0 + _per_row(scale_ref[0], x_ref.shape[1])) + _per_row(shift_ref[0], x_ref.shape[1])
    y = _dot(hn.astype(BF16), w1_ref[...])
    o = 0
    qa = y[:, o:o + q_lora]; o += q_lora
    cl = y[:, o:o + kv_lora]; o += kv_lora
    kr1 = y[:, o:o + LANES]; o += LANES
    kr2 = y[:, o:o + LANES]; o += LANES
    m_ref[0] = y[:, o:o + m_width]
    mk = y[:, o + mk_cols[0]:o + mk_cols[1]]
    o += m_width
    z_ref[0] = y[:, o:o + d_model]; o += d_model
    gt_ref[0] = y[:, o:o + LANES]

    tab = tab_ref[...]
    cosq, sinq = tab[:, 0:LANES], tab[:, LANES:2 * LANES]
    cosk, sink = tab[:, 2 * LANES:3 * LANES], tab[:, 3 * LANES:4 * LANES]

    def sq_norm(xb):
        xf = xb.astype(F32)
        return jnp.sum(xf * xf, axis=-1, keepdims=True)

    lane = lax.broadcasted_iota(jnp.int32, (x.shape[0], LANES), 1)
    qq = _dot(_rms(qa, gq_ref[...]).astype(BF16), wq_ref[...])
    hw = heads * LANES
    qn2 = jnp.zeros((x.shape[0], LANES), F32)
    for h in range(heads):
        sl = slice(h * LANES, (h + 1) * LANES)
        qb = (qq[:, sl] * cosq + qq[:, hw + h * LANES:hw + (h + 1) * LANES] * sinq).astype(BF16)
        q_ref[0, :, sl] = qb
        if expand_kv:
            qn2 = jnp.where(lane == h, sq_norm(qb), qn2)

    cn = _rms(cl, gkv_ref[...])
    c_ref[0] = cn
    kr = kr1 * cosk + kr2 * sink
    kr_ref[0] = kr[:, LANES // 2:LANES // 2 + rope]
    if expand_kv:
        k_ref, vt_ref, qn_ref, kmax_ref, kt_ref = kv_refs
        kv = _dot(cn.astype(BF16), wkv_ref[...])
        kn2 = jnp.zeros((x.shape[0], LANES), F32)
        for h in range(heads):
            sl = slice(h * LANES, (h + 1) * LANES)
            kb = (kv[:, sl] + kr).astype(BF16)
            k_ref[0, :, sl] = kb
            kn2 = jnp.where(lane == h, sq_norm(kb), kn2)
        vt_ref[0] = kv[:, hw:hw + heads * v_head].T.astype(BF16)
        kt_ref[0] = mk.T
        qn_ref[0] = jnp.sqrt(qn2)
        kmax_ref[0, 0] = jnp.sqrt(jnp.max(kn2, axis=0, keepdims=True))


def _in_a(x, shift, scale, g, w1, gq, wq, gkv, wkv, tab, *, tm, heads, q_lora, kv_lora, rope, m_width,
          mk_cols, v_head, expand_kv):
    b, t, d = x.shape
    grid = (b, t // tm)
    tok = lambda last: pl.BlockSpec((1, tm, last), lambda i, j: (i, j, 0))
    tok_t = lambda rows: pl.BlockSpec((1, rows, tm), lambda i, j: (i, 0, j))
    const = lambda a: pl.BlockSpec(a.shape, lambda i, j: (0,) * a.ndim)
    out_shape = [jax.ShapeDtypeStruct((b, t, heads * LANES), BF16),
                 jax.ShapeDtypeStruct((b, t, kv_lora), F32),
                 jax.ShapeDtypeStruct((b, t, rope), F32),
                 jax.ShapeDtypeStruct((b, t, m_width), F32),
                 jax.ShapeDtypeStruct((b, t, d), F32),
                 jax.ShapeDtypeStruct((b, t, LANES), F32)]
    out_specs = [tok(heads * LANES), tok(kv_lora), tok(rope), tok(m_width), tok(d), tok(LANES)]
    if expand_kv:
        out_shape += [jax.ShapeDtypeStruct((b, t, heads * LANES), BF16),
                      jax.ShapeDtypeStruct((b, heads * v_head, t), BF16),
                      jax.ShapeDtypeStruct((b, t, LANES), F32),
                      jax.ShapeDtypeStruct((b, t // tm, 1, LANES), F32),
                      jax.ShapeDtypeStruct((b, mk_cols[1] - mk_cols[0], t), F32)]
        out_specs += [tok(heads * LANES), tok_t(heads * v_head), tok(LANES),
                      pl.BlockSpec((1, 1, 1, LANES), lambda i, j: (i, j, 0, 0)), tok_t(mk_cols[1] - mk_cols[0])]
    kern = functools.partial(_in_a_kernel, heads=heads, q_lora=q_lora, kv_lora=kv_lora, rope=rope,
                             m_width=m_width, mk_cols=mk_cols, d_model=d, v_head=v_head, expand_kv=expand_kv)
    return pl.pallas_call(
        kern, out_shape=out_shape, grid=grid,
        in_specs=[tok(d),
                  pl.BlockSpec((1,) + shift.shape[1:], lambda i, j: (i, 0, 0)),
                  pl.BlockSpec((1,) + scale.shape[1:], lambda i, j: (i, 0, 0)),
                  const(g), const(w1), const(gq), const(wq), const(gkv), const(wkv),
                  pl.BlockSpec((tm, 4 * LANES), lambda i, j: (j, 0))],
        out_specs=out_specs,
        compiler_params=_params("arbitrary", "arbitrary"),
        name="in_proj_a",
    )(x, shift, scale, g, w1, gq, wq, gkv, wkv, tab)


def _flash_kernel(qi_ref, ki_ref, q_ref, k_ref, vt_ref, o_ref, m_sc, l_sc, acc_sc, *,
                  heads, v_head, tq, tk, chunk):
    step_id = pl.program_id(1)
    qi = qi_ref[step_id]
    ki = ki_ref[step_id]

    @pl.when(ki == 0)
    def _():
        m_sc[...] = jnp.full(m_sc.shape, NEG_INF, F32)
        l_sc[...] = jnp.zeros(l_sc.shape, F32)
        acc_sc[...] = jnp.zeros(acc_sc.shape, F32)

    def step(masked):
        if masked:
            kc = (ki * tk + lax.broadcasted_iota(jnp.int32, (tk, tq), 0)) // chunk
            qc = (qi * tq + lax.broadcasted_iota(jnp.int32, (tk, tq), 1)) // chunk
            mask = kc <= qc
        for h in range(heads):
            qh = q_ref[0, :, h * LANES:(h + 1) * LANES]
            kh = k_ref[0, :, h * LANES:(h + 1) * LANES]
            vth = vt_ref[0, h * v_head:(h + 1) * v_head, :]
            rows = slice(h * v_head, (h + 1) * v_head)
            st = _dot_nt(kh, qh)
            if masked:
                st = jnp.where(mask, st, NEG_INF)
            m_prev = m_sc[h]
            m_new = jnp.maximum(m_prev, jnp.max(st, axis=0, keepdims=True))
            alpha = jnp.exp2(m_prev - m_new)
            p = jnp.exp2(st - m_new)
            l_sc[h] = alpha * l_sc[h] + jnp.sum(p, axis=0, keepdims=True)
            acc_sc[rows, :] = alpha * acc_sc[rows, :] + _dot(vth, p.astype(BF16))
            m_sc[h] = m_new

    full = (ki + 1) * tk <= qi * tq + chunk

    @pl.when(full)
    def _():
        step(False)

    @pl.when(jnp.logical_not(full))
    def _():
        step(True)

    @pl.when(ki == ((qi + 1) * tq - 1) // tk)
    def _():
        for h in range(heads):
            rows = slice(h * v_head, (h + 1) * v_head)
            acc_sc[rows, :] = acc_sc[rows, :] / l_sc[h]
        o_ref[0] = acc_sc[...].T


def _flash_bound_kernel(qi_ref, ki_ref, q_ref, k_ref, vt_ref, qn_ref, kmax_ref, o_ref, l_ref,
                        mb_sc, l_sc, acc_sc, *, heads, v_head, tq, tk, chunk):
    step_id = pl.program_id(1)
    qi = qi_ref[step_id]
    ki = ki_ref[step_id]

    @pl.when(ki == 0)
    def _():
        kmax = jnp.max(kmax_ref[0], axis=0) * FLASH_BOUND_SLACK
        for h in range(heads):
            mb_sc[h:h + 1, :] = qn_ref[0, h:h + 1, :] * kmax[:, h:h + 1]
        l_sc[...] = jnp.zeros(l_sc.shape, F32)
        acc_sc[...] = jnp.zeros(acc_sc.shape, F32)

    def step(masked):
        if masked:
            kc = (ki * tk + lax.broadcasted_iota(jnp.int32, (tk, tq), 0)) // chunk
            qc = (qi * tq + lax.broadcasted_iota(jnp.int32, (tk, tq), 1)) // chunk
            mask = kc <= qc
        for h0 in range(0, heads, FLASH_HEAD_GROUP):
            hs = range(h0, h0 + FLASH_HEAD_GROUP)
            st = [_dot_nt(k_ref[0, :, h * LANES:(h + 1) * LANES], q_ref[0, :, h * LANES:(h + 1) * LANES])
                  for h in hs]
            if masked:
                st = [jnp.where(mask, s, NEG_INF) for s in st]
            p = [jnp.exp2(s - mb_sc[h:h + 1, :]) for s, h in zip(st, hs)]
            for x, h in zip(p, hs):
                l_sc[h:h + 1, :] += jnp.sum(x, axis=0, keepdims=True)
            pv = [_dot(vt_ref[0, h * v_head:(h + 1) * v_head, :], x.astype(BF16)) for x, h in zip(p, hs)]
            for x, h in zip(pv, hs):
                acc_sc[h * v_head:(h + 1) * v_head, :] += x

    full = (ki + 1) * tk <= qi * tq + chunk

    @pl.when(full)
    def _():
        step(False)

    @pl.when(jnp.logical_not(full))
    def _():
        step(True)

    @pl.when(ki == ((qi + 1) * tq - 1) // tk)
    def _():
        for h in range(heads):
            rows = slice(h * v_head, (h + 1) * v_head)
            acc_sc[rows, :] = acc_sc[rows, :] / l_sc[h:h + 1, :]
        o_ref[0] = acc_sc[...].T
        l_ref[0] = l_sc[...]


def _flash(q, k, vt, *, heads, v_head, tq, tk, bound=None):
    b, t, _ = q.shape
    pairs = [(i, j) for i in range(t // tq) for j in range(((i + 1) * tq - 1) // tk + 1)]
    qi_tab = jnp.asarray([p[0] for p in pairs], jnp.int32)
    ki_tab = jnp.asarray([p[1] for p in pairs], jnp.int32)
    in_specs = [pl.BlockSpec((1, tq, heads * LANES), lambda i, s, qt, kt: (i, qt[s], 0)),
                pl.BlockSpec((1, tk, heads * LANES), lambda i, s, qt, kt: (i, kt[s], 0)),
                pl.BlockSpec((1, heads * v_head, tk), lambda i, s, qt, kt: (i, 0, kt[s]))]
    o_shape = jax.ShapeDtypeStruct((b, t, heads * v_head), F32)
    o_spec = pl.BlockSpec((1, tq, heads * v_head), lambda i, s, qt, kt: (i, qt[s], 0))
    acc = pltpu.VMEM((heads * v_head, tq), F32)
    if bound is None:
        kern, name, args = _flash_kernel, "flash_attn", (q, k, vt)
        out_shape, out_specs = o_shape, o_spec
        scratch = [pltpu.VMEM((heads, 1, tq), F32), pltpu.VMEM((heads, 1, tq), F32), acc]
    else:
        qn, kmax = bound
        assert qn.shape == (b, heads, t)
        kern, name, args = _flash_bound_kernel, "flash_attn_bound", (q, k, vt, qn, kmax)
        in_specs += [pl.BlockSpec((1, heads, tq), lambda i, s, qt, kt: (i, 0, qt[s])),
                     pl.BlockSpec((1,) + kmax.shape[1:], lambda i, s, qt, kt: (i, 0, 0, 0))]
        out_shape = [o_shape, jax.ShapeDtypeStruct((b, heads, t), F32)]
        out_specs = [o_spec, pl.BlockSpec((1, heads, tq), lambda i, s, qt, kt: (i, 0, qt[s]))]
        scratch = [pltpu.VMEM((heads, tq), F32), pltpu.VMEM((heads, tq), F32), acc]
    grid_spec = pltpu.PrefetchScalarGridSpec(
        num_scalar_prefetch=2, grid=(b, len(pairs)), in_specs=in_specs, out_specs=out_specs,
        scratch_shapes=scratch)
    return pl.pallas_call(
        functools.partial(kern, heads=heads, v_head=v_head, tq=tq, tk=tk, chunk=CHUNK),
        out_shape=out_shape,
        grid_spec=grid_spec,
        compiler_params=_params("arbitrary", "arbitrary"),
        name=name,
    )(qi_tab, ki_tab, *args)


def _latent_attn_kernel(q_ref, cp_ref, krp_ref, cn_ref, krn_ref, wk_ref, wv_ref, o_ref, *,
                        heads, nope, rope, v_head):
    q = q_ref[0]
    qabs, qrope = [], []
    for h in range(heads):
        qabs.append(_dot_nt(q[:, h * LANES:h * LANES + nope], wk_ref[h]))
        qrope.append(q[:, h * LANES + nope:h * LANES + nope + rope])
    qabs = jnp.concatenate(qabs, axis=0).astype(BF16)
    qrope = jnp.concatenate(qrope, axis=0)
    cp = cp_ref[0].astype(BF16)
    cn = cn_ref[0].astype(BF16)
    s_p = _dot_nt(qabs, cp) + _dot_nt(qrope, krp_ref[0].astype(BF16))
    s_n = _dot_nt(qabs, cn) + _dot_nt(qrope, krn_ref[0].astype(BF16))
    m = jnp.maximum(jnp.max(s_p, axis=-1, keepdims=True), jnp.max(s_n, axis=-1, keepdims=True))
    p_p = jnp.exp2(s_p - m)
    p_n = jnp.exp2(s_n - m)
    l = jnp.sum(p_p, axis=-1, keepdims=True) + jnp.sum(p_n, axis=-1, keepdims=True)
    o_lat = (_dot(p_p.astype(BF16), cp) + _dot(p_n.astype(BF16), cn)) / l
    t = q.shape[0]
    for h in range(heads):
        o_ref[0, :, h * v_head:(h + 1) * v_head] = _dot(o_lat[h * t:(h + 1) * t].astype(BF16), wv_ref[h])


def _latent_attn(q, c_past, kr_past, c_new, kr_new, wk, wv, *, heads, nope, rope, v_head):
    b, t, _ = q.shape
    past, kv_lora = c_past.shape[1:]
    blk = lambda n, last: pl.BlockSpec((1, n, last), lambda i: (i, 0, 0))
    const = lambda a: pl.BlockSpec(a.shape, lambda i: (0,) * a.ndim)
    kern = functools.partial(_latent_attn_kernel, heads=heads, nope=nope, rope=rope, v_head=v_head)
    return pl.pallas_call(
        kern,
        out_shape=jax.ShapeDtypeStruct((b, t, heads * v_head), F32),
        grid=(b,),
        in_specs=[blk(t, heads * LANES), blk(past, kv_lora), blk(past, rope), blk(t, kv_lora), blk(t, rope),
                  const(wk), const(wv)],
        out_specs=blk(t, heads * v_head),
        compiler_params=_params("arbitrary"),
        name="latent_attn",
    )(q, c_past, kr_past, c_new, kr_new, wk, wv)


def _mlstm_prep_kernel(m_ref, kt_ref, gc_ref, gr_ref, bc_ref, br_ref, pv_ref, kv_ref, b_ref, st_ref, *,
                       heads, dk, dv, chunk, nchunk):
    row = lax.broadcasted_iota(jnp.int32, (chunk, chunk), 0)
    col = lax.broadcasted_iota(jnp.int32, (chunk, chunk), 1)
    causal = col <= row
    tril = causal.astype(F32)
    triu = (row <= col).astype(F32)
    lane = lax.broadcasted_iota(jnp.int32, (chunk, LANES), 1)
    o_k, o_v = heads * dk, 2 * heads * dk

    rows = [slice(c * chunk, (c + 1) * chunk) for c in range(nchunk)]
    gc = [gc_ref[0, r, :] + bc_ref[...] for r in rows]
    gr = [gr_ref[0, c] + br_ref[...] for c in range(nchunk)]
    bcum_c = [_dot(tril, _log_sigmoid(g), HIGHEST) for g in gc]
    bcum_r = [_dot(_log_sigmoid(g), triu, HIGHEST) for g in gr]
    for c in range(nchunk):
        b_ref[0, rows[c], :] = bcum_c[c]
    ch = [(c, h) for c in range(nchunk) for h in range(heads)]
    v = [m_ref[0, rows[c], o_v + h * dv:o_v + (h + 1) * dv].astype(BF16) for c, h in ch]
    qk = [_dot_nt(m_ref[0, rows[c], h * dk:(h + 1) * dk].astype(BF16),
                  (m_ref[0, rows[c], o_k + h * dk:o_k + (h + 1) * dk] * (dk ** -0.5)).astype(BF16)) for c, h in ch]
    li_r = [gr[c][h:h + 1, :] for c, h in ch]
    b_r = [bcum_r[c][heads + h:heads + h + 1, :] for c, h in ch]
    dmat = [jnp.where(causal, bcum_c[c][:, heads + h:heads + h + 1] - b_r[i] + li_r[i], NEG_INF)
            for i, (c, h) in enumerate(ch)]
    mx = [jnp.max(d, axis=-1, keepdims=True) for d in dmat]
    p0 = [s * jnp.exp(d - m) for s, d, m in zip(qk, dmat, mx)]
    for i, (c, h) in enumerate(ch):
        pv_ref[0, rows[c], h * dv:(h + 1) * dv] = _dot(p0[i].astype(BF16), v[i])
    w_r = [jnp.exp(b_r[i][:, chunk - 1:chunk] - b_r[i] + li_r[i] - mx[i][chunk - 1:chunk, :]) * (dk ** -0.5)
           for i in range(len(ch))]
    for i, (c, h) in enumerate(ch):
        wkt = (kt_ref[0, h * dk:(h + 1) * dk, c * chunk:(c + 1) * chunk] * w_r[i]).astype(BF16)
        kv_ref[0, c, h] = _dot(wkt, jnp.concatenate([v[i], jnp.ones_like(v[i])], axis=1))
    psum = [jnp.sum(p, axis=-1, keepdims=True) for p in p0]
    for c in range(nchunk):
        stats = jnp.zeros((chunk, LANES), F32)
        for h in range(heads):
            stats = jnp.where(lane == h, mx[c * heads + h], stats)
            stats = jnp.where(lane == heads + h, psum[c * heads + h], stats)
        st_ref[0, rows[c], :] = stats


def _mlstm_scan_kernel(q_ref, mo_ref, pv_ref, kv_ref, b_ref, st_ref, gout_ref, c0_ref, n0_ref, m0_ref,
                       h_ref, c1_ref, n1_ref, m1_ref, c_sc, m_sc, *, heads, dk, dv, chunk, nchunk):
    t = pl.program_id(1)
    nseq = q_ref.shape[0]

    @pl.when(t == 0)
    def _():
        c_sc[:, :, :, :dv] = c0_ref[...]
        c_sc[:, :, :, dv:] = jnp.broadcast_to(n0_ref[...], c0_ref.shape)
        m_sc[...] = m0_ref[...]

    def body(c, carry):
        rows = pl.ds(pl.multiple_of(c * chunk, chunk), chunk)
        rep = lambda col: jnp.broadcast_to(col, (chunk, dv))
        ps = [(s, h) for s in range(nseq) for h in range(heads)]
        bcum = [b_ref[s, rows, :] for s in range(nseq)]
        stats = [st_ref[s, rows, :] for s in range(nseq)]
        c2 = [c_sc[s, h] for s, h in ps]
        qc = [_dot(q_ref[s, rows, h * dk:(h + 1) * dk].astype(BF16), c2[i].astype(BF16))
              for i, (s, h) in enumerate(ps)]
        mx = [rep(stats[s][:, h:h + 1]) for s, h in ps]
        inter = [rep(bcum[s][:, heads + h:heads + h + 1]) + m_sc[s, h] for s, h in ps]
        m = [jnp.maximum(a, b) for a, b in zip(inter, mx)]
        w_inter = [jnp.exp(a - b) for a, b in zip(inter, m)]
        r = [jnp.exp(a - b) for a, b in zip(mx, m)]
        for i, (s, h) in enumerate(ps):
            decay_end = w_inter[i][chunk - 1:chunk, :]
            f_new = r[i][chunk - 1:chunk, :]
            c_sc[s, h] = (jnp.concatenate([decay_end, decay_end], axis=1) * c2[i]
                          + jnp.concatenate([f_new, f_new], axis=1) * kv_ref[s, c, h])
            m_sc[s, h] = m[i][chunk - 1:chunk, 0:1]
        num = [w_inter[i] * qc[i][:, :dv] + r[i] * pv_ref[s, rows, h * dv:(h + 1) * dv] for i, (s, h) in enumerate(ps)]
        den = [w_inter[i] * qc[i][:, dv:] + r[i] * rep(stats[s][:, heads + h:heads + h + 1])
               for i, (s, h) in enumerate(ps)]
        hh = [a / jnp.maximum(jnp.abs(b), jnp.exp(-c_)) for a, b, c_ in zip(num, den, m)]
        hn = [_rms(hh[i], gout_ref[:, h * dv:(h + 1) * dv]) for i, (s, h) in enumerate(ps)]
        for i, (s, h) in enumerate(ps):
            h_ref[s, rows, h * dv:(h + 1) * dv] = hn[i] * _sigmoid(mo_ref[s, rows, h * dv:(h + 1) * dv])
        return carry

    lax.fori_loop(0, nchunk, body, 0)

    @pl.when(t == pl.num_programs(1) - 1)
    def _():
        c1_ref[...] = c_sc[:, :, :, :dv]
        n1_ref[...] = c_sc[:, :, :, dv:dv + 1]
        m1_ref[...] = m_sc[...]


def _mlstm_prep(m_slab, kt, gates_c, gates_r, bias_c, bias_r, *, chunk, heads, dk, dv):
    b, t, mw = m_slab.shape
    const = lambda a: pl.BlockSpec(a.shape, lambda i, j: (0,) * a.ndim)
    qkv_w = 2 * heads * dk + heads * dv
    tp = min(t, MLSTM_PREP_TC)
    npc = tp // chunk
    tokp = lambda last: pl.BlockSpec((1, tp, last), lambda i, j: (i, j, 0))
    per_chunk = lambda *s: pl.BlockSpec((1, npc, heads) + s, lambda i, j: (i, j, 0) + (0,) * len(s))
    return pl.pallas_call(
        functools.partial(_mlstm_prep_kernel, heads=heads, dk=dk, dv=dv, chunk=chunk, nchunk=npc),
        out_shape=[jax.ShapeDtypeStruct((b, t, heads * dv), F32),
                   jax.ShapeDtypeStruct((b, t // chunk, heads, dk, 2 * dv), F32),
                   jax.ShapeDtypeStruct((b, t, LANES), F32),
                   jax.ShapeDtypeStruct((b, t, LANES), F32)],
        grid=(b, t // tp),
        in_specs=[tokp(qkv_w), pl.BlockSpec((1, heads * dk, tp), lambda i, j: (i, 0, j)), tokp(LANES),
                  pl.BlockSpec((1, npc, SUBLANES, chunk), lambda i, j: (i, j, 0, 0)),
                  const(bias_c), const(bias_r)],
        out_specs=[tokp(heads * dv), per_chunk(dk, 2 * dv), tokp(LANES), tokp(LANES)],
        compiler_params=_params("arbitrary", "arbitrary"),
        name="mlstm_prep",
    )(m_slab, kt, gates_c, gates_r, bias_c, bias_r)


def _mlstm_scan(m_slab, pv, kv, bcum, stats, gout, c0, n0, m0, *, tc, heads, dk, dv):
    b, t, mw = m_slab.shape
    chunk = min(CHUNK, t)
    const = lambda a: pl.BlockSpec(a.shape, lambda i, j: (0,) * a.ndim)
    qkv_w = 2 * heads * dk + heads * dv
    mo_blk, rem = divmod(qkv_w, heads * dv)
    assert rem == 0 and mw == qkv_w + heads * dv
    nchunk = tc // chunk
    ns = math.gcd(b, SCAN_SEQS)
    tok = lambda last, blk=0: pl.BlockSpec((ns, tc, last), lambda i, j: (i, j, blk))
    per_chunk = lambda *s: pl.BlockSpec((ns, nchunk, heads) + s, lambda i, j: (i, j, 0) + (0,) * len(s))
    st = lambda *s: pl.BlockSpec((ns,) + s, lambda i, j: (i,) + (0,) * len(s))
    hm, c1, n1, m1 = pl.pallas_call(
        functools.partial(_mlstm_scan_kernel, heads=heads, dk=dk, dv=dv, chunk=chunk, nchunk=nchunk),
        out_shape=[jax.ShapeDtypeStruct((b, t, heads * dv), F32),
                   jax.ShapeDtypeStruct((b, heads, dk, dv), F32),
                   jax.ShapeDtypeStruct((b, heads, dk, 1), F32),
                   jax.ShapeDtypeStruct((b, heads, 1, 1), F32)],
        grid=(b // ns, t // tc),
        in_specs=[tok(heads * dk),
                  tok(heads * dv, mo_blk),
                  tok(heads * dv), per_chunk(dk, 2 * dv), tok(LANES), tok(LANES),
                  const(gout), st(heads, dk, dv), st(heads, dk, 1), st(heads, 1, 1)],
        out_specs=[tok(heads * dv), st(heads, dk, dv), st(heads, dk, 1), st(heads, 1, 1)],
        scratch_shapes=[pltpu.VMEM((ns, heads, dk, 2 * dv), F32), pltpu.VMEM((ns, heads, 1, 1), F32)],
        compiler_params=_params("arbitrary", "arbitrary"),
        name="mlstm_scan",
    )(m_slab, m_slab, pv, kv, bcum, stats, gout, c0, n0[..., None], m0)
    return hm, c1, n1[..., 0], m1


def _mid_kernel(oa_ref, ob_ref, z_ref, x_ref, gate_ref, wo_ref, shift_ref, scale_ref, g_ref, w2_ref,
                x1_ref, qkv_ref, z2_ref, ab_ref, *, half, conv_ch, d_model):
    z = z_ref[0]
    ma = (oa_ref[0] * _silu(z[:, :half])).astype(BF16)
    mb = (ob_ref[0] * _silu(z[:, half:])).astype(BF16)
    y = _dot(ma, wo_ref[0:half, :]) + _dot(mb, wo_ref[half:, :])
    x1 = x_ref[0] + _per_row(gate_ref[0], x_ref.shape[1]) * y
    x1_ref[0] = x1
    hn = _rms(x1, g_ref[...]) * (1.0 + _per_row(scale_ref[0], x_ref.shape[1])) + _per_row(shift_ref[0], x_ref.shape[1])
    y2 = _dot(hn.astype(BF16), w2_ref[...])
    qkv_ref[0] = y2[:, :conv_ch]
    z2_ref[0] = y2[:, conv_ch:conv_ch + d_model]
    ab_ref[0] = y2[:, conv_ch + d_model:]


def _mid(oa, ob, z, x, gate, wo, shift, scale, g, w2, *, tm, conv_ch):
    b, t, d = x.shape
    half = oa.shape[-1]
    tok = lambda last: pl.BlockSpec((1, tm, last), lambda i, j: (i, j, 0))
    vec = pl.BlockSpec((1,) + gate.shape[1:], lambda i, j: (i, 0, 0))
    const = lambda a: pl.BlockSpec(a.shape, lambda i, j: (0,) * a.ndim)
    kern = functools.partial(_mid_kernel, half=half, conv_ch=conv_ch, d_model=d)
    return pl.pallas_call(
        kern,
        out_shape=[jax.ShapeDtypeStruct((b, t, d), F32), jax.ShapeDtypeStruct((b, t, conv_ch), F32),
                   jax.ShapeDtypeStruct((b, t, d), F32), jax.ShapeDtypeStruct((b, t, LANES), F32)],
        grid=(b, t // tm),
        in_specs=[tok(half), tok(ob.shape[-1]), tok(d), tok(d), vec, const(wo), vec, vec, const(g), const(w2)],
        out_specs=[tok(d), tok(conv_ch), tok(d), tok(LANES)],
        compiler_params=_params("arbitrary", "arbitrary"),
        name="out_a_in_c",
    )(oa, ob, z, x, gate, wo, shift, scale, g, w2)


def _conv_kernel(qkv_ref, past_ref, wc_ref, ab_ref, alog_ref, dtb_ref, act_ref, gb_ref, ext_sc, *,
                 tm, width, heads, dk):
    @pl.when(pl.program_id(1) == 0)
    def _():
        ext_sc[0:SUBLANES, :] = past_ref[0]

    ext_sc[SUBLANES:SUBLANES + tm, :] = qkv_ref[0]
    conv = wc_ref[width - 1:width, :] * ext_sc[SUBLANES:SUBLANES + tm, :]
    for j in range(width - 1):
        s = SUBLANES - (width - 1) + j
        conv = conv + wc_ref[j:j + 1, :] * ext_sc[s:s + tm, :]
    ext_sc[0:SUBLANES, :] = ext_sc[tm:tm + SUBLANES, :]
    act = _silu(conv)
    for h in range(2 * heads):
        xh = act[:, h * dk:(h + 1) * dk]
        xh = xh * lax.rsqrt(jnp.sum(xh * xh, axis=-1, keepdims=True) + EPS)
        if h < heads:
            xh = xh * (dk ** -0.5)
        act_ref[0, :, h * dk:(h + 1) * dk] = xh
    act_ref[0, :, 2 * heads * dk:] = act[:, 2 * heads * dk:]
    ab = ab_ref[0]
    g = -jnp.exp(alog_ref[...]) * _softplus(ab + dtb_ref[...])
    lane = lax.broadcasted_iota(jnp.int32, ab.shape, 1)
    gb_ref[0] = jnp.where(lane < heads, g, _sigmoid(ab))


def _conv(qkv, past8, wc8, ab, alog, dtb, *, tm, width, heads, dk):
    b, t, ch = qkv.shape
    tok = lambda last: pl.BlockSpec((1, tm, last), lambda i, j: (i, j, 0))
    const = lambda a: pl.BlockSpec(a.shape, lambda i, j: (0,) * a.ndim)
    kern = functools.partial(_conv_kernel, tm=tm, width=width, heads=heads, dk=dk)
    return pl.pallas_call(
        kern,
        out_shape=[jax.ShapeDtypeStruct((b, t, ch), F32), jax.ShapeDtypeStruct((b, t, LANES), F32)],
        grid=(b, t // tm),
        in_specs=[tok(ch), pl.BlockSpec((1, SUBLANES, ch), lambda i, j: (i, 0, 0)), const(wc8), tok(LANES),
                  const(alog), const(dtb)],
        out_specs=[tok(ch), tok(LANES)],
        scratch_shapes=[pltpu.VMEM((tm + SUBLANES, ch), F32)],
        compiler_params=_params("arbitrary", "arbitrary"),
        name="conv_gates",
    )(qkv, past8, wc8, ab, alog, dtb)


def _blockdiag(x, group, chunk):
    w = group * chunk
    br = lax.broadcasted_iota(jnp.int32, (w, w), 0) // chunk
    bc = lax.broadcasted_iota(jnp.int32, (w, w), 1) // chunk
    xb = x.astype(BF16)
    return jnp.where(br == bc, jnp.concatenate([xb] * group, axis=0), jnp.zeros((), BF16))


def _unit_lower_inverses_minus_eye(a_list, group, chunk):
    w = group * chunk
    r = lax.broadcasted_iota(jnp.int32, (chunk, w), 0)
    cc = lax.broadcasted_iota(jnp.int32, (chunk, w), 1) % chunk
    es = [-jnp.where((r // 2 == cc // 2) & (r % 2 == 1) & (cc % 2 == 0), a4, 0.0) for a4 in a_list]
    s = 2
    while s < chunk:
        off = (r // (2 * s) == cc // (2 * s)) & (r % (2 * s) >= s) & (cc % (2 * s) < s)
        a_offs = [jnp.where(off, a4, 0.0) for a4 in a_list]
        ps = [a + _dot(a.astype(BF16), _blockdiag(e, group, chunk)) for a, e in zip(a_offs, es)]
        es = [e - (p + _dot(e.astype(BF16), _blockdiag(p, group, chunk))) for e, p in zip(es, ps)]
        s *= 2
    return es


def _gdn_prep_kernel(act_ref, gbc_ref, gbr_ref, w_ref, uv_ref, kd_ref, attn_ref, eg_ref, *,
                     heads, dk, dv, chunk, nchunk, group):
    row = lax.broadcasted_iota(jnp.int32, (chunk, chunk), 0)
    col = lax.broadcasted_iota(jnp.int32, (chunk, chunk), 1)
    incl = col <= row
    strict = col < row
    tril = incl.astype(F32)
    triu = (row <= col).astype(F32)
    lane = lax.broadcasted_iota(jnp.int32, (chunk, LANES), 1)
    o_k, o_v = heads * dk, 2 * heads * dk

    rows = [slice(c * chunk, (c + 1) * chunk) for c in range(nchunk)]
    gbc = [gbc_ref[0, r, :] for r in rows]
    gcum_c = [_dot(tril, g, HIGHEST) for g in gbc]
    gcum_r = [_dot(gbr_ref[0, c], triu, HIGHEST) for c in range(nchunk)]
    for c in range(nchunk):
        eg_ref[0, rows[c], :] = jnp.where(lane < heads, jnp.exp(gcum_c[c]), 0.0)
    ch = [(c, h) for c in range(nchunk) for h in range(heads)]
    k = [act_ref[0, rows[c], o_k + h * dk:o_k + (h + 1) * dk] for c, h in ch]
    kb = [x.astype(BF16) for x in k]
    kk = [_dot_nt(x, x) for x in kb]
    qk = [_dot_nt(act_ref[0, rows[c], h * dk:(h + 1) * dk].astype(BF16), kb[i]) for i, (c, h) in enumerate(ch)]
    g_c = [gcum_c[c][:, h:h + 1] for c, h in ch]
    beta = [gbc[c][:, heads + h:heads + h + 1] for c, h in ch]
    decay = [jnp.exp(jnp.where(incl, g_c[i] - gcum_r[c][h:h + 1, :], NEG_INF)) for i, (c, h) in enumerate(ch)]
    a_blk = [jnp.where(strict, beta[i] * kk[i] * decay[i], 0.0) for i in range(len(ch))]
    for i, (c, h) in enumerate(ch):
        attn_ref[0, rows[c], h * chunk:(h + 1) * chunk] = (qk[i] * decay[i]).astype(BF16)
        kd_ref[0, rows[c], h * dk:(h + 1) * dk] = (k[i] * jnp.exp(g_c[i][chunk - 1:chunk, :] - g_c[i])).astype(BF16)
    rhs_blk = [jnp.concatenate([beta[i] * act_ref[0, rows[c], o_v + h * dv:o_v + (h + 1) * dv],
                                (beta[i] * jnp.exp(g_c[i])) * k[i]], axis=1) for i, (c, h) in enumerate(ch)]
    problems = [(c, g0) for c in range(nchunk) for g0 in range(0, heads, group)]
    a_list = [jnp.concatenate(a_blk[c * heads + g0:c * heads + g0 + group], axis=1) for c, g0 in problems]
    rhs_list = [jnp.concatenate(rhs_blk[c * heads + g0:c * heads + g0 + group], axis=0) for c, g0 in problems]
    e_list = _unit_lower_inverses_minus_eye(a_list, group, chunk)
    sols = [rhs + _dot(_blockdiag(e, group, chunk), rhs.astype(BF16)) for e, rhs in zip(e_list, rhs_list)]
    for (c, g0), sol in zip(problems, sols):
        for i, h in enumerate(range(g0, g0 + group)):
            uv_ref[0, rows[c], h * dv:(h + 1) * dv] = sol[i * chunk:(i + 1) * chunk, :dv]
            w_ref[0, rows[c], h * dk:(h + 1) * dk] = sol[i * chunk:(i + 1) * chunk, dv:].astype(BF16)


def _gdn_prep(act, gb_c, gb_r, *, chunk, tc, heads, dk, dv):
    b, t, _ = act.shape
    tc = min(tc, t)
    nchunk = tc // chunk
    group = (2 * LANES) // chunk
    kern = functools.partial(_gdn_prep_kernel, heads=heads, dk=dk, dv=dv, chunk=chunk, nchunk=nchunk, group=group)
    tok = lambda last: pl.BlockSpec((1, tc, last), lambda i, j: (i, j, 0))
    return pl.pallas_call(
        kern,
        out_shape=[jax.ShapeDtypeStruct((b, t, heads * dk), BF16), jax.ShapeDtypeStruct((b, t, heads * dv), F32),
                   jax.ShapeDtypeStruct((b, t, heads * dk), BF16), jax.ShapeDtypeStruct((b, t, heads * chunk), BF16),
                   jax.ShapeDtypeStruct((b, t, LANES), F32)],
        grid=(b, t // tc),
        in_specs=[tok(act.shape[-1]), tok(LANES),
                  pl.BlockSpec((1, nchunk, 2 * SUBLANES, chunk), lambda i, j: (i, j, 0, 0))],
        out_specs=[tok(heads * dk), tok(heads * dv), tok(heads * dk), tok(heads * chunk), tok(LANES)],
        compiler_params=_params("arbitrary", "arbitrary"),
        name="gdn_prep",
    )(act, gb_c, gb_r)


def _gdn_scan_kernel(q_ref, w_ref, uv_ref, kd_ref, attn_ref, eg_ref, gout_ref, s0_ref, o_ref, s1_ref, s_sc, *,
                     heads, dk, dv, chunk, nchunk):
    t = pl.program_id(1)
    nseq = q_ref.shape[0]

    @pl.when(t == 0)
    def _():
        s_sc[...] = s0_ref[...]

    def body(c, carry):
        rows = pl.ds(pl.multiple_of(c * chunk, chunk), chunk)
        ps = [(s, h) for s in range(nseq) for h in range(heads)]
        eg = [eg_ref[s, rows, :] for s in range(nseq)]
        s0 = [s_sc[s, h] for s, h in ps]
        s0b = [x.astype(BF16) for x in s0]
        ws = [_dot(w_ref[s, rows, h * dk:(h + 1) * dk], s0b[i]) for i, (s, h) in enumerate(ps)]
        ub = [(uv_ref[s, rows, h * dv:(h + 1) * dv] - ws[i]).astype(BF16) for i, (s, h) in enumerate(ps)]
        eg_h = [jnp.broadcast_to(eg[s][:, h:h + 1], (chunk, dv)) for s, h in ps]
        ku = [_dot_tn(kd_ref[s, rows, h * dk:(h + 1) * dk], ub[i]) for i, (s, h) in enumerate(ps)]
        for i, (s, h) in enumerate(ps):
            s_sc[s, h] = eg_h[i][chunk - 1:chunk, :] * s0[i] + ku[i]
        qs = [_dot(q_ref[s, rows, h * dk:(h + 1) * dk].astype(BF16), s0b[i]) for i, (s, h) in enumerate(ps)]
        au = [_dot(attn_ref[s, rows, h * chunk:(h + 1) * chunk], ub[i]) for i, (s, h) in enumerate(ps)]
        on = [_rms(eg_h[i] * qs[i] + au[i], gout_ref[...]) for i in range(len(ps))]
        for i, (s, h) in enumerate(ps):
            o_ref[s, rows, h * dv:(h + 1) * dv] = on[i]
        return carry

    lax.fori_loop(0, nchunk, body, 0)

    @pl.when(t == pl.num_programs(1) - 1)
    def _():
        s1_ref[...] = s_sc[...]


def _gdn_scan(act, w, uv, kd, attn, eg, gout, s0, *, tc, heads, dk, dv):
    b, t, _ = act.shape
    chunk = min(CHUNK, t)
    nchunk = tc // chunk
    kern = functools.partial(_gdn_scan_kernel, heads=heads, dk=dk, dv=dv, chunk=chunk, nchunk=nchunk)
    ns = math.gcd(b, SCAN_SEQS)
    tok = lambda last: pl.BlockSpec((ns, tc, last), lambda i, j: (i, j, 0))
    state = pl.BlockSpec((ns, heads, dk, dv), lambda i, j: (i, 0, 0, 0))
    return pl.pallas_call(
        kern,
        out_shape=[jax.ShapeDtypeStruct((b, t, heads * dv), F32), jax.ShapeDtypeStruct(s0.shape, F32)],
        grid=(b // ns, t // tc),
        in_specs=[tok(heads * dk),
                  tok(heads * dk), tok(heads * dv), tok(heads * dk), tok(heads * chunk), tok(LANES),
                  pl.BlockSpec(gout.shape, lambda i, j: (0, 0)), state],
        out_specs=[tok(heads * dv), state],
        scratch_shapes=[pltpu.VMEM((ns, heads, dk, dv), F32)],
        compiler_params=_params("arbitrary", "arbitrary"),
        name="gdn_scan",
    )(act, w, uv, kd, attn, eg, gout, s0)


def _final_kernel(o_ref, z_ref, x_ref, gate_ref, wo_ref, g_ref, y_ref):
    mixed = (o_ref[0] * _silu(z_ref[0])).astype(BF16)
    x2 = x_ref[0] + _per_row(gate_ref[0], x_ref.shape[1]) * _dot(mixed, wo_ref[...])
    y_ref[0] = _rms(x2, g_ref[...])


def _final(o, z, x, gate, wo, g, *, tm):
    b, t, d = x.shape
    tok = lambda last: pl.BlockSpec((1, tm, last), lambda i, j: (i, j, 0))
    const = lambda a: pl.BlockSpec(a.shape, lambda i, j: (0,) * a.ndim)
    return pl.pallas_call(
        _final_kernel,
        out_shape=jax.ShapeDtypeStruct((b, t, d), F32),
        grid=(b, t // tm),
        in_specs=[tok(o.shape[-1]), tok(d), tok(d), pl.BlockSpec((1,) + gate.shape[1:], lambda i, j: (i, 0, 0)),
                  const(wo), const(g)],
        out_specs=tok(d),
        compiler_params=_params("arbitrary", "arbitrary"),
        name="out_c_final",
    )(o, z, x, gate, wo, g)


def _pad_lanes(w, width=LANES, at=0):
    out = jnp.zeros(w.shape[:-1] + (width,), w.dtype)
    return out.at[..., at:at + w.shape[-1]].set(w)


def _rot_half_cols(w):
    r = w.shape[-1] // 2
    return jnp.concatenate([-w[..., r:], w[..., :r]], axis=-1)


def _rope_tables(pos, rope, scale):
    freqs = jnp.exp(jnp.arange(0, rope, 2, dtype=F32) * (-math.log(ROPE_BASE) / rope))
    ang = pos.astype(F32)[:, None] * freqs[None, :]
    cos = jnp.concatenate([jnp.cos(ang), jnp.cos(ang)], axis=-1)
    sin = jnp.concatenate([jnp.sin(ang), jnp.sin(ang)], axis=-1)
    half = LANES // 2
    cosk = _pad_lanes(cos, at=half)
    sink = _pad_lanes(sin, at=half)
    ones = _pad_lanes(jnp.ones((pos.shape[0], half), F32))
    return jnp.concatenate([(cosk + ones) * scale, sink * scale, cosk, sink], axis=-1)


def _tokens_on_lanes(a, chunk, rows):
    b, t = a.shape[:2]
    return a[..., :rows].reshape(b, t // chunk, chunk, rows).transpose(0, 1, 3, 2)


def kernel(x_prompt, x_sample, c_prompt, c_sample, cache_kv_latent, cache_k_rope, state_mlstm_C, state_mlstm_n, state_mlstm_m, state_gdn_S, state_gdn_conv, a_w_ada, a_b_ada, a_g_norm, a_w_in, a_g_q_a, a_w_q_b, a_g_kv_a, a_w_kv_b, a_b_i, a_b_f, a_g_out, a_w_out, c_w_ada, c_b_ada, c_g_norm, c_w_in, c_w_conv, c_a_log, c_dt_bias, c_g_out, c_w_out, g_final):
    d = x_prompt.shape[-1]
    q_lora, heads, qk = a_w_q_b.shape
    kv_lora = a_w_kv_b.shape[0]
    rope = cache_k_rope.shape[-1]
    nope = qk - rope
    v_head = a_w_kv_b.shape[2] - nope
    m_heads, m_dv = a_g_out.shape
    m_dk = state_mlstm_C.shape[2]
    g_heads = c_a_log.shape[0]
    g_dk, g_dv = state_gdn_S.shape[2:]
    width = c_w_conv.shape[0]
    conv_ch = c_w_conv.shape[1]
    assert nope + rope <= LANES and nope == LANES // 2 and 2 * m_heads <= SUBLANES and 2 * g_heads <= 2 * SUBLANES

    sizes = (q_lora, kv_lora, rope, m_heads * m_dk, m_heads * m_dk, m_heads * m_dv, m_heads, m_heads,
             m_heads * m_dv, heads * v_head + m_heads * m_dv)
    offs = [0]
    for s in sizes:
        offs.append(offs[-1] + s)
    w_qa, w_c, w_kr, w_mq, w_mk, w_mv, w_mi, w_mf, w_mo, w_z = [a_w_in[:, offs[i]:offs[i + 1]] for i in range(10)]
    half = LANES // 2
    w1 = jnp.concatenate([w_qa, w_c, _pad_lanes(w_kr, at=half), _pad_lanes(_rot_half_cols(w_kr), at=half),
                          w_mq, w_mk, w_mv, w_mo, w_z, _pad_lanes(jnp.concatenate([w_mi, w_mf], axis=1))],
                         axis=1).astype(BF16)
    m_width = 2 * m_heads * m_dk + 2 * m_heads * m_dv
    wq_rope = a_w_q_b[..., nope:]
    wq_main = _pad_lanes(a_w_q_b).reshape(q_lora, heads * LANES)
    wq_rot = _pad_lanes(_rot_half_cols(wq_rope), at=nope).reshape(q_lora, heads * LANES)
    wq = jnp.concatenate([wq_main, wq_rot], axis=1).astype(BF16)
    wkv = jnp.concatenate([_pad_lanes(a_w_kv_b[..., :nope]).reshape(kv_lora, heads * LANES),
                           a_w_kv_b[..., nope:].reshape(kv_lora, heads * v_head)], axis=1).astype(BF16)
    wk_abs = a_w_kv_b[..., :nope].transpose(1, 0, 2).astype(BF16)
    wv_abs = a_w_kv_b[..., nope:].transpose(1, 0, 2).astype(BF16)
    wo_a = a_w_out.astype(BF16)
    csz = (conv_ch, g_heads, g_heads, g_heads * g_dv)
    w_qkv, w_a, w_b, w_zc = [c_w_in[:, sum(csz[:i]):sum(csz[:i + 1])] for i in range(4)]
    w2 = jnp.concatenate([w_qkv, w_zc, _pad_lanes(jnp.concatenate([w_a, w_b], axis=1))], axis=1).astype(BF16)
    wo_c = c_w_out.astype(BF16)
    wc8 = jnp.zeros((SUBLANES, conv_ch), F32).at[:width].set(c_w_conv)
    row = lambda a: a.reshape(1, -1).astype(F32)
    bias_c = _pad_lanes(jnp.concatenate([a_b_i, a_b_f]).reshape(1, -1))
    bias_r = jnp.zeros((SUBLANES, 1), F32).at[:2 * m_heads, 0].set(jnp.concatenate([a_b_i, a_b_f]))
    alog = _pad_lanes(c_a_log.reshape(1, -1))
    dtb = _pad_lanes(c_dt_bias.reshape(1, -1))

    bp, bs = c_prompt.shape[0], c_sample.shape[0]
    c_all = jnp.concatenate([c_prompt, c_sample], axis=0)
    pad = (-c_all.shape[0]) % SUBLANES
    c_all = jnp.pad(c_all, ((0, pad), (0, 0)))
    mod_a = _adaln(c_all, a_w_ada, a_b_ada)
    mod_c = _adaln(c_all, c_w_ada, c_b_ada)

    def mods(mod, lo, hi):
        return [mod[lo:hi, i * d:(i + 1) * d][:, None, :] for i in range(3)]

    def run(x, mod_lo, mod_hi, c_past, kr_past, c0, n0, m0, conv0, s0):
        b, t, _ = x.shape
        past = 0 if c_past is None else c_past.shape[1]
        chunk = min(CHUNK, t)
        tc = min(t, 512)
        expand = c_past is None
        nb = 1 if expand else math.gcd(b, max(1, PROJ_TM // t))
        tm = min(nb * t, PROJ_TM)
        grp = lambda a: a.reshape((b // nb, nb * a.shape[1]) + a.shape[2:])
        ungrp = lambda a: a.reshape((b, a.shape[1] // nb) + a.shape[2:])
        shift_a, scale_a, gate_a = [grp(m) for m in mods(mod_a, mod_lo, mod_hi)]
        shift_c, scale_c, gate_c = [grp(m) for m in mods(mod_c, mod_lo, mod_hi)]
        tab = _rope_tables(past + jnp.arange(t, dtype=jnp.int32), rope, qk ** -0.5 * math.log2(math.e))
        outs = _in_a(grp(x), shift_a, scale_a, row(a_g_norm), w1, row(a_g_q_a), wq, row(a_g_kv_a), wkv,
                     jnp.tile(tab, (nb, 1)),
                     tm=tm, heads=heads, q_lora=q_lora, kv_lora=kv_lora, rope=rope, m_width=m_width,
                     mk_cols=(m_heads * m_dk, 2 * m_heads * m_dk), v_head=v_head, expand_kv=expand)
        q, c_new, kr_new, m_slab, z, gates = [ungrp(a) for a in outs[:6]]
        kt = None
        if expand:
            k_all, vt, kt = outs[6], outs[7], outs[10]
            qn = outs[8][..., :heads].transpose(0, 2, 1)
            tiles = dict(heads=heads, v_head=v_head, tq=min(t, FLASH_TQ), tk=min(t, FLASH_TK))
            o_fast, row_sums = _flash(q, k_all, vt, bound=(qn, outs[9]), **tiles)
            o_mla = lax.cond(jnp.min(row_sums) >= FLASH_ROW_SUM_MIN,
                             lambda: o_fast, lambda: _flash(q, k_all, vt, **tiles))
        else:
            o_mla = _latent_attn(q, c_past, kr_past, c_new, kr_new, wk_abs, wv_abs,
                                 heads=heads, nope=nope, rope=rope, v_head=v_head)
        fold = t == chunk and b > 1
        flat = (lambda a: a.reshape((1, b * a.shape[1]) + a.shape[2:])) if fold else (lambda a: a)
        unflat = (lambda a: a.reshape((b, a.shape[1] // b) + a.shape[2:])) if fold else (lambda a: a)
        m_flat = flat(m_slab)
        if kt is None:
            kt = m_flat[..., m_heads * m_dk:2 * m_heads * m_dk].transpose(0, 2, 1)
        pv, kvs, bcum, stats = [unflat(a) for a in _mlstm_prep(
            m_flat, kt, flat(gates), flat(_tokens_on_lanes(gates, chunk, SUBLANES)), bias_c, bias_r,
            chunk=chunk, heads=m_heads, dk=m_dk, dv=m_dv)]
        hm, c1, n1, m1 = _mlstm_scan(m_slab, pv, kvs, bcum, stats, row(a_g_out), c0, n0, m0.reshape(b, m_heads, 1, 1),
                                     tc=tc, heads=m_heads, dk=m_dk, dv=m_dv)
        x1, qkv, zc, ab = [ungrp(a) for a in _mid(grp(o_mla), grp(hm), grp(z), grp(x), gate_a, wo_a, shift_c, scale_c,
                                                  row(c_g_norm), w2, tm=tm, conv_ch=conv_ch)]
        past8 = jnp.pad(conv0, ((0, 0), (SUBLANES - (width - 1), 0), (0, 0)))
        act, gb = _conv(qkv, past8, wc8, ab, alog, dtb, tm=min(t, PROJ_TM), width=width, heads=g_heads, dk=g_dk)
        w, uv, kd, attn, eg = [unflat(a) for a in _gdn_prep(
            flat(act), flat(gb), flat(_tokens_on_lanes(gb, chunk, 2 * SUBLANES)),
            chunk=chunk, tc=GDN_PREP_TC, heads=g_heads, dk=g_dk, dv=g_dv)]
        o_gdn, s1 = _gdn_scan(act, w, uv, kd, attn, eg, row(c_g_out), s0, tc=min(t, GDN_SCAN_TC),
                              heads=g_heads, dk=g_dk, dv=g_dv)
        y = ungrp(_final(grp(o_gdn), grp(zc), grp(x1), gate_c, wo_c, row(g_final), tm=tm))
        conv1 = jnp.concatenate([conv0, qkv], axis=1)[:, t:] if t < width - 1 else qkv[:, t - (width - 1):]
        return (y, c_new, kr_new, c1, n1, m1.reshape(b, m_heads), conv1, s1)

    dt = x_prompt.dtype
    (y_p, p_kv, p_kr, p_c, p_n, p_m, p_conv, p_s) = run(
        x_prompt, 0, bp, None, None,
        jnp.zeros((bp, m_heads, m_dk, m_dv), dt), jnp.zeros((bp, m_heads, m_dk), dt), jnp.zeros((bp, m_heads), dt),
        jnp.zeros((bp, width - 1, conv_ch), dt), jnp.zeros((bp, g_heads, g_dk, g_dv), dt))
    (y_s, s_kv, s_kr, s_c, s_n, s_m, s_conv, s_s) = run(
        x_sample, bp, bp + bs, cache_kv_latent, cache_k_rope, state_mlstm_C, state_mlstm_n, state_mlstm_m,
        state_gdn_conv, state_gdn_S)
    return (y_p, y_s, p_kv, p_kr, p_c, p_n, p_m, p_s, p_conv,
            s_kv, s_kr, s_c, s_n, s_m, s_s, s_conv)
```

```python
import functools
import math

import jax
import jax.numpy as jnp
from jax import lax
from jax.experimental import pallas as pl
from jax.experimental.pallas import tpu as pltpu

F32 = jnp.float32
BF16 = jnp.bfloat16
HIGHEST = lax.Precision.HIGHEST

CHUNK = 64
EPS = 1e-6
ROPE_BASE = 10000.0
LANES = 128
SUBLANES = 8
VMEM_LIMIT = 56 * 1024 * 1024
NEG_INF = float("-inf")
SCAN_SEQS = 4
GDN_SCAN_TC = 256
PROJ_TM = 512
FLASH_TQ = 1024
FLASH_TK = 1024
FLASH_HEAD_GROUP = 4
FLASH_BOUND_SLACK = 1.0 + 2.0 ** -6
FLASH_ROW_SUM_MIN = 2.0 ** -100
MLSTM_PREP_TC = 512
GDN_PREP_TC = 512


def _params(*sem):
    return pltpu.CompilerParams(dimension_semantics=sem, vmem_limit_bytes=VMEM_LIMIT)


def _dot(a, b, precision=None):
    return jnp.dot(a, b, preferred_element_type=F32, precision=precision)


def _dot_nt(a, b):
    return lax.dot_general(a, b, (((1,), (1,)), ((), ())), preferred_element_type=F32)


def _dot_tn(a, b):
    return lax.dot_general(a, b, (((0,), (0,)), ((), ())), preferred_element_type=F32)


def _rms(x, g):
    return x * lax.rsqrt(jnp.mean(x * x, axis=-1, keepdims=True) + EPS) * g


def _per_row(m, rows):
    n, d = m.shape
    if n == 1:
        return m
    return jnp.concatenate([jnp.broadcast_to(m[i:i + 1], (rows // n, d)) for i in range(n)], axis=0)


def _sigmoid(x):
    return 0.5 * jnp.tanh(0.5 * x) + 0.5


def _silu(x):
    return x * _sigmoid(x)


def _softplus(x):
    return jnp.maximum(x, 0.0) + jnp.log1p(jnp.exp(-jnp.abs(x)))


def _log_sigmoid(x):
    return -_softplus(-x)


def _adaln_kernel(c_ref, w_ref, b_ref, o_ref):
    o_ref[...] = _dot(_silu(c_ref[...]), w_ref[...], HIGHEST) + b_ref[...]


def _adaln(c, w, b):
    n, d = c.shape
    d3 = w.shape[1]
    return pl.pallas_call(
        _adaln_kernel,
        out_shape=jax.ShapeDtypeStruct((n, d3), F32),
        grid=(d3 // d,),
        in_specs=[pl.BlockSpec((n, d), lambda j: (0, 0)),
                  pl.BlockSpec((d, d), lambda j: (0, j)),
                  pl.BlockSpec((1, d), lambda j: (0, j))],
        out_specs=pl.BlockSpec((n, d), lambda j: (0, j)),
        compiler_params=_params("arbitrary"),
        name="adaln",
    )(c, w, b.reshape(1, d3))


def _in_a_kernel(x_ref, shift_ref, scale_ref, g_ref, w1_ref, gq_ref, wq_ref, gkv_ref, wkv_ref, tab_ref,
                 q_ref, c_ref, kr_ref, m_ref, z_ref, gt_ref, *kv_refs,
                 heads, q_lora, kv_lora, rope, m_width, mk_cols, d_model, v_head, expand_kv):
    x = x_ref[0]
    hn = _rms(x, g_ref[...]) * (1.0 + _per_row(scale_ref[0], x_ref.shape[1])) + _per_row(shift_ref[0], x_ref.shape[1])
    y = _dot(hn.astype(BF16), w1_ref[...])
    o = 0
    qa = y[:, o:o + q_lora]; o += q_lora
    cl = y[:, o:o + kv_lora]; o += kv_lora
    kr1 = y[:, o:o + LANES]; o += LANES
    kr2 = y[:, o:o + LANES]; o += LANES
    m_ref[0] = y[:, o:o + m_width]
    mk = y[:, o + mk_cols[0]:o + mk_cols[1]]
    o += m_width
    z_ref[0] = y[:, o:o + d_model]; o += d_model
    gt_ref[0] = y[:, o:o + LANES]

    tab = tab_ref[...]
    cosq, sinq = tab[:, 0:LANES], tab[:, LANES:2 * LANES]
    cosk, sink = tab[:, 2 * LANES:3 * LANES], tab[:, 3 * LANES:4 * LANES]

    def sq_norm(xb):
        xf = xb.astype(F32)
        return jnp.sum(xf * xf, axis=-1, keepdims=True)

    lane = lax.broadcasted_iota(jnp.int32, (x.shape[0], LANES), 1)
    qq = _dot(_rms(qa, gq_ref[...]).astype(BF16), wq_ref[...])
    hw = heads * LANES
    qn2 = jnp.zeros((x.shape[0], LANES), F32)
    for h in range(heads):
        sl = slice(h * LANES, (h + 1) * LANES)
        qb = (qq[:, sl] * cosq + qq[:, hw + h * LANES:hw + (h + 1) * LANES] * sinq).astype(BF16)
        q_ref[0, :, sl] = qb
        if expand_kv:
            qn2 = jnp.where(lane == h, sq_norm(qb), qn2)

    cn = _rms(cl, gkv_ref[...])
    c_ref[0] = cn
    kr = kr1 * cosk + kr2 * sink
    kr_ref[0] = kr[:, LANES // 2:LANES // 2 + rope]
    if expand_kv:
        k_ref, vt_ref, qn_ref, kmax_ref, kt_ref = kv_refs
        kv = _dot(cn.astype(BF16), wkv_ref[...])
        kn2 = jnp.zeros((x.shape[0], LANES), F32)
        for h in range(heads):
            sl = slice(h * LANES, (h + 1) * LANES)
            kb = (kv[:, sl] + kr).astype(BF16)
            k_ref[0, :, sl] = kb
            kn2 = jnp.where(lane == h, sq_norm(kb), kn2)
        vt_ref[0] = kv[:, hw:hw + heads * v_head].T.astype(BF16)
        kt_ref[0] = mk.T
        qn_ref[0] = jnp.sqrt(qn2)
        kmax_ref[0, 0] = jnp.sqrt(jnp.max(kn2, axis=0, keepdims=True))


def _in_a(x, shift, scale, g, w1, gq, wq, gkv, wkv, tab, *, tm, heads, q_lora, kv_lora, rope, m_width,
          mk_cols, v_head, expand_kv):
    b, t, d = x.shape
    grid = (b, t // tm)
    tok = lambda last: pl.BlockSpec((1, tm, last), lambda i, j: (i, j, 0))
    tok_t = lambda rows: pl.BlockSpec((1, rows, tm), lambda i, j: (i, 0, j))
    const = lambda a: pl.BlockSpec(a.shape, lambda i, j: (0,) * a.ndim)
    out_shape = [jax.ShapeDtypeStruct((b, t, heads * LANES), BF16),
                 jax.ShapeDtypeStruct((b, t, kv_lora), F32),
                 jax.ShapeDtypeStruct((b, t, rope), F32),
                 jax.ShapeDtypeStruct((b, t, m_width), F32),
                 jax.ShapeDtypeStruct((b, t, d), F32),
                 jax.ShapeDtypeStruct((b, t, LANES), F32)]
    out_specs = [tok(heads * LANES), tok(kv_lora), tok(rope), tok(m_width), tok(d), tok(LANES)]
    if expand_kv:
        out_shape += [jax.ShapeDtypeStruct((b, t, heads * LANES), BF16),
                      jax.ShapeDtypeStruct((b, heads * v_head, t), BF16),
                      jax.ShapeDtypeStruct((b, t, LANES), F32),
                      jax.ShapeDtypeStruct((b, t // tm, 1, LANES), F32),
                      jax.ShapeDtypeStruct((b, mk_cols[1] - mk_cols[0], t), F32)]
        out_specs += [tok(heads * LANES), tok_t(heads * v_head), tok(LANES),
                      pl.BlockSpec((1, 1, 1, LANES), lambda i, j: (i, j, 0, 0)), tok_t(mk_cols[1] - mk_cols[0])]
    kern = functools.partial(_in_a_kernel, heads=heads, q_lora=q_lora, kv_lora=kv_lora, rope=rope,
                             m_width=m_width, mk_cols=mk_cols, d_model=d, v_head=v_head, expand_kv=expand_kv)
    return pl.pallas_call(
        kern, out_shape=out_shape, grid=grid,
        in_specs=[tok(d),
                  pl.BlockSpec((1,) + shift.shape[1:], lambda i, j: (i, 0, 0)),
                  pl.BlockSpec((1,) + scale.shape[1:], lambda i, j: (i, 0, 0)),
                  const(g), const(w1), const(gq), const(wq), const(gkv), const(wkv),
                  pl.BlockSpec((tm, 4 * LANES), lambda i, j: (j, 0))],
        out_specs=out_specs,
        compiler_params=_params("arbitrary", "arbitrary"),
        name="in_proj_a",
    )(x, shift, scale, g, w1, gq, wq, gkv, wkv, tab)


def _flash_kernel(qi_ref, ki_ref, q_ref, k_ref, vt_ref, o_ref, m_sc, l_sc, acc_sc, *,
                  heads, v_head, tq, tk, chunk):
    step_id = pl.program_id(1)
    qi = qi_ref[step_id]
    ki = ki_ref[step_id]

    @pl.when(ki == 0)
    def _():
        m_sc[...] = jnp.full(m_sc.shape, NEG_INF, F32)
        l_sc[...] = jnp.zeros(l_sc.shape, F32)
        acc_sc[...] = jnp.zeros(acc_sc.shape, F32)

    def step(masked):
        if masked:
            kc = (ki * tk + lax.broadcasted_iota(jnp.int32, (tk, tq), 0)) // chunk
            qc = (qi * tq + lax.broadcasted_iota(jnp.int32, (tk, tq), 1)) // chunk
            mask = kc <= qc
        for h in range(heads):
            qh = q_ref[0, :, h * LANES:(h + 1) * LANES]
            kh = k_ref[0, :, h * LANES:(h + 1) * LANES]
            vth = vt_ref[0, h * v_head:(h + 1) * v_head, :]
            rows = slice(h * v_head, (h + 1) * v_head)
            st = _dot_nt(kh, qh)
            if masked:
                st = jnp.where(mask, st, NEG_INF)
            m_prev = m_sc[h]
            m_new = jnp.maximum(m_prev, jnp.max(st, axis=0, keepdims=True))
            alpha = jnp.exp2(m_prev - m_new)
            p = jnp.exp2(st - m_new)
            l_sc[h] = alpha * l_sc[h] + jnp.sum(p, axis=0, keepdims=True)
            acc_sc[rows, :] = alpha * acc_sc[rows, :] + _dot(vth, p.astype(BF16))
            m_sc[h] = m_new

    full = (ki + 1) * tk <= qi * tq + chunk

    @pl.when(full)
    def _():
        step(False)

    @pl.when(jnp.logical_not(full))
    def _():
        step(True)

    @pl.when(ki == ((qi + 1) * tq - 1) // tk)
    def _():
        for h in range(heads):
            rows = slice(h * v_head, (h + 1) * v_head)
            acc_sc[rows, :] = acc_sc[rows, :] / l_sc[h]
        o_ref[0] = acc_sc[...].T


def _flash_bound_kernel(qi_ref, ki_ref, q_ref, k_ref, vt_ref, qn_ref, kmax_ref, o_ref, l_ref,
                        mb_sc, l_sc, acc_sc, *, heads, v_head, tq, tk, chunk):
    step_id = pl.program_id(1)
    qi = qi_ref[step_id]
    ki = ki_ref[step_id]

    @pl.when(ki == 0)
    def _():
        kmax = jnp.max(kmax_ref[0], axis=0) * FLASH_BOUND_SLACK
        for h in range(heads):
            mb_sc[h:h + 1, :] = qn_ref[0, h:h + 1, :] * kmax[:, h:h + 1]
        l_sc[...] = jnp.zeros(l_sc.shape, F32)
        acc_sc[...] = jnp.zeros(acc_sc.shape, F32)

    def step(masked):
        if masked:
            kc = (ki * tk + lax.broadcasted_iota(jnp.int32, (tk, tq), 0)) // chunk
            qc = (qi * tq + lax.broadcasted_iota(jnp.int32, (tk, tq), 1)) // chunk
            mask = kc <= qc
        for h0 in range(0, heads, FLASH_HEAD_GROUP):
            hs = range(h0, h0 + FLASH_HEAD_GROUP)
            st = [_dot_nt(k_ref[0, :, h * LANES:(h + 1) * LANES], q_ref[0, :, h * LANES:(h + 1) * LANES])
                  for h in hs]
            if masked:
                st = [jnp.where(mask, s, NEG_INF) for s in st]
            p = [jnp.exp2(s - mb_sc[h:h + 1, :]) for s, h in zip(st, hs)]
            for x, h in zip(p, hs):
                l_sc[h:h + 1, :] += jnp.sum(x, axis=0, keepdims=True)
            pv = [_dot(vt_ref[0, h * v_head:(h + 1) * v_head, :], x.astype(BF16)) for x, h in zip(p, hs)]
            for x, h in zip(pv, hs):
                acc_sc[h * v_head:(h + 1) * v_head, :] += x

    full = (ki + 1) * tk <= qi * tq + chunk

    @pl.when(full)
    def _():
        step(False)

    @pl.when(jnp.logical_not(full))
    def _():
        step(True)

    @pl.when(ki == ((qi + 1) * tq - 1) // tk)
    def _():
        for h in range(heads):
            rows = slice(h * v_head, (h + 1) * v_head)
            acc_sc[rows, :] = acc_sc[rows, :] / l_sc[h:h + 1, :]
        o_ref[0] = acc_sc[...].T
        l_ref[0] = l_sc[...]


def _flash(q, k, vt, *, heads, v_head, tq, tk, bound=None):
    b, t, _ = q.shape
    pairs = [(i, j) for i in range(t // tq) for j in range(((i + 1) * tq - 1) // tk + 1)]
    qi_tab = jnp.asarray([p[0] for p in pairs], jnp.int32)
    ki_tab = jnp.asarray([p[1] for p in pairs], jnp.int32)
    in_specs = [pl.BlockSpec((1, tq, heads * LANES), lambda i, s, qt, kt: (i, qt[s], 0)),
                pl.BlockSpec((1, tk, heads * LANES), lambda i, s, qt, kt: (i, kt[s], 0)),
                pl.BlockSpec((1, heads * v_head, tk), lambda i, s, qt, kt: (i, 0, kt[s]))]
    o_shape = jax.ShapeDtypeStruct((b, t, heads * v_head), F32)
    o_spec = pl.BlockSpec((1, tq, heads * v_head), lambda i, s, qt, kt: (i, qt[s], 0))
    acc = pltpu.VMEM((heads * v_head, tq), F32)
    if bound is None:
        kern, name, args = _flash_kernel, "flash_attn", (q, k, vt)
        out_shape, out_specs = o_shape, o_spec
        scratch = [pltpu.VMEM((heads, 1, tq), F32), pltpu.VMEM((heads, 1, tq), F32), acc]
    else:
        qn, kmax = bound
        assert qn.shape == (b, heads, t)
        kern, name, args = _flash_bound_kernel, "flash_attn_bound", (q, k, vt, qn, kmax)
        in_specs += [pl.BlockSpec((1, heads, tq), lambda i, s, qt, kt: (i, 0, qt[s])),
                     pl.BlockSpec((1,) + kmax.shape[1:], lambda i, s, qt, kt: (i, 0, 0, 0))]
        out_shape = [o_shape, jax.ShapeDtypeStruct((b, heads, t), F32)]
        out_specs = [o_spec, pl.BlockSpec((1, heads, tq), lambda i, s, qt, kt: (i, 0, qt[s]))]
        scratch = [pltpu.VMEM((heads, tq), F32), pltpu.VMEM((heads, tq), F32), acc]
    grid_spec = pltpu.PrefetchScalarGridSpec(
        num_scalar_prefetch=2, grid=(b, len(pairs)), in_specs=in_specs, out_specs=out_specs,
        scratch_shapes=scratch)
    return pl.pallas_call(
        functools.partial(kern, heads=heads, v_head=v_head, tq=tq, tk=tk, chunk=CHUNK),
        out_shape=out_shape,
        grid_spec=grid_spec,
        compiler_params=_params("arbitrary", "arbitrary"),
        name=name,
    )(qi_tab, ki_tab, *args)


def _latent_attn_kernel(q_ref, cp_ref, krp_ref, cn_ref, krn_ref, wk_ref, wv_ref, o_ref, *,
                        heads, nope, rope, v_head):
    q = q_ref[0]
    qabs, qrope = [], []
    for h in range(heads):
        qabs.append(_dot_nt(q[:, h * LANES:h * LANES + nope], wk_ref[h]))
        qrope.append(q[:, h * LANES + nope:h * LANES + nope + rope])
    qabs = jnp.concatenate(qabs, axis=0).astype(BF16)
    qrope = jnp.concatenate(qrope, axis=0)
    cp = cp_ref[0].astype(BF16)
    cn = cn_ref[0].astype(BF16)
    s_p = _dot_nt(qabs, cp) + _dot_nt(qrope, krp_ref[0].astype(BF16))
    s_n = _dot_nt(qabs, cn) + _dot_nt(qrope, krn_ref[0].astype(BF16))
    m = jnp.maximum(jnp.max(s_p, axis=-1, keepdims=True), jnp.max(s_n, axis=-1, keepdims=True))
    p_p = jnp.exp2(s_p - m)
    p_n = jnp.exp2(s_n - m)
    l = jnp.sum(p_p, axis=-1, keepdims=True) + jnp.sum(p_n, axis=-1, keepdims=True)
    o_lat = (_dot(p_p.astype(BF16), cp) + _dot(p_n.astype(BF16), cn)) / l
    t = q.shape[0]
    for h in range(heads):
        o_ref[0, :, h * v_head:(h + 1) * v_head] = _dot(o_lat[h * t:(h + 1) * t].astype(BF16), wv_ref[h])


def _latent_attn(q, c_past, kr_past, c_new, kr_new, wk, wv, *, heads, nope, rope, v_head):
    b, t, _ = q.shape
    past, kv_lora = c_past.shape[1:]
    blk = lambda n, last: pl.BlockSpec((1, n, last), lambda i: (i, 0, 0))
    const = lambda a: pl.BlockSpec(a.shape, lambda i: (0,) * a.ndim)
    kern = functools.partial(_latent_attn_kernel, heads=heads, nope=nope, rope=rope, v_head=v_head)
    return pl.pallas_call(
        kern,
        out_shape=jax.ShapeDtypeStruct((b, t, heads * v_head), F32),
        grid=(b,),
        in_specs=[blk(t, heads * LANES), blk(past, kv_lora), blk(past, rope), blk(t, kv_lora), blk(t, rope),
                  const(wk), const(wv)],
        out_specs=blk(t, heads * v_head),
        compiler_params=_params("arbitrary"),
        name="latent_attn",
    )(q, c_past, kr_past, c_new, kr_new, wk, wv)


def _mlstm_prep_kernel(m_ref, kt_ref, gc_ref, gr_ref, bc_ref, br_ref, pv_ref, kv_ref, b_ref, st_ref, *,
                       heads, dk, dv, chunk, nchunk):
    row = lax.broadcasted_iota(jnp.int32, (chunk, chunk), 0)
    col = lax.broadcasted_iota(jnp.int32, (chunk, chunk), 1)
    causal = col <= row
    tril = causal.astype(F32)
    triu = (row <= col).astype(F32)
    lane = lax.broadcasted_iota(jnp.int32, (chunk, LANES), 1)
    o_k, o_v = heads * dk, 2 * heads * dk

    rows = [slice(c * chunk, (c + 1) * chunk) for c in range(nchunk)]
    gc = [gc_ref[0, r, :] + bc_ref[...] for r in rows]
    gr = [gr_ref[0, c] + br_ref[...] for c in range(nchunk)]
    bcum_c = [_dot(tril, _log_sigmoid(g), HIGHEST) for g in gc]
    bcum_r = [_dot(_log_sigmoid(g), triu, HIGHEST) for g in gr]
    for c in range(nchunk):
        b_ref[0, rows[c], :] = bcum_c[c]
    ch = [(c, h) for c in range(nchunk) for h in range(heads)]
    v = [m_ref[0, rows[c], o_v + h * dv:o_v + (h + 1) * dv].astype(BF16) for c, h in ch]
    qk = [_dot_nt(m_ref[0, rows[c], h * dk:(h + 1) * dk].astype(BF16),
                  (m_ref[0, rows[c], o_k + h * dk:o_k + (h + 1) * dk] * (dk ** -0.5)).astype(BF16)) for c, h in ch]
    li_r = [gr[c][h:h + 1, :] for c, h in ch]
    b_r = [bcum_r[c][heads + h:heads + h + 1, :] for c, h in ch]
    dmat = [jnp.where(causal, bcum_c[c][:, heads + h:heads + h + 1] - b_r[i] + li_r[i], NEG_INF)
            for i, (c, h) in enumerate(ch)]
    mx = [jnp.max(d, axis=-1, keepdims=True) for d in dmat]
    p0 = [s * jnp.exp(d - m) for s, d, m in zip(qk, dmat, mx)]
    for i, (c, h) in enumerate(ch):
        pv_ref[0, rows[c], h * dv:(h + 1) * dv] = _dot(p0[i].astype(BF16), v[i])
    w_r = [jnp.exp(b_r[i][:, chunk - 1:chunk] - b_r[i] + li_r[i] - mx[i][chunk - 1:chunk, :]) * (dk ** -0.5)
           for i in range(len(ch))]
    for i, (c, h) in enumerate(ch):
        wkt = (kt_ref[0, h * dk:(h + 1) * dk, c * chunk:(c + 1) * chunk] * w_r[i]).astype(BF16)
        kv_ref[0, c, h] = _dot(wkt, jnp.concatenate([v[i], jnp.ones_like(v[i])], axis=1))
    psum = [jnp.sum(p, axis=-1, keepdims=True) for p in p0]
    for c in range(nchunk):
        stats = jnp.zeros((chunk, LANES), F32)
        for h in range(heads):
            stats = jnp.where(lane == h, mx[c * heads + h], stats)
            stats = jnp.where(lane == heads + h, psum[c * heads + h], stats)
        st_ref[0, rows[c], :] = stats


def _mlstm_scan_kernel(q_ref, mo_ref, pv_ref, kv_ref, b_ref, st_ref, gout_ref, c0_ref, n0_ref, m0_ref,
                       h_ref, c1_ref, n1_ref, m1_ref, c_sc, m_sc, *, heads, dk, dv, chunk, nchunk):
    t = pl.program_id(1)
    nseq = q_ref.shape[0]

    @pl.when(t == 0)
    def _():
        c_sc[:, :, :, :dv] = c0_ref[...]
        c_sc[:, :, :, dv:] = jnp.broadcast_to(n0_ref[...], c0_ref.shape)
        m_sc[...] = m0_ref[...]

    def body(c, carry):
        rows = pl.ds(pl.multiple_of(c * chunk, chunk), chunk)
        rep = lambda col: jnp.broadcast_to(col, (chunk, dv))
        ps = [(s, h) for s in range(nseq) for h in range(heads)]
        bcum = [b_ref[s, rows, :] for s in range(nseq)]
        stats = [st_ref[s, rows, :] for s in range(nseq)]
        c2 = [c_sc[s, h] for s, h in ps]
        qc = [_dot(q_ref[s, rows, h * dk:(h + 1) * dk].astype(BF16), c2[i].astype(BF16))
              for i, (s, h) in enumerate(ps)]
        mx = [rep(stats[s][:, h:h + 1]) for s, h in ps]
        inter = [rep(bcum[s][:, heads + h:heads + h + 1]) + m_sc[s, h] for s, h in ps]
        m = [jnp.maximum(a, b) for a, b in zip(inter, mx)]
        w_inter = [jnp.exp(a - b) for a, b in zip(inter, m)]
        r = [jnp.exp(a - b) for a, b in zip(mx, m)]
        for i, (s, h) in enumerate(ps):
            decay_end = w_inter[i][chunk - 1:chunk, :]
            f_new = r[i][chunk - 1:chunk, :]
            c_sc[s, h] = (jnp.concatenate([decay_end, decay_end], axis=1) * c2[i]
                          + jnp.concatenate([f_new, f_new], axis=1) * kv_ref[s, c, h])
            m_sc[s, h] = m[i][chunk - 1:chunk, 0:1]
        num = [w_inter[i] * qc[i][:, :dv] + r[i] * pv_ref[s, rows, h * dv:(h + 1) * dv] for i, (s, h) in enumerate(ps)]
        den = [w_inter[i] * qc[i][:, dv:] + r[i] * rep(stats[s][:, heads + h:heads + h + 1])
               for i, (s, h) in enumerate(ps)]
        hh = [a / jnp.maximum(jnp.abs(b), jnp.exp(-c_)) for a, b, c_ in zip(num, den, m)]
        hn = [_rms(hh[i], gout_ref[:, h * dv:(h + 1) * dv]) for i, (s, h) in enumerate(ps)]
        for i, (s, h) in enumerate(ps):
            h_ref[s, rows, h * dv:(h + 1) * dv] = hn[i] * _sigmoid(mo_ref[s, rows, h * dv:(h + 1) * dv])
        return carry

    lax.fori_loop(0, nchunk, body, 0)

    @pl.when(t == pl.num_programs(1) - 1)
    def _():
        c1_ref[...] = c_sc[:, :, :, :dv]
        n1_ref[...] = c_sc[:, :, :, dv:dv + 1]
        m1_ref[...] = m_sc[...]


def _mlstm_prep(m_slab, kt, gates_c, gates_r, bias_c, bias_r, *, chunk, heads, dk, dv):
    b, t, mw = m_slab.shape
    const = lambda a: pl.BlockSpec(a.shape, lambda i, j: (0,) * a.ndim)
    qkv_w = 2 * heads * dk + heads * dv
    tp = min(t, MLSTM_PREP_TC)
    npc = tp // chunk
    tokp = lambda last: pl.BlockSpec((1, tp, last), lambda i, j: (i, j, 0))
    per_chunk = lambda *s: pl.BlockSpec((1, npc, heads) + s, lambda i, j: (i, j, 0) + (0,) * len(s))
    return pl.pallas_call(
        functools.partial(_mlstm_prep_kernel, heads=heads, dk=dk, dv=dv, chunk=chunk, nchunk=npc),
        out_shape=[jax.ShapeDtypeStruct((b, t, heads * dv), F32),
                   jax.ShapeDtypeStruct((b, t // chunk, heads, dk, 2 * dv), F32),
                   jax.ShapeDtypeStruct((b, t, LANES), F32),
                   jax.ShapeDtypeStruct((b, t, LANES), F32)],
        grid=(b, t // tp),
        in_specs=[tokp(qkv_w), pl.BlockSpec((1, heads * dk, tp), lambda i, j: (i, 0, j)), tokp(LANES),
                  pl.BlockSpec((1, npc, SUBLANES, chunk), lambda i, j: (i, j, 0, 0)),
                  const(bias_c), const(bias_r)],
        out_specs=[tokp(heads * dv), per_chunk(dk, 2 * dv), tokp(LANES), tokp(LANES)],
        compiler_params=_params("arbitrary", "arbitrary"),
        name="mlstm_prep",
    )(m_slab, kt, gates_c, gates_r, bias_c, bias_r)


def _mlstm_scan(m_slab, pv, kv, bcum, stats, gout, c0, n0, m0, *, tc, heads, dk, dv):
    b, t, mw = m_slab.shape
    chunk = min(CHUNK, t)
    const = lambda a: pl.BlockSpec(a.shape, lambda i, j: (0,) * a.ndim)
    qkv_w = 2 * heads * dk + heads * dv
    mo_blk, rem = divmod(qkv_w, heads * dv)
    assert rem == 0 and mw == qkv_w + heads * dv
    nchunk = tc // chunk
    ns = math.gcd(b, SCAN_SEQS)
    tok = lambda last, blk=0: pl.BlockSpec((ns, tc, last), lambda i, j: (i, j, blk))
    per_chunk = lambda *s: pl.BlockSpec((ns, nchunk, heads) + s, lambda i, j: (i, j, 0) + (0,) * len(s))
    st = lambda *s: pl.BlockSpec((ns,) + s, lambda i, j: (i,) + (0,) * len(s))
    hm, c1, n1, m1 = pl.pallas_call(
        functools.partial(_mlstm_scan_kernel, heads=heads, dk=dk, dv=dv, chunk=chunk, nchunk=nchunk),
        out_shape=[jax.ShapeDtypeStruct((b, t, heads * dv), F32),
                   jax.ShapeDtypeStruct((b, heads, dk, dv), F32),
                   jax.ShapeDtypeStruct((b, heads, dk, 1), F32),
                   jax.ShapeDtypeStruct((b, heads, 1, 1), F32)],
        grid=(b // ns, t // tc),
        in_specs=[tok(heads * dk),
                  tok(heads * dv, mo_blk),
                  tok(heads * dv), per_chunk(dk, 2 * dv), tok(LANES), tok(LANES),
                  const(gout), st(heads, dk, dv), st(heads, dk, 1), st(heads, 1, 1)],
        out_specs=[tok(heads * dv), st(heads, dk, dv), st(heads, dk, 1), st(heads, 1, 1)],
        scratch_shapes=[pltpu.VMEM((ns, heads, dk, 2 * dv), F32), pltpu.VMEM((ns, heads, 1, 1), F32)],
        compiler_params=_params("arbitrary", "arbitrary"),
        name="mlstm_scan",
    )(m_slab, m_slab, pv, kv, bcum, stats, gout, c0, n0[..., None], m0)
    return hm, c1, n1[..., 0], m1


def _mid_kernel(oa_ref, ob_ref, z_ref, x_ref, gate_ref, wo_ref, shift_ref, scale_ref, g_ref, w2_ref,
                x1_ref, qkv_ref, z2_ref, ab_ref, *, half, conv_ch, d_model):
    z = z_ref[0]
    ma = (oa_ref[0] * _silu(z[:, :half])).astype(BF16)
    mb = (ob_ref[0] * _silu(z[:, half:])).astype(BF16)
    y = _dot(ma, wo_ref[0:half, :]) + _dot(mb, wo_ref[half:, :])
    x1 = x_ref[0] + _per_row(gate_ref[0], x_ref.shape[1]) * y
    x1_ref[0] = x1
    hn = _rms(x1, g_ref[...]) * (1.0 + _per_row(scale_ref[0], x_ref.shape[1])) + _per_row(shift_ref[0], x_ref.shape[1])
    y2 = _dot(hn.astype(BF16), w2_ref[...])
    qkv_ref[0] = y2[:, :conv_ch]
    z2_ref[0] = y2[:, conv_ch:conv_ch + d_model]
    ab_ref[0] = y2[:, conv_ch + d_model:]


def _mid(oa, ob, z, x, gate, wo, shift, scale, g, w2, *, tm, conv_ch):
    b, t, d = x.shape
    half = oa.shape[-1]
    tok = lambda last: pl.BlockSpec((1, tm, last), lambda i, j: (i, j, 0))
    vec = pl.BlockSpec((1,) + gate.shape[1:], lambda i, j: (i, 0, 0))
    const = lambda a: pl.BlockSpec(a.shape, lambda i, j: (0,) * a.ndim)
    kern = functools.partial(_mid_kernel, half=half, conv_ch=conv_ch, d_model=d)
    return pl.pallas_call(
        kern,
        out_shape=[jax.ShapeDtypeStruct((b, t, d), F32), jax.ShapeDtypeStruct((b, t, conv_ch), F32),
                   jax.ShapeDtypeStruct((b, t, d), F32), jax.ShapeDtypeStruct((b, t, LANES), F32)],
        grid=(b, t // tm),
        in_specs=[tok(half), tok(ob.shape[-1]), tok(d), tok(d), vec, const(wo), vec, vec, const(g), const(w2)],
        out_specs=[tok(d), tok(conv_ch), tok(d), tok(LANES)],
        compiler_params=_params("arbitrary", "arbitrary"),
        name="out_a_in_c",
    )(oa, ob, z, x, gate, wo, shift, scale, g, w2)


def _conv_kernel(qkv_ref, past_ref, wc_ref, ab_ref, alog_ref, dtb_ref, act_ref, gb_ref, ext_sc, *,
                 tm, width, heads, dk):
    @pl.when(pl.program_id(1) == 0)
    def _():
        ext_sc[0:SUBLANES, :] = past_ref[0]

    ext_sc[SUBLANES:SUBLANES + tm, :] = qkv_ref[0]
    conv = wc_ref[width - 1:width, :] * ext_sc[SUBLANES:SUBLANES + tm, :]
    for j in range(width - 1):
        s = SUBLANES - (width - 1) + j
        conv = conv + wc_ref[j:j + 1, :] * ext_sc[s:s + tm, :]
    ext_sc[0:SUBLANES, :] = ext_sc[tm:tm + SUBLANES, :]
    act = _silu(conv)
    for h in range(2 * heads):
        xh = act[:, h * dk:(h + 1) * dk]
        xh = xh * lax.rsqrt(jnp.sum(xh * xh, axis=-1, keepdims=True) + EPS)
        if h < heads:
            xh = xh * (dk ** -0.5)
        act_ref[0, :, h * dk:(h + 1) * dk] = xh
    act_ref[0, :, 2 * heads * dk:] = act[:, 2 * heads * dk:]
    ab = ab_ref[0]
    g = -jnp.exp(alog_ref[...]) * _softplus(ab + dtb_ref[...])
    lane = lax.broadcasted_iota(jnp.int32, ab.shape, 1)
    gb_ref[0] = jnp.where(lane < heads, g, _sigmoid(ab))


def _conv(qkv, past8, wc8, ab, alog, dtb, *, tm, width, heads, dk):
    b, t, ch = qkv.shape
    tok = lambda last: pl.BlockSpec((1, tm, last), lambda i, j: (i, j, 0))
    const = lambda a: pl.BlockSpec(a.shape, lambda i, j: (0,) * a.ndim)
    kern = functools.partial(_conv_kernel, tm=tm, width=width, heads=heads, dk=dk)
    return pl.pallas_call(
        kern,
        out_shape=[jax.ShapeDtypeStruct((b, t, ch), F32), jax.ShapeDtypeStruct((b, t, LANES), F32)],
        grid=(b, t // tm),
        in_specs=[tok(ch), pl.BlockSpec((1, SUBLANES, ch), lambda i, j: (i, 0, 0)), const(wc8), tok(LANES),
                  const(alog), const(dtb)],
        out_specs=[tok(ch), tok(LANES)],
        scratch_shapes=[pltpu.VMEM((tm + SUBLANES, ch), F32)],
        compiler_params=_params("arbitrary", "arbitrary"),
        name="conv_gates",
    )(qkv, past8, wc8, ab, alog, dtb)


def _blockdiag(x, group, chunk):
    w = group * chunk
    br = lax.broadcasted_iota(jnp.int32, (w, w), 0) // chunk
    bc = lax.broadcasted_iota(jnp.int32, (w, w), 1) // chunk
    xb = x.astype(BF16)
    return jnp.where(br == bc, jnp.concatenate([xb] * group, axis=0), jnp.zeros((), BF16))


def _unit_lower_inverses_minus_eye(a_list, group, chunk):
    w = group * chunk
    r = lax.broadcasted_iota(jnp.int32, (chunk, w), 0)
    cc = lax.broadcasted_iota(jnp.int32, (chunk, w), 1) % chunk
    es = [-jnp.where((r // 2 == cc // 2) & (r % 2 == 1) & (cc % 2 == 0), a4, 0.0) for a4 in a_list]
    s = 2
    while s < chunk:
        off = (r // (2 * s) == cc // (2 * s)) & (r % (2 * s) >= s) & (cc % (2 * s) < s)
        a_offs = [jnp.where(off, a4, 0.0) for a4 in a_list]
        ps = [a + _dot(a.astype(BF16), _blockdiag(e, group, chunk)) for a, e in zip(a_offs, es)]
        es = [e - (p + _dot(e.astype(BF16), _blockdiag(p, group, chunk))) for e, p in zip(es, ps)]
        s *= 2
    return es


def _gdn_prep_kernel(act_ref, gbc_ref, gbr_ref, w_ref, uv_ref, kd_ref, attn_ref, eg_ref, *,
                     heads, dk, dv, chunk, nchunk, group):
    row = lax.broadcasted_iota(jnp.int32, (chunk, chunk), 0)
    col = lax.broadcasted_iota(jnp.int32, (chunk, chunk), 1)
    incl = col <= row
    strict = col < row
    tril = incl.astype(F32)
    triu = (row <= col).astype(F32)
    lane = lax.broadcasted_iota(jnp.int32, (chunk, LANES), 1)
    o_k, o_v = heads * dk, 2 * heads * dk

    rows = [slice(c * chunk, (c + 1) * chunk) for c in range(nchunk)]
    gbc = [gbc_ref[0, r, :] for r in rows]
    gcum_c = [_dot(tril, g, HIGHEST) for g in gbc]
    gcum_r = [_dot(gbr_ref[0, c], triu, HIGHEST) for c in range(nchunk)]
    for c in range(nchunk):
        eg_ref[0, rows[c], :] = jnp.where(lane < heads, jnp.exp(gcum_c[c]), 0.0)
    ch = [(c, h) for c in range(nchunk) for h in range(heads)]
    k = [act_ref[0, rows[c], o_k + h * dk:o_k + (h + 1) * dk] for c, h in ch]
    kb = [x.astype(BF16) for x in k]
    kk = [_dot_nt(x, x) for x in kb]
    qk = [_dot_nt(act_ref[0, rows[c], h * dk:(h + 1) * dk].astype(BF16), kb[i]) for i, (c, h) in enumerate(ch)]
    g_c = [gcum_c[c][:, h:h + 1] for c, h in ch]
    beta = [gbc[c][:, heads + h:heads + h + 1] for c, h in ch]
    decay = [jnp.exp(jnp.where(incl, g_c[i] - gcum_r[c][h:h + 1, :], NEG_INF)) for i, (c, h) in enumerate(ch)]
    a_blk = [jnp.where(strict, beta[i] * kk[i] * decay[i], 0.0) for i in range(len(ch))]
    for i, (c, h) in enumerate(ch):
        attn_ref[0, rows[c], h * chunk:(h + 1) * chunk] = (qk[i] * decay[i]).astype(BF16)
        kd_ref[0, rows[c], h * dk:(h + 1) * dk] = (k[i] * jnp.exp(g_c[i][chunk - 1:chunk, :] - g_c[i])).astype(BF16)
    rhs_blk = [jnp.concatenate([beta[i] * act_ref[0, rows[c], o_v + h * dv:o_v + (h + 1) * dv],
                                (beta[i] * jnp.exp(g_c[i])) * k[i]], axis=1) for i, (c, h) in enumerate(ch)]
    problems = [(c, g0) for c in range(nchunk) for g0 in range(0, heads, group)]
    a_list = [jnp.concatenate(a_blk[c * heads + g0:c * heads + g0 + group], axis=1) for c, g0 in problems]
    rhs_list = [jnp.concatenate(rhs_blk[c * heads + g0:c * heads + g0 + group], axis=0) for c, g0 in problems]
    e_list = _unit_lower_inverses_minus_eye(a_list, group, chunk)
    sols = [rhs + _dot(_blockdiag(e, group, chunk), rhs.astype(BF16)) for e, rhs in zip(e_list, rhs_list)]
    for (c, g0), sol in zip(problems, sols):
        for i, h in enumerate(range(g0, g0 + group)):
            uv_ref[0, rows[c], h * dv:(h + 1) * dv] = sol[i * chunk:(i + 1) * chunk, :dv]
            w_ref[0, rows[c], h * dk:(h + 1) * dk] = sol[i * chunk:(i + 1) * chunk, dv:].astype(BF16)


def _gdn_prep(act, gb_c, gb_r, *, chunk, tc, heads, dk, dv):
    b, t, _ = act.shape
    tc = min(tc, t)
    nchunk = tc // chunk
    group = (2 * LANES) // chunk
    kern = functools.partial(_gdn_prep_kernel, heads=heads, dk=dk, dv=dv, chunk=chunk, nchunk=nchunk, group=group)
    tok = lambda last: pl.BlockSpec((1, tc, last), lambda i, j: (i, j, 0))
    return pl.pallas_call(
        kern,
        out_shape=[jax.ShapeDtypeStruct((b, t, heads * dk), BF16), jax.ShapeDtypeStruct((b, t, heads * dv), F32),
                   jax.ShapeDtypeStruct((b, t, heads * dk), BF16), jax.ShapeDtypeStruct((b, t, heads * chunk), BF16),
                   jax.ShapeDtypeStruct((b, t, LANES), F32)],
        grid=(b, t // tc),
        in_specs=[tok(act.shape[-1]), tok(LANES),
                  pl.BlockSpec((1, nchunk, 2 * SUBLANES, chunk), lambda i, j: (i, j, 0, 0))],
        out_specs=[tok(heads * dk), tok(heads * dv), tok(heads * dk), tok(heads * chunk), tok(LANES)],
        compiler_params=_params("arbitrary", "arbitrary"),
        name="gdn_prep",
    )(act, gb_c, gb_r)


def _gdn_scan_kernel(q_ref, w_ref, uv_ref, kd_ref, attn_ref, eg_ref, gout_ref, s0_ref, o_ref, s1_ref, s_sc, *,
                     heads, dk, dv, chunk, nchunk):
    t = pl.program_id(1)
    nseq = q_ref.shape[0]

    @pl.when(t == 0)
    def _():
        s_sc[...] = s0_ref[...]

    def body(c, carry):
        rows = pl.ds(pl.multiple_of(c * chunk, chunk), chunk)
        ps = [(s, h) for s in range(nseq) for h in range(heads)]
        eg = [eg_ref[s, rows, :] for s in range(nseq)]
        s0 = [s_sc[s, h] for s, h in ps]
        s0b = [x.astype(BF16) for x in s0]
        ws = [_dot(w_ref[s, rows, h * dk:(h + 1) * dk], s0b[i]) for i, (s, h) in enumerate(ps)]
        ub = [(uv_ref[s, rows, h * dv:(h + 1) * dv] - ws[i]).astype(BF16) for i, (s, h) in enumerate(ps)]
        eg_h = [jnp.broadcast_to(eg[s][:, h:h + 1], (chunk, dv)) for s, h in ps]
        ku = [_dot_tn(kd_ref[s, rows, h * dk:(h + 1) * dk], ub[i]) for i, (s, h) in enumerate(ps)]
        for i, (s, h) in enumerate(ps):
            s_sc[s, h] = eg_h[i][chunk - 1:chunk, :] * s0[i] + ku[i]
        qs = [_dot(q_ref[s, rows, h * dk:(h + 1) * dk].astype(BF16), s0b[i]) for i, (s, h) in enumerate(ps)]
        au = [_dot(attn_ref[s, rows, h * chunk:(h + 1) * chunk], ub[i]) for i, (s, h) in enumerate(ps)]
        on = [_rms(eg_h[i] * qs[i] + au[i], gout_ref[...]) for i in range(len(ps))]
        for i, (s, h) in enumerate(ps):
            o_ref[s, rows, h * dv:(h + 1) * dv] = on[i]
        return carry

    lax.fori_loop(0, nchunk, body, 0)

    @pl.when(t == pl.num_programs(1) - 1)
    def _():
        s1_ref[...] = s_sc[...]


def _gdn_scan(act, w, uv, kd, attn, eg, gout, s0, *, tc, heads, dk, dv):
    b, t, _ = act.shape
    chunk = min(CHUNK, t)
    nchunk = tc // chunk
    kern = functools.partial(_gdn_scan_kernel, heads=heads, dk=dk, dv=dv, chunk=chunk, nchunk=nchunk)
    ns = math.gcd(b, SCAN_SEQS)
    tok = lambda last: pl.BlockSpec((ns, tc, last), lambda i, j: (i, j, 0))
    state = pl.BlockSpec((ns, heads, dk, dv), lambda i, j: (i, 0, 0, 0))
    return pl.pallas_call(
        kern,
        out_shape=[jax.ShapeDtypeStruct((b, t, heads * dv), F32), jax.ShapeDtypeStruct(s0.shape, F32)],
        grid=(b // ns, t // tc),
        in_specs=[tok(heads * dk),
                  tok(heads * dk), tok(heads * dv), tok(heads * dk), tok(heads * chunk), tok(LANES),
                  pl.BlockSpec(gout.shape, lambda i, j: (0, 0)), state],
        out_specs=[tok(heads * dv), state],
        scratch_shapes=[pltpu.VMEM((ns, heads, dk, dv), F32)],
        compiler_params=_params("arbitrary", "arbitrary"),
        name="gdn_scan",
    )(act, w, uv, kd, attn, eg, gout, s0)


def _final_kernel(o_ref, z_ref, x_ref, gate_ref, wo_ref, g_ref, y_ref):
    mixed = (o_ref[0] * _silu(z_ref[0])).astype(BF16)
    x2 = x_ref[0] + _per_row(gate_ref[0], x_ref.shape[1]) * _dot(mixed, wo_ref[...])
    y_ref[0] = _rms(x2, g_ref[...])


def _final(o, z, x, gate, wo, g, *, tm):
    b, t, d = x.shape
    tok = lambda last: pl.BlockSpec((1, tm, last), lambda i, j: (i, j, 0))
    const = lambda a: pl.BlockSpec(a.shape, lambda i, j: (0,) * a.ndim)
    return pl.pallas_call(
        _final_kernel,
        out_shape=jax.ShapeDtypeStruct((b, t, d), F32),
        grid=(b, t // tm),
        in_specs=[tok(o.shape[-1]), tok(d), tok(d), pl.BlockSpec((1,) + gate.shape[1:], lambda i, j: (i, 0, 0)),
                  const(wo), const(g)],
        out_specs=tok(d),
        compiler_params=_params("arbitrary", "arbitrary"),
        name="out_c_final",
    )(o, z, x, gate, wo, g)


def _pad_lanes(w, width=LANES, at=0):
    out = jnp.zeros(w.shape[:-1] + (width,), w.dtype)
    return out.at[..., at:at + w.shape[-1]].set(w)


def _rot_half_cols(w):
    r = w.shape[-1] // 2
    return jnp.concatenate([-w[..., r:], w[..., :r]], axis=-1)


def _rope_tables(pos, rope, scale):
    freqs = jnp.exp(jnp.arange(0, rope, 2, dtype=F32) * (-math.log(ROPE_BASE) / rope))
    ang = pos.astype(F32)[:, None] * freqs[None, :]
    cos = jnp.concatenate([jnp.cos(ang), jnp.cos(ang)], axis=-1)
    sin = jnp.concatenate([jnp.sin(ang), jnp.sin(ang)], axis=-1)
    half = LANES // 2
    cosk = _pad_lanes(cos, at=half)
    sink = _pad_lanes(sin, at=half)
    ones = _pad_lanes(jnp.ones((pos.shape[0], half), F32))
    return jnp.concatenate([(cosk + ones) * scale, sink * scale, cosk, sink], axis=-1)


def _tokens_on_lanes(a, chunk, rows):
    b, t = a.shape[:2]
    return a[..., :rows].reshape(b, t // chunk, chunk, rows).transpose(0, 1, 3, 2)


def kernel(x_prompt, x_sample, c_prompt, c_sample, cache_kv_latent, cache_k_rope, state_mlstm_C, state_mlstm_n, state_mlstm_m, state_gdn_S, state_gdn_conv, a_w_ada, a_b_ada, a_g_norm, a_w_in, a_g_q_a, a_w_q_b, a_g_kv_a, a_w_kv_b, a_b_i, a_b_f, a_g_out, a_w_out, c_w_ada, c_b_ada, c_g_norm, c_w_in, c_w_conv, c_a_log, c_dt_bias, c_g_out, c_w_out, g_final):
    d = x_prompt.shape[-1]
    q_lora, heads, qk = a_w_q_b.shape
    kv_lora = a_w_kv_b.shape[0]
    rope = cache_k_rope.shape[-1]
    nope = qk - rope
    v_head = a_w_kv_b.shape[2] - nope
    m_heads, m_dv = a_g_out.shape
    m_dk = state_mlstm_C.shape[2]
    g_heads = c_a_log.shape[0]
    g_dk, g_dv = state_gdn_S.shape[2:]
    width = c_w_conv.shape[0]
    conv_ch = c_w_conv.shape[1]
    assert nope + rope <= LANES and nope == LANES // 2 and 2 * m_heads <= SUBLANES and 2 * g_heads <= 2 * SUBLANES

    sizes = (q_lora, kv_lora, rope, m_heads * m_dk, m_heads * m_dk, m_heads * m_dv, m_heads, m_heads,
             m_heads * m_dv, heads * v_head + m_heads * m_dv)
    offs = [0]
    for s in sizes:
        offs.append(offs[-1] + s)
    w_qa, w_c, w_kr, w_mq, w_mk, w_mv, w_mi, w_mf, w_mo, w_z = [a_w_in[:, offs[i]:offs[i + 1]] for i in range(10)]
    half = LANES // 2
    w1 = jnp.concatenate([w_qa, w_c, _pad_lanes(w_kr, at=half), _pad_lanes(_rot_half_cols(w_kr), at=half),
                          w_mq, w_mk, w_mv, w_mo, w_z, _pad_lanes(jnp.concatenate([w_mi, w_mf], axis=1))],
                         axis=1).astype(BF16)
    m_width = 2 * m_heads * m_dk + 2 * m_heads * m_dv
    wq_rope = a_w_q_b[..., nope:]
    wq_main = _pad_lanes(a_w_q_b).reshape(q_lora, heads * LANES)
    wq_rot = _pad_lanes(_rot_half_cols(wq_rope), at=nope).reshape(q_lora, heads * LANES)
    wq = jnp.concatenate([wq_main, wq_rot], axis=1).astype(BF16)
    wkv = jnp.concatenate([_pad_lanes(a_w_kv_b[..., :nope]).reshape(kv_lora, heads * LANES),
                           a_w_kv_b[..., nope:].reshape(kv_lora, heads * v_head)], axis=1).astype(BF16)
    wk_abs = a_w_kv_b[..., :nope].transpose(1, 0, 2).astype(BF16)
    wv_abs = a_w_kv_b[..., nope:].transpose(1, 0, 2).astype(BF16)
    wo_a = a_w_out.astype(BF16)
    csz = (conv_ch, g_heads, g_heads, g_heads * g_dv)
    w_qkv, w_a, w_b, w_zc = [c_w_in[:, sum(csz[:i]):sum(csz[:i + 1])] for i in range(4)]
    w2 = jnp.concatenate([w_qkv, w_zc, _pad_lanes(jnp.concatenate([w_a, w_b], axis=1))], axis=1).astype(BF16)
    wo_c = c_w_out.astype(BF16)
    wc8 = jnp.zeros((SUBLANES, conv_ch), F32).at[:width].set(c_w_conv)
    row = lambda a: a.reshape(1, -1).astype(F32)
    bias_c = _pad_lanes(jnp.concatenate([a_b_i, a_b_f]).reshape(1, -1))
    bias_r = jnp.zeros((SUBLANES, 1), F32).at[:2 * m_heads, 0].set(jnp.concatenate([a_b_i, a_b_f]))
    alog = _pad_lanes(c_a_log.reshape(1, -1))
    dtb = _pad_lanes(c_dt_bias.reshape(1, -1))

    bp, bs = c_prompt.shape[0], c_sample.shape[0]
    c_all = jnp.concatenate([c_prompt, c_sample], axis=0)
    pad = (-c_all.shape[0]) % SUBLANES
    c_all = jnp.pad(c_all, ((0, pad), (0, 0)))
    mod_a = _adaln(c_all, a_w_ada, a_b_ada)
    mod_c = _adaln(c_all, c_w_ada, c_b_ada)

    def mods(mod, lo, hi):
        return [mod[lo:hi, i * d:(i + 1) * d][:, None, :] for i in range(3)]

    def run(x, mod_lo, mod_hi, c_past, kr_past, c0, n0, m0, conv0, s0):
        b, t, _ = x.shape
        past = 0 if c_past is None else c_past.shape[1]
        chunk = min(CHUNK, t)
        tc = min(t, 512)
        expand = c_past is None
        nb = 1 if expand else math.gcd(b, max(1, PROJ_TM // t))
        tm = min(nb * t, PROJ_TM)
        grp = lambda a: a.reshape((b // nb, nb * a.shape[1]) + a.shape[2:])
        ungrp = lambda a: a.reshape((b, a.shape[1] // nb) + a.shape[2:])
        shift_a, scale_a, gate_a = [grp(m) for m in mods(mod_a, mod_lo, mod_hi)]
        shift_c, scale_c, gate_c = [grp(m) for m in mods(mod_c, mod_lo, mod_hi)]
        tab = _rope_tables(past + jnp.arange(t, dtype=jnp.int32), rope, qk ** -0.5 * math.log2(math.e))
        outs = _in_a(grp(x), shift_a, scale_a, row(a_g_norm), w1, row(a_g_q_a), wq, row(a_g_kv_a), wkv,
                     jnp.tile(tab, (nb, 1)),
                     tm=tm, heads=heads, q_lora=q_lora, kv_lora=kv_lora, rope=rope, m_width=m_width,
                     mk_cols=(m_heads * m_dk, 2 * m_heads * m_dk), v_head=v_head, expand_kv=expand)
        q, c_new, kr_new, m_slab, z, gates = [ungrp(a) for a in outs[:6]]
        kt = None
        if expand:
            k_all, vt, kt = outs[6], outs[7], outs[10]
            qn = outs[8][..., :heads].transpose(0, 2, 1)
            tiles = dict(heads=heads, v_head=v_head, tq=min(t, FLASH_TQ), tk=min(t, FLASH_TK))
            o_fast, row_sums = _flash(q, k_all, vt, bound=(qn, outs[9]), **tiles)
            o_mla = lax.cond(jnp.min(row_sums) >= FLASH_ROW_SUM_MIN,
                             lambda: o_fast, lambda: _flash(q, k_all, vt, **tiles))
        else:
            o_mla = _latent_attn(q, c_past, kr_past, c_new, kr_new, wk_abs, wv_abs,
                                 heads=heads, nope=nope, rope=rope, v_head=v_head)
        fold = t == chunk and b > 1
        flat = (lambda a: a.reshape((1, b * a.shape[1]) + a.shape[2:])) if fold else (lambda a: a)
        unflat = (lambda a: a.reshape((b, a.shape[1] // b) + a.shape[2:])) if fold else (lambda a: a)
        m_flat = flat(m_slab)
        if kt is None:
            kt = m_flat[..., m_heads * m_dk:2 * m_heads * m_dk].transpose(0, 2, 1)
        pv, kvs, bcum, stats = [unflat(a) for a in _mlstm_prep(
            m_flat, kt, flat(gates), flat(_tokens_on_lanes(gates, chunk, SUBLANES)), bias_c, bias_r,
            chunk=chunk, heads=m_heads, dk=m_dk, dv=m_dv)]
        hm, c1, n1, m1 = _mlstm_scan(m_slab, pv, kvs, bcum, stats, row(a_g_out), c0, n0, m0.reshape(b, m_heads, 1, 1),
                                     tc=tc, heads=m_heads, dk=m_dk, dv=m_dv)
        x1, qkv, zc, ab = [ungrp(a) for a in _mid(grp(o_mla), grp(hm), grp(z), grp(x), gate_a, wo_a, shift_c, scale_c,
                                                  row(c_g_norm), w2, tm=tm, conv_ch=conv_ch)]
        past8 = jnp.pad(conv0, ((0, 0), (SUBLANES - (width - 1), 0), (0, 0)))
        act, gb = _conv(qkv, past8, wc8, ab, alog, dtb, tm=min(t, PROJ_TM), width=width, heads=g_heads, dk=g_dk)
        w, uv, kd, attn, eg = [unflat(a) for a in _gdn_prep(
            flat(act), flat(gb), flat(_tokens_on_lanes(gb, chunk, 2 * SUBLANES)),
            chunk=chunk, tc=GDN_PREP_TC, heads=g_heads, dk=g_dk, dv=g_dv)]
        o_gdn, s1 = _gdn_scan(act, w, uv, kd, attn, eg, row(c_g_out), s0, tc=min(t, GDN_SCAN_TC),
                              heads=g_heads, dk=g_dk, dv=g_dv)
        y = ungrp(_final(grp(o_gdn), grp(zc), grp(x1), gate_c, wo_c, row(g_final), tm=tm))
        conv1 = jnp.concatenate([conv0, qkv], axis=1)[:, t:] if t < width - 1 else qkv[:, t - (width - 1):]
        return (y, c_new, kr_new, c1, n1, m1.reshape(b, m_heads), conv1, s1)

    dt = x_prompt.dtype
    (y_p, p_kv, p_kr, p_c, p_n, p_m, p_conv, p_s) = run(
        x_prompt, 0, bp, None, None,
        jnp.zeros((bp, m_heads, m_dk, m_dv), dt), jnp.zeros((bp, m_heads, m_dk), dt), jnp.zeros((bp, m_heads), dt),
        jnp.zeros((bp, width - 1, conv_ch), dt), jnp.zeros((bp, g_heads, g_dk, g_dv), dt))
    (y_s, s_kv, s_kr, s_c, s_n, s_m, s_conv, s_s) = run(
        x_sample, bp, bp + bs, cache_kv_latent, cache_k_rope, state_mlstm_C, state_mlstm_n, state_mlstm_m,
        state_gdn_conv, state_gdn_S)
    return (y_p, y_s, p_kv, p_kr, p_c, p_n, p_m, p_s, p_conv,
            s_kv, s_kr, s_c, s_n, s_m, s_s, s_conv)
```

```python
import functools
import math

import jax
import jax.numpy as jnp
from jax import lax
from jax.experimental import pallas as pl
from jax.experimental.pallas import tpu as pltpu

F32 = jnp.float32
BF16 = jnp.bfloat16
HIGHEST = lax.Precision.HIGHEST

CHUNK = 64
EPS = 1e-6
ROPE_BASE = 10000.0
LANES = 128
SUBLANES = 8
VMEM_LIMIT = 56 * 1024 * 1024
NEG_INF = float("-inf")
SCAN_SEQS = 4
GDN_SCAN_TC = 256
PROJ_TM = 512
FLASH_TQ = 1024
FLASH_TK = 1024
LATENT_ROW_GROUPS = 2
FLASH_HEAD_GROUP = 4
FLASH_BOUND_SLACK = 1.0 + 2.0 ** -6
FLASH_ROW_SUM_MIN = 2.0 ** -100
MLSTM_PREP_TC = 512
GDN_PREP_TC = 512


def _params(*sem):
    return pltpu.CompilerParams(dimension_semantics=sem, vmem_limit_bytes=VMEM_LIMIT)


def _dot(a, b, precision=None):
    return jnp.dot(a, b, preferred_element_type=F32, precision=precision)


def _dot_nt(a, b):
    return lax.dot_general(a, b, (((1,), (1,)), ((), ())), preferred_element_type=F32)


def _dot_tn(a, b):
    return lax.dot_general(a, b, (((0,), (0,)), ((), ())), preferred_element_type=F32)


def _rms(x, g):
    return x * lax.rsqrt(jnp.mean(x * x, axis=-1, keepdims=True) + EPS) * g


def _per_row(m, rows):
    n, d = m.shape
    if n == 1:
        return m
    return jnp.concatenate([jnp.broadcast_to(m[i:i + 1], (rows // n, d)) for i in range(n)], axis=0)


def _sigmoid(x):
    return 0.5 * jnp.tanh(0.5 * x) + 0.5


def _silu(x):
    return x * _sigmoid(x)


def _softplus(x):
    return jnp.maximum(x, 0.0) + jnp.log1p(jnp.exp(-jnp.abs(x)))


def _log_sigmoid(x):
    return -_softplus(-x)


def _adaln_kernel(c_ref, w_ref, b_ref, o_ref):
    o_ref[...] = _dot(_silu(c_ref[...]), w_ref[...], HIGHEST) + b_ref[...]


def _adaln(c, w, b):
    n, d = c.shape
    d3 = w.shape[1]
    return pl.pallas_call(
        _adaln_kernel,
        out_shape=jax.ShapeDtypeStruct((n, d3), F32),
        grid=(d3 // d,),
        in_specs=[pl.BlockSpec((n, d), lambda j: (0, 0)),
                  pl.BlockSpec((d, d), lambda j: (0, j)),
                  pl.BlockSpec((1, d), lambda j: (0, j))],
        out_specs=pl.BlockSpec((n, d), lambda j: (0, j)),
        compiler_params=_params("arbitrary"),
        name="adaln",
    )(c, w, b.reshape(1, d3))


def _in_a_kernel(x_ref, shift_ref, scale_ref, g_ref, w1_ref, gq_ref, wq_ref, gkv_ref, wkv_ref, tab_ref,
                 q_ref, c_ref, kr_ref, m_ref, z_ref, gt_ref, *kv_refs,
                 heads, q_lora, kv_lora, rope, m_width, mk_cols, d_model, v_head, expand_kv):
    x = x_ref[0]
    hn = _rms(x, g_ref[...]) * (1.0 + _per_row(scale_ref[0], x_ref.shape[1])) + _per_row(shift_ref[0], x_ref.shape[1])
    y = _dot(hn.astype(BF16), w1_ref[...])
    o = 0
    qa = y[:, o:o + q_lora]; o += q_lora
    cl = y[:, o:o + kv_lora]; o += kv_lora
    kr1 = y[:, o:o + LANES]; o += LANES
    kr2 = y[:, o:o + LANES]; o += LANES
    m_ref[0] = y[:, o:o + m_width]
    mk = y[:, o + mk_cols[0]:o + mk_cols[1]]
    o += m_width
    z_ref[0] = y[:, o:o + d_model]; o += d_model
    gt_ref[0] = y[:, o:o + LANES]

    tab = tab_ref[...]
    cosq, sinq = tab[:, 0:LANES], tab[:, LANES:2 * LANES]
    cosk, sink = tab[:, 2 * LANES:3 * LANES], tab[:, 3 * LANES:4 * LANES]

    def sq_norm(xb):
        xf = xb.astype(F32)
        return jnp.sum(xf * xf, axis=-1, keepdims=True)

    lane = lax.broadcasted_iota(jnp.int32, (x.shape[0], LANES), 1)
    qq = _dot(_rms(qa, gq_ref[...]).astype(BF16), wq_ref[...])
    hw = heads * LANES
    qn2 = jnp.zeros((x.shape[0], LANES), F32)
    for h in range(heads):
        sl = slice(h * LANES, (h + 1) * LANES)
        qb = (qq[:, sl] * cosq + qq[:, hw + h * LANES:hw + (h + 1) * LANES] * sinq).astype(BF16)
        q_ref[0, :, sl] = qb
        if expand_kv:
            qn2 = jnp.where(lane == h, sq_norm(qb), qn2)

    cn = _rms(cl, gkv_ref[...])
    c_ref[0] = cn
    kr = kr1 * cosk + kr2 * sink
    kr_ref[0] = kr[:, LANES // 2:LANES // 2 + rope]
    if expand_kv:
        k_ref, vt_ref, qn_ref, kmax_ref, kt_ref = kv_refs
        kv = _dot(cn.astype(BF16), wkv_ref[...])
        kn2 = jnp.zeros((x.shape[0], LANES), F32)
        for h in range(heads):
            sl = slice(h * LANES, (h + 1) * LANES)
            kb = (kv[:, sl] + kr).astype(BF16)
            k_ref[0, :, sl] = kb
            kn2 = jnp.where(lane == h, sq_norm(kb), kn2)
        vt_ref[0] = kv[:, hw:hw + heads * v_head].T.astype(BF16)
        kt_ref[0] = mk.T
        qn_ref[0] = jnp.sqrt(qn2)
        kmax_ref[0, 0] = jnp.sqrt(jnp.max(kn2, axis=0, keepdims=True))


def _in_a(x, shift, scale, g, w1, gq, wq, gkv, wkv, tab, *, tm, heads, q_lora, kv_lora, rope, m_width,
          mk_cols, v_head, expand_kv):
    b, t, d = x.shape
    grid = (b, t // tm)
    tok = lambda last: pl.BlockSpec((1, tm, last), lambda i, j: (i, j, 0))
    tok_t = lambda rows: pl.BlockSpec((1, rows, tm), lambda i, j: (i, 0, j))
    const = lambda a: pl.BlockSpec(a.shape, lambda i, j: (0,) * a.ndim)
    out_shape = [jax.ShapeDtypeStruct((b, t, heads * LANES), BF16),
                 jax.ShapeDtypeStruct((b, t, kv_lora), F32),
                 jax.ShapeDtypeStruct((b, t, rope), F32),
                 jax.ShapeDtypeStruct((b, t, m_width), F32),
                 jax.ShapeDtypeStruct((b, t, d), F32),
                 jax.ShapeDtypeStruct((b, t, LANES), F32)]
    out_specs = [tok(heads * LANES), tok(kv_lora), tok(rope), tok(m_width), tok(d), tok(LANES)]
    if expand_kv:
        out_shape += [jax.ShapeDtypeStruct((b, t, heads * LANES), BF16),
                      jax.ShapeDtypeStruct((b, heads * v_head, t), BF16),
                      jax.ShapeDtypeStruct((b, t, LANES), F32),
                      jax.ShapeDtypeStruct((b, t // tm, 1, LANES), F32),
                      jax.ShapeDtypeStruct((b, mk_cols[1] - mk_cols[0], t), F32)]
        out_specs += [tok(heads * LANES), tok_t(heads * v_head), tok(LANES),
                      pl.BlockSpec((1, 1, 1, LANES), lambda i, j: (i, j, 0, 0)), tok_t(mk_cols[1] - mk_cols[0])]
    kern = functools.partial(_in_a_kernel, heads=heads, q_lora=q_lora, kv_lora=kv_lora, rope=rope,
                             m_width=m_width, mk_cols=mk_cols, d_model=d, v_head=v_head, expand_kv=expand_kv)
    return pl.pallas_call(
        kern, out_shape=out_shape, grid=grid,
        in_specs=[tok(d),
                  pl.BlockSpec((1,) + shift.shape[1:], lambda i, j: (i, 0, 0)),
                  pl.BlockSpec((1,) + scale.shape[1:], lambda i, j: (i, 0, 0)),
                  const(g), const(w1), const(gq), const(wq), const(gkv), const(wkv),
                  pl.BlockSpec((tm, 4 * LANES), lambda i, j: (j, 0))],
        out_specs=out_specs,
        compiler_params=_params("arbitrary", "arbitrary"),
        name="in_proj_a",
    )(x, shift, scale, g, w1, gq, wq, gkv, wkv, tab)


def _flash_kernel(qi_ref, ki_ref, q_ref, k_ref, vt_ref, o_ref, m_sc, l_sc, acc_sc, *,
                  heads, v_head, tq, tk, chunk):
    step_id = pl.program_id(1)
    qi = qi_ref[step_id]
    ki = ki_ref[step_id]

    @pl.when(ki == 0)
    def _():
        m_sc[...] = jnp.full(m_sc.shape, NEG_INF, F32)
        l_sc[...] = jnp.zeros(l_sc.shape, F32)
        acc_sc[...] = jnp.zeros(acc_sc.shape, F32)

    def step(masked):
        if masked:
            kc = (ki * tk + lax.broadcasted_iota(jnp.int32, (tk, tq), 0)) // chunk
            qc = (qi * tq + lax.broadcasted_iota(jnp.int32, (tk, tq), 1)) // chunk
            mask = kc <= qc
        for h in range(heads):
            qh = q_ref[0, :, h * LANES:(h + 1) * LANES]
            kh = k_ref[0, :, h * LANES:(h + 1) * LANES]
            vth = vt_ref[0, h * v_head:(h + 1) * v_head, :]
            rows = slice(h * v_head, (h + 1) * v_head)
            st = _dot_nt(kh, qh)
            if masked:
                st = jnp.where(mask, st, NEG_INF)
            m_prev = m_sc[h]
            m_new = jnp.maximum(m_prev, jnp.max(st, axis=0, keepdims=True))
            alpha = jnp.exp2(m_prev - m_new)
            p = jnp.exp2(st - m_new)
            l_sc[h] = alpha * l_sc[h] + jnp.sum(p, axis=0, keepdims=True)
            acc_sc[rows, :] = alpha * acc_sc[rows, :] + _dot(vth, p.astype(BF16))
            m_sc[h] = m_new

    full = (ki + 1) * tk <= qi * tq + chunk

    @pl.when(full)
    def _():
        step(False)

    @pl.when(jnp.logical_not(full))
    def _():
        step(True)

    @pl.when(ki == ((qi + 1) * tq - 1) // tk)
    def _():
        for h in range(heads):
            rows = slice(h * v_head, (h + 1) * v_head)
            acc_sc[rows, :] = acc_sc[rows, :] / l_sc[h]
        o_ref[0] = acc_sc[...].T


def _flash_bound_kernel(qi_ref, ki_ref, q_ref, k_ref, vt_ref, qn_ref, kmax_ref, o_ref, l_ref,
                        mb_sc, l_sc, acc_sc, *, heads, v_head, tq, tk, chunk):
    step_id = pl.program_id(1)
    qi = qi_ref[step_id]
    ki = ki_ref[step_id]

    @pl.when(ki == 0)
    def _():
        kmax = jnp.max(kmax_ref[0], axis=0) * FLASH_BOUND_SLACK
        for h in range(heads):
            mb_sc[h:h + 1, :] = qn_ref[0, h:h + 1, :] * kmax[:, h:h + 1]
        l_sc[...] = jnp.zeros(l_sc.shape, F32)
        acc_sc[...] = jnp.zeros(acc_sc.shape, F32)

    def step(masked):
        if masked:
            kc = (ki * tk + lax.broadcasted_iota(jnp.int32, (tk, tq), 0)) // chunk
            qc = (qi * tq + lax.broadcasted_iota(jnp.int32, (tk, tq), 1)) // chunk
            mask = kc <= qc
        for h0 in range(0, heads, FLASH_HEAD_GROUP):
            hs = range(h0, h0 + FLASH_HEAD_GROUP)
            st = [_dot_nt(k_ref[0, :, h * LANES:(h + 1) * LANES], q_ref[0, :, h * LANES:(h + 1) * LANES])
                  for h in hs]
            if masked:
                st = [jnp.where(mask, s, NEG_INF) for s in st]
            p = [jnp.exp2(s - mb_sc[h:h + 1, :]) for s, h in zip(st, hs)]
            for x, h in zip(p, hs):
                l_sc[h:h + 1, :] += jnp.sum(x, axis=0, keepdims=True)
            pv = [_dot(vt_ref[0, h * v_head:(h + 1) * v_head, :], x.astype(BF16)) for x, h in zip(p, hs)]
            for x, h in zip(pv, hs):
                acc_sc[h * v_head:(h + 1) * v_head, :] += x

    full = (ki + 1) * tk <= qi * tq + chunk

    @pl.when(full)
    def _():
        step(False)

    @pl.when(jnp.logical_not(full))
    def _():
        step(True)

    @pl.when(ki == ((qi + 1) * tq - 1) // tk)
    def _():
        for h in range(heads):
            rows = slice(h * v_head, (h + 1) * v_head)
            acc_sc[rows, :] = acc_sc[rows, :] / l_sc[h:h + 1, :]
        o_ref[0] = acc_sc[...].T
        l_ref[0] = l_sc[...]


def _flash(q, k, vt, *, heads, v_head, tq, tk, bound=None):
    b, t, _ = q.shape
    pairs = [(i, j) for i in range(t // tq) for j in range(((i + 1) * tq - 1) // tk + 1)]
    qi_tab = jnp.asarray([p[0] for p in pairs], jnp.int32)
    ki_tab = jnp.asarray([p[1] for p in pairs], jnp.int32)
    in_specs = [pl.BlockSpec((1, tq, heads * LANES), lambda i, s, qt, kt: (i, qt[s], 0)),
                pl.BlockSpec((1, tk, heads * LANES), lambda i, s, qt, kt: (i, kt[s], 0)),
                pl.BlockSpec((1, heads * v_head, tk), lambda i, s, qt, kt: (i, 0, kt[s]))]
    o_shape = jax.ShapeDtypeStruct((b, t, heads * v_head), F32)
    o_spec = pl.BlockSpec((1, tq, heads * v_head), lambda i, s, qt, kt: (i, qt[s], 0))
    acc = pltpu.VMEM((heads * v_head, tq), F32)
    if bound is None:
        kern, name, args = _flash_kernel, "flash_attn", (q, k, vt)
        out_shape, out_specs = o_shape, o_spec
        scratch = [pltpu.VMEM((heads, 1, tq), F32), pltpu.VMEM((heads, 1, tq), F32), acc]
    else:
        qn, kmax = bound
        assert qn.shape == (b, heads, t)
        kern, name, args = _flash_bound_kernel, "flash_attn_bound", (q, k, vt, qn, kmax)
        in_specs += [pl.BlockSpec((1, heads, tq), lambda i, s, qt, kt: (i, 0, qt[s])),
                     pl.BlockSpec((1,) + kmax.shape[1:], lambda i, s, qt, kt: (i, 0, 0, 0))]
        out_shape = [o_shape, jax.ShapeDtypeStruct((b, heads, t), F32)]
        out_specs = [o_spec, pl.BlockSpec((1, heads, tq), lambda i, s, qt, kt: (i, 0, qt[s]))]
        scratch = [pltpu.VMEM((heads, tq), F32), pltpu.VMEM((heads, tq), F32), acc]
    grid_spec = pltpu.PrefetchScalarGridSpec(
        num_scalar_prefetch=2, grid=(b, len(pairs)), in_specs=in_specs, out_specs=out_specs,
        scratch_shapes=scratch)
    return pl.pallas_call(
        functools.partial(kern, heads=heads, v_head=v_head, tq=tq, tk=tk, chunk=CHUNK),
        out_shape=out_shape,
        grid_spec=grid_spec,
        compiler_params=_params("arbitrary", "arbitrary"),
        name=name,
    )(qi_tab, ki_tab, *args)


def _latent_attn_kernel(q_ref, cp_ref, krp_ref, cn_ref, krn_ref, wk_ref, wv_ref, o_ref, *,
                        heads, nope, rope, v_head):
    q = q_ref[0]
    qabs, qrope = [], []
    for h in range(heads):
        qabs.append(_dot_nt(q[:, h * LANES:h * LANES + nope], wk_ref[h]))
        qrope.append(q[:, h * LANES + nope:h * LANES + nope + rope])
    qabs = jnp.concatenate(qabs, axis=0).astype(BF16)
    qrope = jnp.concatenate(qrope, axis=0)
    cp = cp_ref[0].astype(BF16)
    cn = cn_ref[0].astype(BF16)
    krp = krp_ref[0].astype(BF16)
    krn = krn_ref[0].astype(BF16)
    t = q.shape[0]
    rg = (heads // LATENT_ROW_GROUPS) * t
    grp = [slice(g * rg, (g + 1) * rg) for g in range(LATENT_ROW_GROUPS)]
    s_p = [_dot_nt(qabs[r], cp) + _dot_nt(qrope[r], krp) for r in grp]
    s_n = [_dot_nt(qabs[r], cn) + _dot_nt(qrope[r], krn) for r in grp]
    m = [jnp.maximum(jnp.max(a, axis=-1, keepdims=True), jnp.max(b, axis=-1, keepdims=True)) for a, b in zip(s_p, s_n)]
    p_p = [jnp.exp2(a - c) for a, c in zip(s_p, m)]
    p_n = [jnp.exp2(a - c) for a, c in zip(s_n, m)]
    l = [jnp.sum(a, axis=-1, keepdims=True) + jnp.sum(b, axis=-1, keepdims=True) for a, b in zip(p_p, p_n)]
    o_lat = jnp.concatenate([(_dot(a.astype(BF16), cp) + _dot(b.astype(BF16), cn)) / c
                             for a, b, c in zip(p_p, p_n, l)], axis=0)
    for h in range(heads):
        o_ref[0, :, h * v_head:(h + 1) * v_head] = _dot(o_lat[h * t:(h + 1) * t].astype(BF16), wv_ref[h])


def _latent_attn(q, c_past, kr_past, c_new, kr_new, wk, wv, *, heads, nope, rope, v_head):
    b, t, _ = q.shape
    past, kv_lora = c_past.shape[1:]
    blk = lambda n, last: pl.BlockSpec((1, n, last), lambda i: (i, 0, 0))
    const = lambda a: pl.BlockSpec(a.shape, lambda i: (0,) * a.ndim)
    kern = functools.partial(_latent_attn_kernel, heads=heads, nope=nope, rope=rope, v_head=v_head)
    return pl.pallas_call(
        kern,
        out_shape=jax.ShapeDtypeStruct((b, t, heads * v_head), F32),
        grid=(b,),
        in_specs=[blk(t, heads * LANES), blk(past, kv_lora), blk(past, rope), blk(t, kv_lora), blk(t, rope),
                  const(wk), const(wv)],
        out_specs=blk(t, heads * v_head),
        compiler_params=_params("arbitrary"),
        name="latent_attn",
    )(q, c_past, kr_past, c_new, kr_new, wk, wv)


def _mlstm_prep_kernel(m_ref, kt_ref, gc_ref, gr_ref, bc_ref, br_ref, pv_ref, kv_ref, b_ref, st_ref, *,
                       heads, dk, dv, chunk, nchunk):
    row = lax.broadcasted_iota(jnp.int32, (chunk, chunk), 0)
    col = lax.broadcasted_iota(jnp.int32, (chunk, chunk), 1)
    causal = col <= row
    tril = causal.astype(F32)
    triu = (row <= col).astype(F32)
    lane = lax.broadcasted_iota(jnp.int32, (chunk, LANES), 1)
    o_k, o_v = heads * dk, 2 * heads * dk

    rows = [slice(c * chunk, (c + 1) * chunk) for c in range(nchunk)]
    gc = [gc_ref[0, r, :] + bc_ref[...] for r in rows]
    gr = [gr_ref[0, c] + br_ref[...] for c in range(nchunk)]
    bcum_c = [_dot(tril, _log_sigmoid(g), HIGHEST) for g in gc]
    bcum_r = [_dot(_log_sigmoid(g), triu, HIGHEST) for g in gr]
    for c in range(nchunk):
        b_ref[0, rows[c], :] = bcum_c[c]
    ch = [(c, h) for c in range(nchunk) for h in range(heads)]
    v = [m_ref[0, rows[c], o_v + h * dv:o_v + (h + 1) * dv].astype(BF16) for c, h in ch]
    qk = [_dot_nt(m_ref[0, rows[c], h * dk:(h + 1) * dk].astype(BF16),
                  (m_ref[0, rows[c], o_k + h * dk:o_k + (h + 1) * dk] * (dk ** -0.5)).astype(BF16)) for c, h in ch]
    li_r = [gr[c][h:h + 1, :] for c, h in ch]
    b_r = [bcum_r[c][heads + h:heads + h + 1, :] for c, h in ch]
    dmat = [jnp.where(causal, bcum_c[c][:, heads + h:heads + h + 1] - b_r[i] + li_r[i], NEG_INF)
            for i, (c, h) in enumerate(ch)]
    mx = [jnp.max(d, axis=-1, keepdims=True) for d in dmat]
    p0 = [s * jnp.exp(d - m) for s, d, m in zip(qk, dmat, mx)]
    for i, (c, h) in enumerate(ch):
        pv_ref[0, rows[c], h * dv:(h + 1) * dv] = _dot(p0[i].astype(BF16), v[i])
    w_r = [jnp.exp(b_r[i][:, chunk - 1:chunk] - b_r[i] + li_r[i] - mx[i][chunk - 1:chunk, :]) * (dk ** -0.5)
           for i in range(len(ch))]
    for i, (c, h) in enumerate(ch):
        wkt = (kt_ref[0, h * dk:(h + 1) * dk, c * chunk:(c + 1) * chunk] * w_r[i]).astype(BF16)
        kv_ref[0, c, h] = _dot(wkt, jnp.concatenate([v[i], jnp.ones_like(v[i])], axis=1))
    psum = [jnp.sum(p, axis=-1, keepdims=True) for p in p0]
    for c in range(nchunk):
        stats = jnp.zeros((chunk, LANES), F32)
        for h in range(heads):
            stats = jnp.where(lane == h, mx[c * heads + h], stats)
            stats = jnp.where(lane == heads + h, psum[c * heads + h], stats)
        st_ref[0, rows[c], :] = stats


def _mlstm_scan_kernel(q_ref, mo_ref, pv_ref, kv_ref, b_ref, st_ref, gout_ref, c0_ref, n0_ref, m0_ref,
                       h_ref, c1_ref, n1_ref, m1_ref, c_sc, m_sc, *, heads, dk, dv, chunk, nchunk):
    t = pl.program_id(1)
    nseq = q_ref.shape[0]

    @pl.when(t == 0)
    def _():
        c_sc[:, :, :, :dv] = c0_ref[...]
        c_sc[:, :, :, dv:] = jnp.broadcast_to(n0_ref[...], c0_ref.shape)
        m_sc[...] = m0_ref[...]

    def body(c, carry):
        rows = pl.ds(pl.multiple_of(c * chunk, chunk), chunk)
        rep = lambda col: jnp.broadcast_to(col, (chunk, dv))
        ps = [(s, h) for s in range(nseq) for h in range(heads)]
        bcum = [b_ref[s, rows, :] for s in range(nseq)]
        stats = [st_ref[s, rows, :] for s in range(nseq)]
        c2 = [c_sc[s, h] for s, h in ps]
        qc = [_dot(q_ref[s, rows, h * dk:(h + 1) * dk].astype(BF16), c2[i].astype(BF16))
              for i, (s, h) in enumerate(ps)]
        mx = [rep(stats[s][:, h:h + 1]) for s, h in ps]
        inter = [rep(bcum[s][:, heads + h:heads + h + 1]) + m_sc[s, h] for s, h in ps]
        m = [jnp.maximum(a, b) for a, b in zip(inter, mx)]
        w_inter = [jnp.exp(a - b) for a, b in zip(inter, m)]
        r = [jnp.exp(a - b) for a, b in zip(mx, m)]
        for i, (s, h) in enumerate(ps):
            decay_end = w_inter[i][chunk - 1:chunk, :]
            f_new = r[i][chunk - 1:chunk, :]
            c_sc[s, h] = (jnp.concatenate([decay_end, decay_end], axis=1) * c2[i]
                          + jnp.concatenate([f_new, f_new], axis=1) * kv_ref[s, c, h])
            m_sc[s, h] = m[i][chunk - 1:chunk, 0:1]
        num = [w_inter[i] * qc[i][:, :dv] + r[i] * pv_ref[s, rows, h * dv:(h + 1) * dv] for i, (s, h) in enumerate(ps)]
        den = [w_inter[i] * qc[i][:, dv:] + r[i] * rep(stats[s][:, heads + h:heads + h + 1])
               for i, (s, h) in enumerate(ps)]
        hh = [a / jnp.maximum(jnp.abs(b), jnp.exp(-c_)) for a, b, c_ in zip(num, den, m)]
        hn = [_rms(hh[i], gout_ref[:, h * dv:(h + 1) * dv]) for i, (s, h) in enumerate(ps)]
        for i, (s, h) in enumerate(ps):
            h_ref[s, rows, h * dv:(h + 1) * dv] = hn[i] * _sigmoid(mo_ref[s, rows, h * dv:(h + 1) * dv])
        return carry

    lax.fori_loop(0, nchunk, body, 0)

    @pl.when(t == pl.num_programs(1) - 1)
    def _():
        c1_ref[...] = c_sc[:, :, :, :dv]
        n1_ref[...] = c_sc[:, :, :, dv:dv + 1]
        m1_ref[...] = m_sc[...]


def _mlstm_prep(m_slab, kt, gates_c, gates_r, bias_c, bias_r, *, chunk, heads, dk, dv):
    b, t, mw = m_slab.shape
    const = lambda a: pl.BlockSpec(a.shape, lambda i, j: (0,) * a.ndim)
    qkv_w = 2 * heads * dk + heads * dv
    tp = min(t, MLSTM_PREP_TC)
    npc = tp // chunk
    tokp = lambda last: pl.BlockSpec((1, tp, last), lambda i, j: (i, j, 0))
    per_chunk = lambda *s: pl.BlockSpec((1, npc, heads) + s, lambda i, j: (i, j, 0) + (0,) * len(s))
    return pl.pallas_call(
        functools.partial(_mlstm_prep_kernel, heads=heads, dk=dk, dv=dv, chunk=chunk, nchunk=npc),
        out_shape=[jax.ShapeDtypeStruct((b, t, heads * dv), F32),
                   jax.ShapeDtypeStruct((b, t // chunk, heads, dk, 2 * dv), F32),
                   jax.ShapeDtypeStruct((b, t, LANES), F32),
                   jax.ShapeDtypeStruct((b, t, LANES), F32)],
        grid=(b, t // tp),
        in_specs=[tokp(qkv_w), pl.BlockSpec((1, heads * dk, tp), lambda i, j: (i, 0, j)), tokp(LANES),
                  pl.BlockSpec((1, npc, SUBLANES, chunk), lambda i, j: (i, j, 0, 0)),
                  const(bias_c), const(bias_r)],
        out_specs=[tokp(heads * dv), per_chunk(dk, 2 * dv), tokp(LANES), tokp(LANES)],
        compiler_params=_params("arbitrary", "arbitrary"),
        name="mlstm_prep",
    )(m_slab, kt, gates_c, gates_r, bias_c, bias_r)


def _mlstm_scan(m_slab, pv, kv, bcum, stats, gout, c0, n0, m0, *, tc, heads, dk, dv):
    b, t, mw = m_slab.shape
    chunk = min(CHUNK, t)
    const = lambda a: pl.BlockSpec(a.shape, lambda i, j: (0,) * a.ndim)
    qkv_w = 2 * heads * dk + heads * dv
    mo_blk, rem = divmod(qkv_w, heads * dv)
    assert rem == 0 and mw == qkv_w + heads * dv
    nchunk = tc // chunk
    ns = math.gcd(b, SCAN_SEQS)
    tok = lambda last, blk=0: pl.BlockSpec((ns, tc, last), lambda i, j: (i, j, blk))
    per_chunk = lambda *s: pl.BlockSpec((ns, nchunk, heads) + s, lambda i, j: (i, j, 0) + (0,) * len(s))
    st = lambda *s: pl.BlockSpec((ns,) + s, lambda i, j: (i,) + (0,) * len(s))
    hm, c1, n1, m1 = pl.pallas_call(
        functools.partial(_mlstm_scan_kernel, heads=heads, dk=dk, dv=dv, chunk=chunk, nchunk=nchunk),
        out_shape=[jax.ShapeDtypeStruct((b, t, heads * dv), F32),
                   jax.ShapeDtypeStruct((b, heads, dk, dv), F32),
                   jax.ShapeDtypeStruct((b, heads, dk, 1), F32),
                   jax.ShapeDtypeStruct((b, heads, 1, 1), F32)],
        grid=(b // ns, t // tc),
        in_specs=[tok(heads * dk),
                  tok(heads * dv, mo_blk),
                  tok(heads * dv), per_chunk(dk, 2 * dv), tok(LANES), tok(LANES),
                  const(gout), st(heads, dk, dv), st(heads, dk, 1), st(heads, 1, 1)],
        out_specs=[tok(heads * dv), st(heads, dk, dv), st(heads, dk, 1), st(heads, 1, 1)],
        scratch_shapes=[pltpu.VMEM((ns, heads, dk, 2 * dv), F32), pltpu.VMEM((ns, heads, 1, 1), F32)],
        compiler_params=_params("arbitrary", "arbitrary"),
        name="mlstm_scan",
    )(m_slab, m_slab, pv, kv, bcum, stats, gout, c0, n0[..., None], m0)
    return hm, c1, n1[..., 0], m1


def _mid_kernel(oa_ref, ob_ref, z_ref, x_ref, gate_ref, wo_ref, shift_ref, scale_ref, g_ref, w2_ref,
                x1_ref, qkv_ref, z2_ref, ab_ref, *, half, conv_ch, d_model):
    z = z_ref[0]
    ma = (oa_ref[0] * _silu(z[:, :half])).astype(BF16)
    mb = (ob_ref[0] * _silu(z[:, half:])).astype(BF16)
    y = _dot(ma, wo_ref[0:half, :]) + _dot(mb, wo_ref[half:, :])
    x1 = x_ref[0] + _per_row(gate_ref[0], x_ref.shape[1]) * y
    x1_ref[0] = x1
    hn = _rms(x1, g_ref[...]) * (1.0 + _per_row(scale_ref[0], x_ref.shape[1])) + _per_row(shift_ref[0], x_ref.shape[1])
    y2 = _dot(hn.astype(BF16), w2_ref[...])
    qkv_ref[0] = y2[:, :conv_ch]
    z2_ref[0] = y2[:, conv_ch:conv_ch + d_model]
    ab_ref[0] = y2[:, conv_ch + d_model:]


def _mid(oa, ob, z, x, gate, wo, shift, scale, g, w2, *, tm, conv_ch):
    b, t, d = x.shape
    half = oa.shape[-1]
    tok = lambda last: pl.BlockSpec((1, tm, last), lambda i, j: (i, j, 0))
    vec = pl.BlockSpec((1,) + gate.shape[1:], lambda i, j: (i, 0, 0))
    const = lambda a: pl.BlockSpec(a.shape, lambda i, j: (0,) * a.ndim)
    kern = functools.partial(_mid_kernel, half=half, conv_ch=conv_ch, d_model=d)
    return pl.pallas_call(
        kern,
        out_shape=[jax.ShapeDtypeStruct((b, t, d), F32), jax.ShapeDtypeStruct((b, t, conv_ch), F32),
                   jax.ShapeDtypeStruct((b, t, d), F32), jax.ShapeDtypeStruct((b, t, LANES), F32)],
        grid=(b, t // tm),
        in_specs=[tok(half), tok(ob.shape[-1]), tok(d), tok(d), vec, const(wo), vec, vec, const(g), const(w2)],
        out_specs=[tok(d), tok(conv_ch), tok(d), tok(LANES)],
        compiler_params=_params("arbitrary", "arbitrary"),
        name="out_a_in_c",
    )(oa, ob, z, x, gate, wo, shift, scale, g, w2)


def _conv_kernel(qkv_ref, past_ref, wc_ref, ab_ref, alog_ref, dtb_ref, act_ref, gb_ref, ext_sc, *,
                 tm, width, heads, dk):
    @pl.when(pl.program_id(1) == 0)
    def _():
        ext_sc[0:SUBLANES, :] = past_ref[0]

    ext_sc[SUBLANES:SUBLANES + tm, :] = qkv_ref[0]
    conv = wc_ref[width - 1:width, :] * ext_sc[SUBLANES:SUBLANES + tm, :]
    for j in range(width - 1):
        s = SUBLANES - (width - 1) + j
        conv = conv + wc_ref[j:j + 1, :] * ext_sc[s:s + tm, :]
    ext_sc[0:SUBLANES, :] = ext_sc[tm:tm + SUBLANES, :]
    act = _silu(conv)
    for h in range(2 * heads):
        xh = act[:, h * dk:(h + 1) * dk]
        xh = xh * lax.rsqrt(jnp.sum(xh * xh, axis=-1, keepdims=True) + EPS)
        if h < heads:
            xh = xh * (dk ** -0.5)
        act_ref[0, :, h * dk:(h + 1) * dk] = xh
    act_ref[0, :, 2 * heads * dk:] = act[:, 2 * heads * dk:]
    ab = ab_ref[0]
    g = -jnp.exp(alog_ref[...]) * _softplus(ab + dtb_ref[...])
    lane = lax.broadcasted_iota(jnp.int32, ab.shape, 1)
    gb_ref[0] = jnp.where(lane < heads, g, _sigmoid(ab))


def _conv(qkv, past8, wc8, ab, alog, dtb, *, tm, width, heads, dk):
    b, t, ch = qkv.shape
    tok = lambda last: pl.BlockSpec((1, tm, last), lambda i, j: (i, j, 0))
    const = lambda a: pl.BlockSpec(a.shape, lambda i, j: (0,) * a.ndim)
    kern = functools.partial(_conv_kernel, tm=tm, width=width, heads=heads, dk=dk)
    return pl.pallas_call(
        kern,
        out_shape=[jax.ShapeDtypeStruct((b, t, ch), F32), jax.ShapeDtypeStruct((b, t, LANES), F32)],
        grid=(b, t // tm),
        in_specs=[tok(ch), pl.BlockSpec((1, SUBLANES, ch), lambda i, j: (i, 0, 0)), const(wc8), tok(LANES),
                  const(alog), const(dtb)],
        out_specs=[tok(ch), tok(LANES)],
        scratch_shapes=[pltpu.VMEM((tm + SUBLANES, ch), F32)],
        compiler_params=_params("arbitrary", "arbitrary"),
        name="conv_gates",
    )(qkv, past8, wc8, ab, alog, dtb)


def _blockdiag(x, group, chunk):
    w = group * chunk
    br = lax.broadcasted_iota(jnp.int32, (w, w), 0) // chunk
    bc = lax.broadcasted_iota(jnp.int32, (w, w), 1) // chunk
    xb = x.astype(BF16)
    return jnp.where(br == bc, jnp.concatenate([xb] * group, axis=0), jnp.zeros((), BF16))


def _unit_lower_inverses_minus_eye(a_list, group, chunk):
    w = group * chunk
    r = lax.broadcasted_iota(jnp.int32, (chunk, w), 0)
    cc = lax.broadcasted_iota(jnp.int32, (chunk, w), 1) % chunk
    es = [-jnp.where((r // 2 == cc // 2) & (r % 2 == 1) & (cc % 2 == 0), a4, 0.0) for a4 in a_list]
    s = 2
    while s < chunk:
        off = (r // (2 * s) == cc // (2 * s)) & (r % (2 * s) >= s) & (cc % (2 * s) < s)
        a_offs = [jnp.where(off, a4, 0.0) for a4 in a_list]
        ps = [a + _dot(a.astype(BF16), _blockdiag(e, group, chunk)) for a, e in zip(a_offs, es)]
        es = [e - (p + _dot(e.astype(BF16), _blockdiag(p, group, chunk))) for e, p in zip(es, ps)]
        s *= 2
    return es


def _gdn_prep_kernel(act_ref, gbc_ref, gbr_ref, w_ref, uv_ref, kd_ref, attn_ref, eg_ref, *,
                     heads, dk, dv, chunk, nchunk, group):
    row = lax.broadcasted_iota(jnp.int32, (chunk, chunk), 0)
    col = lax.broadcasted_iota(jnp.int32, (chunk, chunk), 1)
    incl = col <= row
    strict = col < row
    tril = incl.astype(F32)
    triu = (row <= col).astype(F32)
    lane = lax.broadcasted_iota(jnp.int32, (chunk, LANES), 1)
    o_k, o_v = heads * dk, 2 * heads * dk

    rows = [slice(c * chunk, (c + 1) * chunk) for c in range(nchunk)]
    gbc = [gbc_ref[0, r, :] for r in rows]
    gcum_c = [_dot(tril, g, HIGHEST) for g in gbc]
    gcum_r = [_dot(gbr_ref[0, c], triu, HIGHEST) for c in range(nchunk)]
    for c in range(nchunk):
        eg_ref[0, rows[c], :] = jnp.where(lane < heads, jnp.exp(gcum_c[c]), 0.0)
    ch = [(c, h) for c in range(nchunk) for h in range(heads)]
    k = [act_ref[0, rows[c], o_k + h * dk:o_k + (h + 1) * dk] for c, h in ch]
    kb = [x.astype(BF16) for x in k]
    kk = [_dot_nt(x, x) for x in kb]
    qk = [_dot_nt(act_ref[0, rows[c], h * dk:(h + 1) * dk].astype(BF16), kb[i]) for i, (c, h) in enumerate(ch)]
    g_c = [gcum_c[c][:, h:h + 1] for c, h in ch]
    beta = [gbc[c][:, heads + h:heads + h + 1] for c, h in ch]
    decay = [jnp.exp(jnp.where(incl, g_c[i] - gcum_r[c][h:h + 1, :], NEG_INF)) for i, (c, h) in enumerate(ch)]
    a_blk = [jnp.where(strict, beta[i] * kk[i] * decay[i], 0.0) for i in range(len(ch))]
    for i, (c, h) in enumerate(ch):
        attn_ref[0, rows[c], h * chunk:(h + 1) * chunk] = (qk[i] * decay[i]).astype(BF16)
        kd_ref[0, rows[c], h * dk:(h + 1) * dk] = (k[i] * jnp.exp(g_c[i][chunk - 1:chunk, :] - g_c[i])).astype(BF16)
    rhs_blk = [jnp.concatenate([beta[i] * act_ref[0, rows[c], o_v + h * dv:o_v + (h + 1) * dv],
                                (beta[i] * jnp.exp(g_c[i])) * k[i]], axis=1) for i, (c, h) in enumerate(ch)]
    problems = [(c, g0) for c in range(nchunk) for g0 in range(0, heads, group)]
    a_list = [jnp.concatenate(a_blk[c * heads + g0:c * heads + g0 + group], axis=1) for c, g0 in problems]
    rhs_list = [jnp.concatenate(rhs_blk[c * heads + g0:c * heads + g0 + group], axis=0) for c, g0 in problems]
    e_list = _unit_lower_inverses_minus_eye(a_list, group, chunk)
    sols = [rhs + _dot(_blockdiag(e, group, chunk), rhs.astype(BF16)) for e, rhs in zip(e_list, rhs_list)]
    for (c, g0), sol in zip(problems, sols):
        for i, h in enumerate(range(g0, g0 + group)):
            uv_ref[0, rows[c], h * dv:(h + 1) * dv] = sol[i * chunk:(i + 1) * chunk, :dv]
            w_ref[0, rows[c], h * dk:(h + 1) * dk] = sol[i * chunk:(i + 1) * chunk, dv:].astype(BF16)


def _gdn_prep(act, gb_c, gb_r, *, chunk, tc, heads, dk, dv):
    b, t, _ = act.shape
    tc = min(tc, t)
    nchunk = tc // chunk
    group = (2 * LANES) // chunk
    kern = functools.partial(_gdn_prep_kernel, heads=heads, dk=dk, dv=dv, chunk=chunk, nchunk=nchunk, group=group)
    tok = lambda last: pl.BlockSpec((1, tc, last), lambda i, j: (i, j, 0))
    return pl.pallas_call(
        kern,
        out_shape=[jax.ShapeDtypeStruct((b, t, heads * dk), BF16), jax.ShapeDtypeStruct((b, t, heads * dv), F32),
                   jax.ShapeDtypeStruct((b, t, heads * dk), BF16), jax.ShapeDtypeStruct((b, t, heads * chunk), BF16),
                   jax.ShapeDtypeStruct((b, t, LANES), F32)],
        grid=(b, t // tc),
        in_specs=[tok(act.shape[-1]), tok(LANES),
                  pl.BlockSpec((1, nchunk, 2 * SUBLANES, chunk), lambda i, j: (i, j, 0, 0))],
        out_specs=[tok(heads * dk), tok(heads * dv), tok(heads * dk), tok(heads * chunk), tok(LANES)],
        compiler_params=_params("arbitrary", "arbitrary"),
        name="gdn_prep",
    )(act, gb_c, gb_r)


def _gdn_scan_kernel(q_ref, w_ref, uv_ref, kd_ref, attn_ref, eg_ref, gout_ref, s0_ref, o_ref, s1_ref, s_sc, *,
                     heads, dk, dv, chunk, nchunk):
    t = pl.program_id(1)
    nseq = q_ref.shape[0]

    @pl.when(t == 0)
    def _():
        s_sc[...] = s0_ref[...]

    def body(c, carry):
        rows = pl.ds(pl.multiple_of(c * chunk, chunk), chunk)
        ps = [(s, h) for s in range(nseq) for h in range(heads)]
        eg = [eg_ref[s, rows, :] for s in range(nseq)]
        s0 = [s_sc[s, h] for s, h in ps]
        s0b = [x.astype(BF16) for x in s0]
        ws = [_dot(w_ref[s, rows, h * dk:(h + 1) * dk], s0b[i]) for i, (s, h) in enumerate(ps)]
        ub = [(uv_ref[s, rows, h * dv:(h + 1) * dv] - ws[i]).astype(BF16) for i, (s, h) in enumerate(ps)]
        eg_h = [jnp.broadcast_to(eg[s][:, h:h + 1], (chunk, dv)) for s, h in ps]
        ku = [_dot_tn(kd_ref[s, rows, h * dk:(h + 1) * dk], ub[i]) for i, (s, h) in enumerate(ps)]
        for i, (s, h) in enumerate(ps):
            s_sc[s, h] = eg_h[i][chunk - 1:chunk, :] * s0[i] + ku[i]
        qs = [_dot(q_ref[s, rows, h * dk:(h + 1) * dk].astype(BF16), s0b[i]) for i, (s, h) in enumerate(ps)]
        au = [_dot(attn_ref[s, rows, h * chunk:(h + 1) * chunk], ub[i]) for i, (s, h) in enumerate(ps)]
        on = [_rms(eg_h[i] * qs[i] + au[i], gout_ref[...]) for i in range(len(ps))]
        for i, (s, h) in enumerate(ps):
            o_ref[s, rows, h * dv:(h + 1) * dv] = on[i]
        return carry

    lax.fori_loop(0, nchunk, body, 0)

    @pl.when(t == pl.num_programs(1) - 1)
    def _():
        s1_ref[...] = s_sc[...]


def _gdn_scan(act, w, uv, kd, attn, eg, gout, s0, *, tc, heads, dk, dv):
    b, t, _ = act.shape
    chunk = min(CHUNK, t)
    nchunk = tc // chunk
    kern = functools.partial(_gdn_scan_kernel, heads=heads, dk=dk, dv=dv, chunk=chunk, nchunk=nchunk)
    ns = math.gcd(b, SCAN_SEQS)
    tok = lambda last: pl.BlockSpec((ns, tc, last), lambda i, j: (i, j, 0))
    state = pl.BlockSpec((ns, heads, dk, dv), lambda i, j: (i, 0, 0, 0))
    return pl.pallas_call(
        kern,
        out_shape=[jax.ShapeDtypeStruct((b, t, heads * dv), F32), jax.ShapeDtypeStruct(s0.shape, F32)],
        grid=(b // ns, t // tc),
        in_specs=[tok(heads * dk),
                  tok(heads * dk), tok(heads * dv), tok(heads * dk), tok(heads * chunk), tok(LANES),
                  pl.BlockSpec(gout.shape, lambda i, j: (0, 0)), state],
        out_specs=[tok(heads * dv), state],
        scratch_shapes=[pltpu.VMEM((ns, heads, dk, dv), F32)],
        compiler_params=_params("arbitrary", "arbitrary"),
        name="gdn_scan",
    )(act, w, uv, kd, attn, eg, gout, s0)


def _final_kernel(o_ref, z_ref, x_ref, gate_ref, wo_ref, g_ref, y_ref):
    mixed = (o_ref[0] * _silu(z_ref[0])).astype(BF16)
    x2 = x_ref[0] + _per_row(gate_ref[0], x_ref.shape[1]) * _dot(mixed, wo_ref[...])
    y_ref[0] = _rms(x2, g_ref[...])


def _final(o, z, x, gate, wo, g, *, tm):
    b, t, d = x.shape
    tok = lambda last: pl.BlockSpec((1, tm, last), lambda i, j: (i, j, 0))
    const = lambda a: pl.BlockSpec(a.shape, lambda i, j: (0,) * a.ndim)
    return pl.pallas_call(
        _final_kernel,
        out_shape=jax.ShapeDtypeStruct((b, t, d), F32),
        grid=(b, t // tm),
        in_specs=[tok(o.shape[-1]), tok(d), tok(d), pl.BlockSpec((1,) + gate.shape[1:], lambda i, j: (i, 0, 0)),
                  const(wo), const(g)],
        out_specs=tok(d),
        compiler_params=_params("arbitrary", "arbitrary"),
        name="out_c_final",
    )(o, z, x, gate, wo, g)


def _pad_lanes(w, width=LANES, at=0):
    out = jnp.zeros(w.shape[:-1] + (width,), w.dtype)
    return out.at[..., at:at + w.shape[-1]].set(w)


def _rot_half_cols(w):
    r = w.shape[-1] // 2
    return jnp.concatenate([-w[..., r:], w[..., :r]], axis=-1)


def _rope_tables(pos, rope, scale):
    freqs = jnp.exp(jnp.arange(0, rope, 2, dtype=F32) * (-math.log(ROPE_BASE) / rope))
    ang = pos.astype(F32)[:, None] * freqs[None, :]
    cos = jnp.concatenate([jnp.cos(ang), jnp.cos(ang)], axis=-1)
    sin = jnp.concatenate([jnp.sin(ang), jnp.sin(ang)], axis=-1)
    half = LANES // 2
    cosk = _pad_lanes(cos, at=half)
    sink = _pad_lanes(sin, at=half)
    ones = _pad_lanes(jnp.ones((pos.shape[0], half), F32))
    return jnp.concatenate([(cosk + ones) * scale, sink * scale, cosk, sink], axis=-1)


def _tokens_on_lanes(a, chunk, rows):
    b, t = a.shape[:2]
    return a[..., :rows].reshape(b, t // chunk, chunk, rows).transpose(0, 1, 3, 2)


def kernel(x_prompt, x_sample, c_prompt, c_sample, cache_kv_latent, cache_k_rope, state_mlstm_C, state_mlstm_n, state_mlstm_m, state_gdn_S, state_gdn_conv, a_w_ada, a_b_ada, a_g_norm, a_w_in, a_g_q_a, a_w_q_b, a_g_kv_a, a_w_kv_b, a_b_i, a_b_f, a_g_out, a_w_out, c_w_ada, c_b_ada, c_g_norm, c_w_in, c_w_conv, c_a_log, c_dt_bias, c_g_out, c_w_out, g_final):
    d = x_prompt.shape[-1]
    q_lora, heads, qk = a_w_q_b.shape
    kv_lora = a_w_kv_b.shape[0]
    rope = cache_k_rope.shape[-1]
    nope = qk - rope
    v_head = a_w_kv_b.shape[2] - nope
    m_heads, m_dv = a_g_out.shape
    m_dk = state_mlstm_C.shape[2]
    g_heads = c_a_log.shape[0]
    g_dk, g_dv = state_gdn_S.shape[2:]
    width = c_w_conv.shape[0]
    conv_ch = c_w_conv.shape[1]
    assert nope + rope <= LANES and nope == LANES // 2 and 2 * m_heads <= SUBLANES and 2 * g_heads <= 2 * SUBLANES

    sizes = (q_lora, kv_lora, rope, m_heads * m_dk, m_heads * m_dk, m_heads * m_dv, m_heads, m_heads,
             m_heads * m_dv, heads * v_head + m_heads * m_dv)
    offs = [0]
    for s in sizes:
        offs.append(offs[-1] + s)
    w_qa, w_c, w_kr, w_mq, w_mk, w_mv, w_mi, w_mf, w_mo, w_z = [a_w_in[:, offs[i]:offs[i + 1]] for i in range(10)]
    half = LANES // 2
    w1 = jnp.concatenate([w_qa, w_c, _pad_lanes(w_kr, at=half), _pad_lanes(_rot_half_cols(w_kr), at=half),
                          w_mq, w_mk, w_mv, w_mo, w_z, _pad_lanes(jnp.concatenate([w_mi, w_mf], axis=1))],
                         axis=1).astype(BF16)
    m_width = 2 * m_heads * m_dk + 2 * m_heads * m_dv
    wq_rope = a_w_q_b[..., nope:]
    wq_main = _pad_lanes(a_w_q_b).reshape(q_lora, heads * LANES)
    wq_rot = _pad_lanes(_rot_half_cols(wq_rope), at=nope).reshape(q_lora, heads * LANES)
    wq = jnp.concatenate([wq_main, wq_rot], axis=1).astype(BF16)
    wkv = jnp.concatenate([_pad_lanes(a_w_kv_b[..., :nope]).reshape(kv_lora, heads * LANES),
                           a_w_kv_b[..., nope:].reshape(kv_lora, heads * v_head)], axis=1).astype(BF16)
    wk_abs = a_w_kv_b[..., :nope].transpose(1, 0, 2).astype(BF16)
    wv_abs = a_w_kv_b[..., nope:].transpose(1, 0, 2).astype(BF16)
    wo_a = a_w_out.astype(BF16)
    csz = (conv_ch, g_heads, g_heads, g_heads * g_dv)
    w_qkv, w_a, w_b, w_zc = [c_w_in[:, sum(csz[:i]):sum(csz[:i + 1])] for i in range(4)]
    w2 = jnp.concatenate([w_qkv, w_zc, _pad_lanes(jnp.concatenate([w_a, w_b], axis=1))], axis=1).astype(BF16)
    wo_c = c_w_out.astype(BF16)
    wc8 = jnp.zeros((SUBLANES, conv_ch), F32).at[:width].set(c_w_conv)
    row = lambda a: a.reshape(1, -1).astype(F32)
    bias_c = _pad_lanes(jnp.concatenate([a_b_i, a_b_f]).reshape(1, -1))
    bias_r = jnp.zeros((SUBLANES, 1), F32).at[:2 * m_heads, 0].set(jnp.concatenate([a_b_i, a_b_f]))
    alog = _pad_lanes(c_a_log.reshape(1, -1))
    dtb = _pad_lanes(c_dt_bias.reshape(1, -1))

    bp, bs = c_prompt.shape[0], c_sample.shape[0]
    c_all = jnp.concatenate([c_prompt, c_sample], axis=0)
    pad = (-c_all.shape[0]) % SUBLANES
    c_all = jnp.pad(c_all, ((0, pad), (0, 0)))
    mod_a = _adaln(c_all, a_w_ada, a_b_ada)
    mod_c = _adaln(c_all, c_w_ada, c_b_ada)

    def mods(mod, lo, hi):
        return [mod[lo:hi, i * d:(i + 1) * d][:, None, :] for i in range(3)]

    def run(x, mod_lo, mod_hi, c_past, kr_past, c0, n0, m0, conv0, s0):
        b, t, _ = x.shape
        past = 0 if c_past is None else c_past.shape[1]
        chunk = min(CHUNK, t)
        tc = min(t, 512)
        expand = c_past is None
        nb = 1 if expand else math.gcd(b, max(1, PROJ_TM // t))
        tm = min(nb * t, PROJ_TM)
        grp = lambda a: a.reshape((b // nb, nb * a.shape[1]) + a.shape[2:])
        ungrp = lambda a: a.reshape((b, a.shape[1] // nb) + a.shape[2:])
        shift_a, scale_a, gate_a = [grp(m) for m in mods(mod_a, mod_lo, mod_hi)]
        shift_c, scale_c, gate_c = [grp(m) for m in mods(mod_c, mod_lo, mod_hi)]
        tab = _rope_tables(past + jnp.arange(t, dtype=jnp.int32), rope, qk ** -0.5 * math.log2(math.e))
        outs = _in_a(grp(x), shift_a, scale_a, row(a_g_norm), w1, row(a_g_q_a), wq, row(a_g_kv_a), wkv,
                     jnp.tile(tab, (nb, 1)),
                     tm=tm, heads=heads, q_lora=q_lora, kv_lora=kv_lora, rope=rope, m_width=m_width,
                     mk_cols=(m_heads * m_dk, 2 * m_heads * m_dk), v_head=v_head, expand_kv=expand)
        q, c_new, kr_new, m_slab, z, gates = [ungrp(a) for a in outs[:6]]
        kt = None
        if expand:
            k_all, vt, kt = outs[6], outs[7], outs[10]
            qn = outs[8][..., :heads].transpose(0, 2, 1)
            tiles = dict(heads=heads, v_head=v_head, tq=min(t, FLASH_TQ), tk=min(t, FLASH_TK))
            o_fast, row_sums = _flash(q, k_all, vt, bound=(qn, outs[9]), **tiles)
            o_mla = lax.cond(jnp.min(row_sums) >= FLASH_ROW_SUM_MIN,
                             lambda: o_fast, lambda: _flash(q, k_all, vt, **tiles))
        else:
            o_mla = _latent_attn(q, c_past, kr_past, c_new, kr_new, wk_abs, wv_abs,
                                 heads=heads, nope=nope, rope=rope, v_head=v_head)
        fold = t == chunk and b > 1
        flat = (lambda a: a.reshape((1, b * a.shape[1]) + a.shape[2:])) if fold else (lambda a: a)
        unflat = (lambda a: a.reshape((b, a.shape[1] // b) + a.shape[2:])) if fold else (lambda a: a)
        m_flat = flat(m_slab)
        if kt is None:
            kt = m_flat[..., m_heads * m_dk:2 * m_heads * m_dk].transpose(0, 2, 1)
        pv, kvs, bcum, stats = [unflat(a) for a in _mlstm_prep(
            m_flat, kt, flat(gates), flat(_tokens_on_lanes(gates, chunk, SUBLANES)), bias_c, bias_r,
            chunk=chunk, heads=m_heads, dk=m_dk, dv=m_dv)]
        hm, c1, n1, m1 = _mlstm_scan(m_slab, pv, kvs, bcum, stats, row(a_g_out), c0, n0, m0.reshape(b, m_heads, 1, 1),
                                     tc=tc, heads=m_heads, dk=m_dk, dv=m_dv)
        x1, qkv, zc, ab = [ungrp(a) for a in _mid(grp(o_mla), grp(hm), grp(z), grp(x), gate_a, wo_a, shift_c, scale_c,
                                                  row(c_g_norm), w2, tm=tm, conv_ch=conv_ch)]
        past8 = jnp.pad(conv0, ((0, 0), (SUBLANES - (width - 1), 0), (0, 0)))
        act, gb = _conv(qkv, past8, wc8, ab, alog, dtb, tm=min(t, PROJ_TM), width=width, heads=g_heads, dk=g_dk)
        w, uv, kd, attn, eg = [unflat(a) for a in _gdn_prep(
            flat(act), flat(gb), flat(_tokens_on_lanes(gb, chunk, 2 * SUBLANES)),
            chunk=chunk, tc=GDN_PREP_TC, heads=g_heads, dk=g_dk, dv=g_dv)]
        o_gdn, s1 = _gdn_scan(act, w, uv, kd, attn, eg, row(c_g_out), s0, tc=min(t, GDN_SCAN_TC),
                              heads=g_heads, dk=g_dk, dv=g_dv)
        y = ungrp(_final(grp(o_gdn), grp(zc), grp(x1), gate_c, wo_c, row(g_final), tm=tm))
        conv1 = jnp.concatenate([conv0, qkv], axis=1)[:, t:] if t < width - 1 else qkv[:, t - (width - 1):]
        return (y, c_new, kr_new, c1, n1, m1.reshape(b, m_heads), conv1, s1)

    dt = x_prompt.dtype
    (y_p, p_kv, p_kr, p_c, p_n, p_m, p_conv, p_s) = run(
        x_prompt, 0, bp, None, None,
        jnp.zeros((bp, m_heads, m_dk, m_dv), dt), jnp.zeros((bp, m_heads, m_dk), dt), jnp.zeros((bp, m_heads), dt),
        jnp.zeros((bp, width - 1, conv_ch), dt), jnp.zeros((bp, g_heads, g_dk, g_dv), dt))
    (y_s, s_kv, s_kr, s_c, s_n, s_m, s_conv, s_s) = run(
        x_sample, bp, bp + bs, cache_kv_latent, cache_k_rope, state_mlstm_C, state_mlstm_n, state_mlstm_m,
        state_gdn_conv, state_gdn_S)
    return (y_p, y_s, p_kv, p_kr, p_c, p_n, p_m, p_s, p_conv,
            s_kv, s_kr, s_c, s_n, s_m, s_s, s_conv)
```

```python
import functools
import math

import jax
import jax.numpy as jnp
from jax import lax
from jax.experimental import pallas as pl
from jax.experimental.pallas import tpu as pltpu

F32 = jnp.float32
BF16 = jnp.bfloat16
HIGHEST = lax.Precision.HIGHEST

CHUNK = 64
EPS = 1e-6
ROPE_BASE = 10000.0
LANES = 128
SUBLANES = 8
VMEM_LIMIT = 56 * 1024 * 1024
NEG_INF = float("-inf")
SCAN_SEQS = 4
GDN_SCAN_TC = 256
PROJ_TM = 512
FLASH_TQ = 1024
FLASH_TK = 1024
LATENT_ROW_GROUPS = 2
FLASH_HEAD_GROUP = 4
FLASH_BOUND_SLACK = 1.0 + 2.0 ** -6
FLASH_ROW_SUM_MIN = 2.0 ** -100
MLSTM_PREP_TC = 512
GDN_PREP_TC = 512


def _params(*sem):
    return pltpu.CompilerParams(dimension_semantics=sem, vmem_limit_bytes=VMEM_LIMIT)


def _dot(a, b, precision=None):
    return jnp.dot(a, b, preferred_element_type=F32, precision=precision)


def _dot_nt(a, b):
    return lax.dot_general(a, b, (((1,), (1,)), ((), ())), preferred_element_type=F32)


def _dot_tn(a, b):
    return lax.dot_general(a, b, (((0,), (0,)), ((), ())), preferred_element_type=F32)


def _rms(x, g):
    return x * lax.rsqrt(jnp.mean(x * x, axis=-1, keepdims=True) + EPS) * g


def _per_row(m, rows):
    n, d = m.shape
    if n == 1:
        return m
    return jnp.concatenate([jnp.broadcast_to(m[i:i + 1], (rows // n, d)) for i in range(n)], axis=0)


def _sigmoid(x):
    return 0.5 * jnp.tanh(0.5 * x) + 0.5


def _silu(x):
    return x * _sigmoid(x)


def _softplus(x):
    return jnp.maximum(x, 0.0) + jnp.log1p(jnp.exp(-jnp.abs(x)))


def _log_sigmoid(x):
    return -_softplus(-x)


def _adaln_kernel(c_ref, w_ref, b_ref, o_ref):
    o_ref[...] = _dot(_silu(c_ref[...]), w_ref[...], HIGHEST) + b_ref[...]


def _adaln(c, w, b):
    n, d = c.shape
    d3 = w.shape[1]
    return pl.pallas_call(
        _adaln_kernel,
        out_shape=jax.ShapeDtypeStruct((n, d3), F32),
        grid=(d3 // d,),
        in_specs=[pl.BlockSpec((n, d), lambda j: (0, 0)),
                  pl.BlockSpec((d, d), lambda j: (0, j)),
                  pl.BlockSpec((1, d), lambda j: (0, j))],
        out_specs=pl.BlockSpec((n, d), lambda j: (0, j)),
        compiler_params=_params("arbitrary"),
        name="adaln",
    )(c, w, b.reshape(1, d3))


def _in_a_kernel(x_ref, shift_ref, scale_ref, g_ref, w1_ref, gq_ref, wq_ref, gkv_ref, wkv_ref, tab_ref,
                 q_ref, c_ref, kr_ref, m_ref, z_ref, gt_ref, *kv_refs,
                 heads, q_lora, kv_lora, rope, m_width, mk_cols, d_model, v_head, expand_kv):
    x = x_ref[0]
    hn = _rms(x, g_ref[...]) * (1.0 + _per_row(scale_ref[0], x_ref.shape[1])) + _per_row(shift_ref[0], x_ref.shape[1])
    y = _dot(hn.astype(BF16), w1_ref[...])
    o = 0
    qa = y[:, o:o + q_lora]; o += q_lora
    cl = y[:, o:o + kv_lora]; o += kv_lora
    kr1 = y[:, o:o + LANES]; o += LANES
    kr2 = y[:, o:o + LANES]; o += LANES
    m_ref[0] = y[:, o:o + m_width]
    mk = y[:, o + mk_cols[0]:o + mk_cols[1]]
    o += m_width
    z_ref[0] = y[:, o:o + d_model]; o += d_model
    gt_ref[0] = y[:, o:o + LANES]

    tab = tab_ref[...]
    cosq, sinq = tab[:, 0:LANES], tab[:, LANES:2 * LANES]
    cosk, sink = tab[:, 2 * LANES:3 * LANES], tab[:, 3 * LANES:4 * LANES]

    def sq_norm(xb):
        xf = xb.astype(F32)
        return jnp.sum(xf * xf, axis=-1, keepdims=True)

    lane = lax.broadcasted_iota(jnp.int32, (x.shape[0], LANES), 1)
    qq = _dot(_rms(qa, gq_ref[...]).astype(BF16), wq_ref[...])
    hw = heads * LANES
    qn2 = jnp.zeros((x.shape[0], LANES), F32)
    for h in range(heads):
        sl = slice(h * LANES, (h + 1) * LANES)
        qb = (qq[:, sl] * cosq + qq[:, hw + h * LANES:hw + (h + 1) * LANES] * sinq).astype(BF16)
        q_ref[0, :, sl] = qb
        if expand_kv:
            qn2 = jnp.where(lane == h, sq_norm(qb), qn2)

    cn = _rms(cl, gkv_ref[...])
    c_ref[0] = cn
    kr = kr1 * cosk + kr2 * sink
    kr_ref[0] = kr[:, LANES // 2:LANES // 2 + rope]
    if expand_kv:
        k_ref, vt_ref, qn_ref, kmax_ref, kt_ref = kv_refs
        kv = _dot(cn.astype(BF16), wkv_ref[...])
        kn2 = jnp.zeros((x.shape[0], LANES), F32)
        for h in range(heads):
            sl = slice(h * LANES, (h + 1) * LANES)
            kb = (kv[:, sl] + kr).astype(BF16)
            k_ref[0, :, sl] = kb
            kn2 = jnp.where(lane == h, sq_norm(kb), kn2)
        vt_ref[0] = kv[:, hw:hw + heads * v_head].T.astype(BF16)
        kt_ref[0] = mk.T
        qn_ref[0] = jnp.sqrt(qn2)
        kmax_ref[0, 0] = jnp.sqrt(jnp.max(kn2, axis=0, keepdims=True))


def _in_a(x, shift, scale, g, w1, gq, wq, gkv, wkv, tab, *, tm, heads, q_lora, kv_lora, rope, m_width,
          mk_cols, v_head, expand_kv):
    b, t, d = x.shape
    grid = (b, t // tm)
    tok = lambda last: pl.BlockSpec((1, tm, last), lambda i, j: (i, j, 0))
    tok_t = lambda rows: pl.BlockSpec((1, rows, tm), lambda i, j: (i, 0, j))
    const = lambda a: pl.BlockSpec(a.shape, lambda i, j: (0,) * a.ndim)
    out_shape = [jax.ShapeDtypeStruct((b, t, heads * LANES), BF16),
                 jax.ShapeDtypeStruct((b, t, kv_lora), F32),
                 jax.ShapeDtypeStruct((b, t, rope), F32),
                 jax.ShapeDtypeStruct((b, t, m_width), F32),
                 jax.ShapeDtypeStruct((b, t, d), F32),
                 jax.ShapeDtypeStruct((b, t, LANES), F32)]
    out_specs = [tok(heads * LANES), tok(kv_lora), tok(rope), tok(m_width), tok(d), tok(LANES)]
    if expand_kv:
        out_shape += [jax.ShapeDtypeStruct((b, t, heads * LANES), BF16),
                      jax.ShapeDtypeStruct((b, heads * v_head, t), BF16),
                      jax.ShapeDtypeStruct((b, t, LANES), F32),
                      jax.ShapeDtypeStruct((b, t // tm, 1, LANES), F32),
                      jax.ShapeDtypeStruct((b, mk_cols[1] - mk_cols[0], t), F32)]
        out_specs += [tok(heads * LANES), tok_t(heads * v_head), tok(LANES),
                      pl.BlockSpec((1, 1, 1, LANES), lambda i, j: (i, j, 0, 0)), tok_t(mk_cols[1] - mk_cols[0])]
    kern = functools.partial(_in_a_kernel, heads=heads, q_lora=q_lora, kv_lora=kv_lora, rope=rope,
                             m_width=m_width, mk_cols=mk_cols, d_model=d, v_head=v_head, expand_kv=expand_kv)
    return pl.pallas_call(
        kern, out_shape=out_shape, grid=grid,
        in_specs=[tok(d),
                  pl.BlockSpec((1,) + shift.shape[1:], lambda i, j: (i, 0, 0)),
                  pl.BlockSpec((1,) + scale.shape[1:], lambda i, j: (i, 0, 0)),
                  const(g), const(w1), const(gq), const(wq), const(gkv), const(wkv),
                  pl.BlockSpec((tm, 4 * LANES), lambda i, j: (j, 0))],
        out_specs=out_specs,
        compiler_params=_params("arbitrary", "arbitrary"),
        name="in_proj_a",
    )(x, shift, scale, g, w1, gq, wq, gkv, wkv, tab)


def _flash_kernel(qi_ref, ki_ref, q_ref, k_ref, vt_ref, o_ref, m_sc, l_sc, acc_sc, *,
                  heads, v_head, tq, tk, chunk):
    step_id = pl.program_id(1)
    qi = qi_ref[step_id]
    ki = ki_ref[step_id]

    @pl.when(ki == 0)
    def _():
        m_sc[...] = jnp.full(m_sc.shape, NEG_INF, F32)
        l_sc[...] = jnp.zeros(l_sc.shape, F32)
        acc_sc[...] = jnp.zeros(acc_sc.shape, F32)

    def step(masked):
        if masked:
            kc = (ki * tk + lax.broadcasted_iota(jnp.int32, (tk, tq), 0)) // chunk
            qc = (qi * tq + lax.broadcasted_iota(jnp.int32, (tk, tq), 1)) // chunk
            mask = kc <= qc
        for h in range(heads):
            qh = q_ref[0, :, h * LANES:(h + 1) * LANES]
            kh = k_ref[0, :, h * LANES:(h + 1) * LANES]
            vth = vt_ref[0, h * v_head:(h + 1) * v_head, :]
            rows = slice(h * v_head, (h + 1) * v_head)
            st = _dot_nt(kh, qh)
            if masked:
                st = jnp.where(mask, st, NEG_INF)
            m_prev = m_sc[h]
            m_new = jnp.maximum(m_prev, jnp.max(st, axis=0, keepdims=True))
            alpha = jnp.exp2(m_prev - m_new)
            p = jnp.exp2(st - m_new)
            l_sc[h] = alpha * l_sc[h] + jnp.sum(p, axis=0, keepdims=True)
            acc_sc[rows, :] = alpha * acc_sc[rows, :] + _dot(vth, p.astype(BF16))
            m_sc[h] = m_new

    full = (ki + 1) * tk <= qi * tq + chunk

    @pl.when(full)
    def _():
        step(False)

    @pl.when(jnp.logical_not(full))
    def _():
        step(True)

    @pl.when(ki == ((qi + 1) * tq - 1) // tk)
    def _():
        for h in range(heads):
            rows = slice(h * v_head, (h + 1) * v_head)
            acc_sc[rows, :] = acc_sc[rows, :] / l_sc[h]
        o_ref[0] = acc_sc[...].T


def _flash_bound_kernel(qi_ref, ki_ref, q_ref, k_ref, vt_ref, qn_ref, kmax_ref, o_ref, l_ref,
                        mb_sc, l_sc, acc_sc, *, heads, v_head, tq, tk, chunk):
    step_id = pl.program_id(1)
    qi = qi_ref[step_id]
    ki = ki_ref[step_id]

    @pl.when(ki == 0)
    def _():
        kmax = jnp.max(kmax_ref[0], axis=0) * FLASH_BOUND_SLACK
        for h in range(heads):
            mb_sc[h:h + 1, :] = qn_ref[0, h:h + 1, :] * kmax[:, h:h + 1]
        l_sc[...] = jnp.zeros(l_sc.shape, F32)
        acc_sc[...] = jnp.zeros(acc_sc.shape, F32)

    def step(masked):
        if masked:
            kc = (ki * tk + lax.broadcasted_iota(jnp.int32, (tk, tq), 0)) // chunk
            qc = (qi * tq + lax.broadcasted_iota(jnp.int32, (tk, tq), 1)) // chunk
            mask = kc <= qc
        for h0 in range(0, heads, FLASH_HEAD_GROUP):
            hs = range(h0, h0 + FLASH_HEAD_GROUP)
            st = [_dot_nt(k_ref[0, :, h * LANES:(h + 1) * LANES], q_ref[0, :, h * LANES:(h + 1) * LANES])
                  for h in hs]
            if masked:
                st = [jnp.where(mask, s, NEG_INF) for s in st]
            p = [jnp.exp2(s - mb_sc[h:h + 1, :]) for s, h in zip(st, hs)]
            for x, h in zip(p, hs):
                l_sc[h:h + 1, :] += jnp.sum(x, axis=0, keepdims=True)
            pv = [_dot(vt_ref[0, h * v_head:(h + 1) * v_head, :], x.astype(BF16)) for x, h in zip(p, hs)]
            for x, h in zip(pv, hs):
                acc_sc[h * v_head:(h + 1) * v_head, :] += x

    def diagonal_step():
        hq = tq // 2
        kc = lax.broadcasted_iota(jnp.int32, (hq, hq), 0) // chunk
        qc = lax.broadcasted_iota(jnp.int32, (hq, hq), 1) // chunk
        mask = kc <= qc
        for h0 in range(0, heads, FLASH_HEAD_GROUP):
            ps = [(h, kb, qb) for h in range(h0, h0 + FLASH_HEAD_GROUP) for kb, qb in ((0, 0), (0, 1), (1, 1))]
            ksl = lambda kb: slice(kb * hq, (kb + 1) * hq)
            st = [_dot_nt(k_ref[0, ksl(kb), h * LANES:(h + 1) * LANES], q_ref[0, ksl(qb), h * LANES:(h + 1) * LANES])
                  for h, kb, qb in ps]
            st = [jnp.where(mask, s, NEG_INF) if kb == qb else s for s, (h, kb, qb) in zip(st, ps)]
            p = [jnp.exp2(s - mb_sc[h:h + 1, ksl(qb)]) for s, (h, kb, qb) in zip(st, ps)]
            for x, (h, kb, qb) in zip(p, ps):
                l_sc[h:h + 1, ksl(qb)] += jnp.sum(x, axis=0, keepdims=True)
            pv = [_dot(vt_ref[0, h * v_head:(h + 1) * v_head, ksl(kb)], x.astype(BF16)) for x, (h, kb, qb) in zip(p, ps)]
            for x, (h, kb, qb) in zip(pv, ps):
                acc_sc[h * v_head:(h + 1) * v_head, ksl(qb)] += x

    full = (ki + 1) * tk <= qi * tq + chunk

    @pl.when(full)
    def _():
        step(False)

    @pl.when(jnp.logical_not(full))
    def _():
        if tq == tk and tq % (2 * LANES) == 0 and (tq // 2) % chunk == 0:
            diagonal_step()
        else:
            step(True)

    @pl.when(ki == ((qi + 1) * tq - 1) // tk)
    def _():
        for h in range(heads):
            rows = slice(h * v_head, (h + 1) * v_head)
            acc_sc[rows, :] = acc_sc[rows, :] / l_sc[h:h + 1, :]
        o_ref[0] = acc_sc[...].T
        l_ref[0] = l_sc[...]


def _flash(q, k, vt, *, heads, v_head, tq, tk, bound=None):
    b, t, _ = q.shape
    pairs = [(i, j) for i in range(t // tq) for j in range(((i + 1) * tq - 1) // tk + 1)]
    qi_tab = jnp.asarray([p[0] for p in pairs], jnp.int32)
    ki_tab = jnp.asarray([p[1] for p in pairs], jnp.int32)
    in_specs = [pl.BlockSpec((1, tq, heads * LANES), lambda i, s, qt, kt: (i, qt[s], 0)),
                pl.BlockSpec((1, tk, heads * LANES), lambda i, s, qt, kt: (i, kt[s], 0)),
                pl.BlockSpec((1, heads * v_head, tk), lambda i, s, qt, kt: (i, 0, kt[s]))]
    o_shape = jax.ShapeDtypeStruct((b, t, heads * v_head), F32)
    o_spec = pl.BlockSpec((1, tq, heads * v_head), lambda i, s, qt, kt: (i, qt[s], 0))
    acc = pltpu.VMEM((heads * v_head, tq), F32)
    if bound is None:
        kern, name, args = _flash_kernel, "flash_attn", (q, k, vt)
        out_shape, out_specs = o_shape, o_spec
        scratch = [pltpu.VMEM((heads, 1, tq), F32), pltpu.VMEM((heads, 1, tq), F32), acc]
    else:
        qn, kmax = bound
        assert qn.shape == (b, heads, t)
        kern, name, args = _flash_bound_kernel, "flash_attn_bound", (q, k, vt, qn, kmax)
        in_specs += [pl.BlockSpec((1, heads, tq), lambda i, s, qt, kt: (i, 0, qt[s])),
                     pl.BlockSpec((1,) + kmax.shape[1:], lambda i, s, qt, kt: (i, 0, 0, 0))]
        out_shape = [o_shape, jax.ShapeDtypeStruct((b, heads, t), F32)]
        out_specs = [o_spec, pl.BlockSpec((1, heads, tq), lambda i, s, qt, kt: (i, 0, qt[s]))]
        scratch = [pltpu.VMEM((heads, tq), F32), pltpu.VMEM((heads, tq), F32), acc]
    grid_spec = pltpu.PrefetchScalarGridSpec(
        num_scalar_prefetch=2, grid=(b, len(pairs)), in_specs=in_specs, out_specs=out_specs,
        scratch_shapes=scratch)
    return pl.pallas_call(
        functools.partial(kern, heads=heads, v_head=v_head, tq=tq, tk=tk, chunk=CHUNK),
        out_shape=out_shape,
        grid_spec=grid_spec,
        compiler_params=_params("arbitrary", "arbitrary"),
        name=name,
    )(qi_tab, ki_tab, *args)


def _latent_attn_kernel(q_ref, cp_ref, krp_ref, cn_ref, krn_ref, wk_ref, wv_ref, o_ref, *,
                        heads, nope, rope, v_head):
    q = q_ref[0]
    qabs, qrope = [], []
    for h in range(heads):
        qabs.append(_dot_nt(q[:, h * LANES:h * LANES + nope], wk_ref[h]))
        qrope.append(q[:, h * LANES + nope:h * LANES + nope + rope])
    qabs = jnp.concatenate(qabs, axis=0).astype(BF16)
    qrope = jnp.concatenate(qrope, axis=0)
    cp = cp_ref[0].astype(BF16)
    cn = cn_ref[0].astype(BF16)
    krp = krp_ref[0].astype(BF16)
    krn = krn_ref[0].astype(BF16)
    t = q.shape[0]
    rg = (heads // LATENT_ROW_GROUPS) * t
    grp = [slice(g * rg, (g + 1) * rg) for g in range(LATENT_ROW_GROUPS)]
    s_p = [_dot_nt(qabs[r], cp) + _dot_nt(qrope[r], krp) for r in grp]
    s_n = [_dot_nt(qabs[r], cn) + _dot_nt(qrope[r], krn) for r in grp]
    m = [jnp.maximum(jnp.max(a, axis=-1, keepdims=True), jnp.max(b, axis=-1, keepdims=True)) for a, b in zip(s_p, s_n)]
    p_p = [jnp.exp2(a - c) for a, c in zip(s_p, m)]
    p_n = [jnp.exp2(a - c) for a, c in zip(s_n, m)]
    l = [jnp.sum(a, axis=-1, keepdims=True) + jnp.sum(b, axis=-1, keepdims=True) for a, b in zip(p_p, p_n)]
    o_lat = jnp.concatenate([(_dot(a.astype(BF16), cp) + _dot(b.astype(BF16), cn)) / c
                             for a, b, c in zip(p_p, p_n, l)], axis=0)
    for h in range(heads):
        o_ref[0, :, h * v_head:(h + 1) * v_head] = _dot(o_lat[h * t:(h + 1) * t].astype(BF16), wv_ref[h])


def _latent_attn(q, c_past, kr_past, c_new, kr_new, wk, wv, *, heads, nope, rope, v_head):
    b, t, _ = q.shape
    past, kv_lora = c_past.shape[1:]
    blk = lambda n, last: pl.BlockSpec((1, n, last), lambda i: (i, 0, 0))
    const = lambda a: pl.BlockSpec(a.shape, lambda i: (0,) * a.ndim)
    kern = functools.partial(_latent_attn_kernel, heads=heads, nope=nope, rope=rope, v_head=v_head)
    return pl.pallas_call(
        kern,
        out_shape=jax.ShapeDtypeStruct((b, t, heads * v_head), F32),
        grid=(b,),
        in_specs=[blk(t, heads * LANES), blk(past, kv_lora), blk(past, rope), blk(t, kv_lora), blk(t, rope),
                  const(wk), const(wv)],
        out_specs=blk(t, heads * v_head),
        compiler_params=_params("arbitrary"),
        name="latent_attn",
    )(q, c_past, kr_past, c_new, kr_new, wk, wv)


def _mlstm_prep_kernel(m_ref, kt_ref, gc_ref, gr_ref, bc_ref, br_ref, pv_ref, kv_ref, b_ref, st_ref, *,
                       heads, dk, dv, chunk, nchunk):
    row = lax.broadcasted_iota(jnp.int32, (chunk, chunk), 0)
    col = lax.broadcasted_iota(jnp.int32, (chunk, chunk), 1)
    causal = col <= row
    tril = causal.astype(F32)
    triu = (row <= col).astype(F32)
    lane = lax.broadcasted_iota(jnp.int32, (chunk, LANES), 1)
    o_k, o_v = heads * dk, 2 * heads * dk

    rows = [slice(c * chunk, (c + 1) * chunk) for c in range(nchunk)]
    gc = [gc_ref[0, r, :] + bc_ref[...] for r in rows]
    gr = [gr_ref[0, c] + br_ref[...] for c in range(nchunk)]
    bcum_c = [_dot(tril, _log_sigmoid(g), HIGHEST) for g in gc]
    bcum_r = [_dot(_log_sigmoid(g), triu, HIGHEST) for g in gr]
    for c in range(nchunk):
        b_ref[0, rows[c], :] = bcum_c[c]
    ch = [(c, h) for c in range(nchunk) for h in range(heads)]
    v = [m_ref[0, rows[c], o_v + h * dv:o_v + (h + 1) * dv].astype(BF16) for c, h in ch]
    qk = [_dot_nt(m_ref[0, rows[c], h * dk:(h + 1) * dk].astype(BF16),
                  (m_ref[0, rows[c], o_k + h * dk:o_k + (h + 1) * dk] * (dk ** -0.5)).astype(BF16)) for c, h in ch]
    li_r = [gr[c][h:h + 1, :] for c, h in ch]
    b_r = [bcum_r[c][heads + h:heads + h + 1, :] for c, h in ch]
    dmat = [jnp.where(causal, bcum_c[c][:, heads + h:heads + h + 1] - b_r[i] + li_r[i], NEG_INF)
            for i, (c, h) in enumerate(ch)]
    mx = [jnp.max(d, axis=-1, keepdims=True) for d in dmat]
    p0 = [s * jnp.exp(d - m) for s, d, m in zip(qk, dmat, mx)]
    for i, (c, h) in enumerate(ch):
        pv_ref[0, rows[c], h * dv:(h + 1) * dv] = _dot(p0[i].astype(BF16), v[i])
    w_r = [jnp.exp(b_r[i][:, chunk - 1:chunk] - b_r[i] + li_r[i] - mx[i][chunk - 1:chunk, :]) * (dk ** -0.5)
           for i in range(len(ch))]
    for i, (c, h) in enumerate(ch):
        wkt = (kt_ref[0, h * dk:(h + 1) * dk, c * chunk:(c + 1) * chunk] * w_r[i]).astype(BF16)
        kv_ref[0, c, h] = _dot(wkt, jnp.concatenate([v[i], jnp.ones_like(v[i])], axis=1))
    psum = [jnp.sum(p, axis=-1, keepdims=True) for p in p0]
    for c in range(nchunk):
        stats = jnp.zeros((chunk, LANES), F32)
        for h in range(heads):
            stats = jnp.where(lane == h, mx[c * heads + h], stats)
            stats = jnp.where(lane == heads + h, psum[c * heads + h], stats)
        st_ref[0, rows[c], :] = stats


def _mlstm_scan_kernel(q_ref, mo_ref, pv_ref, kv_ref, b_ref, st_ref, gout_ref, c0_ref, n0_ref, m0_ref,
                       h_ref, c1_ref, n1_ref, m1_ref, c_sc, m_sc, *, heads, dk, dv, chunk, nchunk):
    t = pl.program_id(1)
    nseq = q_ref.shape[0]

    @pl.when(t == 0)
    def _():
        c_sc[:, :, :, :dv] = c0_ref[...]
        c_sc[:, :, :, dv:] = jnp.broadcast_to(n0_ref[...], c0_ref.shape)
        m_sc[...] = m0_ref[...]

    def body(c, carry):
        rows = pl.ds(pl.multiple_of(c * chunk, chunk), chunk)
        rep = lambda col: jnp.broadcast_to(col, (chunk, dv))
        ps = [(s, h) for s in range(nseq) for h in range(heads)]
        bcum = [b_ref[s, rows, :] for s in range(nseq)]
        stats = [st_ref[s, rows, :] for s in range(nseq)]
        c2 = [c_sc[s, h] for s, h in ps]
        qc = [_dot(q_ref[s, rows, h * dk:(h + 1) * dk].astype(BF16), c2[i].astype(BF16))
              for i, (s, h) in enumerate(ps)]
        mx = [rep(stats[s][:, h:h + 1]) for s, h in ps]
        inter = [rep(bcum[s][:, heads + h:heads + h + 1]) + m_sc[s, h] for s, h in ps]
        m = [jnp.maximum(a, b) for a, b in zip(inter, mx)]
        w_inter = [jnp.exp(a - b) for a, b in zip(inter, m)]
        r = [jnp.exp(a - b) for a, b in zip(mx, m)]
        for i, (s, h) in enumerate(ps):
            decay_end = w_inter[i][chunk - 1:chunk, :]
            f_new = r[i][chunk - 1:chunk, :]
            c_sc[s, h] = (jnp.concatenate([decay_end, decay_end], axis=1) * c2[i]
                          + jnp.concatenate([f_new, f_new], axis=1) * kv_ref[s, c, h])
            m_sc[s, h] = m[i][chunk - 1:chunk, 0:1]
        num = [w_inter[i] * qc[i][:, :dv] + r[i] * pv_ref[s, rows, h * dv:(h + 1) * dv] for i, (s, h) in enumerate(ps)]
        den = [w_inter[i] * qc[i][:, dv:] + r[i] * rep(stats[s][:, heads + h:heads + h + 1])
               for i, (s, h) in enumerate(ps)]
        hh = [a / jnp.maximum(jnp.abs(b), jnp.exp(-c_)) for a, b, c_ in zip(num, den, m)]
        hn = [_rms(hh[i], gout_ref[:, h * dv:(h + 1) * dv]) for i, (s, h) in enumerate(ps)]
        for i, (s, h) in enumerate(ps):
            h_ref[s, rows, h * dv:(h + 1) * dv] = hn[i] * _sigmoid(mo_ref[s, rows, h * dv:(h + 1) * dv])
        return carry

    lax.fori_loop(0, nchunk, body, 0)

    @pl.when(t == pl.num_programs(1) - 1)
    def _():
        c1_ref[...] = c_sc[:, :, :, :dv]
        n1_ref[...] = c_sc[:, :, :, dv:dv + 1]
        m1_ref[...] = m_sc[...]


def _mlstm_prep(m_slab, kt, gates_c, gates_r, bias_c, bias_r, *, chunk, heads, dk, dv):
    b, t, mw = m_slab.shape
    const = lambda a: pl.BlockSpec(a.shape, lambda i, j: (0,) * a.ndim)
    qkv_w = 2 * heads * dk + heads * dv
    tp = min(t, MLSTM_PREP_TC)
    npc = tp // chunk
    tokp = lambda last: pl.BlockSpec((1, tp, last), lambda i, j: (i, j, 0))
    per_chunk = lambda *s: pl.BlockSpec((1, npc, heads) + s, lambda i, j: (i, j, 0) + (0,) * len(s))
    return pl.pallas_call(
        functools.partial(_mlstm_prep_kernel, heads=heads, dk=dk, dv=dv, chunk=chunk, nchunk=npc),
        out_shape=[jax.ShapeDtypeStruct((b, t, heads * dv), F32),
                   jax.ShapeDtypeStruct((b, t // chunk, heads, dk, 2 * dv), F32),
                   jax.ShapeDtypeStruct((b, t, LANES), F32),
                   jax.ShapeDtypeStruct((b, t, LANES), F32)],
        grid=(b, t // tp),
        in_specs=[tokp(qkv_w), pl.BlockSpec((1, heads * dk, tp), lambda i, j: (i, 0, j)), tokp(LANES),
                  pl.BlockSpec((1, npc, SUBLANES, chunk), lambda i, j: (i, j, 0, 0)),
                  const(bias_c), const(bias_r)],
        out_specs=[tokp(heads * dv), per_chunk(dk, 2 * dv), tokp(LANES), tokp(LANES)],
        compiler_params=_params("arbitrary", "arbitrary"),
        name="mlstm_prep",
    )(m_slab, kt, gates_c, gates_r, bias_c, bias_r)


def _mlstm_scan(m_slab, pv, kv, bcum, stats, gout, c0, n0, m0, *, tc, heads, dk, dv):
    b, t, mw = m_slab.shape
    chunk = min(CHUNK, t)
    const = lambda a: pl.BlockSpec(a.shape, lambda i, j: (0,) * a.ndim)
    qkv_w = 2 * heads * dk + heads * dv
    mo_blk, rem = divmod(qkv_w, heads * dv)
    assert rem == 0 and mw == qkv_w + heads * dv
    nchunk = tc // chunk
    ns = math.gcd(b, SCAN_SEQS)
    tok = lambda last, blk=0: pl.BlockSpec((ns, tc, last), lambda i, j: (i, j, blk))
    per_chunk = lambda *s: pl.BlockSpec((ns, nchunk, heads) + s, lambda i, j: (i, j, 0) + (0,) * len(s))
    st = lambda *s: pl.BlockSpec((ns,) + s, lambda i, j: (i,) + (0,) * len(s))
    hm, c1, n1, m1 = pl.pallas_call(
        functools.partial(_mlstm_scan_kernel, heads=heads, dk=dk, dv=dv, chunk=chunk, nchunk=nchunk),
        out_shape=[jax.ShapeDtypeStruct((b, t, heads * dv), F32),
                   jax.ShapeDtypeStruct((b, heads, dk, dv), F32),
                   jax.ShapeDtypeStruct((b, heads, dk, 1), F32),
                   jax.ShapeDtypeStruct((b, heads, 1, 1), F32)],
        grid=(b // ns, t // tc),
        in_specs=[tok(heads * dk),
                  tok(heads * dv, mo_blk),
                  tok(heads * dv), per_chunk(dk, 2 * dv), tok(LANES), tok(LANES),
                  const(gout), st(heads, dk, dv), st(heads, dk, 1), st(heads, 1, 1)],
        out_specs=[tok(heads * dv), st(heads, dk, dv), st(heads, dk, 1), st(heads, 1, 1)],
        scratch_shapes=[pltpu.VMEM((ns, heads, dk, 2 * dv), F32), pltpu.VMEM((ns, heads, 1, 1), F32)],
        compiler_params=_params("arbitrary", "arbitrary"),
        name="mlstm_scan",
    )(m_slab, m_slab, pv, kv, bcum, stats, gout, c0, n0[..., None], m0)
    return hm, c1, n1[..., 0], m1


def _mid_kernel(oa_ref, ob_ref, z_ref, x_ref, gate_ref, wo_ref, shift_ref, scale_ref, g_ref, w2_ref,
                x1_ref, qkv_ref, z2_ref, ab_ref, *, half, conv_ch, d_model):
    z = z_ref[0]
    ma = (oa_ref[0] * _silu(z[:, :half])).astype(BF16)
    mb = (ob_ref[0] * _silu(z[:, half:])).astype(BF16)
    y = _dot(ma, wo_ref[0:half, :]) + _dot(mb, wo_ref[half:, :])
    x1 = x_ref[0] + _per_row(gate_ref[0], x_ref.shape[1]) * y
    x1_ref[0] = x1
    hn = _rms(x1, g_ref[...]) * (1.0 + _per_row(scale_ref[0], x_ref.shape[1])) + _per_row(shift_ref[0], x_ref.shape[1])
    y2 = _dot(hn.astype(BF16), w2_ref[...])
    qkv_ref[0] = y2[:, :conv_ch]
    z2_ref[0] = y2[:, conv_ch:conv_ch + d_model]
    ab_ref[0] = y2[:, conv_ch + d_model:]


def _mid(oa, ob, z, x, gate, wo, shift, scale, g, w2, *, tm, conv_ch):
    b, t, d = x.shape
    half = oa.shape[-1]
    tok = lambda last: pl.BlockSpec((1, tm, last), lambda i, j: (i, j, 0))
    vec = pl.BlockSpec((1,) + gate.shape[1:], lambda i, j: (i, 0, 0))
    const = lambda a: pl.BlockSpec(a.shape, lambda i, j: (0,) * a.ndim)
    kern = functools.partial(_mid_kernel, half=half, conv_ch=conv_ch, d_model=d)
    return pl.pallas_call(
        kern,
        out_shape=[jax.ShapeDtypeStruct((b, t, d), F32), jax.ShapeDtypeStruct((b, t, conv_ch), F32),
                   jax.ShapeDtypeStruct((b, t, d), F32), jax.ShapeDtypeStruct((b, t, LANES), F32)],
        grid=(b, t // tm),
        in_specs=[tok(half), tok(ob.shape[-1]), tok(d), tok(d), vec, const(wo), vec, vec, const(g), const(w2)],
        out_specs=[tok(d), tok(conv_ch), tok(d), tok(LANES)],
        compiler_params=_params("arbitrary", "arbitrary"),
        name="out_a_in_c",
    )(oa, ob, z, x, gate, wo, shift, scale, g, w2)


def _conv_kernel(qkv_ref, past_ref, wc_ref, ab_ref, alog_ref, dtb_ref, act_ref, gb_ref, ext_sc, *,
                 tm, width, heads, dk):
    @pl.when(pl.program_id(1) == 0)
    def _():
        ext_sc[0:SUBLANES, :] = past_ref[0]

    ext_sc[SUBLANES:SUBLANES + tm, :] = qkv_ref[0]
    conv = wc_ref[width - 1:width, :] * ext_sc[SUBLANES:SUBLANES + tm, :]
    for j in range(width - 1):
        s = SUBLANES - (width - 1) + j
        conv = conv + wc_ref[j:j + 1, :] * ext_sc[s:s + tm, :]
    ext_sc[0:SUBLANES, :] = ext_sc[tm:tm + SUBLANES, :]
    act = _silu(conv)
    for h in range(2 * heads):
        xh = act[:, h * dk:(h + 1) * dk]
        xh = xh * lax.rsqrt(jnp.sum(xh * xh, axis=-1, keepdims=True) + EPS)
        if h < heads:
            xh = xh * (dk ** -0.5)
        act_ref[0, :, h * dk:(h + 1) * dk] = xh
    act_ref[0, :, 2 * heads * dk:] = act[:, 2 * heads * dk:]
    ab = ab_ref[0]
    g = -jnp.exp(alog_ref[...]) * _softplus(ab + dtb_ref[...])
    lane = lax.broadcasted_iota(jnp.int32, ab.shape, 1)
    gb_ref[0] = jnp.where(lane < heads, g, _sigmoid(ab))


def _conv(qkv, past8, wc8, ab, alog, dtb, *, tm, width, heads, dk):
    b, t, ch = qkv.shape
    tok = lambda last: pl.BlockSpec((1, tm, last), lambda i, j: (i, j, 0))
    const = lambda a: pl.BlockSpec(a.shape, lambda i, j: (0,) * a.ndim)
    kern = functools.partial(_conv_kernel, tm=tm, width=width, heads=heads, dk=dk)
    return pl.pallas_call(
        kern,
        out_shape=[jax.ShapeDtypeStruct((b, t, ch), F32), jax.ShapeDtypeStruct((b, t, LANES), F32)],
        grid=(b, t // tm),
        in_specs=[tok(ch), pl.BlockSpec((1, SUBLANES, ch), lambda i, j: (i, 0, 0)), const(wc8), tok(LANES),
                  const(alog), const(dtb)],
        out_specs=[tok(ch), tok(LANES)],
        scratch_shapes=[pltpu.VMEM((tm + SUBLANES, ch), F32)],
        compiler_params=_params("arbitrary", "arbitrary"),
        name="conv_gates",
    )(qkv, past8, wc8, ab, alog, dtb)


def _blockdiag(x, group, chunk):
    w = group * chunk
    br = lax.broadcasted_iota(jnp.int32, (w, w), 0) // chunk
    bc = lax.broadcasted_iota(jnp.int32, (w, w), 1) // chunk
    xb = x.astype(BF16)
    return jnp.where(br == bc, jnp.concatenate([xb] * group, axis=0), jnp.zeros((), BF16))


def _unit_lower_inverses_minus_eye(a_list, group, chunk):
    w = group * chunk
    r = lax.broadcasted_iota(jnp.int32, (chunk, w), 0)
    cc = lax.broadcasted_iota(jnp.int32, (chunk, w), 1) % chunk
    es = [-jnp.where((r // 2 == cc // 2) & (r % 2 == 1) & (cc % 2 == 0), a4, 0.0) for a4 in a_list]
    s = 2
    while s < chunk:
        off = (r // (2 * s) == cc // (2 * s)) & (r % (2 * s) >= s) & (cc % (2 * s) < s)
        a_offs = [jnp.where(off, a4, 0.0) for a4 in a_list]
        ps = [a + _dot(a.astype(BF16), _blockdiag(e, group, chunk)) for a, e in zip(a_offs, es)]
        es = [e - (p + _dot(e.astype(BF16), _blockdiag(p, group, chunk))) for e, p in zip(es, ps)]
        s *= 2
    return es


def _gdn_prep_kernel(act_ref, gbc_ref, gbr_ref, w_ref, uv_ref, kd_ref, attn_ref, eg_ref, *,
                     heads, dk, dv, chunk, nchunk, group):
    row = lax.broadcasted_iota(jnp.int32, (chunk, chunk), 0)
    col = lax.broadcasted_iota(jnp.int32, (chunk, chunk), 1)
    incl = col <= row
    strict = col < row
    tril = incl.astype(F32)
    triu = (row <= col).astype(F32)
    lane = lax.broadcasted_iota(jnp.int32, (chunk, LANES), 1)
    o_k, o_v = heads * dk, 2 * heads * dk

    rows = [slice(c * chunk, (c + 1) * chunk) for c in range(nchunk)]
    gbc = [gbc_ref[0, r, :] for r in rows]
    gcum_c = [_dot(tril, g, HIGHEST) for g in gbc]
    gcum_r = [_dot(gbr_ref[0, c], triu, HIGHEST) for c in range(nchunk)]
    for c in range(nchunk):
        eg_ref[0, rows[c], :] = jnp.where(lane < heads, jnp.exp(gcum_c[c]), 0.0)
    ch = [(c, h) for c in range(nchunk) for h in range(heads)]
    k = [act_ref[0, rows[c], o_k + h * dk:o_k + (h + 1) * dk] for c, h in ch]
    kb = [x.astype(BF16) for x in k]
    kk = [_dot_nt(x, x) for x in kb]
    qk = [_dot_nt(act_ref[0, rows[c], h * dk:(h + 1) * dk].astype(BF16), kb[i]) for i, (c, h) in enumerate(ch)]
    g_c = [gcum_c[c][:, h:h + 1] for c, h in ch]
    beta = [gbc[c][:, heads + h:heads + h + 1] for c, h in ch]
    decay = [jnp.exp(jnp.where(incl, g_c[i] - gcum_r[c][h:h + 1, :], NEG_INF)) for i, (c, h) in enumerate(ch)]
    a_blk = [jnp.where(strict, beta[i] * kk[i] * decay[i], 0.0) for i in range(len(ch))]
    for i, (c, h) in enumerate(ch):
        attn_ref[0, rows[c], h * chunk:(h + 1) * chunk] = (qk[i] * decay[i]).astype(BF16)
        kd_ref[0, rows[c], h * dk:(h + 1) * dk] = (k[i] * jnp.exp(g_c[i][chunk - 1:chunk, :] - g_c[i])).astype(BF16)
    rhs_blk = [jnp.concatenate([beta[i] * act_ref[0, rows[c], o_v + h * dv:o_v + (h + 1) * dv],
                                (beta[i] * jnp.exp(g_c[i])) * k[i]], axis=1) for i, (c, h) in enumerate(ch)]
    problems = [(c, g0) for c in range(nchunk) for g0 in range(0, heads, group)]
    a_list = [jnp.concatenate(a_blk[c * heads + g0:c * heads + g0 + group], axis=1) for c, g0 in problems]
    rhs_list = [jnp.concatenate(rhs_blk[c * heads + g0:c * heads + g0 + group], axis=0) for c, g0 in problems]
    e_list = _unit_lower_inverses_minus_eye(a_list, group, chunk)
    sols = [rhs + _dot(_blockdiag(e, group, chunk), rhs.astype(BF16)) for e, rhs in zip(e_list, rhs_list)]
    for (c, g0), sol in zip(problems, sols):
        for i, h in enumerate(range(g0, g0 + group)):
            uv_ref[0, rows[c], h * dv:(h + 1) * dv] = sol[i * chunk:(i + 1) * chunk, :dv]
            w_ref[0, rows[c], h * dk:(h + 1) * dk] = sol[i * chunk:(i + 1) * chunk, dv:].astype(BF16)


def _gdn_prep(act, gb_c, gb_r, *, chunk, tc, heads, dk, dv):
    b, t, _ = act.shape
    tc = min(tc, t)
    nchunk = tc // chunk
    group = (2 * LANES) // chunk
    kern = functools.partial(_gdn_prep_kernel, heads=heads, dk=dk, dv=dv, chunk=chunk, nchunk=nchunk, group=group)
    tok = lambda last: pl.BlockSpec((1, tc, last), lambda i, j: (i, j, 0))
    return pl.pallas_call(
        kern,
        out_shape=[jax.ShapeDtypeStruct((b, t, heads * dk), BF16), jax.ShapeDtypeStruct((b, t, heads * dv), F32),
                   jax.ShapeDtypeStruct((b, t, heads * dk), BF16), jax.ShapeDtypeStruct((b, t, heads * chunk), BF16),
                   jax.ShapeDtypeStruct((b, t, LANES), F32)],
        grid=(b, t // tc),
        in_specs=[tok(act.shape[-1]), tok(LANES),
                  pl.BlockSpec((1, nchunk, 2 * SUBLANES, chunk), lambda i, j: (i, j, 0, 0))],
        out_specs=[tok(heads * dk), tok(heads * dv), tok(heads * dk), tok(heads * chunk), tok(LANES)],
        compiler_params=_params("arbitrary", "arbitrary"),
        name="gdn_prep",
    )(act, gb_c, gb_r)


def _gdn_scan_kernel(q_ref, w_ref, uv_ref, kd_ref, attn_ref, eg_ref, gout_ref, s0_ref, o_ref, s1_ref, s_sc, *,
                     heads, dk, dv, chunk, nchunk):
    t = pl.program_id(1)
    nseq = q_ref.shape[0]

    @pl.when(t == 0)
    def _():
        s_sc[...] = s0_ref[...]

    def body(c, carry):
        rows = pl.ds(pl.multiple_of(c * chunk, chunk), chunk)
        ps = [(s, h) for s in range(nseq) for h in range(heads)]
        eg = [eg_ref[s, rows, :] for s in range(nseq)]
        s0 = [s_sc[s, h] for s, h in ps]
        s0b = [x.astype(BF16) for x in s0]
        ws = [_dot(w_ref[s, rows, h * dk:(h + 1) * dk], s0b[i]) for i, (s, h) in enumerate(ps)]
        ub = [(uv_ref[s, rows, h * dv:(h + 1) * dv] - ws[i]).astype(BF16) for i, (s, h) in enumerate(ps)]
        eg_h = [jnp.broadcast_to(eg[s][:, h:h + 1], (chunk, dv)) for s, h in ps]
        ku = [_dot_tn(kd_ref[s, rows, h * dk:(h + 1) * dk], ub[i]) for i, (s, h) in enumerate(ps)]
        for i, (s, h) in enumerate(ps):
            s_sc[s, h] = eg_h[i][chunk - 1:chunk, :] * s0[i] + ku[i]
        qs = [_dot(q_ref[s, rows, h * dk:(h + 1) * dk].astype(BF16), s0b[i]) for i, (s, h) in enumerate(ps)]
        au = [_dot(attn_ref[s, rows, h * chunk:(h + 1) * chunk], ub[i]) for i, (s, h) in enumerate(ps)]
        on = [_rms(eg_h[i] * qs[i] + au[i], gout_ref[...]) for i in range(len(ps))]
        for i, (s, h) in enumerate(ps):
            o_ref[s, rows, h * dv:(h + 1) * dv] = on[i]
        return carry

    lax.fori_loop(0, nchunk, body, 0)

    @pl.when(t == pl.num_programs(1) - 1)
    def _():
        s1_ref[...] = s_sc[...]


def _gdn_scan(act, w, uv, kd, attn, eg, gout, s0, *, tc, heads, dk, dv):
    b, t, _ = act.shape
    chunk = min(CHUNK, t)
    nchunk = tc // chunk
    kern = functools.partial(_gdn_scan_kernel, heads=heads, dk=dk, dv=dv, chunk=chunk, nchunk=nchunk)
    ns = math.gcd(b, SCAN_SEQS)
    tok = lambda last: pl.BlockSpec((ns, tc, last), lambda i, j: (i, j, 0))
    state = pl.BlockSpec((ns, heads, dk, dv), lambda i, j: (i, 0, 0, 0))
    return pl.pallas_call(
        kern,
        out_shape=[jax.ShapeDtypeStruct((b, t, heads * dv), F32), jax.ShapeDtypeStruct(s0.shape, F32)],
        grid=(b // ns, t // tc),
        in_specs=[tok(heads * dk),
                  tok(heads * dk), tok(heads * dv), tok(heads * dk), tok(heads * chunk), tok(LANES),
                  pl.BlockSpec(gout.shape, lambda i, j: (0, 0)), state],
        out_specs=[tok(heads * dv), state],
        scratch_shapes=[pltpu.VMEM((ns, heads, dk, dv), F32)],
        compiler_params=_params("arbitrary", "arbitrary"),
        name="gdn_scan",
    )(act, w, uv, kd, attn, eg, gout, s0)


def _final_kernel(o_ref, z_ref, x_ref, gate_ref, wo_ref, g_ref, y_ref):
    mixed = (o_ref[0] * _silu(z_ref[0])).astype(BF16)
    x2 = x_ref[0] + _per_row(gate_ref[0], x_ref.shape[1]) * _dot(mixed, wo_ref[...])
    y_ref[0] = _rms(x2, g_ref[...])


def _final(o, z, x, gate, wo, g, *, tm):
    b, t, d = x.shape
    tok = lambda last: pl.BlockSpec((1, tm, last), lambda i, j: (i, j, 0))
    const = lambda a: pl.BlockSpec(a.shape, lambda i, j: (0,) * a.ndim)
    return pl.pallas_call(
        _final_kernel,
        out_shape=jax.ShapeDtypeStruct((b, t, d), F32),
        grid=(b, t // tm),
        in_specs=[tok(o.shape[-1]), tok(d), tok(d), pl.BlockSpec((1,) + gate.shape[1:], lambda i, j: (i, 0, 0)),
                  const(wo), const(g)],
        out_specs=tok(d),
        compiler_params=_params("arbitrary", "arbitrary"),
        name="out_c_final",
    )(o, z, x, gate, wo, g)


def _pad_lanes(w, width=LANES, at=0):
    out = jnp.zeros(w.shape[:-1] + (width,), w.dtype)
    return out.at[..., at:at + w.shape[-1]].set(w)


def _rot_half_cols(w):
    r = w.shape[-1] // 2
    return jnp.concatenate([-w[..., r:], w[..., :r]], axis=-1)


def _rope_tables(pos, rope, scale):
    freqs = jnp.exp(jnp.arange(0, rope, 2, dtype=F32) * (-math.log(ROPE_BASE) / rope))
    ang = pos.astype(F32)[:, None] * freqs[None, :]
    cos = jnp.concatenate([jnp.cos(ang), jnp.cos(ang)], axis=-1)
    sin = jnp.concatenate([jnp.sin(ang), jnp.sin(ang)], axis=-1)
    half = LANES // 2
    cosk = _pad_lanes(cos, at=half)
    sink = _pad_lanes(sin, at=half)
    ones = _pad_lanes(jnp.ones((pos.shape[0], half), F32))
    return jnp.concatenate([(cosk + ones) * scale, sink * scale, cosk, sink], axis=-1)


def _tokens_on_lanes(a, chunk, rows):
    b, t = a.shape[:2]
    return a[..., :rows].reshape(b, t // chunk, chunk, rows).transpose(0, 1, 3, 2)


def kernel(x_prompt, x_sample, c_prompt, c_sample, cache_kv_latent, cache_k_rope, state_mlstm_C, state_mlstm_n, state_mlstm_m, state_gdn_S, state_gdn_conv, a_w_ada, a_b_ada, a_g_norm, a_w_in, a_g_q_a, a_w_q_b, a_g_kv_a, a_w_kv_b, a_b_i, a_b_f, a_g_out, a_w_out, c_w_ada, c_b_ada, c_g_norm, c_w_in, c_w_conv, c_a_log, c_dt_bias, c_g_out, c_w_out, g_final):
    d = x_prompt.shape[-1]
    q_lora, heads, qk = a_w_q_b.shape
    kv_lora = a_w_kv_b.shape[0]
    rope = cache_k_rope.shape[-1]
    nope = qk - rope
    v_head = a_w_kv_b.shape[2] - nope
    m_heads, m_dv = a_g_out.shape
    m_dk = state_mlstm_C.shape[2]
    g_heads = c_a_log.shape[0]
    g_dk, g_dv = state_gdn_S.shape[2:]
    width = c_w_conv.shape[0]
    conv_ch = c_w_conv.shape[1]
    assert nope + rope <= LANES and nope == LANES // 2 and 2 * m_heads <= SUBLANES and 2 * g_heads <= 2 * SUBLANES

    sizes = (q_lora, kv_lora, rope, m_heads * m_dk, m_heads * m_dk, m_heads * m_dv, m_heads, m_heads,
             m_heads * m_dv, heads * v_head + m_heads * m_dv)
    offs = [0]
    for s in sizes:
        offs.append(offs[-1] + s)
    w_qa, w_c, w_kr, w_mq, w_mk, w_mv, w_mi, w_mf, w_mo, w_z = [a_w_in[:, offs[i]:offs[i + 1]] for i in range(10)]
    half = LANES // 2
    w1 = jnp.concatenate([w_qa, w_c, _pad_lanes(w_kr, at=half), _pad_lanes(_rot_half_cols(w_kr), at=half),
                          w_mq, w_mk, w_mv, w_mo, w_z, _pad_lanes(jnp.concatenate([w_mi, w_mf], axis=1))],
                         axis=1).astype(BF16)
    m_width = 2 * m_heads * m_dk + 2 * m_heads * m_dv
    wq_rope = a_w_q_b[..., nope:]
    wq_main = _pad_lanes(a_w_q_b).reshape(q_lora, heads * LANES)
    wq_rot = _pad_lanes(_rot_half_cols(wq_rope), at=nope).reshape(q_lora, heads * LANES)
    wq = jnp.concatenate([wq_main, wq_rot], axis=1).astype(BF16)
    wkv = jnp.concatenate([_pad_lanes(a_w_kv_b[..., :nope]).reshape(kv_lora, heads * LANES),
                           a_w_kv_b[..., nope:].reshape(kv_lora, heads * v_head)], axis=1).astype(BF16)
    wk_abs = a_w_kv_b[..., :nope].transpose(1, 0, 2).astype(BF16)
    wv_abs = a_w_kv_b[..., nope:].transpose(1, 0, 2).astype(BF16)
    wo_a = a_w_out.astype(BF16)
    csz = (conv_ch, g_heads, g_heads, g_heads * g_dv)
    w_qkv, w_a, w_b, w_zc = [c_w_in[:, sum(csz[:i]):sum(csz[:i + 1])] for i in range(4)]
    w2 = jnp.concatenate([w_qkv, w_zc, _pad_lanes(jnp.concatenate([w_a, w_b], axis=1))], axis=1).astype(BF16)
    wo_c = c_w_out.astype(BF16)
    wc8 = jnp.zeros((SUBLANES, conv_ch), F32).at[:width].set(c_w_conv)
    row = lambda a: a.reshape(1, -1).astype(F32)
    bias_c = _pad_lanes(jnp.concatenate([a_b_i, a_b_f]).reshape(1, -1))
    bias_r = jnp.zeros((SUBLANES, 1), F32).at[:2 * m_heads, 0].set(jnp.concatenate([a_b_i, a_b_f]))
    alog = _pad_lanes(c_a_log.reshape(1, -1))
    dtb = _pad_lanes(c_dt_bias.reshape(1, -1))

    bp, bs = c_prompt.shape[0], c_sample.shape[0]
    c_all = jnp.concatenate([c_prompt, c_sample], axis=0)
    pad = (-c_all.shape[0]) % SUBLANES
    c_all = jnp.pad(c_all, ((0, pad), (0, 0)))
    mod_a = _adaln(c_all, a_w_ada, a_b_ada)
    mod_c = _adaln(c_all, c_w_ada, c_b_ada)

    def mods(mod, lo, hi):
        return [mod[lo:hi, i * d:(i + 1) * d][:, None, :] for i in range(3)]

    def run(x, mod_lo, mod_hi, c_past, kr_past, c0, n0, m0, conv0, s0):
        b, t, _ = x.shape
        past = 0 if c_past is None else c_past.shape[1]
        chunk = min(CHUNK, t)
        tc = min(t, 512)
        expand = c_past is None
        nb = 1 if expand else math.gcd(b, max(1, PROJ_TM // t))
        tm = min(nb * t, PROJ_TM)
        grp = lambda a: a.reshape((b // nb, nb * a.shape[1]) + a.shape[2:])
        ungrp = lambda a: a.reshape((b, a.shape[1] // nb) + a.shape[2:])
        shift_a, scale_a, gate_a = [grp(m) for m in mods(mod_a, mod_lo, mod_hi)]
        shift_c, scale_c, gate_c = [grp(m) for m in mods(mod_c, mod_lo, mod_hi)]
        tab = _rope_tables(past + jnp.arange(t, dtype=jnp.int32), rope, qk ** -0.5 * math.log2(math.e))
        outs = _in_a(grp(x), shift_a, scale_a, row(a_g_norm), w1, row(a_g_q_a), wq, row(a_g_kv_a), wkv,
                     jnp.tile(tab, (nb, 1)),
                     tm=tm, heads=heads, q_lora=q_lora, kv_lora=kv_lora, rope=rope, m_width=m_width,
                     mk_cols=(m_heads * m_dk, 2 * m_heads * m_dk), v_head=v_head, expand_kv=expand)
        q, c_new, kr_new, m_slab, z, gates = [ungrp(a) for a in outs[:6]]
        kt = None
        if expand:
            k_all, vt, kt = outs[6], outs[7], outs[10]
            qn = outs[8][..., :heads].transpose(0, 2, 1)
            tiles = dict(heads=heads, v_head=v_head, tq=min(t, FLASH_TQ), tk=min(t, FLASH_TK))
            o_fast, row_sums = _flash(q, k_all, vt, bound=(qn, outs[9]), **tiles)
            o_mla = lax.cond(jnp.min(row_sums) >= FLASH_ROW_SUM_MIN,
                             lambda: o_fast, lambda: _flash(q, k_all, vt, **tiles))
        else:
            o_mla = _latent_attn(q, c_past, kr_past, c_new, kr_new, wk_abs, wv_abs,
                                 heads=heads, nope=nope, rope=rope, v_head=v_head)
        fold = t == chunk and b > 1
        flat = (lambda a: a.reshape((1, b * a.shape[1]) + a.shape[2:])) if fold else (lambda a: a)
        unflat = (lambda a: a.reshape((b, a.shape[1] // b) + a.shape[2:])) if fold else (lambda a: a)
        m_flat = flat(m_slab)
        if kt is None:
            kt = m_flat[..., m_heads * m_dk:2 * m_heads * m_dk].transpose(0, 2, 1)
        pv, kvs, bcum, stats = [unflat(a) for a in _mlstm_prep(
            m_flat, kt, flat(gates), flat(_tokens_on_lanes(gates, chunk, SUBLANES)), bias_c, bias_r,
            chunk=chunk, heads=m_heads, dk=m_dk, dv=m_dv)]
        hm, c1, n1, m1 = _mlstm_scan(m_slab, pv, kvs, bcum, stats, row(a_g_out), c0, n0, m0.reshape(b, m_heads, 1, 1),
                                     tc=tc, heads=m_heads, dk=m_dk, dv=m_dv)
        x1, qkv, zc, ab = [ungrp(a) for a in _mid(grp(o_mla), grp(hm), grp(z), grp(x), gate_a, wo_a, shift_c, scale_c,
                                                  row(c_g_norm), w2, tm=tm, conv_ch=conv_ch)]
        past8 = jnp.pad(conv0, ((0, 0), (SUBLANES - (width - 1), 0), (0, 0)))
        act, gb = _conv(qkv, past8, wc8, ab, alog, dtb, tm=min(t, PROJ_TM), width=width, heads=g_heads, dk=g_dk)
        w, uv, kd, attn, eg = [unflat(a) for a in _gdn_prep(
            flat(act), flat(gb), flat(_tokens_on_lanes(gb, chunk, 2 * SUBLANES)),
            chunk=chunk, tc=GDN_PREP_TC, heads=g_heads, dk=g_dk, dv=g_dv)]
        o_gdn, s1 = _gdn_scan(act, w, uv, kd, attn, eg, row(c_g_out), s0, tc=min(t, GDN_SCAN_TC),
                              heads=g_heads, dk=g_dk, dv=g_dv)
        y = ungrp(_final(grp(o_gdn), grp(zc), grp(x1), gate_c, wo_c, row(g_final), tm=tm))
        conv1 = jnp.concatenate([conv0, qkv], axis=1)[:, t:] if t < width - 1 else qkv[:, t - (width - 1):]
        return (y, c_new, kr_new, c1, n1, m1.reshape(b, m_heads), conv1, s1)

    dt = x_prompt.dtype
    (y_p, p_kv, p_kr, p_c, p_n, p_m, p_conv, p_s) = run(
        x_prompt, 0, bp, None, None,
        jnp.zeros((bp, m_heads, m_dk, m_dv), dt), jnp.zeros((bp, m_heads, m_dk), dt), jnp.zeros((bp, m_heads), dt),
        jnp.zeros((bp, width - 1, conv_ch), dt), jnp.zeros((bp, g_heads, g_dk, g_dv), dt))
    (y_s, s_kv, s_kr, s_c, s_n, s_m, s_conv, s_s) = run(
        x_sample, bp, bp + bs, cache_kv_latent, cache_k_rope, state_mlstm_C, state_mlstm_n, state_mlstm_m,
        state_gdn_conv, state_gdn_S)
    return (y_p, y_s, p_kv, p_kr, p_c, p_n, p_m, p_s, p_conv,
            s_kv, s_kr, s_c, s_n, s_m, s_s, s_conv)
```

```python
import functools
import math

import jax
import jax.numpy as jnp
from jax import lax
from jax.experimental import pallas as pl
from jax.experimental.pallas import tpu as pltpu

F32 = jnp.float32
BF16 = jnp.bfloat16
HIGHEST = lax.Precision.HIGHEST

CHUNK = 64
EPS = 1e-6
ROPE_BASE = 10000.0
LANES = 128
SUBLANES = 8
VMEM_LIMIT = 56 * 1024 * 1024
NEG_INF = float("-inf")
SCAN_SEQS = 4
GDN_SCAN_TC = 256
PROJ_TM = 512
FINAL_TM = 1024
FLASH_TQ = 1024
FLASH_TK = 1024
LATENT_ROW_GROUPS = 2
FLASH_HEAD_GROUP = 4
FLASH_BOUND_SLACK = 1.0 + 2.0 ** -6
FLASH_ROW_SUM_MIN = 2.0 ** -100
MLSTM_PREP_TC = 512
GDN_PREP_TC = 512


def _params(*sem):
    return pltpu.CompilerParams(dimension_semantics=sem, vmem_limit_bytes=VMEM_LIMIT)


def _dot(a, b, precision=None):
    return jnp.dot(a, b, preferred_element_type=F32, precision=precision)


def _dot_nt(a, b):
    return lax.dot_general(a, b, (((1,), (1,)), ((), ())), preferred_element_type=F32)


def _dot_tn(a, b):
    return lax.dot_general(a, b, (((0,), (0,)), ((), ())), preferred_element_type=F32)


def _rms(x, g):
    return x * lax.rsqrt(jnp.mean(x * x, axis=-1, keepdims=True) + EPS) * g


def _per_row(m, rows):
    n, d = m.shape
    if n == 1:
        return m
    return jnp.concatenate([jnp.broadcast_to(m[i:i + 1], (rows // n, d)) for i in range(n)], axis=0)


def _sigmoid(x):
    return 0.5 * jnp.tanh(0.5 * x) + 0.5


def _silu(x):
    return x * _sigmoid(x)


def _softplus(x):
    return jnp.maximum(x, 0.0) + jnp.log1p(jnp.exp(-jnp.abs(x)))


def _log_sigmoid(x):
    return -_softplus(-x)


def _adaln_kernel(c_ref, w_ref, b_ref, o_ref):
    o_ref[...] = _dot(_silu(c_ref[...]), w_ref[...], HIGHEST) + b_ref[...]


def _adaln(c, w, b):
    n, d = c.shape
    d3 = w.shape[1]
    return pl.pallas_call(
        _adaln_kernel,
        out_shape=jax.ShapeDtypeStruct((n, d3), F32),
        grid=(d3 // d,),
        in_specs=[pl.BlockSpec((n, d), lambda j: (0, 0)),
                  pl.BlockSpec((d, d), lambda j: (0, j)),
                  pl.BlockSpec((1, d), lambda j: (0, j))],
        out_specs=pl.BlockSpec((n, d), lambda j: (0, j)),
        compiler_params=_params("arbitrary"),
        name="adaln",
    )(c, w, b.reshape(1, d3))


def _in_a_kernel(x_ref, shift_ref, scale_ref, g_ref, w1_ref, gq_ref, wq_ref, gkv_ref, wkv_ref, tab_ref,
                 q_ref, c_ref, kr_ref, m_ref, z_ref, gt_ref, *kv_refs,
                 heads, q_lora, kv_lora, rope, m_width, mk_cols, d_model, v_head, expand_kv):
    x = x_ref[0]
    hn = _rms(x, g_ref[...]) * (1.0 + _per_row(scale_ref[0], x_ref.shape[1])) + _per_row(shift_ref[0], x_ref.shape[1])
    y = _dot(hn.astype(BF16), w1_ref[...])
    o = 0
    qa = y[:, o:o + q_lora]; o += q_lora
    cl = y[:, o:o + kv_lora]; o += kv_lora
    kr1 = y[:, o:o + LANES]; o += LANES
    kr2 = y[:, o:o + LANES]; o += LANES
    m_ref[0] = y[:, o:o + m_width]
    mk = y[:, o + mk_cols[0]:o + mk_cols[1]]
    o += m_width
    z_ref[0] = y[:, o:o + d_model]; o += d_model
    gt_ref[0] = y[:, o:o + LANES]

    tab = tab_ref[...]
    cosq, sinq = tab[:, 0:LANES], tab[:, LANES:2 * LANES]
    cosk, sink = tab[:, 2 * LANES:3 * LANES], tab[:, 3 * LANES:4 * LANES]

    def sq_norm(xb):
        xf = xb.astype(F32)
        return jnp.sum(xf * xf, axis=-1, keepdims=True)

    lane = lax.broadcasted_iota(jnp.int32, (x.shape[0], LANES), 1)
    qq = _dot(_rms(qa, gq_ref[...]).astype(BF16), wq_ref[...])
    hw = heads * LANES
    qn2 = jnp.zeros((x.shape[0], LANES), F32)
    for h in range(heads):
        sl = slice(h * LANES, (h + 1) * LANES)
        qb = (qq[:, sl] * cosq + qq[:, hw + h * LANES:hw + (h + 1) * LANES] * sinq).astype(BF16)
        q_ref[0, :, sl] = qb
        if expand_kv:
            qn2 = jnp.where(lane == h, sq_norm(qb), qn2)

    cn = _rms(cl, gkv_ref[...])
    c_ref[0] = cn
    kr = kr1 * cosk + kr2 * sink
    kr_ref[0] = kr[:, LANES // 2:LANES // 2 + rope]
    if expand_kv:
        k_ref, vt_ref, qn_ref, kmax_ref, kt_ref = kv_refs
        kv = _dot(cn.astype(BF16), wkv_ref[...])
        kn2 = jnp.zeros((x.shape[0], LANES), F32)
        for h in range(heads):
            sl = slice(h * LANES, (h + 1) * LANES)
            kb = (kv[:, sl] + kr).astype(BF16)
            k_ref[0, :, sl] = kb
            kn2 = jnp.where(lane == h, sq_norm(kb), kn2)
        vt_ref[0] = kv[:, hw:hw + heads * v_head].T.astype(BF16)
        kt_ref[0] = mk.T
        qn_ref[0] = jnp.sqrt(qn2)
        kmax_ref[0, 0] = jnp.sqrt(jnp.max(kn2, axis=0, keepdims=True))


def _in_a(x, shift, scale, g, w1, gq, wq, gkv, wkv, tab, *, tm, heads, q_lora, kv_lora, rope, m_width,
          mk_cols, v_head, expand_kv):
    b, t, d = x.shape
    grid = (b, t // tm)
    tok = lambda last: pl.BlockSpec((1, tm, last), lambda i, j: (i, j, 0))
    tok_t = lambda rows: pl.BlockSpec((1, rows, tm), lambda i, j: (i, 0, j))
    const = lambda a: pl.BlockSpec(a.shape, lambda i, j: (0,) * a.ndim)
    out_shape = [jax.ShapeDtypeStruct((b, t, heads * LANES), BF16),
                 jax.ShapeDtypeStruct((b, t, kv_lora), F32),
                 jax.ShapeDtypeStruct((b, t, rope), F32),
                 jax.ShapeDtypeStruct((b, t, m_width), F32),
                 jax.ShapeDtypeStruct((b, t, d), F32),
                 jax.ShapeDtypeStruct((b, t, LANES), F32)]
    out_specs = [tok(heads * LANES), tok(kv_lora), tok(rope), tok(m_width), tok(d), tok(LANES)]
    if expand_kv:
        out_shape += [jax.ShapeDtypeStruct((b, t, heads * LANES), BF16),
                      jax.ShapeDtypeStruct((b, heads * v_head, t), BF16),
                      jax.ShapeDtypeStruct((b, t, LANES), F32),
                      jax.ShapeDtypeStruct((b, t // tm, 1, LANES), F32),
                      jax.ShapeDtypeStruct((b, mk_cols[1] - mk_cols[0], t), F32)]
        out_specs += [tok(heads * LANES), tok_t(heads * v_head), tok(LANES),
                      pl.BlockSpec((1, 1, 1, LANES), lambda i, j: (i, j, 0, 0)), tok_t(mk_cols[1] - mk_cols[0])]
    kern = functools.partial(_in_a_kernel, heads=heads, q_lora=q_lora, kv_lora=kv_lora, rope=rope,
                             m_width=m_width, mk_cols=mk_cols, d_model=d, v_head=v_head, expand_kv=expand_kv)
    return pl.pallas_call(
        kern, out_shape=out_shape, grid=grid,
        in_specs=[tok(d),
                  pl.BlockSpec((1,) + shift.shape[1:], lambda i, j: (i, 0, 0)),
                  pl.BlockSpec((1,) + scale.shape[1:], lambda i, j: (i, 0, 0)),
                  const(g), const(w1), const(gq), const(wq), const(gkv), const(wkv),
                  pl.BlockSpec((tm, 4 * LANES), lambda i, j: (j, 0))],
        out_specs=out_specs,
        compiler_params=_params("arbitrary", "arbitrary"),
        name="in_proj_a",
    )(x, shift, scale, g, w1, gq, wq, gkv, wkv, tab)


def _flash_kernel(qi_ref, ki_ref, q_ref, k_ref, vt_ref, o_ref, m_sc, l_sc, acc_sc, *,
                  heads, v_head, tq, tk, chunk):
    step_id = pl.program_id(1)
    qi = qi_ref[step_id]
    ki = ki_ref[step_id]

    @pl.when(ki == 0)
    def _():
        m_sc[...] = jnp.full(m_sc.shape, NEG_INF, F32)
        l_sc[...] = jnp.zeros(l_sc.shape, F32)
        acc_sc[...] = jnp.zeros(acc_sc.shape, F32)

    def step(masked):
        if masked:
            kc = (ki * tk + lax.broadcasted_iota(jnp.int32, (tk, tq), 0)) // chunk
            qc = (qi * tq + lax.broadcasted_iota(jnp.int32, (tk, tq), 1)) // chunk
            mask = kc <= qc
        for h in range(heads):
            qh = q_ref[0, :, h * LANES:(h + 1) * LANES]
            kh = k_ref[0, :, h * LANES:(h + 1) * LANES]
            vth = vt_ref[0, h * v_head:(h + 1) * v_head, :]
            rows = slice(h * v_head, (h + 1) * v_head)
            st = _dot_nt(kh, qh)
            if masked:
                st = jnp.where(mask, st, NEG_INF)
            m_prev = m_sc[h]
            m_new = jnp.maximum(m_prev, jnp.max(st, axis=0, keepdims=True))
            alpha = jnp.exp2(m_prev - m_new)
            p = jnp.exp2(st - m_new)
            l_sc[h] = alpha * l_sc[h] + jnp.sum(p, axis=0, keepdims=True)
            acc_sc[rows, :] = alpha * acc_sc[rows, :] + _dot(vth, p.astype(BF16))
            m_sc[h] = m_new

    full = (ki + 1) * tk <= qi * tq + chunk

    @pl.when(full)
    def _():
        step(False)

    @pl.when(jnp.logical_not(full))
    def _():
        step(True)

    @pl.when(ki == ((qi + 1) * tq - 1) // tk)
    def _():
        for h in range(heads):
            rows = slice(h * v_head, (h + 1) * v_head)
            acc_sc[rows, :] = acc_sc[rows, :] / l_sc[h]
        o_ref[0] = acc_sc[...].T


def _flash_bound_kernel(qi_ref, ki_ref, q_ref, k_ref, vt_ref, qn_ref, kmax_ref, o_ref, l_ref,
                        mb_sc, l_sc, acc_sc, *, heads, v_head, tq, tk, chunk):
    step_id = pl.program_id(1)
    qi = qi_ref[step_id]
    ki = ki_ref[step_id]

    @pl.when(ki == 0)
    def _():
        kmax = jnp.max(kmax_ref[0], axis=0) * FLASH_BOUND_SLACK
        for h in range(heads):
            mb_sc[h:h + 1, :] = qn_ref[0, h:h + 1, :] * kmax[:, h:h + 1]
        l_sc[...] = jnp.zeros(l_sc.shape, F32)
        acc_sc[...] = jnp.zeros(acc_sc.shape, F32)

    def step(masked):
        if masked:
            kc = (ki * tk + lax.broadcasted_iota(jnp.int32, (tk, tq), 0)) // chunk
            qc = (qi * tq + lax.broadcasted_iota(jnp.int32, (tk, tq), 1)) // chunk
            mask = kc <= qc
        for h0 in range(0, heads, FLASH_HEAD_GROUP):
            hs = range(h0, h0 + FLASH_HEAD_GROUP)
            st = [_dot_nt(k_ref[0, :, h * LANES:(h + 1) * LANES], q_ref[0, :, h * LANES:(h + 1) * LANES])
                  for h in hs]
            if masked:
                st = [jnp.where(mask, s, NEG_INF) for s in st]
            p = [jnp.exp2(s - mb_sc[h:h + 1, :]) for s, h in zip(st, hs)]
            for x, h in zip(p, hs):
                l_sc[h:h + 1, :] += jnp.sum(x, axis=0, keepdims=True)
            pv = [_dot(vt_ref[0, h * v_head:(h + 1) * v_head, :], x.astype(BF16)) for x, h in zip(p, hs)]
            for x, h in zip(pv, hs):
                acc_sc[h * v_head:(h + 1) * v_head, :] += x

    def diagonal_step():
        hq = tq // 2
        kc = lax.broadcasted_iota(jnp.int32, (hq, hq), 0) // chunk
        qc = lax.broadcasted_iota(jnp.int32, (hq, hq), 1) // chunk
        mask = kc <= qc
        for h0 in range(0, heads, FLASH_HEAD_GROUP):
            ps = [(h, kb, qb) for h in range(h0, h0 + FLASH_HEAD_GROUP) for kb, qb in ((0, 0), (0, 1), (1, 1))]
            ksl = lambda kb: slice(kb * hq, (kb + 1) * hq)
            st = [_dot_nt(k_ref[0, ksl(kb), h * LANES:(h + 1) * LANES], q_ref[0, ksl(qb), h * LANES:(h + 1) * LANES])
                  for h, kb, qb in ps]
            st = [jnp.where(mask, s, NEG_INF) if kb == qb else s for s, (h, kb, qb) in zip(st, ps)]
            p = [jnp.exp2(s - mb_sc[h:h + 1, ksl(qb)]) for s, (h, kb, qb) in zip(st, ps)]
            for x, (h, kb, qb) in zip(p, ps):
                l_sc[h:h + 1, ksl(qb)] += jnp.sum(x, axis=0, keepdims=True)
            pv = [_dot(vt_ref[0, h * v_head:(h + 1) * v_head, ksl(kb)], x.astype(BF16)) for x, (h, kb, qb) in zip(p, ps)]
            for x, (h, kb, qb) in zip(pv, ps):
                acc_sc[h * v_head:(h + 1) * v_head, ksl(qb)] += x

    full = (ki + 1) * tk <= qi * tq + chunk

    @pl.when(full)
    def _():
        step(False)

    @pl.when(jnp.logical_not(full))
    def _():
        if tq == tk and tq % (2 * LANES) == 0 and (tq // 2) % chunk == 0:
            diagonal_step()
        else:
            step(True)

    @pl.when(ki == ((qi + 1) * tq - 1) // tk)
    def _():
        for h in range(heads):
            rows = slice(h * v_head, (h + 1) * v_head)
            acc_sc[rows, :] = acc_sc[rows, :] / l_sc[h:h + 1, :]
        o_ref[0] = acc_sc[...].T
        l_ref[0] = l_sc[...]


def _flash(q, k, vt, *, heads, v_head, tq, tk, bound=None):
    b, t, _ = q.shape
    pairs = [(i, j) for i in range(t // tq) for j in range(((i + 1) * tq - 1) // tk + 1)]
    qi_tab = jnp.asarray([p[0] for p in pairs], jnp.int32)
    ki_tab = jnp.asarray([p[1] for p in pairs], jnp.int32)
    in_specs = [pl.BlockSpec((1, tq, heads * LANES), lambda i, s, qt, kt: (i, qt[s], 0)),
                pl.BlockSpec((1, tk, heads * LANES), lambda i, s, qt, kt: (i, kt[s], 0)),
                pl.BlockSpec((1, heads * v_head, tk), lambda i, s, qt, kt: (i, 0, kt[s]))]
    o_shape = jax.ShapeDtypeStruct((b, t, heads * v_head), F32)
    o_spec = pl.BlockSpec((1, tq, heads * v_head), lambda i, s, qt, kt: (i, qt[s], 0))
    acc = pltpu.VMEM((heads * v_head, tq), F32)
    if bound is None:
        kern, name, args = _flash_kernel, "flash_attn", (q, k, vt)
        out_shape, out_specs = o_shape, o_spec
        scratch = [pltpu.VMEM((heads, 1, tq), F32), pltpu.VMEM((heads, 1, tq), F32), acc]
    else:
        qn, kmax = bound
        assert qn.shape == (b, heads, t)
        kern, name, args = _flash_bound_kernel, "flash_attn_bound", (q, k, vt, qn, kmax)
        in_specs += [pl.BlockSpec((1, heads, tq), lambda i, s, qt, kt: (i, 0, qt[s])),
                     pl.BlockSpec((1,) + kmax.shape[1:], lambda i, s, qt, kt: (i, 0, 0, 0))]
        out_shape = [o_shape, jax.ShapeDtypeStruct((b, heads, t), F32)]
        out_specs = [o_spec, pl.BlockSpec((1, heads, tq), lambda i, s, qt, kt: (i, 0, qt[s]))]
        scratch = [pltpu.VMEM((heads, tq), F32), pltpu.VMEM((heads, tq), F32), acc]
    grid_spec = pltpu.PrefetchScalarGridSpec(
        num_scalar_prefetch=2, grid=(b, len(pairs)), in_specs=in_specs, out_specs=out_specs,
        scratch_shapes=scratch)
    return pl.pallas_call(
        functools.partial(kern, heads=heads, v_head=v_head, tq=tq, tk=tk, chunk=CHUNK),
        out_shape=out_shape,
        grid_spec=grid_spec,
        compiler_params=_params("arbitrary", "arbitrary"),
        name=name,
    )(qi_tab, ki_tab, *args)


def _latent_attn_kernel(q_ref, cp_ref, krp_ref, cn_ref, krn_ref, wk_ref, wv_ref, o_ref, *,
                        heads, nope, rope, v_head):
    q = q_ref[0]
    qabs, qrope = [], []
    for h in range(heads):
        qabs.append(_dot_nt(q[:, h * LANES:h * LANES + nope], wk_ref[h]))
        qrope.append(q[:, h * LANES + nope:h * LANES + nope + rope])
    qabs = jnp.concatenate(qabs, axis=0).astype(BF16)
    qrope = jnp.concatenate(qrope, axis=0)
    cp = cp_ref[0].astype(BF16)
    cn = cn_ref[0].astype(BF16)
    krp = krp_ref[0].astype(BF16)
    krn = krn_ref[0].astype(BF16)
    t = q.shape[0]
    rg = (heads // LATENT_ROW_GROUPS) * t
    grp = [slice(g * rg, (g + 1) * rg) for g in range(LATENT_ROW_GROUPS)]
    s_p = [_dot_nt(qabs[r], cp) + _dot_nt(qrope[r], krp) for r in grp]
    s_n = [_dot_nt(qabs[r], cn) + _dot_nt(qrope[r], krn) for r in grp]
    m = [jnp.maximum(jnp.max(a, axis=-1, keepdims=True), jnp.max(b, axis=-1, keepdims=True)) for a, b in zip(s_p, s_n)]
    p_p = [jnp.exp2(a - c) for a, c in zip(s_p, m)]
    p_n = [jnp.exp2(a - c) for a, c in zip(s_n, m)]
    l = [jnp.sum(a, axis=-1, keepdims=True) + jnp.sum(b, axis=-1, keepdims=True) for a, b in zip(p_p, p_n)]
    o_lat = jnp.concatenate([(_dot(a.astype(BF16), cp) + _dot(b.astype(BF16), cn)) / c
                             for a, b, c in zip(p_p, p_n, l)], axis=0)
    for h in range(heads):
        o_ref[0, :, h * v_head:(h + 1) * v_head] = _dot(o_lat[h * t:(h + 1) * t].astype(BF16), wv_ref[h])


def _latent_attn(q, c_past, kr_past, c_new, kr_new, wk, wv, *, heads, nope, rope, v_head):
    b, t, _ = q.shape
    past, kv_lora = c_past.shape[1:]
    blk = lambda n, last: pl.BlockSpec((1, n, last), lambda i: (i, 0, 0))
    const = lambda a: pl.BlockSpec(a.shape, lambda i: (0,) * a.ndim)
    kern = functools.partial(_latent_attn_kernel, heads=heads, nope=nope, rope=rope, v_head=v_head)
    return pl.pallas_call(
        kern,
        out_shape=jax.ShapeDtypeStruct((b, t, heads * v_head), F32),
        grid=(b,),
        in_specs=[blk(t, heads * LANES), blk(past, kv_lora), blk(past, rope), blk(t, kv_lora), blk(t, rope),
                  const(wk), const(wv)],
        out_specs=blk(t, heads * v_head),
        compiler_params=_params("arbitrary"),
        name="latent_attn",
    )(q, c_past, kr_past, c_new, kr_new, wk, wv)


def _mlstm_prep_kernel(m_ref, kt_ref, gc_ref, gr_ref, bc_ref, br_ref, pv_ref, kv_ref, b_ref, st_ref, *,
                       heads, dk, dv, chunk, nchunk):
    row = lax.broadcasted_iota(jnp.int32, (chunk, chunk), 0)
    col = lax.broadcasted_iota(jnp.int32, (chunk, chunk), 1)
    causal = col <= row
    tril = causal.astype(F32)
    triu = (row <= col).astype(F32)
    lane = lax.broadcasted_iota(jnp.int32, (chunk, LANES), 1)
    o_k, o_v = heads * dk, 2 * heads * dk

    rows = [slice(c * chunk, (c + 1) * chunk) for c in range(nchunk)]
    gc = [gc_ref[0, r, :] + bc_ref[...] for r in rows]
    gr = [gr_ref[0, c] + br_ref[...] for c in range(nchunk)]
    bcum_c = [_dot(tril, _log_sigmoid(g), HIGHEST) for g in gc]
    bcum_r = [_dot(_log_sigmoid(g), triu, HIGHEST) for g in gr]
    for c in range(nchunk):
        b_ref[0, rows[c], :] = bcum_c[c]
    ch = [(c, h) for c in range(nchunk) for h in range(heads)]
    v = [m_ref[0, rows[c], o_v + h * dv:o_v + (h + 1) * dv].astype(BF16) for c, h in ch]
    qk = [_dot_nt(m_ref[0, rows[c], h * dk:(h + 1) * dk].astype(BF16),
                  (m_ref[0, rows[c], o_k + h * dk:o_k + (h + 1) * dk] * (dk ** -0.5)).astype(BF16)) for c, h in ch]
    li_r = [gr[c][h:h + 1, :] for c, h in ch]
    b_r = [bcum_r[c][heads + h:heads + h + 1, :] for c, h in ch]
    dmat = [jnp.where(causal, bcum_c[c][:, heads + h:heads + h + 1] - b_r[i] + li_r[i], NEG_INF)
            for i, (c, h) in enumerate(ch)]
    mx = [jnp.max(d, axis=-1, keepdims=True) for d in dmat]
    p0 = [s * jnp.exp(d - m) for s, d, m in zip(qk, dmat, mx)]
    for i, (c, h) in enumerate(ch):
        pv_ref[0, rows[c], h * dv:(h + 1) * dv] = _dot(p0[i].astype(BF16), v[i])
    w_r = [jnp.exp(b_r[i][:, chunk - 1:chunk] - b_r[i] + li_r[i] - mx[i][chunk - 1:chunk, :]) * (dk ** -0.5)
           for i in range(len(ch))]
    for i, (c, h) in enumerate(ch):
        wkt = (kt_ref[0, h * dk:(h + 1) * dk, c * chunk:(c + 1) * chunk] * w_r[i]).astype(BF16)
        kv_ref[0, c, h] = _dot(wkt, jnp.concatenate([v[i], jnp.ones_like(v[i])], axis=1))
    psum = [jnp.sum(p, axis=-1, keepdims=True) for p in p0]
    for c in range(nchunk):
        stats = jnp.zeros((chunk, LANES), F32)
        for h in range(heads):
            stats = jnp.where(lane == h, mx[c * heads + h], stats)
            stats = jnp.where(lane == heads + h, psum[c * heads + h], stats)
        st_ref[0, rows[c], :] = stats


def _mlstm_scan_kernel(q_ref, mo_ref, pv_ref, kv_ref, b_ref, st_ref, gout_ref, c0_ref, n0_ref, m0_ref,
                       h_ref, c1_ref, n1_ref, m1_ref, c_sc, m_sc, *, heads, dk, dv, chunk, nchunk):
    t = pl.program_id(1)
    nseq = q_ref.shape[0]

    @pl.when(t == 0)
    def _():
        c_sc[:, :, :, :dv] = c0_ref[...]
        c_sc[:, :, :, dv:] = jnp.broadcast_to(n0_ref[...], c0_ref.shape)
        m_sc[...] = m0_ref[...]

    def body(c, carry):
        rows = pl.ds(pl.multiple_of(c * chunk, chunk), chunk)
        rep = lambda col: jnp.broadcast_to(col, (chunk, dv))
        ps = [(s, h) for s in range(nseq) for h in range(heads)]
        bcum = [b_ref[s, rows, :] for s in range(nseq)]
        stats = [st_ref[s, rows, :] for s in range(nseq)]
        c2 = [c_sc[s, h] for s, h in ps]
        qc = [_dot(q_ref[s, rows, h * dk:(h + 1) * dk].astype(BF16), c2[i].astype(BF16))
              for i, (s, h) in enumerate(ps)]
        mx = [rep(stats[s][:, h:h + 1]) for s, h in ps]
        inter = [rep(bcum[s][:, heads + h:heads + h + 1]) + m_sc[s, h] for s, h in ps]
        m = [jnp.maximum(a, b) for a, b in zip(inter, mx)]
        w_inter = [jnp.exp(a - b) for a, b in zip(inter, m)]
        r = [jnp.exp(a - b) for a, b in zip(mx, m)]
        for i, (s, h) in enumerate(ps):
            decay_end = w_inter[i][chunk - 1:chunk, :]
            f_new = r[i][chunk - 1:chunk, :]
            c_sc[s, h] = (jnp.concatenate([decay_end, decay_end], axis=1) * c2[i]
                          + jnp.concatenate([f_new, f_new], axis=1) * kv_ref[s, c, h])
            m_sc[s, h] = m[i][chunk - 1:chunk, 0:1]
        num = [w_inter[i] * qc[i][:, :dv] + r[i] * pv_ref[s, rows, h * dv:(h + 1) * dv] for i, (s, h) in enumerate(ps)]
        den = [w_inter[i] * qc[i][:, dv:] + r[i] * rep(stats[s][:, heads + h:heads + h + 1])
               for i, (s, h) in enumerate(ps)]
        hh = [a / jnp.maximum(jnp.abs(b), jnp.exp(-c_)) for a, b, c_ in zip(num, den, m)]
        hn = [_rms(hh[i], gout_ref[:, h * dv:(h + 1) * dv]) for i, (s, h) in enumerate(ps)]
        for i, (s, h) in enumerate(ps):
            h_ref[s, rows, h * dv:(h + 1) * dv] = hn[i] * _sigmoid(mo_ref[s, rows, h * dv:(h + 1) * dv])
        return carry

    lax.fori_loop(0, nchunk, body, 0)

    @pl.when(t == pl.num_programs(1) - 1)
    def _():
        c1_ref[...] = c_sc[:, :, :, :dv]
        n1_ref[...] = c_sc[:, :, :, dv:dv + 1]
        m1_ref[...] = m_sc[...]


def _mlstm_prep(m_slab, kt, gates_c, gates_r, bias_c, bias_r, *, chunk, heads, dk, dv):
    b, t, mw = m_slab.shape
    const = lambda a: pl.BlockSpec(a.shape, lambda i, j: (0,) * a.ndim)
    qkv_w = 2 * heads * dk + heads * dv
    tp = min(t, MLSTM_PREP_TC)
    npc = tp // chunk
    tokp = lambda last: pl.BlockSpec((1, tp, last), lambda i, j: (i, j, 0))
    per_chunk = lambda *s: pl.BlockSpec((1, npc, heads) + s, lambda i, j: (i, j, 0) + (0,) * len(s))
    return pl.pallas_call(
        functools.partial(_mlstm_prep_kernel, heads=heads, dk=dk, dv=dv, chunk=chunk, nchunk=npc),
        out_shape=[jax.ShapeDtypeStruct((b, t, heads * dv), F32),
                   jax.ShapeDtypeStruct((b, t // chunk, heads, dk, 2 * dv), F32),
                   jax.ShapeDtypeStruct((b, t, LANES), F32),
                   jax.ShapeDtypeStruct((b, t, LANES), F32)],
        grid=(b, t // tp),
        in_specs=[tokp(qkv_w), pl.BlockSpec((1, heads * dk, tp), lambda i, j: (i, 0, j)), tokp(LANES),
                  pl.BlockSpec((1, npc, SUBLANES, chunk), lambda i, j: (i, j, 0, 0)),
                  const(bias_c), const(bias_r)],
        out_specs=[tokp(heads * dv), per_chunk(dk, 2 * dv), tokp(LANES), tokp(LANES)],
        compiler_params=_params("arbitrary", "arbitrary"),
        name="mlstm_prep",
    )(m_slab, kt, gates_c, gates_r, bias_c, bias_r)


def _mlstm_scan(m_slab, pv, kv, bcum, stats, gout, c0, n0, m0, *, tc, heads, dk, dv):
    b, t, mw = m_slab.shape
    chunk = min(CHUNK, t)
    const = lambda a: pl.BlockSpec(a.shape, lambda i, j: (0,) * a.ndim)
    qkv_w = 2 * heads * dk + heads * dv
    mo_blk, rem = divmod(qkv_w, heads * dv)
    assert rem == 0 and mw == qkv_w + heads * dv
    nchunk = tc // chunk
    ns = math.gcd(b, SCAN_SEQS)
    tok = lambda last, blk=0: pl.BlockSpec((ns, tc, last), lambda i, j: (i, j, blk))
    per_chunk = lambda *s: pl.BlockSpec((ns, nchunk, heads) + s, lambda i, j: (i, j, 0) + (0,) * len(s))
    st = lambda *s: pl.BlockSpec((ns,) + s, lambda i, j: (i,) + (0,) * len(s))
    hm, c1, n1, m1 = pl.pallas_call(
        functools.partial(_mlstm_scan_kernel, heads=heads, dk=dk, dv=dv, chunk=chunk, nchunk=nchunk),
        out_shape=[jax.ShapeDtypeStruct((b, t, heads * dv), F32),
                   jax.ShapeDtypeStruct((b, heads, dk, dv), F32),
                   jax.ShapeDtypeStruct((b, heads, dk, 1), F32),
                   jax.ShapeDtypeStruct((b, heads, 1, 1), F32)],
        grid=(b // ns, t // tc),
        in_specs=[tok(heads * dk),
                  tok(heads * dv, mo_blk),
                  tok(heads * dv), per_chunk(dk, 2 * dv), tok(LANES), tok(LANES),
                  const(gout), st(heads, dk, dv), st(heads, dk, 1), st(heads, 1, 1)],
        out_specs=[tok(heads * dv), st(heads, dk, dv), st(heads, dk, 1), st(heads, 1, 1)],
        scratch_shapes=[pltpu.VMEM((ns, heads, dk, 2 * dv), F32), pltpu.VMEM((ns, heads, 1, 1), F32)],
        compiler_params=_params("arbitrary", "arbitrary"),
        name="mlstm_scan",
    )(m_slab, m_slab, pv, kv, bcum, stats, gout, c0, n0[..., None], m0)
    return hm, c1, n1[..., 0], m1


def _mid_kernel(oa_ref, ob_ref, z_ref, x_ref, gate_ref, wo_ref, shift_ref, scale_ref, g_ref, w2_ref,
                x1_ref, qkv_ref, z2_ref, ab_ref, *, half, conv_ch, d_model):
    z = z_ref[0]
    ma = (oa_ref[0] * _silu(z[:, :half])).astype(BF16)
    mb = (ob_ref[0] * _silu(z[:, half:])).astype(BF16)
    y = _dot(ma, wo_ref[0:half, :]) + _dot(mb, wo_ref[half:, :])
    x1 = x_ref[0] + _per_row(gate_ref[0], x_ref.shape[1]) * y
    x1_ref[0] = x1
    hn = _rms(x1, g_ref[...]) * (1.0 + _per_row(scale_ref[0], x_ref.shape[1])) + _per_row(shift_ref[0], x_ref.shape[1])
    y2 = _dot(hn.astype(BF16), w2_ref[...])
    qkv_ref[0] = y2[:, :conv_ch]
    z2_ref[0] = y2[:, conv_ch:conv_ch + d_model]
    ab_ref[0] = y2[:, conv_ch + d_model:]


def _mid(oa, ob, z, x, gate, wo, shift, scale, g, w2, *, tm, conv_ch):
    b, t, d = x.shape
    half = oa.shape[-1]
    tok = lambda last: pl.BlockSpec((1, tm, last), lambda i, j: (i, j, 0))
    vec = pl.BlockSpec((1,) + gate.shape[1:], lambda i, j: (i, 0, 0))
    const = lambda a: pl.BlockSpec(a.shape, lambda i, j: (0,) * a.ndim)
    kern = functools.partial(_mid_kernel, half=half, conv_ch=conv_ch, d_model=d)
    return pl.pallas_call(
        kern,
        out_shape=[jax.ShapeDtypeStruct((b, t, d), F32), jax.ShapeDtypeStruct((b, t, conv_ch), F32),
                   jax.ShapeDtypeStruct((b, t, d), F32), jax.ShapeDtypeStruct((b, t, LANES), F32)],
        grid=(b, t // tm),
        in_specs=[tok(half), tok(ob.shape[-1]), tok(d), tok(d), vec, const(wo), vec, vec, const(g), const(w2)],
        out_specs=[tok(d), tok(conv_ch), tok(d), tok(LANES)],
        compiler_params=_params("arbitrary", "arbitrary"),
        name="out_a_in_c",
    )(oa, ob, z, x, gate, wo, shift, scale, g, w2)


def _conv_kernel(qkv_ref, past_ref, wc_ref, ab_ref, alog_ref, dtb_ref, act_ref, gb_ref, ext_sc, *,
                 tm, width, heads, dk):
    @pl.when(pl.program_id(1) == 0)
    def _():
        ext_sc[0:SUBLANES, :] = past_ref[0]

    ext_sc[SUBLANES:SUBLANES + tm, :] = qkv_ref[0]
    conv = wc_ref[width - 1:width, :] * ext_sc[SUBLANES:SUBLANES + tm, :]
    for j in range(width - 1):
        s = SUBLANES - (width - 1) + j
        conv = conv + wc_ref[j:j + 1, :] * ext_sc[s:s + tm, :]
    ext_sc[0:SUBLANES, :] = ext_sc[tm:tm + SUBLANES, :]
    act = _silu(conv)
    for h in range(2 * heads):
        xh = act[:, h * dk:(h + 1) * dk]
        xh = xh * lax.rsqrt(jnp.sum(xh * xh, axis=-1, keepdims=True) + EPS)
        if h < heads:
            xh = xh * (dk ** -0.5)
        act_ref[0, :, h * dk:(h + 1) * dk] = xh
    act_ref[0, :, 2 * heads * dk:] = act[:, 2 * heads * dk:]
    ab = ab_ref[0]
    g = -jnp.exp(alog_ref[...]) * _softplus(ab + dtb_ref[...])
    lane = lax.broadcasted_iota(jnp.int32, ab.shape, 1)
    gb_ref[0] = jnp.where(lane < heads, g, _sigmoid(ab))


def _conv(qkv, past8, wc8, ab, alog, dtb, *, tm, width, heads, dk):
    b, t, ch = qkv.shape
    tok = lambda last: pl.BlockSpec((1, tm, last), lambda i, j: (i, j, 0))
    const = lambda a: pl.BlockSpec(a.shape, lambda i, j: (0,) * a.ndim)
    kern = functools.partial(_conv_kernel, tm=tm, width=width, heads=heads, dk=dk)
    return pl.pallas_call(
        kern,
        out_shape=[jax.ShapeDtypeStruct((b, t, ch), F32), jax.ShapeDtypeStruct((b, t, LANES), F32)],
        grid=(b, t // tm),
        in_specs=[tok(ch), pl.BlockSpec((1, SUBLANES, ch), lambda i, j: (i, 0, 0)), const(wc8), tok(LANES),
                  const(alog), const(dtb)],
        out_specs=[tok(ch), tok(LANES)],
        scratch_shapes=[pltpu.VMEM((tm + SUBLANES, ch), F32)],
        compiler_params=_params("arbitrary", "arbitrary"),
        name="conv_gates",
    )(qkv, past8, wc8, ab, alog, dtb)


def _blockdiag(x, group, chunk):
    w = group * chunk
    br = lax.broadcasted_iota(jnp.int32, (w, w), 0) // chunk
    bc = lax.broadcasted_iota(jnp.int32, (w, w), 1) // chunk
    xb = x.astype(BF16)
    return jnp.where(br == bc, jnp.concatenate([xb] * group, axis=0), jnp.zeros((), BF16))


def _unit_lower_inverses_minus_eye(a_list, group, chunk):
    w = group * chunk
    r = lax.broadcasted_iota(jnp.int32, (chunk, w), 0)
    cc = lax.broadcasted_iota(jnp.int32, (chunk, w), 1) % chunk
    es = [-jnp.where((r // 2 == cc // 2) & (r % 2 == 1) & (cc % 2 == 0), a4, 0.0) for a4 in a_list]
    s = 2
    while s < chunk:
        off = (r // (2 * s) == cc // (2 * s)) & (r % (2 * s) >= s) & (cc % (2 * s) < s)
        a_offs = [jnp.where(off, a4, 0.0) for a4 in a_list]
        ps = [a + _dot(a.astype(BF16), _blockdiag(e, group, chunk)) for a, e in zip(a_offs, es)]
        es = [e - (p + _dot(e.astype(BF16), _blockdiag(p, group, chunk))) for e, p in zip(es, ps)]
        s *= 2
    return es


def _gdn_prep_kernel(act_ref, gbc_ref, gbr_ref, w_ref, uv_ref, kd_ref, attn_ref, eg_ref, *,
                     heads, dk, dv, chunk, nchunk, group):
    row = lax.broadcasted_iota(jnp.int32, (chunk, chunk), 0)
    col = lax.broadcasted_iota(jnp.int32, (chunk, chunk), 1)
    incl = col <= row
    strict = col < row
    tril = incl.astype(F32)
    triu = (row <= col).astype(F32)
    lane = lax.broadcasted_iota(jnp.int32, (chunk, LANES), 1)
    o_k, o_v = heads * dk, 2 * heads * dk

    rows = [slice(c * chunk, (c + 1) * chunk) for c in range(nchunk)]
    gbc = [gbc_ref[0, r, :] for r in rows]
    gcum_c = [_dot(tril, g, HIGHEST) for g in gbc]
    gcum_r = [_dot(gbr_ref[0, c], triu, HIGHEST) for c in range(nchunk)]
    for c in range(nchunk):
        eg_ref[0, rows[c], :] = jnp.where(lane < heads, jnp.exp(gcum_c[c]), 0.0)
    ch = [(c, h) for c in range(nchunk) for h in range(heads)]
    k = [act_ref[0, rows[c], o_k + h * dk:o_k + (h + 1) * dk] for c, h in ch]
    kb = [x.astype(BF16) for x in k]
    kk = [_dot_nt(x, x) for x in kb]
    qk = [_dot_nt(act_ref[0, rows[c], h * dk:(h + 1) * dk].astype(BF16), kb[i]) for i, (c, h) in enumerate(ch)]
    g_c = [gcum_c[c][:, h:h + 1] for c, h in ch]
    beta = [gbc[c][:, heads + h:heads + h + 1] for c, h in ch]
    decay = [jnp.exp(jnp.where(incl, g_c[i] - gcum_r[c][h:h + 1, :], NEG_INF)) for i, (c, h) in enumerate(ch)]
    a_blk = [jnp.where(strict, beta[i] * kk[i] * decay[i], 0.0) for i in range(len(ch))]
    for i, (c, h) in enumerate(ch):
        attn_ref[0, rows[c], h * chunk:(h + 1) * chunk] = (qk[i] * decay[i]).astype(BF16)
        kd_ref[0, rows[c], h * dk:(h + 1) * dk] = (k[i] * jnp.exp(g_c[i][chunk - 1:chunk, :] - g_c[i])).astype(BF16)
    rhs_blk = [jnp.concatenate([beta[i] * act_ref[0, rows[c], o_v + h * dv:o_v + (h + 1) * dv],
                                (beta[i] * jnp.exp(g_c[i])) * k[i]], axis=1) for i, (c, h) in enumerate(ch)]
    problems = [(c, g0) for c in range(nchunk) for g0 in range(0, heads, group)]
    a_list = [jnp.concatenate(a_blk[c * heads + g0:c * heads + g0 + group], axis=1) for c, g0 in problems]
    rhs_list = [jnp.concatenate(rhs_blk[c * heads + g0:c * heads + g0 + group], axis=0) for c, g0 in problems]
    e_list = _unit_lower_inverses_minus_eye(a_list, group, chunk)
    sols = [rhs + _dot(_blockdiag(e, group, chunk), rhs.astype(BF16)) for e, rhs in zip(e_list, rhs_list)]
    for (c, g0), sol in zip(problems, sols):
        for i, h in enumerate(range(g0, g0 + group)):
            uv_ref[0, rows[c], h * dv:(h + 1) * dv] = sol[i * chunk:(i + 1) * chunk, :dv]
            w_ref[0, rows[c], h * dk:(h + 1) * dk] = sol[i * chunk:(i + 1) * chunk, dv:].astype(BF16)


def _gdn_prep(act, gb_c, gb_r, *, chunk, tc, heads, dk, dv):
    b, t, _ = act.shape
    tc = min(tc, t)
    nchunk = tc // chunk
    group = (2 * LANES) // chunk
    kern = functools.partial(_gdn_prep_kernel, heads=heads, dk=dk, dv=dv, chunk=chunk, nchunk=nchunk, group=group)
    tok = lambda last: pl.BlockSpec((1, tc, last), lambda i, j: (i, j, 0))
    return pl.pallas_call(
        kern,
        out_shape=[jax.ShapeDtypeStruct((b, t, heads * dk), BF16), jax.ShapeDtypeStruct((b, t, heads * dv), F32),
                   jax.ShapeDtypeStruct((b, t, heads * dk), BF16), jax.ShapeDtypeStruct((b, t, heads * chunk), BF16),
                   jax.ShapeDtypeStruct((b, t, LANES), F32)],
        grid=(b, t // tc),
        in_specs=[tok(act.shape[-1]), tok(LANES),
                  pl.BlockSpec((1, nchunk, 2 * SUBLANES, chunk), lambda i, j: (i, j, 0, 0))],
        out_specs=[tok(heads * dk), tok(heads * dv), tok(heads * dk), tok(heads * chunk), tok(LANES)],
        compiler_params=_params("arbitrary", "arbitrary"),
        name="gdn_prep",
    )(act, gb_c, gb_r)


def _gdn_scan_kernel(q_ref, w_ref, uv_ref, kd_ref, attn_ref, eg_ref, gout_ref, s0_ref, o_ref, s1_ref, s_sc, *,
                     heads, dk, dv, chunk, nchunk):
    t = pl.program_id(1)
    nseq = q_ref.shape[0]

    @pl.when(t == 0)
    def _():
        s_sc[...] = s0_ref[...]

    def body(c, carry):
        rows = pl.ds(pl.multiple_of(c * chunk, chunk), chunk)
        ps = [(s, h) for s in range(nseq) for h in range(heads)]
        eg = [eg_ref[s, rows, :] for s in range(nseq)]
        s0 = [s_sc[s, h] for s, h in ps]
        s0b = [x.astype(BF16) for x in s0]
        ws = [_dot(w_ref[s, rows, h * dk:(h + 1) * dk], s0b[i]) for i, (s, h) in enumerate(ps)]
        ub = [(uv_ref[s, rows, h * dv:(h + 1) * dv] - ws[i]).astype(BF16) for i, (s, h) in enumerate(ps)]
        eg_h = [jnp.broadcast_to(eg[s][:, h:h + 1], (chunk, dv)) for s, h in ps]
        ku = [_dot_tn(kd_ref[s, rows, h * dk:(h + 1) * dk], ub[i]) for i, (s, h) in enumerate(ps)]
        for i, (s, h) in enumerate(ps):
            s_sc[s, h] = eg_h[i][chunk - 1:chunk, :] * s0[i] + ku[i]
        qs = [_dot(q_ref[s, rows, h * dk:(h + 1) * dk].astype(BF16), s0b[i]) for i, (s, h) in enumerate(ps)]
        au = [_dot(attn_ref[s, rows, h * chunk:(h + 1) * chunk], ub[i]) for i, (s, h) in enumerate(ps)]
        on = [_rms(eg_h[i] * qs[i] + au[i], gout_ref[...]) for i in range(len(ps))]
        for i, (s, h) in enumerate(ps):
            o_ref[s, rows, h * dv:(h + 1) * dv] = on[i]
        return carry

    lax.fori_loop(0, nchunk, body, 0)

    @pl.when(t == pl.num_programs(1) - 1)
    def _():
        s1_ref[...] = s_sc[...]


def _gdn_scan(act, w, uv, kd, attn, eg, gout, s0, *, tc, heads, dk, dv):
    b, t, _ = act.shape
    chunk = min(CHUNK, t)
    nchunk = tc // chunk
    kern = functools.partial(_gdn_scan_kernel, heads=heads, dk=dk, dv=dv, chunk=chunk, nchunk=nchunk)
    ns = math.gcd(b, SCAN_SEQS)
    tok = lambda last: pl.BlockSpec((ns, tc, last), lambda i, j: (i, j, 0))
    state = pl.BlockSpec((ns, heads, dk, dv), lambda i, j: (i, 0, 0, 0))
    return pl.pallas_call(
        kern,
        out_shape=[jax.ShapeDtypeStruct((b, t, heads * dv), F32), jax.ShapeDtypeStruct(s0.shape, F32)],
        grid=(b // ns, t // tc),
        in_specs=[tok(heads * dk),
                  tok(heads * dk), tok(heads * dv), tok(heads * dk), tok(heads * chunk), tok(LANES),
                  pl.BlockSpec(gout.shape, lambda i, j: (0, 0)), state],
        out_specs=[tok(heads * dv), state],
        scratch_shapes=[pltpu.VMEM((ns, heads, dk, dv), F32)],
        compiler_params=_params("arbitrary", "arbitrary"),
        name="gdn_scan",
    )(act, w, uv, kd, attn, eg, gout, s0)


def _final_kernel(o_ref, z_ref, x_ref, gate_ref, wo_ref, g_ref, y_ref):
    mixed = (o_ref[0] * _silu(z_ref[0])).astype(BF16)
    x2 = x_ref[0] + _per_row(gate_ref[0], x_ref.shape[1]) * _dot(mixed, wo_ref[...])
    y_ref[0] = _rms(x2, g_ref[...])


def _final(o, z, x, gate, wo, g, *, tm):
    b, t, d = x.shape
    tok = lambda last: pl.BlockSpec((1, tm, last), lambda i, j: (i, j, 0))
    const = lambda a: pl.BlockSpec(a.shape, lambda i, j: (0,) * a.ndim, pipeline_mode=pl.Buffered(1))
    return pl.pallas_call(
        _final_kernel,
        out_shape=jax.ShapeDtypeStruct((b, t, d), F32),
        grid=(b, t // tm),
        in_specs=[tok(o.shape[-1]), tok(d), tok(d), pl.BlockSpec((1,) + gate.shape[1:], lambda i, j: (i, 0, 0)),
                  const(wo), const(g)],
        out_specs=tok(d),
        compiler_params=_params("arbitrary", "arbitrary"),
        name="out_c_final",
    )(o, z, x, gate, wo, g)


def _pad_lanes(w, width=LANES, at=0):
    out = jnp.zeros(w.shape[:-1] + (width,), w.dtype)
    return out.at[..., at:at + w.shape[-1]].set(w)


def _rot_half_cols(w):
    r = w.shape[-1] // 2
    return jnp.concatenate([-w[..., r:], w[..., :r]], axis=-1)


def _rope_tables(pos, rope, scale):
    freqs = jnp.exp(jnp.arange(0, rope, 2, dtype=F32) * (-math.log(ROPE_BASE) / rope))
    ang = pos.astype(F32)[:, None] * freqs[None, :]
    cos = jnp.concatenate([jnp.cos(ang), jnp.cos(ang)], axis=-1)
    sin = jnp.concatenate([jnp.sin(ang), jnp.sin(ang)], axis=-1)
    half = LANES // 2
    cosk = _pad_lanes(cos, at=half)
    sink = _pad_lanes(sin, at=half)
    ones = _pad_lanes(jnp.ones((pos.shape[0], half), F32))
    return jnp.concatenate([(cosk + ones) * scale, sink * scale, cosk, sink], axis=-1)


def _tokens_on_lanes(a, chunk, rows):
    b, t = a.shape[:2]
    return a[..., :rows].reshape(b, t // chunk, chunk, rows).transpose(0, 1, 3, 2)


def kernel(x_prompt, x_sample, c_prompt, c_sample, cache_kv_latent, cache_k_rope, state_mlstm_C, state_mlstm_n, state_mlstm_m, state_gdn_S, state_gdn_conv, a_w_ada, a_b_ada, a_g_norm, a_w_in, a_g_q_a, a_w_q_b, a_g_kv_a, a_w_kv_b, a_b_i, a_b_f, a_g_out, a_w_out, c_w_ada, c_b_ada, c_g_norm, c_w_in, c_w_conv, c_a_log, c_dt_bias, c_g_out, c_w_out, g_final):
    d = x_prompt.shape[-1]
    q_lora, heads, qk = a_w_q_b.shape
    kv_lora = a_w_kv_b.shape[0]
    rope = cache_k_rope.shape[-1]
    nope = qk - rope
    v_head = a_w_kv_b.shape[2] - nope
    m_heads, m_dv = a_g_out.shape
    m_dk = state_mlstm_C.shape[2]
    g_heads = c_a_log.shape[0]
    g_dk, g_dv = state_gdn_S.shape[2:]
    width = c_w_conv.shape[0]
    conv_ch = c_w_conv.shape[1]
    assert nope + rope <= LANES and nope == LANES // 2 and 2 * m_heads <= SUBLANES and 2 * g_heads <= 2 * SUBLANES

    sizes = (q_lora, kv_lora, rope, m_heads * m_dk, m_heads * m_dk, m_heads * m_dv, m_heads, m_heads,
             m_heads * m_dv, heads * v_head + m_heads * m_dv)
    offs = [0]
    for s in sizes:
        offs.append(offs[-1] + s)
    w_qa, w_c, w_kr, w_mq, w_mk, w_mv, w_mi, w_mf, w_mo, w_z = [a_w_in[:, offs[i]:offs[i + 1]] for i in range(10)]
    half = LANES // 2
    w1 = jnp.concatenate([w_qa, w_c, _pad_lanes(w_kr, at=half), _pad_lanes(_rot_half_cols(w_kr), at=half),
                          w_mq, w_mk, w_mv, w_mo, w_z, _pad_lanes(jnp.concatenate([w_mi, w_mf], axis=1))],
                         axis=1).astype(BF16)
    m_width = 2 * m_heads * m_dk + 2 * m_heads * m_dv
    wq_rope = a_w_q_b[..., nope:]
    wq_main = _pad_lanes(a_w_q_b).reshape(q_lora, heads * LANES)
    wq_rot = _pad_lanes(_rot_half_cols(wq_rope), at=nope).reshape(q_lora, heads * LANES)
    wq = jnp.concatenate([wq_main, wq_rot], axis=1).astype(BF16)
    wkv = jnp.concatenate([_pad_lanes(a_w_kv_b[..., :nope]).reshape(kv_lora, heads * LANES),
                           a_w_kv_b[..., nope:].reshape(kv_lora, heads * v_head)], axis=1).astype(BF16)
    wk_abs = a_w_kv_b[..., :nope].transpose(1, 0, 2).astype(BF16)
    wv_abs = a_w_kv_b[..., nope:].transpose(1, 0, 2).astype(BF16)
    wo_a = a_w_out.astype(BF16)
    csz = (conv_ch, g_heads, g_heads, g_heads * g_dv)
    w_qkv, w_a, w_b, w_zc = [c_w_in[:, sum(csz[:i]):sum(csz[:i + 1])] for i in range(4)]
    w2 = jnp.concatenate([w_qkv, w_zc, _pad_lanes(jnp.concatenate([w_a, w_b], axis=1))], axis=1).astype(BF16)
    wo_c = c_w_out.astype(BF16)
    wc8 = jnp.zeros((SUBLANES, conv_ch), F32).at[:width].set(c_w_conv)
    row = lambda a: a.reshape(1, -1).astype(F32)
    bias_c = _pad_lanes(jnp.concatenate([a_b_i, a_b_f]).reshape(1, -1))
    bias_r = jnp.zeros((SUBLANES, 1), F32).at[:2 * m_heads, 0].set(jnp.concatenate([a_b_i, a_b_f]))
    alog = _pad_lanes(c_a_log.reshape(1, -1))
    dtb = _pad_lanes(c_dt_bias.reshape(1, -1))

    bp, bs = c_prompt.shape[0], c_sample.shape[0]
    c_all = jnp.concatenate([c_prompt, c_sample], axis=0)
    pad = (-c_all.shape[0]) % SUBLANES
    c_all = jnp.pad(c_all, ((0, pad), (0, 0)))
    mod_a = _adaln(c_all, a_w_ada, a_b_ada)
    mod_c = _adaln(c_all, c_w_ada, c_b_ada)

    def mods(mod, lo, hi):
        return [mod[lo:hi, i * d:(i + 1) * d][:, None, :] for i in range(3)]

    def run(x, mod_lo, mod_hi, c_past, kr_past, c0, n0, m0, conv0, s0):
        b, t, _ = x.shape
        past = 0 if c_past is None else c_past.shape[1]
        chunk = min(CHUNK, t)
        tc = min(t, 512)
        expand = c_past is None
        nb = 1 if expand else math.gcd(b, max(1, PROJ_TM // t))
        tm = min(nb * t, PROJ_TM)
        grp = lambda a: a.reshape((b // nb, nb * a.shape[1]) + a.shape[2:])
        ungrp = lambda a: a.reshape((b, a.shape[1] // nb) + a.shape[2:])
        shift_a, scale_a, gate_a = [grp(m) for m in mods(mod_a, mod_lo, mod_hi)]
        shift_c, scale_c, gate_c = [grp(m) for m in mods(mod_c, mod_lo, mod_hi)]
        tab = _rope_tables(past + jnp.arange(t, dtype=jnp.int32), rope, qk ** -0.5 * math.log2(math.e))
        outs = _in_a(grp(x), shift_a, scale_a, row(a_g_norm), w1, row(a_g_q_a), wq, row(a_g_kv_a), wkv,
                     jnp.tile(tab, (nb, 1)),
                     tm=tm, heads=heads, q_lora=q_lora, kv_lora=kv_lora, rope=rope, m_width=m_width,
                     mk_cols=(m_heads * m_dk, 2 * m_heads * m_dk), v_head=v_head, expand_kv=expand)
        q, c_new, kr_new, m_slab, z, gates = [ungrp(a) for a in outs[:6]]
        kt = None
        if expand:
            k_all, vt, kt = outs[6], outs[7], outs[10]
            qn = outs[8][..., :heads].transpose(0, 2, 1)
            tiles = dict(heads=heads, v_head=v_head, tq=min(t, FLASH_TQ), tk=min(t, FLASH_TK))
            o_fast, row_sums = _flash(q, k_all, vt, bound=(qn, outs[9]), **tiles)
            o_mla = lax.cond(jnp.min(row_sums) >= FLASH_ROW_SUM_MIN,
                             lambda: o_fast, lambda: _flash(q, k_all, vt, **tiles))
        else:
            o_mla = _latent_attn(q, c_past, kr_past, c_new, kr_new, wk_abs, wv_abs,
                                 heads=heads, nope=nope, rope=rope, v_head=v_head)
        fold = t == chunk and b > 1
        flat = (lambda a: a.reshape((1, b * a.shape[1]) + a.shape[2:])) if fold else (lambda a: a)
        unflat = (lambda a: a.reshape((b, a.shape[1] // b) + a.shape[2:])) if fold else (lambda a: a)
        m_flat = flat(m_slab)
        if kt is None:
            kt = m_flat[..., m_heads * m_dk:2 * m_heads * m_dk].transpose(0, 2, 1)
        pv, kvs, bcum, stats = [unflat(a) for a in _mlstm_prep(
            m_flat, kt, flat(gates), flat(_tokens_on_lanes(gates, chunk, SUBLANES)), bias_c, bias_r,
            chunk=chunk, heads=m_heads, dk=m_dk, dv=m_dv)]
        hm, c1, n1, m1 = _mlstm_scan(m_slab, pv, kvs, bcum, stats, row(a_g_out), c0, n0, m0.reshape(b, m_heads, 1, 1),
                                     tc=tc, heads=m_heads, dk=m_dk, dv=m_dv)
        x1, qkv, zc, ab = [ungrp(a) for a in _mid(grp(o_mla), grp(hm), grp(z), grp(x), gate_a, wo_a, shift_c, scale_c,
                                                  row(c_g_norm), w2, tm=tm, conv_ch=conv_ch)]
        past8 = jnp.pad(conv0, ((0, 0), (SUBLANES - (width - 1), 0), (0, 0)))
        act, gb = _conv(qkv, past8, wc8, ab, alog, dtb, tm=min(t, PROJ_TM), width=width, heads=g_heads, dk=g_dk)
        w, uv, kd, attn, eg = [unflat(a) for a in _gdn_prep(
            flat(act), flat(gb), flat(_tokens_on_lanes(gb, chunk, 2 * SUBLANES)),
            chunk=chunk, tc=GDN_PREP_TC, heads=g_heads, dk=g_dk, dv=g_dv)]
        o_gdn, s1 = _gdn_scan(act, w, uv, kd, attn, eg, row(c_g_out), s0, tc=min(t, GDN_SCAN_TC),
                              heads=g_heads, dk=g_dk, dv=g_dv)
        y = ungrp(_final(grp(o_gdn), grp(zc), grp(x1), gate_c, wo_c, row(g_final), tm=min(nb * t, FINAL_TM)))
        conv1 = jnp.concatenate([conv0, qkv], axis=1)[:, t:] if t < width - 1 else qkv[:, t - (width - 1):]
        return (y, c_new, kr_new, c1, n1, m1.reshape(b, m_heads), conv1, s1)

    dt = x_prompt.dtype
    (y_p, p_kv, p_kr, p_c, p_n, p_m, p_conv, p_s) = run(
        x_prompt, 0, bp, None, None,
        jnp.zeros((bp, m_heads, m_dk, m_dv), dt), jnp.zeros((bp, m_heads, m_dk), dt), jnp.zeros((bp, m_heads), dt),
        jnp.zeros((bp, width - 1, conv_ch), dt), jnp.zeros((bp, g_heads, g_dk, g_dv), dt))
    (y_s, s_kv, s_kr, s_c, s_n, s_m, s_conv, s_s) = run(
        x_sample, bp, bp + bs, cache_kv_latent, cache_k_rope, state_mlstm_C, state_mlstm_n, state_mlstm_m,
        state_gdn_conv, state_gdn_S)
    return (y_p, y_s, p_kv, p_kr, p_c, p_n, p_m, p_s, p_conv,
            s_kv, s_kr, s_c, s_n, s_m, s_s, s_conv)
```
